```python
import math
import jax, jax.numpy as jnp
from jax import lax
import numpy as np

D_MODEL = 1024
BATCH = 16
SEQ = 256
DEPTH = 2
DEC_BATCH = 4
DEC_SEQ = 2048
PAST_LEN = 256

GRID_W = 64
EPS = 1e-6
ML_HEADS = 4
ML_DIM = 64
ML_WIDTH = ML_HEADS * ML_DIM
ML_CHUNK = 128
CM_GROUPS = 4
CM_DIM = 64
CM_WIDTH = CM_GROUPS * CM_DIM
CM_CHUNK = 128
MLA_HEADS = 8
MLA_NOPE = 64
MLA_ROPE = 32
MLA_V = 64
MLA_Q_RANK = 256
MLA_KV_RANK = 256
MLA_WIDTH = MLA_HEADS * MLA_V
MLA_QBLOCK = 128
ROPE_AXIS = MLA_ROPE // 2
ROPE_THETA = 10000.0
MIX_WIDTH = ML_WIDTH + CM_WIDTH + MLA_WIDTH
IN_SPLITS = (ML_WIDTH, ML_WIDTH, ML_WIDTH, ML_WIDTH, ML_HEADS, ML_HEADS, ML_HEADS, ML_HEADS, CM_WIDTH, CM_WIDTH, MLA_Q_RANK, MLA_KV_RANK, MLA_ROPE)
IN_COLS = 4 * ML_WIDTH + 4 * ML_HEADS + 2 * CM_WIDTH + MLA_Q_RANK + MLA_KV_RANK + MLA_ROPE
PEER_HEADS = 8
PEER_NKEYS = 128
PEER_EXPERTS = PEER_NKEYS * PEER_NKEYS
PEER_DKEY = 256
PEER_TOPK = 16
PEER_TOKBLOCK = 128

kernel_name = 'hybrid_mlstm_sgu_mla_peer_diffusion_step'


def rmsnorm(x, g):
    xf = x.astype(jnp.float32)
    y = xf * lax.rsqrt(jnp.mean(xf * xf, axis=-1, keepdims=True) + EPS)
    return (y * g.astype(jnp.float32)).astype(x.dtype)


def split_cols(proj):
    out = []
    off = 0
    for w in IN_SPLITS:
        out.append(proj[..., off:off + w])
        off += w
    return out


def mlstm_chunkwise(q, k, v, i_pre, f_pre, C0, n0, m0):
    B, T, H, Dh = q.shape
    L = ML_CHUNK
    nc = T // L
    f32 = jnp.float32

    def to_chunks(a):
        return a.astype(f32).reshape(B, nc, L, H, Dh).transpose(1, 0, 3, 2, 4)

    def gates_to_chunks(a):
        return a.astype(f32).reshape(B, nc, L, H).transpose(1, 0, 3, 2)

    qc = to_chunks(q)
    kc = to_chunks(k) * (Dh ** -0.5)
    vc = to_chunks(v)
    ic = gates_to_chunks(i_pre)
    lfc = jax.nn.log_sigmoid(gates_to_chunks(f_pre))
    causal = jnp.tril(jnp.ones((L, L), dtype=bool))

    def step(carry, xs):
        C, n, m = carry
        qb, kb, vb, ib, lfb = xs
        b = jnp.cumsum(lfb, axis=-1)
        d = jnp.where(causal, b[..., :, None] - b[..., None, :] + ib[..., None, :], -jnp.inf)
        m_inter = b + m[..., None]
        m_t = jnp.maximum(m_inter, d.max(-1))
        w_intra = jnp.exp(d - m_t[..., None])
        w_inter = jnp.exp(m_inter - m_t)
        sw = jnp.einsum('bhtd,bhsd->bhts', qb, kb) * w_intra
        num = jnp.einsum('bhts,bhse->bhte', sw, vb) + w_inter[..., None] * jnp.einsum('bhtd,bhde->bhte', qb, C)
        den = sw.sum(-1) + w_inter * jnp.einsum('bhtd,bhd->bht', qb, n)
        h = num / jnp.maximum(jnp.abs(den), jnp.exp(-m_t))[..., None]
        b_end = b[..., -1]
        log_w = b_end[..., None] - b + ib
        m_new = jnp.maximum(b_end + m, log_w.max(-1))
        w_k = jnp.exp(log_w - m_new[..., None])
        decay = jnp.exp(b_end + m - m_new)
        C_new = decay[..., None, None] * C + jnp.einsum('bhs,bhsd,bhse->bhde', w_k, kb, vb)
        n_new = decay[..., None] * n + jnp.einsum('bhs,bhsd->bhd', w_k, kb)
        return (C_new, n_new, m_new), h

    (C, n, m), h = lax.scan(step, (C0.astype(f32), n0.astype(f32), m0.astype(f32)), (qc, kc, vc, ic, lfc))
    h = h.transpose(1, 0, 3, 2, 4).reshape(B, T, H, Dh)
    return h.astype(q.dtype), C, n, m


def chunk_spatial_gating(u, v, norm_g, ws, bs):
    B, T, _ = v.shape
    nc = T // CM_CHUNK
    vn = rmsnorm(v, norm_g).reshape(B, nc, CM_CHUNK, CM_GROUPS, CM_DIM)
    mixed = jnp.einsum('gts,bnsgd->bntgd', ws, vn) + bs.T[None, None, :, :, None]
    return u * mixed.reshape(B, T, CM_WIDTH)


def axial_angles(rows):
    row = jnp.repeat(jnp.arange(rows, dtype=jnp.float32), GRID_W)
    col = jnp.tile(jnp.arange(GRID_W, dtype=jnp.float32), rows)
    freqs = ROPE_THETA ** (-jnp.arange(0, ROPE_AXIS, 2, dtype=jnp.float32) / ROPE_AXIS)
    return row[:, None] * freqs, col[:, None] * freqs


def rotate(x, ang):
    ang = ang.reshape((ang.shape[0],) + (1,) * (x.ndim - 3) + (ang.shape[-1],))
    cos = jnp.cos(ang).astype(x.dtype)
    sin = jnp.sin(ang).astype(x.dtype)
    x1, x2 = jnp.split(x, 2, axis=-1)
    return jnp.concatenate([x1 * cos - x2 * sin, x2 * cos + x1 * sin], axis=-1)


def axial_rope(x, angles):
    ang_r, ang_c = angles
    return jnp.concatenate([rotate(x[..., :ROPE_AXIS], ang_r), rotate(x[..., ROPE_AXIS:], ang_c)], axis=-1)


def mla_kv(ckv_n, w_ukv):
    B, S, _ = ckv_n.shape
    kv = (ckv_n @ w_ukv).reshape(B, S, MLA_HEADS, MLA_NOPE + MLA_V)
    return kv[..., :MLA_NOPE], kv[..., MLA_NOPE:]


def mla_attention(q_nope, q_rope, k_nope, k_rope, v):
    B, Tq, H, _ = q_nope.shape
    nb = Tq // MLA_QBLOCK
    scale = (MLA_NOPE + MLA_ROPE) ** -0.5

    def blocks(a):
        return a.reshape((B, nb, MLA_QBLOCK) + a.shape[2:]).swapaxes(0, 1)

    def one(args):
        qn, qr = args
        s = jnp.einsum('bqhd,bkhd->bhqk', qn, k_nope) + jnp.einsum('bqhd,bkd->bhqk', qr, k_rope)
        p = jax.nn.softmax(s.astype(jnp.float32) * scale, axis=-1).astype(v.dtype)
        return jnp.einsum('bhqk,bkhd->bqhd', p, v)

    out = lax.map(one, (blocks(q_nope), blocks(q_rope)))
    return out.swapaxes(0, 1).reshape(B, Tq, H * MLA_V)


def peer_layer(h, w_q, subkeys, u_tab, v_tab):
    B, T, D = h.shape
    xb = h.reshape(-1, PEER_TOKBLOCK, D)
    half = PEER_DKEY // 2

    def one(xt):
        q = (xt @ w_q).reshape(PEER_TOKBLOCK, PEER_HEADS, 2, half)
        s1 = jnp.einsum('thd,kd->thk', q[:, :, 0], subkeys[0])
        s2 = jnp.einsum('thd,kd->thk', q[:, :, 1], subkeys[1])
        v1, i1 = lax.top_k(s1, PEER_TOPK)
        v2, i2 = lax.top_k(s2, PEER_TOPK)
        cand = (v1[..., :, None] + v2[..., None, :]).reshape(PEER_TOKBLOCK, PEER_HEADS, PEER_TOPK * PEER_TOPK)
        cidx = (i1[..., :, None] * PEER_NKEYS + i2[..., None, :]).reshape(PEER_TOKBLOCK, PEER_HEADS, PEER_TOPK * PEER_TOPK)
        sc, pos = lax.top_k(cand, PEER_TOPK)
        eidx = jnp.take_along_axis(cidx, pos, axis=-1)
        g = jax.nn.softmax(sc.astype(jnp.float32), axis=-1).astype(xt.dtype)
        act = jax.nn.gelu(jnp.einsum('thkd,td->thk', jnp.take(u_tab, eidx, axis=0), xt))
        return jnp.einsum('thk,thkd->td', g * act, jnp.take(v_tab, eidx, axis=0))

    return lax.map(one, xb).reshape(B, T, D)


def trunk_layer(x, cvec, p, ml_state, ctx_cache, angles):
    B, T, _ = x.shape
    mod = (jax.nn.silu(cvec) @ p['ada_w'] + p['ada_b'])[:, None, :]
    sh1, sc1, g1, sh2, sc2, g2 = jnp.split(mod, 6, axis=-1)
    h = rmsnorm(x, p['norm1_g']) * (1 + sc1) + sh1
    (mq, mk, mv, mo, i_f, f_f, i_b, f_b, cu, cv, cq, ckv, kr) = split_cols(h @ p['w_in'])

    def hd(a):
        return a.reshape(B, T, ML_HEADS, ML_DIM)

    def rev(a):
        return jnp.flip(a, axis=1)

    gb = p['mlstm_gate_b']
    C0, n0, m0 = ml_state
    qh, kh, vh = hd(mq), hd(mk), hd(mv)
    hf, Cf, nf, mf = mlstm_chunkwise(qh, kh, vh, i_f + gb[0:4], f_f + gb[4:8], C0[:, 0], n0[:, 0], m0[:, 0])
    hb, Cb, nb_, mb = mlstm_chunkwise(rev(qh), rev(kh), rev(vh), rev(i_b + gb[8:12]), rev(f_b + gb[12:16]), C0[:, 1], n0[:, 1], m0[:, 1])
    ml_out = jax.nn.sigmoid(mo) * (hf + rev(hb)).reshape(B, T, ML_WIDTH)
    new_ml = (jnp.stack([Cf, Cb], axis=1), jnp.stack([nf, nb_], axis=1), jnp.stack([mf, mb], axis=1))

    cm_out = chunk_spatial_gating(cu, cv, p['cm_norm_g'], p['cm_ws'], p['cm_b'])

    q = (rmsnorm(cq, p['mla_q_norm_g']) @ p['mla_w_uq']).reshape(B, T, MLA_HEADS, MLA_NOPE + MLA_ROPE)
    q_nope, q_rope = q[..., :MLA_NOPE], q[..., MLA_NOPE:]
    ckv_n = rmsnorm(ckv, p['mla_kv_norm_g'])
    k_nope, vals = mla_kv(ckv_n, p['mla_w_ukv'])
    k_rope = kr
    if angles is not None:
        q_rope = axial_rope(q_rope, angles)
        k_rope = axial_rope(kr, angles)
    if ctx_cache is not None:
        ctx_ckv, ctx_kr = ctx_cache
        ck_nope, c_vals = mla_kv(ctx_ckv, p['mla_w_ukv'])
        k_nope = jnp.concatenate([ck_nope, k_nope], axis=1)
        vals = jnp.concatenate([c_vals, vals], axis=1)
        k_rope = jnp.concatenate([ctx_kr, k_rope], axis=1)
    att = mla_attention(q_nope, q_rope, k_nope, k_rope, vals)

    mix = jnp.concatenate([ml_out, cm_out, att], axis=-1) @ p['w_out']
    x = x + g1 * mix
    h2 = rmsnorm(x, p['norm2_g']) * (1 + sc2) + sh2
    x = x + g2 * peer_layer(h2, p['peer_w_q'], p['peer_subkeys'], p['peer_u'], p['peer_v'])
    return x, (ckv_n, kr), new_ml


def setup_inputs(seed: int = 0) -> dict:
    key = jax.random.key(seed)
    ks = jax.random.split(key, 40)
    f32 = jnp.float32

    def nrm(k, shape, s):
        return jax.random.normal(k, shape, f32) * s

    def gain(k, shape):
        return 1.0 + 0.05 * jax.random.normal(k, shape, f32)

    gk = jax.random.split(ks[30], 4)
    mlstm_gate_b = jnp.concatenate([
        nrm(gk[0], (DEPTH, ML_HEADS), 0.1), 3.0 + nrm(gk[1], (DEPTH, ML_HEADS), 0.1),
        nrm(gk[2], (DEPTH, ML_HEADS), 0.1), 3.0 + nrm(gk[3], (DEPTH, ML_HEADS), 0.1)], axis=-1)
    return {
        'x_prompt': nrm(ks[0], (BATCH, SEQ, D_MODEL), 1.0),
        'x_sample': nrm(ks[1], (DEC_BATCH, DEC_SEQ, D_MODEL), 1.0),
        'c': nrm(ks[2], (DEC_BATCH, D_MODEL), 1.0),
        'cache_mla_ckv': nrm(ks[3], (DEC_BATCH, DEPTH, PAST_LEN, MLA_KV_RANK), 1.0),
        'cache_mla_krope': nrm(ks[4], (DEC_BATCH, DEPTH, PAST_LEN, MLA_ROPE), 1.0),
        'state_mlstm_C': nrm(ks[5], (DEC_BATCH, DEPTH, 2, ML_HEADS, ML_DIM, ML_DIM), 0.1),
        'state_mlstm_n': nrm(ks[6], (DEC_BATCH, DEPTH, 2, ML_HEADS, ML_DIM), 0.1),
        'state_mlstm_m': nrm(ks[7], (DEC_BATCH, DEPTH, 2, ML_HEADS), 0.5),
        'c_ctx': nrm(ks[8], (D_MODEL,), 1.0),
        'norm1_g': gain(ks[9], (DEPTH, D_MODEL)),
        'ada_w': nrm(ks[10], (DEPTH, D_MODEL, 6 * D_MODEL), 0.5 * D_MODEL ** -0.5),
        'ada_b': nrm(ks[11], (DEPTH, 6 * D_MODEL), 0.02),
        'w_in': nrm(ks[12], (DEPTH, D_MODEL, IN_COLS), D_MODEL ** -0.5),
        'mlstm_gate_b': mlstm_gate_b,
        'cm_norm_g': gain(ks[13], (DEPTH, CM_WIDTH)),
        'cm_ws': nrm(ks[14], (DEPTH, CM_GROUPS, CM_CHUNK, CM_CHUNK), CM_CHUNK ** -0.5),
        'cm_b': nrm(ks[15], (DEPTH, CM_GROUPS, CM_CHUNK), 0.02),
        'mla_q_norm_g': gain(ks[16], (DEPTH, MLA_Q_RANK)),
        'mla_w_uq': nrm(ks[17], (DEPTH, MLA_Q_RANK, MLA_HEADS * (MLA_NOPE + MLA_ROPE)), MLA_Q_RANK ** -0.5),
        'mla_kv_norm_g': gain(ks[18], (DEPTH, MLA_KV_RANK)),
        'mla_w_ukv': nrm(ks[19], (DEPTH, MLA_KV_RANK, MLA_HEADS * (MLA_NOPE + MLA_V)), MLA_KV_RANK ** -0.5),
        'w_out': nrm(ks[20], (DEPTH, MIX_WIDTH, D_MODEL), MIX_WIDTH ** -0.5),
        'norm2_g': gain(ks[21], (DEPTH, D_MODEL)),
        'peer_w_q': nrm(ks[22], (DEPTH, D_MODEL, PEER_HEADS * PEER_DKEY), D_MODEL ** -0.5),
        'peer_subkeys': nrm(ks[23], (DEPTH, 2, PEER_NKEYS, PEER_DKEY // 2), (PEER_DKEY // 2) ** -0.5),
        'peer_u': nrm(ks[24], (DEPTH, PEER_EXPERTS, D_MODEL), D_MODEL ** -0.5),
        'peer_v': nrm(ks[25], (DEPTH, PEER_EXPERTS, D_MODEL), PEER_HEADS ** -0.5),
        'final_g': gain(ks[26], (D_MODEL,)),
    }


def reference(x_prompt, x_sample, c, cache_mla_ckv, cache_mla_krope, state_mlstm_C, state_mlstm_n, state_mlstm_m,
              c_ctx, norm1_g, ada_w, ada_b, w_in, mlstm_gate_b, cm_norm_g, cm_ws, cm_b, mla_q_norm_g, mla_w_uq,
              mla_kv_norm_g, mla_w_ukv, w_out, norm2_g, peer_w_q, peer_subkeys, peer_u, peer_v, final_g):
    def layer_params(l):
        return {'norm1_g': norm1_g[l], 'ada_w': ada_w[l], 'ada_b': ada_b[l], 'w_in': w_in[l],
                'mlstm_gate_b': mlstm_gate_b[l], 'cm_norm_g': cm_norm_g[l], 'cm_ws': cm_ws[l], 'cm_b': cm_b[l],
                'mla_q_norm_g': mla_q_norm_g[l], 'mla_w_uq': mla_w_uq[l], 'mla_kv_norm_g': mla_kv_norm_g[l],
                'mla_w_ukv': mla_w_ukv[l], 'w_out': w_out[l], 'norm2_g': norm2_g[l], 'peer_w_q': peer_w_q[l],
                'peer_subkeys': peer_subkeys[l], 'peer_u': peer_u[l], 'peer_v': peer_v[l]}

    b0 = x_prompt.shape[0]
    zero_state = (jnp.zeros((b0, 2, ML_HEADS, ML_DIM, ML_DIM), jnp.float32),
                  jnp.zeros((b0, 2, ML_HEADS, ML_DIM), jnp.float32),
                  jnp.zeros((b0, 2, ML_HEADS), jnp.float32))
    xc = x_prompt
    ckvs, krs, Cs, ns, ms = [], [], [], [], []
    for l in range(DEPTH):
        xc, (ckv_l, kr_l), (C_l, n_l, m_l) = trunk_layer(xc, c_ctx[None, :], layer_params(l), zero_state, None, None)
        ckvs.append(ckv_l)
        krs.append(kr_l)
        Cs.append(C_l)
        ns.append(n_l)
        ms.append(m_l)
    y_prompt = rmsnorm(xc, final_g)

    rows = x_sample.shape[1] // GRID_W
    angles = axial_angles(rows)
    xs = x_sample
    for l in range(DEPTH):
        state_l = (state_mlstm_C[:, l], state_mlstm_n[:, l], state_mlstm_m[:, l])
        xs, _, _ = trunk_layer(xs, c, layer_params(l), state_l, (cache_mla_ckv[:, l], cache_mla_krope[:, l]), angles)
    y_sample = rmsnorm(xs, final_g)

    return (y_prompt, y_sample, jnp.stack(ckvs, axis=1), jnp.stack(krs, axis=1), jnp.stack(Cs, axis=1),
            jnp.stack(ns, axis=1), jnp.stack(ms, axis=1))
```

```python
import functools

import numpy as np
import jax
import jax.numpy as jnp
from jax import lax
from jax.experimental import pallas as pl
from jax.experimental.pallas import tpu as pltpu

F32 = jnp.float32
BF16 = jnp.bfloat16

D_MODEL = 1024
N_CTX_SEQ = 16
CTX_LEN = 256
N_SMP_SEQ = 4
SMP_LEN = 2048
PAST_LEN = 256
DEPTH = 2
GRID_W = 64
EPS = 1e-6
T_CTX = N_CTX_SEQ * CTX_LEN
T_SMP = N_SMP_SEQ * SMP_LEN
T_ALL = T_CTX + T_SMP
N_MOD_ROWS = 8

ML_HEADS = 4
ML_DIM = 64
ML_WIDTH = 256
CHUNK = 128
CM_GROUPS = 4
CM_WIDTH = 256
MLA_HEADS = 8
MLA_NOPE = 64
MLA_ROPE = 32
MLA_V = 64
MLA_RANK = 256
HEAD_PAD = 128
ROPE_AXIS = 16
ROPE_THETA = 10000.0
PEER_HEADS = 8
PEER_NKEYS = 128
PEER_EXPERTS = PEER_NKEYS * PEER_NKEYS
PEER_TOPK = 16

TM = 256
TN_TOPK = 256
TM_EXP = 512
TE_EXP = 1024
VMEM_LIMIT = 56 * 1024 * 1024

NEG_INF = float("-inf")


def _dot(a, b):
    return jnp.dot(a, b, preferred_element_type=F32)


def _dot_nt(a, b):
    return lax.dot_general(a, b, (((1,), (1,)), ((), ())), preferred_element_type=F32)


def _dot_tn(a, b):
    return lax.dot_general(a, b, (((0,), (0,)), ((), ())), preferred_element_type=F32)


def _split3(a):
    a1 = a.astype(BF16)
    r1 = a - a1.astype(F32)
    a2 = r1.astype(BF16)
    a3 = (r1 - a2.astype(F32)).astype(BF16)
    return a1, a2, a3


def _rms(x, g):
    return x * lax.rsqrt(jnp.mean(x * x, axis=-1, keepdims=True) + EPS) * g


def _sigmoid(x):
    return 1.0 / (1.0 + jnp.exp(-x))


def _log_sigmoid(x):
    return jnp.minimum(x, 0.0) - jnp.log(1.0 + jnp.exp(-jnp.abs(x)))


def _gelu_tanh(x):
    return 0.5 * x * (1.0 + jnp.tanh(0.7978845608028654 * (x + 0.044715 * (x * x * x))))


def _mod_row_of_block(i, rows_per_block):
    ctx_blocks = T_CTX // rows_per_block
    per_seq = SMP_LEN // rows_per_block
    return jnp.where(i < ctx_blocks, 0, 1 + (i - ctx_blocks) // per_seq)


def _ada_kernel(cv_ref, w_ref, b_ref, o_ref):
    cv = cv_ref[...]
    s = cv * _sigmoid(cv)
    w = w_ref[0]
    w1, w2, w3 = _split3(w)
    s1, s2, s3 = _split3(s)
    acc = _dot(s1, w1) + (_dot(s1, w2) + _dot(s2, w1)) + (_dot(s1, w3) + _dot(s2, w2) + _dot(s3, w1))
    o_ref[0] = acc + b_ref[0]


def _ada_call(cvecs, ada_w, ada_b):
    tn = 1024
    return pl.pallas_call(
        _ada_kernel,
        grid=(DEPTH, 6 * D_MODEL // tn),
        in_specs=[
            pl.BlockSpec((N_MOD_ROWS, D_MODEL), lambda l, j: (0, 0)),
            pl.BlockSpec((1, D_MODEL, tn), lambda l, j: (l, 0, j)),
            pl.BlockSpec((1, 1, tn), lambda l, j: (l, 0, j)),
        ],
        out_specs=pl.BlockSpec((1, N_MOD_ROWS, tn), lambda l, j: (l, 0, j)),
        out_shape=jax.ShapeDtypeStruct((DEPTH, N_MOD_ROWS, 6 * D_MODEL), F32),
        compiler_params=pltpu.CompilerParams(dimension_semantics=("parallel", "parallel")),
        name="ada_mod",
    )(cvecs, ada_w, ada_b.reshape(DEPTH, 1, 6 * D_MODEL))


def _proj_kernel(x_ref, mod_ref, g1_ref, wml_ref, wg_ref, wgt_ref, gbr_ref, gbc_ref, wcm_ref, wmla_ref, wkr_ref,
                 cmg_ref, ws_ref, cmb_ref, qg_ref, kvg_ref, wuqa_ref, wuqb_ref, wk_ref, wv_ref, cos_ref, sin_ref,
                 ml_ref, gates_ref, gatest_ref, cm_ref, ckvn_ref, kr_ref, q_ref, k_ref, v_ref):
    x = x_ref[...]
    mod = mod_ref[0]
    sh1 = mod[:, 0:D_MODEL]
    sc1 = mod[:, D_MODEL:2 * D_MODEL]
    h = _rms(x, g1_ref[...]) * (1.0 + sc1) + sh1
    hb = h.astype(BF16)

    ml_ref[...] = _dot(hb, wml_ref[...])
    gates_ref[...] = _dot(hb, wg_ref[...]) + gbr_ref[...]
    gatest_ref[...] = _dot_nt(wgt_ref[...], hb) + gbc_ref[...]

    cm = _dot(hb, wcm_ref[...])
    u = cm[:, 0:CM_WIDTH]
    vn = _rms(cm[:, CM_WIDTH:2 * CM_WIDTH], cmg_ref[...]).astype(BF16)
    lane_group = lax.broadcasted_iota(jnp.int32, (CHUNK, CM_WIDTH), 1) >> 6
    for c in range(TM // CHUNK):
        rows = slice(c * CHUNK, (c + 1) * CHUNK)
        vc = vn[rows]
        mixed = jnp.zeros((CHUNK, CM_WIDTH), F32)
        for g in range(CM_GROUPS):
            mixed = jnp.where(lane_group == g, _dot(ws_ref[g], vc), mixed)
        cm_ref[rows, :] = u[rows] * (mixed + cmb_ref[...])

    mla = _dot(hb, wmla_ref[...])
    qn = _rms(mla[:, 0:MLA_RANK], qg_ref[...]).astype(BF16)
    ckvn = _rms(mla[:, MLA_RANK:2 * MLA_RANK], kvg_ref[...])
    ckvn_ref[...] = ckvn
    kr_ref[...] = _dot(hb, wkr_ref[...])
    cos = cos_ref[...]
    sin = sin_ref[...]
    kr_rot = mla[:, 2 * MLA_RANK:2 * MLA_RANK + HEAD_PAD] * cos + mla[:, 2 * MLA_RANK + HEAD_PAD:] * sin
    qa = _dot(qn, wuqa_ref[...])
    qb = _dot(qn, wuqb_ref[...])
    ckvb = ckvn.astype(BF16)
    kp = _dot(ckvb, wk_ref[...])
    for hd in range(MLA_HEADS):
        cols = slice(hd * HEAD_PAD, (hd + 1) * HEAD_PAD)
        q_ref[hd] = (qa[:, cols] * cos + qb[:, cols] * sin).astype(BF16)
        k_ref[hd] = (kp[:, cols] + kr_rot).astype(BF16)
    v_ref[...] = _dot(ckvb, wv_ref[...]).astype(BF16)


def _proj_call(x, mod, lw, cos_t, sin_t):
    nblk = T_ALL // TM
    full = lambda shape: pl.BlockSpec(shape, lambda i: (0,) * len(shape))
    rows = lambda w: pl.BlockSpec((TM, w), lambda i: (i, 0))
    in_specs = [
        rows(D_MODEL),
        pl.BlockSpec((1, 1, 6 * D_MODEL), lambda i: (_mod_row_of_block(i, TM), 0, 0)),
        full((1, D_MODEL)),
        full((D_MODEL, 4 * ML_WIDTH)),
        full((D_MODEL, 16)),
        full((16, D_MODEL)),
        full((1, 16)),
        full((16, 1)),
        full((D_MODEL, 2 * CM_WIDTH)),
        full((D_MODEL, 2 * MLA_RANK + 2 * HEAD_PAD)),
        full((D_MODEL, MLA_ROPE)),
        full((1, CM_WIDTH)),
        full((CM_GROUPS, CHUNK, CHUNK)),
        full((CHUNK, CM_WIDTH)),
        full((1, MLA_RANK)),
        full((1, MLA_RANK)),
        full((MLA_RANK, MLA_HEADS * HEAD_PAD)),
        full((MLA_RANK, MLA_HEADS * HEAD_PAD)),
        full((MLA_RANK, MLA_HEADS * HEAD_PAD)),
        full((MLA_RANK, MLA_HEADS * MLA_V)),
        rows(HEAD_PAD),
        rows(HEAD_PAD),
    ]
    out_specs = [
        rows(4 * ML_WIDTH),
        rows(16),
        pl.BlockSpec((16, TM), lambda i: (0, i)),
        rows(CM_WIDTH),
        rows(MLA_RANK),
        rows(MLA_ROPE),
        pl.BlockSpec((MLA_HEADS, TM, HEAD_PAD), lambda i: (0, i, 0)),
        pl.BlockSpec((MLA_HEADS, TM, HEAD_PAD), lambda i: (0, i, 0)),
        rows(MLA_HEADS * MLA_V),
    ]
    out_shape = [
        jax.ShapeDtypeStruct((T_ALL, 4 * ML_WIDTH), F32),
        jax.ShapeDtypeStruct((T_ALL, 16), F32),
        jax.ShapeDtypeStruct((16, T_ALL), F32),
        jax.ShapeDtypeStruct((T_ALL, CM_WIDTH), F32),
        jax.ShapeDtypeStruct((T_ALL, MLA_RANK), F32),
        jax.ShapeDtypeStruct((T_ALL, MLA_ROPE), F32),
        jax.ShapeDtypeStruct((MLA_HEADS, T_ALL, HEAD_PAD), BF16),
        jax.ShapeDtypeStruct((MLA_HEADS, T_ALL, HEAD_PAD), BF16),
        jax.ShapeDtypeStruct((T_ALL, MLA_HEADS * MLA_V), BF16),
    ]
    return pl.pallas_call(
        _proj_kernel,
        grid=(nblk,),
        in_specs=in_specs,
        out_specs=out_specs,
        out_shape=out_shape,
        compiler_params=pltpu.CompilerParams(dimension_semantics=("parallel",), vmem_limit_bytes=VMEM_LIMIT),
        name="proj_in",
    )(x, mod, lw["g1"], lw["wml"], lw["wg"], lw["wgt"], lw["gb_row"], lw["gb_col"], lw["wcm"], lw["wmla"],
      lw["wkr"], lw["cmg"], lw["ws"], lw["cmb"], lw["qg"], lw["kvg"], lw["wuqa"], lw["wuqb"], lw["wk"], lw["wv"],
      cos_t, sin_t)


def _mlstm_schedule():
    seq, fwd, bwd, first = [], [], [], []
    base = 0
    for s in range(N_CTX_SEQ + N_SMP_SEQ):
        nc = (CTX_LEN if s < N_CTX_SEQ else SMP_LEN) // CHUNK
        for j in range(nc):
            seq.append(s)
            fwd.append(base + j)
            bwd.append(base + nc - 1 - j)
            first.append(1 if j == 0 else 0)
        base += nc
    as_i32 = lambda a: jnp.asarray(np.asarray(a, np.int32))
    return as_i32(seq), as_i32(fwd), as_i32(bwd), as_i32(first)


def _mlstm_direction(ml, g_col, g_row, direction, c_ref, n_ref, m_ref):
    L = CHUNK
    t_idx = lax.broadcasted_iota(jnp.int32, (L, L), 0)
    s_idx = lax.broadcasted_iota(jnp.int32, (L, L), 1)
    visible = (s_idx <= t_idx) if direction == 0 else (s_idx >= t_idx)
    tri = jnp.where(visible, 1.0, 0.0).astype(BF16)
    i0 = 8 * direction
    i_col = g_col[:, i0:i0 + ML_HEADS]
    i_row = g_row[i0:i0 + ML_HEADS, :]
    lf_col = _log_sigmoid(g_col[:, i0 + ML_HEADS:i0 + 2 * ML_HEADS])
    lf_row = _log_sigmoid(g_row[i0 + ML_HEADS:i0 + 2 * ML_HEADS, :])
    c1, c2, c3 = _split3(lf_col)
    b_cols = _dot(tri, c1) + _dot(tri, c2) + _dot(tri, c3)
    r1, r2, r3 = _split3(lf_row)
    b_rows = _dot_nt(r1, tri) + _dot_nt(r2, tri) + _dot_nt(r3, tri)
    b_end_all = jnp.sum(lf_col, axis=0, keepdims=True)
    outs = []
    for hd in range(ML_HEADS):
        r = direction * ML_HEADS + hd
        lanes = slice(hd * ML_DIM, (hd + 1) * ML_DIM)
        q = ml[:, lanes]
        k = ml[:, ML_WIDTH + hd * ML_DIM:ML_WIDTH + (hd + 1) * ML_DIM] * (ML_DIM ** -0.5)
        v = ml[:, 2 * ML_WIDTH + hd * ML_DIM:2 * ML_WIDTH + (hd + 1) * ML_DIM]
        qb = q.astype(BF16)
        vb = v.astype(BF16)
        C = c_ref[0, r]
        n = n_ref[0, r:r + 1, :]
        m = m_ref[0, r:r + 1, 0:1]
        b_col = b_cols[:, hd:hd + 1]
        b_row = b_rows[hd:hd + 1, :]
        d = jnp.where(visible, b_col - b_row + i_row[hd:hd + 1, :], NEG_INF)
        m_inter = b_col + m
        m_t = jnp.maximum(m_inter, jnp.max(d, axis=-1, keepdims=True))
        w_intra = jnp.exp(d - m_t)
        w_inter = jnp.exp(m_inter - m_t)
        sw = _dot_nt(qb, k.astype(BF16)) * w_intra
        num = _dot(sw.astype(BF16), vb) + w_inter * _dot(qb, C.astype(BF16))
        den = jnp.sum(sw, axis=-1, keepdims=True) + w_inter * jnp.sum(q * n, axis=-1, keepdims=True)
        outs.append(num / jnp.maximum(jnp.abs(den), jnp.exp(-m_t)))
        b_end = b_end_all[:, hd:hd + 1]
        log_w = b_end - b_col + i_col[:, hd:hd + 1]
        m_new = jnp.maximum(b_end + m, jnp.max(log_w, axis=0, keepdims=True))
        w_k = jnp.exp(log_w - m_new)
        decay = jnp.exp(b_end + m - m_new)
        kw = w_k * k
        c_ref[0, r] = decay * C + _dot_tn(kw.astype(BF16), vb)
        n_ref[0, r:r + 1, :] = decay * n + jnp.sum(kw, axis=0, keepdims=True)
        m_ref[0, r:r + 1, :] = jnp.broadcast_to(m_new, (1, 128))
    return jnp.concatenate(outs, axis=-1)


def _mlstm_kernel(seq_ref, fwd_ref, bwd_ref, first_ref, mlf_ref, mlb_ref, gcf_ref, gcb_ref, grf_ref, grb_ref,
                  c0_ref, n0_ref, m0_ref, hf_ref, hb_ref, c_ref, n_ref, m_ref):
    step = pl.program_id(0)

    @pl.when(first_ref[step] == 1)
    def _():
        c_ref[...] = c0_ref[...]
        n_ref[...] = n0_ref[...]
        m_ref[...] = m0_ref[...]

    hf_ref[...] = _mlstm_direction(mlf_ref[...], gcf_ref[...], grf_ref[...], 0, c_ref, n_ref, m_ref)
    hb_ref[...] = _mlstm_direction(mlb_ref[...], gcb_ref[...], grb_ref[...], 1, c_ref, n_ref, m_ref)


def _mlstm_call(ml, gates, gates_t, c0, n0, m0):
    seq, fwd, bwd, first = _mlstm_schedule()
    nseq = N_CTX_SEQ + N_SMP_SEQ
    nsteps = int(seq.shape[0])
    nstate = 2 * ML_HEADS
    f_rows = lambda w: pl.BlockSpec((CHUNK, w), lambda i, sq, fw, bw, fs: (fw[i], 0))
    b_rows = lambda w: pl.BlockSpec((CHUNK, w), lambda i, sq, fw, bw, fs: (bw[i], 0))
    c_spec = pl.BlockSpec((1, nstate, ML_DIM, ML_DIM), lambda i, sq, fw, bw, fs: (sq[i], 0, 0, 0))
    n_spec = pl.BlockSpec((1, nstate, ML_DIM), lambda i, sq, fw, bw, fs: (sq[i], 0, 0))
    m_spec = pl.BlockSpec((1, nstate, 128), lambda i, sq, fw, bw, fs: (sq[i], 0, 0))
    grid_spec = pltpu.PrefetchScalarGridSpec(
        num_scalar_prefetch=4,
        grid=(nsteps,),
        in_specs=[
            f_rows(4 * ML_WIDTH), b_rows(4 * ML_WIDTH), f_rows(16), b_rows(16),
            pl.BlockSpec((16, CHUNK), lambda i, sq, fw, bw, fs: (0, fw[i])),
            pl.BlockSpec((16, CHUNK), lambda i, sq, fw, bw, fs: (0, bw[i])),
            c_spec, n_spec, m_spec,
        ],
        out_specs=[f_rows(ML_WIDTH), b_rows(ML_WIDTH), c_spec, n_spec, m_spec],
    )
    return pl.pallas_call(
        _mlstm_kernel,
        grid_spec=grid_spec,
        out_shape=[
            jax.ShapeDtypeStruct((T_ALL, ML_WIDTH), F32),
            jax.ShapeDtypeStruct((T_ALL, ML_WIDTH), F32),
            jax.ShapeDtypeStruct((nseq, nstate, ML_DIM, ML_DIM), F32),
            jax.ShapeDtypeStruct((nseq, nstate, ML_DIM), F32),
            jax.ShapeDtypeStruct((nseq, nstate, 128), F32),
        ],
        compiler_params=pltpu.CompilerParams(dimension_semantics=("arbitrary",), vmem_limit_bytes=VMEM_LIMIT),
        name="mlstm",
    )(seq, fwd, bwd, first, ml, ml, gates, gates, gates_t, gates_t, c0, n0, m0)


def _cache_kv_kernel(ckv_ref, kr_ref, wk_ref, wv_ref, place_ref, k_ref, v_ref):
    ckvb = ckv_ref[...].astype(BF16)
    kp = _dot(ckvb, wk_ref[...])
    kr128 = _dot(kr_ref[...].astype(BF16), place_ref[...])
    for hd in range(MLA_HEADS):
        k_ref[hd] = (kp[:, hd * HEAD_PAD:(hd + 1) * HEAD_PAD] + kr128).astype(BF16)
    v_ref[...] = _dot(ckvb, wv_ref[...]).astype(BF16)


def _cache_kv_call(ckv, kr, wk, wv, place):
    n = N_SMP_SEQ * PAST_LEN
    tb = PAST_LEN
    return pl.pallas_call(
        _cache_kv_kernel,
        grid=(n // tb,),
        in_specs=[
            pl.BlockSpec((tb, MLA_RANK), lambda i: (i, 0)),
            pl.BlockSpec((tb, MLA_ROPE), lambda i: (i, 0)),
            pl.BlockSpec((MLA_RANK, MLA_HEADS * HEAD_PAD), lambda i: (0, 0)),
            pl.BlockSpec((MLA_RANK, MLA_HEADS * MLA_V), lambda i: (0, 0)),
            pl.BlockSpec((MLA_ROPE, HEAD_PAD), lambda i: (0, 0)),
        ],
        out_specs=[
            pl.BlockSpec((MLA_HEADS, tb, HEAD_PAD), lambda i: (0, i, 0)),
            pl.BlockSpec((tb, MLA_HEADS * MLA_V), lambda i: (i, 0)),
        ],
        out_shape=[
            jax.ShapeDtypeStruct((MLA_HEADS, n, HEAD_PAD), BF16),
            jax.ShapeDtypeStruct((n, MLA_HEADS * MLA_V), BF16),
        ],
        compiler_params=pltpu.CompilerParams(dimension_semantics=("parallel",)),
        name="cache_kv",
    )(ckv, kr, wk, wv, place)


def _attn_kernel(*refs, has_cache):
    if has_cache:
        q_ref, kn_ref, vn_ref, kc_ref, vc_ref, _, o_ref = refs
    else:
        q_ref, kn_ref, vn_ref, o_ref = refs
    scale = (MLA_NOPE + MLA_ROPE) ** -0.5
    outs = []
    for hd in range(MLA_HEADS):
        lanes = slice(hd * MLA_V, (hd + 1) * MLA_V)
        q = q_ref[hd]
        s_n = _dot_nt(q, kn_ref[hd])
        m = jnp.max(s_n, axis=-1, keepdims=True)
        if has_cache:
            s_c = _dot_nt(q, kc_ref[hd])
            m = jnp.maximum(m, jnp.max(s_c, axis=-1, keepdims=True))
        p_n = jnp.exp((s_n - m) * scale)
        l = jnp.sum(p_n, axis=-1, keepdims=True)
        o = _dot(p_n.astype(BF16), vn_ref[:, lanes])
        if has_cache:
            p_c = jnp.exp((s_c - m) * scale)
            l = l + jnp.sum(p_c, axis=-1, keepdims=True)
            o = o + _dot(p_c.astype(BF16), vc_ref[:, lanes])
        outs.append(o / l)
    o_ref[...] = jnp.concatenate(outs, axis=-1)


def _attn_ctx_call(q, k, v):
    tq = CTX_LEN
    return pl.pallas_call(
        functools.partial(_attn_kernel, has_cache=False),
        grid=(N_CTX_SEQ,),
        in_specs=[
            pl.BlockSpec((MLA_HEADS, tq, HEAD_PAD), lambda s: (0, s, 0)),
            pl.BlockSpec((MLA_HEADS, tq, HEAD_PAD), lambda s: (0, s, 0)),
            pl.BlockSpec((tq, MLA_HEADS * MLA_V), lambda s: (s, 0)),
        ],
        out_specs=pl.BlockSpec((tq, MLA_HEADS * MLA_V), lambda s: (s, 0)),
        out_shape=jax.ShapeDtypeStruct((T_ALL, MLA_HEADS * MLA_V), F32),
        compiler_params=pltpu.CompilerParams(dimension_semantics=("parallel",), vmem_limit_bytes=VMEM_LIMIT),
        name="attn_ctx",
    )(q, k, v)


def _attn_smp_call(q, k, v, kc, vc, att_prev):
    tq = 256
    qb_per_seq = SMP_LEN // tq
    ctx_qb = T_CTX // tq
    ctx_kb = T_CTX // SMP_LEN
    return pl.pallas_call(
        functools.partial(_attn_kernel, has_cache=True),
        grid=(N_SMP_SEQ, qb_per_seq),
        in_specs=[
            pl.BlockSpec((MLA_HEADS, tq, HEAD_PAD), lambda b, i: (0, ctx_qb + b * qb_per_seq + i, 0)),
            pl.BlockSpec((MLA_HEADS, SMP_LEN, HEAD_PAD), lambda b, i: (0, ctx_kb + b, 0)),
            pl.BlockSpec((SMP_LEN, MLA_HEADS * MLA_V), lambda b, i: (ctx_kb + b, 0)),
            pl.BlockSpec((MLA_HEADS, PAST_LEN, HEAD_PAD), lambda b, i: (0, b, 0)),
            pl.BlockSpec((PAST_LEN, MLA_HEADS * MLA_V), lambda b, i: (b, 0)),
            pl.BlockSpec(memory_space=pl.ANY),
        ],
        out_specs=pl.BlockSpec((tq, MLA_HEADS * MLA_V), lambda b, i: (ctx_qb + b * qb_per_seq + i, 0)),
        out_shape=jax.ShapeDtypeStruct((T_ALL, MLA_HEADS * MLA_V), F32),
        input_output_aliases={5: 0},
        compiler_params=pltpu.CompilerParams(dimension_semantics=("parallel", "parallel"),
                                             vmem_limit_bytes=VMEM_LIMIT),
        name="attn_smp",
    )(q, k, v, kc, vc, att_prev)


def _out_kernel(x_ref, mod_ref, ml_ref, hf_ref, hb_ref, cm_ref, att_ref, wo_ref, g2_ref, wq_ref, sk_ref,
                x1_ref, h2_ref, st_ref):
    mod = mod_ref[0]
    g1 = mod[:, 2 * D_MODEL:3 * D_MODEL]
    sh2 = mod[:, 3 * D_MODEL:4 * D_MODEL]
    sc2 = mod[:, 4 * D_MODEL:5 * D_MODEL]
    mlo = _sigmoid(ml_ref[:, 3 * ML_WIDTH:4 * ML_WIDTH]) * (hf_ref[...] + hb_ref[...])
    mix = (_dot(mlo.astype(BF16), wo_ref[0:ML_WIDTH, :])
           + _dot(cm_ref[...].astype(BF16), wo_ref[ML_WIDTH:ML_WIDTH + CM_WIDTH, :])
           + _dot(att_ref[...].astype(BF16), wo_ref[ML_WIDTH + CM_WIDTH:, :]))
    x1 = x_ref[...] + g1 * mix
    x1_ref[...] = x1
    h2 = (_rms(x1, g2_ref[...]) * (1.0 + sc2) + sh2).astype(BF16)
    h2_ref[...] = h2
    qp = _dot(h2, wq_ref[...]).astype(BF16)
    for hh in range(2 * PEER_HEADS):
        st_ref[hh] = _dot_nt(sk_ref[hh % 2], qp[:, hh * 128:(hh + 1) * 128])


def _out_call(x, mod, ml, hf, hb, cm, att, lw):
    nblk = T_ALL // TM
    rows = lambda w: pl.BlockSpec((TM, w), lambda i: (i, 0))
    full = lambda shape: pl.BlockSpec(shape, lambda i: (0,) * len(shape))
    return pl.pallas_call(
        _out_kernel,
        grid=(nblk,),
        in_specs=[
            rows(D_MODEL),
            pl.BlockSpec((1, 1, 6 * D_MODEL), lambda i: (_mod_row_of_block(i, TM), 0, 0)),
            rows(4 * ML_WIDTH), rows(ML_WIDTH), rows(ML_WIDTH), rows(CM_WIDTH), rows(MLA_HEADS * MLA_V),
            full((D_MODEL, D_MODEL)), full((1, D_MODEL)), full((D_MODEL, 2 * PEER_HEADS * 128)),
            full((2, PEER_NKEYS, 128)),
        ],
        out_specs=[rows(D_MODEL), rows(D_MODEL), pl.BlockSpec((2 * PEER_HEADS, PEER_NKEYS, TM), lambda i: (0, 0, i))],
        out_shape=[
            jax.ShapeDtypeStruct((T_ALL, D_MODEL), F32),
            jax.ShapeDtypeStruct((T_ALL, D_MODEL), BF16),
            jax.ShapeDtypeStruct((2 * PEER_HEADS, PEER_NKEYS, T_ALL), F32),
        ],
        compiler_params=pltpu.CompilerParams(dimension_semantics=("parallel",), vmem_limit_bytes=VMEM_LIMIT),
        name="proj_out",
    )(x, mod, ml, hf, hb, cm, att, lw["wo"], lw["g2"], lw["wq"], lw["sk"])


def _top16_rows(s):
    vals = []
    for _ in range(PEER_TOPK):
        m = jnp.max(s, axis=0, keepdims=True)
        vals.append(m)
        s = jnp.where(s == m, NEG_INF, s)
    return vals


def _topk_kernel(st_ref, e1_ref, e2_ref, tau_ref):
    n = st_ref.shape[-1]
    row16 = lax.broadcasted_iota(jnp.int32, (PEER_TOPK, n), 0)

    def head(hd, carry):
        s1 = st_ref[2 * hd]
        s2 = st_ref[2 * hd + 1]
        v1 = _top16_rows(s1)
        v2 = _top16_rows(s2)
        v2arr = jnp.zeros((PEER_TOPK, n), F32)
        for qi in range(PEER_TOPK):
            v2arr = jnp.where(row16 == qi, v2[qi], v2arr)
        pieces = [v1[0] + v2arr] + [v1[p] + v2arr[0:8] for p in range(1, PEER_TOPK)]
        c = _top16_rows(jnp.concatenate(pieces, axis=0))
        z = jnp.zeros((1, n), F32)
        for ck in c:
            z = z + jnp.exp(ck - c[0])
        e1_ref[hd] = jnp.exp(s1 - v1[0]) / z
        e2_ref[hd] = jnp.exp(s2 - v2[0])
        tau_ref[hd] = c[PEER_TOPK - 1]
        return carry

    lax.fori_loop(0, PEER_HEADS, head, 0)


def _topk_call(st):
    tn = TN_TOPK
    spec = pl.BlockSpec((PEER_HEADS, PEER_NKEYS, tn), lambda i: (0, 0, i))
    return pl.pallas_call(
        _topk_kernel,
        grid=(T_ALL // tn,),
        in_specs=[pl.BlockSpec((2 * PEER_HEADS, PEER_NKEYS, tn), lambda i: (0, 0, i))],
        out_specs=[spec, spec, pl.BlockSpec((PEER_HEADS, 1, tn), lambda i: (0, 0, i))],
        out_shape=[
            jax.ShapeDtypeStruct((PEER_HEADS, PEER_NKEYS, T_ALL), F32),
            jax.ShapeDtypeStruct((PEER_HEADS, PEER_NKEYS, T_ALL), F32),
            jax.ShapeDtypeStruct((PEER_HEADS, 1, T_ALL), F32),
        ],
        compiler_params=pltpu.CompilerParams(dimension_semantics=("parallel",), vmem_limit_bytes=VMEM_LIMIT),
        name="peer_topk",
    )(st)


def _expert_kernel(h2_ref, u_ref, vt_ref, st_ref, e1_ref, e2_ref, tau_ref, x1_ref, mod_ref, fg_ref,
                   o_ref, acc_ref, g_ref, a_ref, *, final_norm):
    j = pl.program_id(1)
    n_tiles = pl.num_programs(1)
    tm = h2_ref.shape[0]
    a_per_tile = TE_EXP // PEER_NKEYS

    @pl.when(j == 0)
    def _():
        acc_ref[...] = jnp.zeros_like(acc_ref)

    g_ref[...] = _gelu_tanh(_dot_nt(u_ref[...], h2_ref[...]))

    def one_a(ai, carry):
        a = j * a_per_tile + ai
        gate = jnp.zeros((PEER_NKEYS, tm), F32)
        for hd in range(PEER_HEADS):
            s1_row = st_ref[2 * hd, pl.ds(a, 1), :]
            e1_row = e1_ref[hd, pl.ds(a, 1), :]
            hit = (s1_row + st_ref[2 * hd + 1]) >= tau_ref[hd]
            gate = gate + jnp.where(hit, e1_row * e2_ref[hd], 0.0)
        r0 = pl.multiple_of(ai * PEER_NKEYS, PEER_NKEYS)
        a_ref[pl.ds(r0, PEER_NKEYS), :] = (gate * g_ref[pl.ds(r0, PEER_NKEYS), :]).astype(BF16)
        return carry

    lax.fori_loop(0, a_per_tile, one_a, 0)
    acc_ref[...] += _dot(vt_ref[...], a_ref[...])

    @pl.when(j == n_tiles - 1)
    def _():
        g2 = mod_ref[0][:, 5 * D_MODEL:6 * D_MODEL]
        y = x1_ref[...] + g2 * acc_ref[...].T
        if final_norm:
            y = _rms(y, fg_ref[...])
        o_ref[...] = y


def _expert_call(h2, u_bf, vt_bf, st, e1, e2, tau, x1, mod, final_g, final_norm):
    tm, te = TM_EXP, TE_EXP
    stat = lambda n: pl.BlockSpec((n, PEER_NKEYS, tm), lambda i, j: (0, 0, i))
    return pl.pallas_call(
        functools.partial(_expert_kernel, final_norm=final_norm),
        grid=(T_ALL // tm, PEER_EXPERTS // te),
        in_specs=[
            pl.BlockSpec((tm, D_MODEL), lambda i, j: (i, 0)),
            pl.BlockSpec((te, D_MODEL), lambda i, j: (j, 0)),
            pl.BlockSpec((D_MODEL, te), lambda i, j: (0, j)),
            stat(2 * PEER_HEADS), stat(PEER_HEADS), stat(PEER_HEADS),
            pl.BlockSpec((PEER_HEADS, 1, tm), lambda i, j: (0, 0, i)),
            pl.BlockSpec((tm, D_MODEL), lambda i, j: (i, 0)),
            pl.BlockSpec((1, 1, 6 * D_MODEL), lambda i, j: (_mod_row_of_block(i, tm), 0, 0)),
            pl.BlockSpec((1, D_MODEL), lambda i, j: (0, 0)),
        ],
        out_specs=pl.BlockSpec((tm, D_MODEL), lambda i, j: (i, 0)),
        out_shape=jax.ShapeDtypeStruct((T_ALL, D_MODEL), F32),
        scratch_shapes=[
            pltpu.VMEM((D_MODEL, tm), F32),
            pltpu.VMEM((te, tm), F32),
            pltpu.VMEM((te, tm), BF16),
        ],
        compiler_params=pltpu.CompilerParams(dimension_semantics=("parallel", "arbitrary"),
                                             vmem_limit_bytes=VMEM_LIMIT),
        name="peer_experts",
    )(h2, u_bf, vt_bf, st, e1, e2, tau, x1, mod, final_g)


def _rope_swap_cols(w):
    return jnp.concatenate([-w[:, 8:16], w[:, 0:8], -w[:, 24:32], w[:, 16:24]], axis=1)


def _pad_heads(parts, n_heads):
    k = next(p[0].shape[0] for p in parts if p[0] is not None)
    cols = []
    for hd in range(n_heads):
        for arr, w in parts:
            cols.append(jnp.zeros((k, w), F32) if arr is None else arr[:, hd * w:(hd + 1) * w])
    return jnp.concatenate(cols, axis=1)


def _rope_tables():
    pos = np.arange(SMP_LEN)
    freqs = ROPE_THETA ** (-np.arange(0, ROPE_AXIS, 2, dtype=np.float32) / ROPE_AXIS)
    ang_r = (pos // GRID_W).astype(np.float32)[:, None] * freqs
    ang_c = (pos % GRID_W).astype(np.float32)[:, None] * freqs
    ang = jnp.asarray(np.concatenate([ang_r, ang_r, ang_c, ang_c], axis=1).astype(np.float32))
    cos32 = jnp.cos(ang)
    sin32 = jnp.sin(ang)
    ones = jnp.ones((SMP_LEN, MLA_NOPE), F32)
    cos_s = jnp.concatenate([ones, cos32, ones[:, :HEAD_PAD - MLA_NOPE - MLA_ROPE]], axis=1)
    sin_s = jnp.concatenate([0 * ones, sin32, 0 * ones[:, :HEAD_PAD - MLA_NOPE - MLA_ROPE]], axis=1)
    cos_t = jnp.concatenate([jnp.ones((T_CTX, HEAD_PAD), F32)] + [cos_s] * N_SMP_SEQ, axis=0)
    sin_t = jnp.concatenate([jnp.zeros((T_CTX, HEAD_PAD), F32)] + [sin_s] * N_SMP_SEQ, axis=0)
    return cos_t, sin_t


def _layer_weights(l, norm1_g, w_in, mlstm_gate_b, cm_norm_g, cm_ws, cm_b, mla_q_norm_g, mla_w_uq, mla_kv_norm_g,
                   mla_w_ukv, w_out, norm2_g, peer_w_q, peer_subkeys, peer_u, peer_v):
    w = w_in[l]
    o_g = 4 * ML_WIDTH
    o_cm = o_g + 16
    o_cq = o_cm + 2 * CM_WIDTH
    o_ckv = o_cq + MLA_RANK
    o_kr = o_ckv + MLA_RANK
    w_kr = w[:, o_kr:o_kr + MLA_ROPE]
    zeros_l = jnp.zeros((D_MODEL, MLA_NOPE), F32)
    zeros_r = jnp.zeros((D_MODEL, HEAD_PAD - MLA_NOPE - MLA_ROPE), F32)
    kr128 = jnp.concatenate([zeros_l, w_kr, zeros_r], axis=1)
    krsw128 = jnp.concatenate([zeros_l, _rope_swap_cols(w_kr), zeros_r], axis=1)
    uq = mla_w_uq[l].reshape(MLA_RANK, MLA_HEADS, MLA_NOPE + MLA_ROPE)
    uq_nope = uq[:, :, :MLA_NOPE].reshape(MLA_RANK, -1)
    uq_rope = uq[:, :, MLA_NOPE:].reshape(MLA_RANK, -1)
    uq_rope_sw = jnp.concatenate(
        [_rope_swap_cols(uq_rope[:, hd * MLA_ROPE:(hd + 1) * MLA_ROPE]) for hd in range(MLA_HEADS)], axis=1)
    pad_w = HEAD_PAD - MLA_NOPE - MLA_ROPE
    ukv = mla_w_ukv[l].reshape(MLA_RANK, MLA_HEADS, MLA_NOPE + MLA_V)
    uk = ukv[:, :, :MLA_NOPE].reshape(MLA_RANK, -1)
    uv = ukv[:, :, MLA_NOPE:].reshape(MLA_RANK, -1)
    gb = mlstm_gate_b[l]
    return {
        "g1": norm1_g[l].reshape(1, D_MODEL),
        "wml": w[:, 0:o_g].astype(BF16),
        "wg": w[:, o_g:o_cm].astype(BF16),
        "wgt": w[:, o_g:o_cm].T.astype(BF16),
        "gb_row": gb.reshape(1, 16),
        "gb_col": gb.reshape(16, 1),
        "wcm": w[:, o_cm:o_cq].astype(BF16),
        "wmla": jnp.concatenate([w[:, o_cq:o_kr], kr128, krsw128], axis=1).astype(BF16),
        "wkr": w_kr.astype(BF16),
        "cmg": cm_norm_g[l].reshape(1, CM_WIDTH),
        "ws": cm_ws[l].astype(BF16),
        "cmb": jnp.repeat(cm_b[l].T, CM_WIDTH // CM_GROUPS, axis=1),
        "qg": mla_q_norm_g[l].reshape(1, MLA_RANK),
        "kvg": mla_kv_norm_g[l].reshape(1, MLA_RANK),
        "wuqa": _pad_heads([(uq_nope, MLA_NOPE), (uq_rope, MLA_ROPE), (None, pad_w)], MLA_HEADS).astype(BF16),
        "wuqb": _pad_heads([(None, MLA_NOPE), (uq_rope_sw, MLA_ROPE), (None, pad_w)], MLA_HEADS).astype(BF16),
        "wk": _pad_heads([(uk, MLA_NOPE), (None, HEAD_PAD - MLA_NOPE)], MLA_HEADS).astype(BF16),
        "wv": uv.astype(BF16),
        "wo": w_out[l].astype(BF16),
        "g2": norm2_g[l].reshape(1, D_MODEL),
        "wq": peer_w_q[l].astype(BF16),
        "sk": peer_subkeys[l].astype(BF16),
        "u": peer_u[l].astype(BF16),
        "vt": peer_v[l].T.astype(BF16),
    }


def kernel(x_prompt, x_sample, c, cache_mla_ckv, cache_mla_krope, state_mlstm_C, state_mlstm_n, state_mlstm_m, c_ctx, norm1_g, ada_w, ada_b, w_in, mlstm_gate_b, cm_norm_g, cm_ws, cm_b, mla_q_norm_g, mla_w_uq, mla_kv_norm_g, mla_w_ukv, w_out, norm2_g, peer_w_q, peer_subkeys, peer_u, peer_v, final_g):
    nstate = 2 * ML_HEADS
    x = jnp.concatenate([x_prompt.reshape(T_CTX, D_MODEL), x_sample.reshape(T_SMP, D_MODEL)], axis=0)
    cvecs = jnp.concatenate([c_ctx[None, :], c, jnp.zeros((N_MOD_ROWS - 1 - N_SMP_SEQ, D_MODEL), F32)], axis=0)
    mod_all = _ada_call(cvecs, ada_w, ada_b)
    cos_t, sin_t = _rope_tables()
    place = jnp.concatenate([jnp.zeros((MLA_ROPE, MLA_NOPE), F32), jnp.eye(MLA_ROPE, dtype=F32),
                             jnp.zeros((MLA_ROPE, HEAD_PAD - MLA_NOPE - MLA_ROPE), F32)], axis=1).astype(BF16)
    final_g2 = final_g.reshape(1, D_MODEL)

    ckvs, krs, Cs, ns, ms = [], [], [], [], []
    for l in range(DEPTH):
        lw = _layer_weights(l, norm1_g, w_in, mlstm_gate_b, cm_norm_g, cm_ws, cm_b, mla_q_norm_g, mla_w_uq,
                            mla_kv_norm_g, mla_w_ukv, w_out, norm2_g, peer_w_q, peer_subkeys, peer_u, peer_v)
        mod = mod_all[l].reshape(N_MOD_ROWS, 1, 6 * D_MODEL)
        ml, gates, gates_t, cm, ckvn, kr, q, k, v = _proj_call(x, mod, lw, cos_t, sin_t)

        c0 = jnp.concatenate([jnp.zeros((N_CTX_SEQ, nstate, ML_DIM, ML_DIM), F32),
                              state_mlstm_C[:, l].reshape(N_SMP_SEQ, nstate, ML_DIM, ML_DIM)], axis=0)
        n0 = jnp.concatenate([jnp.zeros((N_CTX_SEQ, nstate, ML_DIM), F32),
                              state_mlstm_n[:, l].reshape(N_SMP_SEQ, nstate, ML_DIM)], axis=0)
        m0 = jnp.concatenate([jnp.zeros((N_CTX_SEQ, nstate), F32),
                              state_mlstm_m[:, l].reshape(N_SMP_SEQ, nstate)], axis=0)
        m0 = jnp.broadcast_to(m0[:, :, None], (N_CTX_SEQ + N_SMP_SEQ, nstate, 128))
        hf, hb, c_fin, n_fin, m_fin = _mlstm_call(ml, gates, gates_t, c0, n0, m0)

        kc, vc = _cache_kv_call(cache_mla_ckv[:, l].reshape(N_SMP_SEQ * PAST_LEN, MLA_RANK),
                                cache_mla_krope[:, l].reshape(N_SMP_SEQ * PAST_LEN, MLA_ROPE),
                                lw["wk"], lw["wv"], place)
        att = _attn_ctx_call(q, k, v)
        att = _attn_smp_call(q, k, v, kc, vc, att)

        x1, h2, st = _out_call(x, mod, ml, hf, hb, cm, att, lw)
        e1, e2, tau = _topk_call(st)
        x = _expert_call(h2, lw["u"], lw["vt"], st, e1, e2, tau, x1, mod, final_g2, final_norm=(l == DEPTH - 1))

        ckvs.append(ckvn[:T_CTX].reshape(N_CTX_SEQ, CTX_LEN, MLA_RANK))
        krs.append(kr[:T_CTX].reshape(N_CTX_SEQ, CTX_LEN, MLA_ROPE))
        Cs.append(c_fin[:N_CTX_SEQ].reshape(N_CTX_SEQ, 2, ML_HEADS, ML_DIM, ML_DIM))
        ns.append(n_fin[:N_CTX_SEQ].reshape(N_CTX_SEQ, 2, ML_HEADS, ML_DIM))
        ms.append(m_fin[:N_CTX_SEQ, :, 0].reshape(N_CTX_SEQ, 2, ML_HEADS))

    y_prompt = x[:T_CTX].reshape(N_CTX_SEQ, CTX_LEN, D_MODEL)
    y_sample = x[T_CTX:].reshape(N_SMP_SEQ, SMP_LEN, D_MODEL)
    return (y_prompt, y_sample, jnp.stack(ckvs, axis=1), jnp.stack(krs, axis=1), jnp.stack(Cs, axis=1),
            jnp.stack(ns, axis=1), jnp.stack(ms, axis=1))
```

```python
import functools

import numpy as np
import jax
import jax.numpy as jnp
from jax import lax
from jax.experimental import pallas as pl
from jax.experimental.pallas import tpu as pltpu

F32 = jnp.float32
BF16 = jnp.bfloat16

D_MODEL = 1024
N_CTX_SEQ = 16
CTX_LEN = 256
N_SMP_SEQ = 4
SMP_LEN = 2048
PAST_LEN = 256
DEPTH = 2
GRID_W = 64
EPS = 1e-6
T_CTX = N_CTX_SEQ * CTX_LEN
T_SMP = N_SMP_SEQ * SMP_LEN
T_ALL = T_CTX + T_SMP
N_MOD_ROWS = 8

ML_HEADS = 4
ML_DIM = 64
ML_WIDTH = 256
CHUNK = 128
CM_GROUPS = 4
CM_WIDTH = 256
MLA_HEADS = 8
MLA_NOPE = 64
MLA_ROPE = 32
MLA_V = 64
MLA_RANK = 256
HEAD_PAD = 128
ROPE_AXIS = 16
ROPE_THETA = 10000.0
PEER_HEADS = 8
PEER_NKEYS = 128
PEER_EXPERTS = PEER_NKEYS * PEER_NKEYS
PEER_TOPK = 16

TM = 256
TN_TOPK = 256
TM_EXP = 512
TE_EXP = 2048
EXP_SUB = 512
VMEM_LIMIT = 56 * 1024 * 1024

NEG_INF = float("-inf")


def _dot(a, b):
    return jnp.dot(a, b, preferred_element_type=F32)


def _dot_nt(a, b):
    return lax.dot_general(a, b, (((1,), (1,)), ((), ())), preferred_element_type=F32)


def _dot_tn(a, b):
    return lax.dot_general(a, b, (((0,), (0,)), ((), ())), preferred_element_type=F32)


def _split3(a):
    a1 = a.astype(BF16)
    r1 = a - a1.astype(F32)
    a2 = r1.astype(BF16)
    a3 = (r1 - a2.astype(F32)).astype(BF16)
    return a1, a2, a3


def _rms(x, g):
    return x * lax.rsqrt(jnp.mean(x * x, axis=-1, keepdims=True) + EPS) * g


def _sigmoid(x):
    return 1.0 / (1.0 + jnp.exp(-x))


def _log_sigmoid(x):
    return jnp.minimum(x, 0.0) - jnp.log(1.0 + jnp.exp(-jnp.abs(x)))


def _gelu_tanh(x):
    return 0.5 * x * (1.0 + jnp.tanh(0.7978845608028654 * (x + 0.044715 * (x * x * x))))


def _mod_row_of_block(i, rows_per_block):
    ctx_blocks = T_CTX // rows_per_block
    per_seq = SMP_LEN // rows_per_block
    return jnp.where(i < ctx_blocks, 0, 1 + (i - ctx_blocks) // per_seq)


def _ada_kernel(cv_ref, w_ref, b_ref, o_ref):
    cv = cv_ref[...]
    s = cv * _sigmoid(cv)
    w = w_ref[0]
    w1, w2, w3 = _split3(w)
    s1, s2, s3 = _split3(s)
    acc = _dot(s1, w1) + (_dot(s1, w2) + _dot(s2, w1)) + (_dot(s1, w3) + _dot(s2, w2) + _dot(s3, w1))
    o_ref[0] = acc + b_ref[0]


def _ada_call(cvecs, ada_w, ada_b):
    tn = 1024
    return pl.pallas_call(
        _ada_kernel,
        grid=(DEPTH, 6 * D_MODEL // tn),
        in_specs=[
            pl.BlockSpec((N_MOD_ROWS, D_MODEL), lambda l, j: (0, 0)),
            pl.BlockSpec((1, D_MODEL, tn), lambda l, j: (l, 0, j)),
            pl.BlockSpec((1, 1, tn), lambda l, j: (l, 0, j)),
        ],
        out_specs=pl.BlockSpec((1, N_MOD_ROWS, tn), lambda l, j: (l, 0, j)),
        out_shape=jax.ShapeDtypeStruct((DEPTH, N_MOD_ROWS, 6 * D_MODEL), F32),
        compiler_params=pltpu.CompilerParams(dimension_semantics=("parallel", "parallel")),
        name="ada_mod",
    )(cvecs, ada_w, ada_b.reshape(DEPTH, 1, 6 * D_MODEL))


def _proj_kernel(x_ref, mod_ref, g1_ref, wml_ref, wg_ref, wgt_ref, gbr_ref, gbc_ref, wcm_ref, wmla_ref, wkr_ref,
                 cmg_ref, ws_ref, cmb_ref, qg_ref, kvg_ref, wuqa_ref, wuqb_ref, wk_ref, wv_ref, cos_ref, sin_ref,
                 ml_ref, gates_ref, gatest_ref, cm_ref, ckvn_ref, kr_ref, q_ref, k_ref, v_ref):
    x = x_ref[...]
    mod = mod_ref[0]
    sh1 = mod[:, 0:D_MODEL]
    sc1 = mod[:, D_MODEL:2 * D_MODEL]
    h = _rms(x, g1_ref[...]) * (1.0 + sc1) + sh1
    hb = h.astype(BF16)

    ml_ref[...] = _dot(hb, wml_ref[...])
    gates_ref[...] = _dot(hb, wg_ref[...]) + gbr_ref[...]
    gatest_ref[...] = _dot_nt(wgt_ref[...], hb) + gbc_ref[...]

    cm = _dot(hb, wcm_ref[...])
    u = cm[:, 0:CM_WIDTH]
    vn = _rms(cm[:, CM_WIDTH:2 * CM_WIDTH], cmg_ref[...]).astype(BF16)
    lane_group = lax.broadcasted_iota(jnp.int32, (CHUNK, CM_WIDTH), 1) >> 6
    for c in range(TM // CHUNK):
        rows = slice(c * CHUNK, (c + 1) * CHUNK)
        vc = vn[rows]
        mixed = jnp.zeros((CHUNK, CM_WIDTH), F32)
        for g in range(CM_GROUPS):
            mixed = jnp.where(lane_group == g, _dot(ws_ref[g], vc), mixed)
        cm_ref[rows, :] = u[rows] * (mixed + cmb_ref[...])

    mla = _dot(hb, wmla_ref[...])
    qn = _rms(mla[:, 0:MLA_RANK], qg_ref[...]).astype(BF16)
    ckvn = _rms(mla[:, MLA_RANK:2 * MLA_RANK], kvg_ref[...])
    ckvn_ref[...] = ckvn
    kr_ref[...] = _dot(hb, wkr_ref[...])
    cos = cos_ref[...]
    sin = sin_ref[...]
    kr_rot = mla[:, 2 * MLA_RANK:2 * MLA_RANK + HEAD_PAD] * cos + mla[:, 2 * MLA_RANK + HEAD_PAD:] * sin
    qa = _dot(qn, wuqa_ref[...])
    qb = _dot(qn, wuqb_ref[...])
    ckvb = ckvn.astype(BF16)
    kp = _dot(ckvb, wk_ref[...])
    for hd in range(MLA_HEADS):
        cols = slice(hd * HEAD_PAD, (hd + 1) * HEAD_PAD)
        q_ref[hd] = (qa[:, cols] * cos + qb[:, cols] * sin).astype(BF16)
        k_ref[hd] = (kp[:, cols] + kr_rot).astype(BF16)
    v_ref[...] = _dot(ckvb, wv_ref[...]).astype(BF16)


def _proj_call(x, mod, lw, cos_t, sin_t):
    nblk = T_ALL // TM
    full = lambda shape: pl.BlockSpec(shape, lambda i: (0,) * len(shape))
    rows = lambda w: pl.BlockSpec((TM, w), lambda i: (i, 0))
    in_specs = [
        rows(D_MODEL),
        pl.BlockSpec((1, 1, 6 * D_MODEL), lambda i: (_mod_row_of_block(i, TM), 0, 0)),
        full((1, D_MODEL)),
        full((D_MODEL, 4 * ML_WIDTH)),
        full((D_MODEL, 16)),
        full((16, D_MODEL)),
        full((1, 16)),
        full((16, 1)),
        full((D_MODEL, 2 * CM_WIDTH)),
        full((D_MODEL, 2 * MLA_RANK + 2 * HEAD_PAD)),
        full((D_MODEL, MLA_ROPE)),
        full((1, CM_WIDTH)),
        full((CM_GROUPS, CHUNK, CHUNK)),
        full((CHUNK, CM_WIDTH)),
        full((1, MLA_RANK)),
        full((1, MLA_RANK)),
        full((MLA_RANK, MLA_HEADS * HEAD_PAD)),
        full((MLA_RANK, MLA_HEADS * HEAD_PAD)),
        full((MLA_RANK, MLA_HEADS * HEAD_PAD)),
        full((MLA_RANK, MLA_HEADS * MLA_V)),
        rows(HEAD_PAD),
        rows(HEAD_PAD),
    ]
    out_specs = [
        rows(4 * ML_WIDTH),
        rows(16),
        pl.BlockSpec((16, TM), lambda i: (0, i)),
        rows(CM_WIDTH),
        rows(MLA_RANK),
        rows(MLA_ROPE),
        pl.BlockSpec((MLA_HEADS, TM, HEAD_PAD), lambda i: (0, i, 0)),
        pl.BlockSpec((MLA_HEADS, TM, HEAD_PAD), lambda i: (0, i, 0)),
        rows(MLA_HEADS * MLA_V),
    ]
    out_shape = [
        jax.ShapeDtypeStruct((T_ALL, 4 * ML_WIDTH), F32),
        jax.ShapeDtypeStruct((T_ALL, 16), F32),
        jax.ShapeDtypeStruct((16, T_ALL), F32),
        jax.ShapeDtypeStruct((T_ALL, CM_WIDTH), F32),
        jax.ShapeDtypeStruct((T_ALL, MLA_RANK), F32),
        jax.ShapeDtypeStruct((T_ALL, MLA_ROPE), F32),
        jax.ShapeDtypeStruct((MLA_HEADS, T_ALL, HEAD_PAD), BF16),
        jax.ShapeDtypeStruct((MLA_HEADS, T_ALL, HEAD_PAD), BF16),
        jax.ShapeDtypeStruct((T_ALL, MLA_HEADS * MLA_V), BF16),
    ]
    return pl.pallas_call(
        _proj_kernel,
        grid=(nblk,),
        in_specs=in_specs,
        out_specs=out_specs,
        out_shape=out_shape,
        compiler_params=pltpu.CompilerParams(dimension_semantics=("parallel",), vmem_limit_bytes=VMEM_LIMIT),
        name="proj_in",
    )(x, mod, lw["g1"], lw["wml"], lw["wg"], lw["wgt"], lw["gb_row"], lw["gb_col"], lw["wcm"], lw["wmla"],
      lw["wkr"], lw["cmg"], lw["ws"], lw["cmb"], lw["qg"], lw["kvg"], lw["wuqa"], lw["wuqb"], lw["wk"], lw["wv"],
      cos_t, sin_t)


def _mlstm_schedule():
    seq, fwd, bwd, first = [], [], [], []
    base = 0
    for s in range(N_CTX_SEQ + N_SMP_SEQ):
        nc = (CTX_LEN if s < N_CTX_SEQ else SMP_LEN) // CHUNK
        for j in range(nc):
            seq.append(s)
            fwd.append(base + j)
            bwd.append(base + nc - 1 - j)
            first.append(1 if j == 0 else 0)
        base += nc
    as_i32 = lambda a: jnp.asarray(np.asarray(a, np.int32))
    return as_i32(seq), as_i32(fwd), as_i32(bwd), as_i32(first)


def _mlstm_direction(ml, g_col, g_row, direction, c_ref, n_ref, m_ref):
    L = CHUNK
    t_idx = lax.broadcasted_iota(jnp.int32, (L, L), 0)
    s_idx = lax.broadcasted_iota(jnp.int32, (L, L), 1)
    visible = (s_idx <= t_idx) if direction == 0 else (s_idx >= t_idx)
    tri = jnp.where(visible, 1.0, 0.0).astype(BF16)
    i0 = 8 * direction
    i_col = g_col[:, i0:i0 + ML_HEADS]
    i_row = g_row[i0:i0 + ML_HEADS, :]
    lf_col = _log_sigmoid(g_col[:, i0 + ML_HEADS:i0 + 2 * ML_HEADS])
    lf_row = _log_sigmoid(g_row[i0 + ML_HEADS:i0 + 2 * ML_HEADS, :])
    c1, c2, c3 = _split3(lf_col)
    b_cols = _dot(tri, c1) + _dot(tri, c2) + _dot(tri, c3)
    r1, r2, r3 = _split3(lf_row)
    b_rows = _dot_nt(r1, tri) + _dot_nt(r2, tri) + _dot_nt(r3, tri)
    b_end_all = jnp.sum(lf_col, axis=0, keepdims=True)
    outs = []
    for hd in range(ML_HEADS):
        r = direction * ML_HEADS + hd
        lanes = slice(hd * ML_DIM, (hd + 1) * ML_DIM)
        q = ml[:, lanes]
        k = ml[:, ML_WIDTH + hd * ML_DIM:ML_WIDTH + (hd + 1) * ML_DIM] * (ML_DIM ** -0.5)
        v = ml[:, 2 * ML_WIDTH + hd * ML_DIM:2 * ML_WIDTH + (hd + 1) * ML_DIM]
        qb = q.astype(BF16)
        vb = v.astype(BF16)
        C = c_ref[0, r]
        n = n_ref[0, r:r + 1, :]
        m = m_ref[0, r:r + 1, 0:1]
        b_col = b_cols[:, hd:hd + 1]
        b_row = b_rows[hd:hd + 1, :]
        d = jnp.where(visible, b_col - b_row + i_row[hd:hd + 1, :], NEG_INF)
        m_inter = b_col + m
        m_t = jnp.maximum(m_inter, jnp.max(d, axis=-1, keepdims=True))
        w_intra = jnp.exp(d - m_t)
        w_inter = jnp.exp(m_inter - m_t)
        sw = _dot_nt(qb, k.astype(BF16)) * w_intra
        num = _dot(sw.astype(BF16), vb) + w_inter * _dot(qb, C.astype(BF16))
        den = jnp.sum(sw, axis=-1, keepdims=True) + w_inter * jnp.sum(q * n, axis=-1, keepdims=True)
        outs.append(num / jnp.maximum(jnp.abs(den), jnp.exp(-m_t)))
        b_end = b_end_all[:, hd:hd + 1]
        log_w = b_end - b_col + i_col[:, hd:hd + 1]
        m_new = jnp.maximum(b_end + m, jnp.max(log_w, axis=0, keepdims=True))
        w_k = jnp.exp(log_w - m_new)
        decay = jnp.exp(b_end + m - m_new)
        kw = w_k * k
        c_ref[0, r] = decay * C + _dot_tn(kw.astype(BF16), vb)
        n_ref[0, r:r + 1, :] = decay * n + jnp.sum(kw, axis=0, keepdims=True)
        m_ref[0, r:r + 1, :] = jnp.broadcast_to(m_new, (1, 128))
    return jnp.concatenate(outs, axis=-1)


def _mlstm_kernel(seq_ref, fwd_ref, bwd_ref, first_ref, mlf_ref, mlb_ref, gcf_ref, gcb_ref, grf_ref, grb_ref,
                  c0_ref, n0_ref, m0_ref, hf_ref, hb_ref, c_ref, n_ref, m_ref):
    step = pl.program_id(0)

    @pl.when(first_ref[step] == 1)
    def _():
        c_ref[...] = c0_ref[...]
        n_ref[...] = n0_ref[...]
        m_ref[...] = m0_ref[...]

    hf_ref[...] = _mlstm_direction(mlf_ref[...], gcf_ref[...], grf_ref[...], 0, c_ref, n_ref, m_ref)
    hb_ref[...] = _mlstm_direction(mlb_ref[...], gcb_ref[...], grb_ref[...], 1, c_ref, n_ref, m_ref)


def _mlstm_call(ml, gates, gates_t, c0, n0, m0):
    seq, fwd, bwd, first = _mlstm_schedule()
    nseq = N_CTX_SEQ + N_SMP_SEQ
    nsteps = int(seq.shape[0])
    nstate = 2 * ML_HEADS
    f_rows = lambda w: pl.BlockSpec((CHUNK, w), lambda i, sq, fw, bw, fs: (fw[i], 0))
    b_rows = lambda w: pl.BlockSpec((CHUNK, w), lambda i, sq, fw, bw, fs: (bw[i], 0))
    c_spec = pl.BlockSpec((1, nstate, ML_DIM, ML_DIM), lambda i, sq, fw, bw, fs: (sq[i], 0, 0, 0))
    n_spec = pl.BlockSpec((1, nstate, ML_DIM), lambda i, sq, fw, bw, fs: (sq[i], 0, 0))
    m_spec = pl.BlockSpec((1, nstate, 128), lambda i, sq, fw, bw, fs: (sq[i], 0, 0))
    grid_spec = pltpu.PrefetchScalarGridSpec(
        num_scalar_prefetch=4,
        grid=(nsteps,),
        in_specs=[
            f_rows(4 * ML_WIDTH), b_rows(4 * ML_WIDTH), f_rows(16), b_rows(16),
            pl.BlockSpec((16, CHUNK), lambda i, sq, fw, bw, fs: (0, fw[i])),
            pl.BlockSpec((16, CHUNK), lambda i, sq, fw, bw, fs: (0, bw[i])),
            c_spec, n_spec, m_spec,
        ],
        out_specs=[f_rows(ML_WIDTH), b_rows(ML_WIDTH), c_spec, n_spec, m_spec],
    )
    return pl.pallas_call(
        _mlstm_kernel,
        grid_spec=grid_spec,
        out_shape=[
            jax.ShapeDtypeStruct((T_ALL, ML_WIDTH), F32),
            jax.ShapeDtypeStruct((T_ALL, ML_WIDTH), F32),
            jax.ShapeDtypeStruct((nseq, nstate, ML_DIM, ML_DIM), F32),
            jax.ShapeDtypeStruct((nseq, nstate, ML_DIM), F32),
            jax.ShapeDtypeStruct((nseq, nstate, 128), F32),
        ],
        compiler_params=pltpu.CompilerParams(dimension_semantics=("arbitrary",), vmem_limit_bytes=VMEM_LIMIT),
        name="mlstm",
    )(seq, fwd, bwd, first, ml, ml, gates, gates, gates_t, gates_t, c0, n0, m0)


def _cache_kv_kernel(ckv_ref, kr_ref, wk_ref, wv_ref, place_ref, k_ref, v_ref):
    ckvb = ckv_ref[...].astype(BF16)
    kp = _dot(ckvb, wk_ref[...])
    kr128 = _dot(kr_ref[...].astype(BF16), place_ref[...])
    for hd in range(MLA_HEADS):
        k_ref[hd] = (kp[:, hd * HEAD_PAD:(hd + 1) * HEAD_PAD] + kr128).astype(BF16)
    v_ref[...] = _dot(ckvb, wv_ref[...]).astype(BF16)


def _cache_kv_call(ckv, kr, wk, wv, place):
    n = N_SMP_SEQ * PAST_LEN
    tb = PAST_LEN
    return pl.pallas_call(
        _cache_kv_kernel,
        grid=(n // tb,),
        in_specs=[
            pl.BlockSpec((tb, MLA_RANK), lambda i: (i, 0)),
            pl.BlockSpec((tb, MLA_ROPE), lambda i: (i, 0)),
            pl.BlockSpec((MLA_RANK, MLA_HEADS * HEAD_PAD), lambda i: (0, 0)),
            pl.BlockSpec((MLA_RANK, MLA_HEADS * MLA_V), lambda i: (0, 0)),
            pl.BlockSpec((MLA_ROPE, HEAD_PAD), lambda i: (0, 0)),
        ],
        out_specs=[
            pl.BlockSpec((MLA_HEADS, tb, HEAD_PAD), lambda i: (0, i, 0)),
            pl.BlockSpec((tb, MLA_HEADS * MLA_V), lambda i: (i, 0)),
        ],
        out_shape=[
            jax.ShapeDtypeStruct((MLA_HEADS, n, HEAD_PAD), BF16),
            jax.ShapeDtypeStruct((n, MLA_HEADS * MLA_V), BF16),
        ],
        compiler_params=pltpu.CompilerParams(dimension_semantics=("parallel",)),
        name="cache_kv",
    )(ckv, kr, wk, wv, place)


def _attn_kernel(*refs, has_cache):
    if has_cache:
        q_ref, kn_ref, vn_ref, kc_ref, vc_ref, _, o_ref = refs
    else:
        q_ref, kn_ref, vn_ref, o_ref = refs
    scale = (MLA_NOPE + MLA_ROPE) ** -0.5
    outs = []
    for hd in range(MLA_HEADS):
        lanes = slice(hd * MLA_V, (hd + 1) * MLA_V)
        q = q_ref[hd]
        s_n = _dot_nt(q, kn_ref[hd])
        m = jnp.max(s_n, axis=-1, keepdims=True)
        if has_cache:
            s_c = _dot_nt(q, kc_ref[hd])
            m = jnp.maximum(m, jnp.max(s_c, axis=-1, keepdims=True))
        p_n = jnp.exp((s_n - m) * scale)
        l = jnp.sum(p_n, axis=-1, keepdims=True)
        o = _dot(p_n.astype(BF16), vn_ref[:, lanes])
        if has_cache:
            p_c = jnp.exp((s_c - m) * scale)
            l = l + jnp.sum(p_c, axis=-1, keepdims=True)
            o = o + _dot(p_c.astype(BF16), vc_ref[:, lanes])
        outs.append(o / l)
    o_ref[...] = jnp.concatenate(outs, axis=-1)


def _attn_ctx_call(q, k, v):
    tq = CTX_LEN
    return pl.pallas_call(
        functools.partial(_attn_kernel, has_cache=False),
        grid=(N_CTX_SEQ,),
        in_specs=[
            pl.BlockSpec((MLA_HEADS, tq, HEAD_PAD), lambda s: (0, s, 0)),
            pl.BlockSpec((MLA_HEADS, tq, HEAD_PAD), lambda s: (0, s, 0)),
            pl.BlockSpec((tq, MLA_HEADS * MLA_V), lambda s: (s, 0)),
        ],
        out_specs=pl.BlockSpec((tq, MLA_HEADS * MLA_V), lambda s: (s, 0)),
        out_shape=jax.ShapeDtypeStruct((T_ALL, MLA_HEADS * MLA_V), F32),
        compiler_params=pltpu.CompilerParams(dimension_semantics=("parallel",), vmem_limit_bytes=VMEM_LIMIT),
        name="attn_ctx",
    )(q, k, v)


def _attn_smp_call(q, k, v, kc, vc, att_prev):
    tq = 256
    qb_per_seq = SMP_LEN // tq
    ctx_qb = T_CTX // tq
    ctx_kb = T_CTX // SMP_LEN
    return pl.pallas_call(
        functools.partial(_attn_kernel, has_cache=True),
        grid=(N_SMP_SEQ, qb_per_seq),
        in_specs=[
            pl.BlockSpec((MLA_HEADS, tq, HEAD_PAD), lambda b, i: (0, ctx_qb + b * qb_per_seq + i, 0)),
            pl.BlockSpec((MLA_HEADS, SMP_LEN, HEAD_PAD), lambda b, i: (0, ctx_kb + b, 0)),
            pl.BlockSpec((SMP_LEN, MLA_HEADS * MLA_V), lambda b, i: (ctx_kb + b, 0)),
            pl.BlockSpec((MLA_HEADS, PAST_LEN, HEAD_PAD), lambda b, i: (0, b, 0)),
            pl.BlockSpec((PAST_LEN, MLA_HEADS * MLA_V), lambda b, i: (b, 0)),
            pl.BlockSpec(memory_space=pl.ANY),
        ],
        out_specs=pl.BlockSpec((tq, MLA_HEADS * MLA_V), lambda b, i: (ctx_qb + b * qb_per_seq + i, 0)),
        out_shape=jax.ShapeDtypeStruct((T_ALL, MLA_HEADS * MLA_V), F32),
        input_output_aliases={5: 0},
        compiler_params=pltpu.CompilerParams(dimension_semantics=("parallel", "parallel"),
                                             vmem_limit_bytes=VMEM_LIMIT),
        name="attn_smp",
    )(q, k, v, kc, vc, att_prev)


def _out_kernel(x_ref, mod_ref, ml_ref, hf_ref, hb_ref, cm_ref, att_ref, wo_ref, g2_ref, wq_ref, sk_ref,
                x1_ref, h2t_ref, st_ref):
    mod = mod_ref[0]
    g1 = mod[:, 2 * D_MODEL:3 * D_MODEL]
    sh2 = mod[:, 3 * D_MODEL:4 * D_MODEL]
    sc2 = mod[:, 4 * D_MODEL:5 * D_MODEL]
    mlo = _sigmoid(ml_ref[:, 3 * ML_WIDTH:4 * ML_WIDTH]) * (hf_ref[...] + hb_ref[...])
    mix = (_dot(mlo.astype(BF16), wo_ref[0:ML_WIDTH, :])
           + _dot(cm_ref[...].astype(BF16), wo_ref[ML_WIDTH:ML_WIDTH + CM_WIDTH, :])
           + _dot(att_ref[...].astype(BF16), wo_ref[ML_WIDTH + CM_WIDTH:, :]))
    x1 = x_ref[...] + g1 * mix
    x1_ref[...] = x1
    h2f = _rms(x1, g2_ref[...]) * (1.0 + sc2) + sh2
    h2t_ref[...] = h2f.T.astype(BF16)
    qp = _dot(h2f.astype(BF16), wq_ref[...]).astype(BF16)
    for hh in range(2 * PEER_HEADS):
        st_ref[hh] = _dot_nt(sk_ref[hh % 2], qp[:, hh * 128:(hh + 1) * 128])


def _out_call(x, mod, ml, hf, hb, cm, att, lw):
    nblk = T_ALL // TM
    rows = lambda w: pl.BlockSpec((TM, w), lambda i: (i, 0))
    full = lambda shape: pl.BlockSpec(shape, lambda i: (0,) * len(shape))
    return pl.pallas_call(
        _out_kernel,
        grid=(nblk,),
        in_specs=[
            rows(D_MODEL),
            pl.BlockSpec((1, 1, 6 * D_MODEL), lambda i: (_mod_row_of_block(i, TM), 0, 0)),
            rows(4 * ML_WIDTH), rows(ML_WIDTH), rows(ML_WIDTH), rows(CM_WIDTH), rows(MLA_HEADS * MLA_V),
            full((D_MODEL, D_MODEL)), full((1, D_MODEL)), full((D_MODEL, 2 * PEER_HEADS * 128)),
            full((2, PEER_NKEYS, 128)),
        ],
        out_specs=[rows(D_MODEL), pl.BlockSpec((D_MODEL, TM), lambda i: (0, i)),
                   pl.BlockSpec((2 * PEER_HEADS, PEER_NKEYS, TM), lambda i: (0, 0, i))],
        out_shape=[
            jax.ShapeDtypeStruct((T_ALL, D_MODEL), F32),
            jax.ShapeDtypeStruct((D_MODEL, T_ALL), BF16),
            jax.ShapeDtypeStruct((2 * PEER_HEADS, PEER_NKEYS, T_ALL), F32),
        ],
        compiler_params=pltpu.CompilerParams(dimension_semantics=("parallel",), vmem_limit_bytes=VMEM_LIMIT),
        name="proj_out",
    )(x, mod, ml, hf, hb, cm, att, lw["wo"], lw["g2"], lw["wq"], lw["sk"])


def _top16_rows(s, want_rank=False):
    vals = []
    rank = jnp.full(s.shape, float(PEER_TOPK), F32) if want_rank else None
    for k in range(PEER_TOPK):
        m = jnp.max(s, axis=0, keepdims=True)
        vals.append(m)
        hit = s == m
        if want_rank:
            rank = jnp.where(hit, float(k), rank)
        s = jnp.where(hit, NEG_INF, s)
    return vals, rank


def _pack_rows_bf16(x):
    r, n = x.shape
    x4 = x.reshape(r // 16, 2, 8, n)
    lo = x4[:, 0].reshape(r // 2, n)
    hi = x4[:, 1].reshape(r // 2, n)
    return pltpu.bitcast(pltpu.pack_elementwise([lo, hi], packed_dtype=BF16), BF16)


def _dup_bf16_words(x):
    u = pltpu.bitcast(x.astype(BF16).astype(F32), jnp.int32)
    return u | lax.shift_right_logical(u, jnp.full(u.shape, 16, jnp.int32))


def _rows_to_array(rows, row_iota):
    arr = jnp.zeros(row_iota.shape, F32)
    for i, r in enumerate(rows):
        arr = jnp.where(row_iota == i, r, arr)
    return arr


def _topk_kernel(st_ref, e1_ref, cut_ref, e2_ref, r2_ref):
    n = st_ref.shape[-1]
    row16 = lax.broadcasted_iota(jnp.int32, (PEER_TOPK, n), 0)

    def head(hd, carry):
        s1 = st_ref[2 * hd]
        s2 = st_ref[2 * hd + 1]
        v1, _ = _top16_rows(s1)
        v2, rank2 = _top16_rows(s2, want_rank=True)
        v1arr = _rows_to_array(v1, row16)
        v2arr = _rows_to_array(v2, row16)
        pieces = [v1[0] + v2arr] + [v1[p] + v2arr[0:8] for p in range(1, PEER_TOPK)]
        c, _ = _top16_rows(jnp.concatenate(pieces, axis=0))
        tau = c[PEER_TOPK - 1]
        z = jnp.zeros((1, n), F32)
        for ck in c:
            z = z + jnp.exp(ck - c[0])
        cut = jnp.zeros(s1.shape, F32)
        for q in range(PEER_TOPK):
            sigma = jnp.min(jnp.where(v1arr + v2[q] >= tau, v1arr, jnp.inf), axis=0, keepdims=True)
            cut = cut + jnp.where(s1 >= sigma, 1.0, 0.0)
        e1_ref[hd] = _dup_bf16_words(jnp.exp(s1 - v1[0]) / z)
        cut_ref[hd] = _dup_bf16_words(cut)
        e2_ref[hd] = _pack_rows_bf16(jnp.exp(s2 - v2[0]))
        r2_ref[hd] = _pack_rows_bf16(rank2)
        return carry

    lax.fori_loop(0, PEER_HEADS, head, 0)


def _topk_call(st):
    tn = TN_TOPK
    spec = pl.BlockSpec((PEER_HEADS, PEER_NKEYS, tn), lambda i: (0, 0, i))
    stat = lambda dt: jax.ShapeDtypeStruct((PEER_HEADS, PEER_NKEYS, T_ALL), dt)
    return pl.pallas_call(
        _topk_kernel,
        grid=(T_ALL // tn,),
        in_specs=[pl.BlockSpec((2 * PEER_HEADS, PEER_NKEYS, tn), lambda i: (0, 0, i))],
        out_specs=[spec, spec, spec, spec],
        out_shape=[stat(jnp.int32), stat(jnp.int32), stat(BF16), stat(BF16)],
        compiler_params=pltpu.CompilerParams(dimension_semantics=("parallel",), vmem_limit_bytes=VMEM_LIMIT),
        name="peer_topk",
    )(st)


def _row_tile_bf16(row):
    blk = pltpu.bitcast(jnp.broadcast_to(row, (8, row.shape[1])), BF16)
    return jnp.concatenate([blk] * (PEER_NKEYS // 16), axis=0)


def _gelu_tanh_bf16(x):
    z2 = x * (1.5957691216057308 + 0.07135481627159584 * (x * x))
    return x / (1.0 + jnp.exp(-z2))


def _expert_kernel(h2t_ref, u_ref, vt_ref, e1_ref, cut_ref, e2_ref, r2_ref, x1_ref, mod_ref, fg_ref,
                   o_ref, acc_ref, a_ref, *, final_norm):
    j = pl.program_id(1)
    n_tiles = pl.num_programs(1)
    tm = h2t_ref.shape[1]
    a_per_tile = TE_EXP // PEER_NKEYS

    @pl.when(j == 0)
    def _():
        acc_ref[...] = jnp.zeros_like(acc_ref)

    for sub in range(TE_EXP // EXP_SUB):
        srows = slice(sub * EXP_SUB, (sub + 1) * EXP_SUB)
        s = _pack_rows_bf16(_dot(u_ref[srows, :], h2t_ref[...]))
        g = _gelu_tanh_bf16(s)
        for ai in range(sub * EXP_SUB // PEER_NKEYS, (sub + 1) * EXP_SUB // PEER_NKEYS):
            a = j * a_per_tile + ai
            gate = jnp.zeros((PEER_NKEYS, tm), BF16)
            for hd in range(PEER_HEADS):
                cut_a = _row_tile_bf16(cut_ref[hd, pl.ds(a, 1), :])
                e1_a = _row_tile_bf16(e1_ref[hd, pl.ds(a, 1), :])
                gate = gate + jnp.where(r2_ref[hd] < cut_a, e2_ref[hd], jnp.zeros((), BF16)) * e1_a
            grow = ai * PEER_NKEYS - sub * EXP_SUB
            a_ref[ai * PEER_NKEYS:(ai + 1) * PEER_NKEYS, :] = gate * g[grow:grow + PEER_NKEYS, :]
    acc_ref[...] += _dot(vt_ref[...], a_ref[...])

    @pl.when(j == n_tiles - 1)
    def _():
        g2 = mod_ref[0][:, 5 * D_MODEL:6 * D_MODEL]
        y = x1_ref[...] + g2 * acc_ref[...].T
        if final_norm:
            y = _rms(y, fg_ref[...])
        o_ref[...] = y


def _expert_call(h2t, u_bf, vt_bf, e1, cut, e2, r2, x1, mod, final_g, final_norm):
    tm, te = TM_EXP, TE_EXP
    stat = pl.BlockSpec((PEER_HEADS, PEER_NKEYS, tm), lambda i, j: (0, 0, i))
    return pl.pallas_call(
        functools.partial(_expert_kernel, final_norm=final_norm),
        grid=(T_ALL // tm, PEER_EXPERTS // te),
        in_specs=[
            pl.BlockSpec((D_MODEL, tm), lambda i, j: (0, i)),
            pl.BlockSpec((te, D_MODEL), lambda i, j: (j, 0)),
            pl.BlockSpec((D_MODEL, te), lambda i, j: (0, j)),
            stat, stat, stat, stat,
            pl.BlockSpec((tm, D_MODEL), lambda i, j: (i, 0)),
            pl.BlockSpec((1, 1, 6 * D_MODEL), lambda i, j: (_mod_row_of_block(i, tm), 0, 0)),
            pl.BlockSpec((1, D_MODEL), lambda i, j: (0, 0)),
        ],
        out_specs=pl.BlockSpec((tm, D_MODEL), lambda i, j: (i, 0)),
        out_shape=jax.ShapeDtypeStruct((T_ALL, D_MODEL), F32),
        scratch_shapes=[
            pltpu.VMEM((D_MODEL, tm), F32),
            pltpu.VMEM((te, tm), BF16),
        ],
        compiler_params=pltpu.CompilerParams(dimension_semantics=("parallel", "arbitrary"),
                                             vmem_limit_bytes=VMEM_LIMIT),
        name="peer_experts",
    )(h2t, u_bf, vt_bf, e1, cut, e2, r2, x1, mod, final_g)


def _rope_swap_cols(w):
    return jnp.concatenate([-w[:, 8:16], w[:, 0:8], -w[:, 24:32], w[:, 16:24]], axis=1)


def _pad_heads(parts, n_heads):
    k = next(p[0].shape[0] for p in parts if p[0] is not None)
    cols = []
    for hd in range(n_heads):
        for arr, w in parts:
            cols.append(jnp.zeros((k, w), F32) if arr is None else arr[:, hd * w:(hd + 1) * w])
    return jnp.concatenate(cols, axis=1)


def _rope_tables():
    pos = np.arange(SMP_LEN)
    freqs = ROPE_THETA ** (-np.arange(0, ROPE_AXIS, 2, dtype=np.float32) / ROPE_AXIS)
    ang_r = (pos // GRID_W).astype(np.float32)[:, None] * freqs
    ang_c = (pos % GRID_W).astype(np.float32)[:, None] * freqs
    ang = jnp.asarray(np.concatenate([ang_r, ang_r, ang_c, ang_c], axis=1).astype(np.float32))
    cos32 = jnp.cos(ang)
    sin32 = jnp.sin(ang)
    ones = jnp.ones((SMP_LEN, MLA_NOPE), F32)
    cos_s = jnp.concatenate([ones, cos32, ones[:, :HEAD_PAD - MLA_NOPE - MLA_ROPE]], axis=1)
    sin_s = jnp.concatenate([0 * ones, sin32, 0 * ones[:, :HEAD_PAD - MLA_NOPE - MLA_ROPE]], axis=1)
    cos_t = jnp.concatenate([jnp.ones((T_CTX, HEAD_PAD), F32)] + [cos_s] * N_SMP_SEQ, axis=0)
    sin_t = jnp.concatenate([jnp.zeros((T_CTX, HEAD_PAD), F32)] + [sin_s] * N_SMP_SEQ, axis=0)
    return cos_t, sin_t


def _layer_weights(l, norm1_g, w_in, mlstm_gate_b, cm_norm_g, cm_ws, cm_b, mla_q_norm_g, mla_w_uq, mla_kv_norm_g,
                   mla_w_ukv, w_out, norm2_g, peer_w_q, peer_subkeys, peer_u, peer_v):
    w = w_in[l]
    o_g = 4 * ML_WIDTH
    o_cm = o_g + 16
    o_cq = o_cm + 2 * CM_WIDTH
    o_ckv = o_cq + MLA_RANK
    o_kr = o_ckv + MLA_RANK
    w_kr = w[:, o_kr:o_kr + MLA_ROPE]
    zeros_l = jnp.zeros((D_MODEL, MLA_NOPE), F32)
    zeros_r = jnp.zeros((D_MODEL, HEAD_PAD - MLA_NOPE - MLA_ROPE), F32)
    kr128 = jnp.concatenate([zeros_l, w_kr, zeros_r], axis=1)
    krsw128 = jnp.concatenate([zeros_l, _rope_swap_cols(w_kr), zeros_r], axis=1)
    uq = mla_w_uq[l].reshape(MLA_RANK, MLA_HEADS, MLA_NOPE + MLA_ROPE)
    uq_nope = uq[:, :, :MLA_NOPE].reshape(MLA_RANK, -1)
    uq_rope = uq[:, :, MLA_NOPE:].reshape(MLA_RANK, -1)
    uq_rope_sw = jnp.concatenate(
        [_rope_swap_cols(uq_rope[:, hd * MLA_ROPE:(hd + 1) * MLA_ROPE]) for hd in range(MLA_HEADS)], axis=1)
    pad_w = HEAD_PAD - MLA_NOPE - MLA_ROPE
    ukv = mla_w_ukv[l].reshape(MLA_RANK, MLA_HEADS, MLA_NOPE + MLA_V)
    uk = ukv[:, :, :MLA_NOPE].reshape(MLA_RANK, -1)
    uv = ukv[:, :, MLA_NOPE:].reshape(MLA_RANK, -1)
    gb = mlstm_gate_b[l]
    return {
        "g1": norm1_g[l].reshape(1, D_MODEL),
        "wml": w[:, 0:o_g].astype(BF16),
        "wg": w[:, o_g:o_cm].astype(BF16),
        "wgt": w[:, o_g:o_cm].T.astype(BF16),
        "gb_row": gb.reshape(1, 16),
        "gb_col": gb.reshape(16, 1),
        "wcm": w[:, o_cm:o_cq].astype(BF16),
        "wmla": jnp.concatenate([w[:, o_cq:o_kr], kr128, krsw128], axis=1).astype(BF16),
        "wkr": w_kr.astype(BF16),
        "cmg": cm_norm_g[l].reshape(1, CM_WIDTH),
        "ws": cm_ws[l].astype(BF16),
        "cmb": jnp.repeat(cm_b[l].T, CM_WIDTH // CM_GROUPS, axis=1),
        "qg": mla_q_norm_g[l].reshape(1, MLA_RANK),
        "kvg": mla_kv_norm_g[l].reshape(1, MLA_RANK),
        "wuqa": _pad_heads([(uq_nope, MLA_NOPE), (uq_rope, MLA_ROPE), (None, pad_w)], MLA_HEADS).astype(BF16),
        "wuqb": _pad_heads([(None, MLA_NOPE), (uq_rope_sw, MLA_ROPE), (None, pad_w)], MLA_HEADS).astype(BF16),
        "wk": _pad_heads([(uk, MLA_NOPE), (None, HEAD_PAD - MLA_NOPE)], MLA_HEADS).astype(BF16),
        "wv": uv.astype(BF16),
        "wo": w_out[l].astype(BF16),
        "g2": norm2_g[l].reshape(1, D_MODEL),
        "wq": peer_w_q[l].astype(BF16),
        "sk": peer_subkeys[l].astype(BF16),
        "u": peer_u[l].astype(BF16),
        "vt": peer_v[l].reshape(PEER_EXPERTS // 16, 2, 8, D_MODEL).swapaxes(1, 2).reshape(PEER_EXPERTS, D_MODEL)
              .T.astype(BF16),
    }


def kernel(x_prompt, x_sample, c, cache_mla_ckv, cache_mla_krope, state_mlstm_C, state_mlstm_n, state_mlstm_m, c_ctx, norm1_g, ada_w, ada_b, w_in, mlstm_gate_b, cm_norm_g, cm_ws, cm_b, mla_q_norm_g, mla_w_uq, mla_kv_norm_g, mla_w_ukv, w_out, norm2_g, peer_w_q, peer_subkeys, peer_u, peer_v, final_g):
    nstate = 2 * ML_HEADS
    x = jnp.concatenate([x_prompt.reshape(T_CTX, D_MODEL), x_sample.reshape(T_SMP, D_MODEL)], axis=0)
    cvecs = jnp.concatenate([c_ctx[None, :], c, jnp.zeros((N_MOD_ROWS - 1 - N_SMP_SEQ, D_MODEL), F32)], axis=0)
    mod_all = _ada_call(cvecs, ada_w, ada_b)
    cos_t, sin_t = _rope_tables()
    place = jnp.concatenate([jnp.zeros((MLA_ROPE, MLA_NOPE), F32), jnp.eye(MLA_ROPE, dtype=F32),
                             jnp.zeros((MLA_ROPE, HEAD_PAD - MLA_NOPE - MLA_ROPE), F32)], axis=1).astype(BF16)
    final_g2 = final_g.reshape(1, D_MODEL)

    ckvs, krs, Cs, ns, ms = [], [], [], [], []
    for l in range(DEPTH):
        lw = _layer_weights(l, norm1_g, w_in, mlstm_gate_b, cm_norm_g, cm_ws, cm_b, mla_q_norm_g, mla_w_uq,
                            mla_kv_norm_g, mla_w_ukv, w_out, norm2_g, peer_w_q, peer_subkeys, peer_u, peer_v)
        mod = mod_all[l].reshape(N_MOD_ROWS, 1, 6 * D_MODEL)
        ml, gates, gates_t, cm, ckvn, kr, q, k, v = _proj_call(x, mod, lw, cos_t, sin_t)

        c0 = jnp.concatenate([jnp.zeros((N_CTX_SEQ, nstate, ML_DIM, ML_DIM), F32),
                              state_mlstm_C[:, l].reshape(N_SMP_SEQ, nstate, ML_DIM, ML_DIM)], axis=0)
        n0 = jnp.concatenate([jnp.zeros((N_CTX_SEQ, nstate, ML_DIM), F32),
                              state_mlstm_n[:, l].reshape(N_SMP_SEQ, nstate, ML_DIM)], axis=0)
        m0 = jnp.concatenate([jnp.zeros((N_CTX_SEQ, nstate), F32),
                              state_mlstm_m[:, l].reshape(N_SMP_SEQ, nstate)], axis=0)
        m0 = jnp.broadcast_to(m0[:, :, None], (N_CTX_SEQ + N_SMP_SEQ, nstate, 128))
        hf, hb, c_fin, n_fin, m_fin = _mlstm_call(ml, gates, gates_t, c0, n0, m0)

        kc, vc = _cache_kv_call(cache_mla_ckv[:, l].reshape(N_SMP_SEQ * PAST_LEN, MLA_RANK),
                                cache_mla_krope[:, l].reshape(N_SMP_SEQ * PAST_LEN, MLA_ROPE),
                                lw["wk"], lw["wv"], place)
        att = _attn_ctx_call(q, k, v)
        att = _attn_smp_call(q, k, v, kc, vc, att)

        x1, h2t, st = _out_call(x, mod, ml, hf, hb, cm, att, lw)
        e1, cut, e2, r2 = _topk_call(st)
        x = _expert_call(h2t, lw["u"], lw["vt"], e1, cut, e2, r2, x1, mod, final_g2, final_norm=(l == DEPTH - 1))

        ckvs.append(ckvn[:T_CTX].reshape(N_CTX_SEQ, CTX_LEN, MLA_RANK))
        krs.append(kr[:T_CTX].reshape(N_CTX_SEQ, CTX_LEN, MLA_ROPE))
        Cs.append(c_fin[:N_CTX_SEQ].reshape(N_CTX_SEQ, 2, ML_HEADS, ML_DIM, ML_DIM))
        ns.append(n_fin[:N_CTX_SEQ].reshape(N_CTX_SEQ, 2, ML_HEADS, ML_DIM))
        ms.append(m_fin[:N_CTX_SEQ, :, 0].reshape(N_CTX_SEQ, 2, ML_HEADS))

    y_prompt = x[:T_CTX].reshape(N_CTX_SEQ, CTX_LEN, D_MODEL)
    y_sample = x[T_CTX:].reshape(N_SMP_SEQ, SMP_LEN, D_MODEL)
    return (y_prompt, y_sample, jnp.stack(ckvs, axis=1), jnp.stack(krs, axis=1), jnp.stack(Cs, axis=1),
            jnp.stack(ns, axis=1), jnp.stack(ms, axis=1))
```

```python
import functools

import numpy as np
import jax
import jax.numpy as jnp
from jax import lax
from jax.experimental import pallas as pl
from jax.experimental.pallas import tpu as pltpu

F32 = jnp.float32
BF16 = jnp.bfloat16

D_MODEL = 1024
N_CTX_SEQ = 16
CTX_LEN = 256
N_SMP_SEQ = 4
SMP_LEN = 2048
PAST_LEN = 256
DEPTH = 2
GRID_W = 64
EPS = 1e-6
T_CTX = N_CTX_SEQ * CTX_LEN
T_SMP = N_SMP_SEQ * SMP_LEN
T_ALL = T_CTX + T_SMP
N_MOD_ROWS = 8

ML_HEADS = 4
ML_DIM = 64
ML_WIDTH = 256
CHUNK = 128
CM_GROUPS = 4
CM_WIDTH = 256
MLA_HEADS = 8
MLA_NOPE = 64
MLA_ROPE = 32
MLA_V = 64
MLA_RANK = 256
HEAD_PAD = 128
ROPE_AXIS = 16
ROPE_THETA = 10000.0
PEER_HEADS = 8
PEER_NKEYS = 128
PEER_EXPERTS = PEER_NKEYS * PEER_NKEYS
PEER_TOPK = 16

TM = 256
TN_TOPK = 256
TM_EXP = 512
TE_EXP = 2048
EXP_SUB = 512
VMEM_LIMIT = 56 * 1024 * 1024

NEG_INF = float("-inf")


def _dot(a, b):
    return jnp.dot(a, b, preferred_element_type=F32)


def _dot_nt(a, b):
    return lax.dot_general(a, b, (((1,), (1,)), ((), ())), preferred_element_type=F32)


def _dot_tn(a, b):
    return lax.dot_general(a, b, (((0,), (0,)), ((), ())), preferred_element_type=F32)


def _split3(a):
    a1 = a.astype(BF16)
    r1 = a - a1.astype(F32)
    a2 = r1.astype(BF16)
    a3 = (r1 - a2.astype(F32)).astype(BF16)
    return a1, a2, a3


def _rms(x, g):
    return x * lax.rsqrt(jnp.mean(x * x, axis=-1, keepdims=True) + EPS) * g


def _sigmoid(x):
    return 1.0 / (1.0 + jnp.exp(-x))


def _log_sigmoid(x):
    return jnp.minimum(x, 0.0) - jnp.log(1.0 + jnp.exp(-jnp.abs(x)))


def _gelu_tanh(x):
    return 0.5 * x * (1.0 + jnp.tanh(0.7978845608028654 * (x + 0.044715 * (x * x * x))))


def _mod_row_of_block(i, rows_per_block):
    ctx_blocks = T_CTX // rows_per_block
    per_seq = SMP_LEN // rows_per_block
    return jnp.where(i < ctx_blocks, 0, 1 + (i - ctx_blocks) // per_seq)


def _ada_kernel(cv_ref, w_ref, b_ref, o_ref):
    cv = cv_ref[...]
    s = cv * _sigmoid(cv)
    w = w_ref[0]
    w1, w2, w3 = _split3(w)
    s1, s2, s3 = _split3(s)
    acc = _dot(s1, w1) + (_dot(s1, w2) + _dot(s2, w1)) + (_dot(s1, w3) + _dot(s2, w2) + _dot(s3, w1))
    o_ref[0] = acc + b_ref[0]


def _ada_call(cvecs, ada_w, ada_b):
    tn = 1024
    return pl.pallas_call(
        _ada_kernel,
        grid=(DEPTH, 6 * D_MODEL // tn),
        in_specs=[
            pl.BlockSpec((N_MOD_ROWS, D_MODEL), lambda l, j: (0, 0)),
            pl.BlockSpec((1, D_MODEL, tn), lambda l, j: (l, 0, j)),
            pl.BlockSpec((1, 1, tn), lambda l, j: (l, 0, j)),
        ],
        out_specs=pl.BlockSpec((1, N_MOD_ROWS, tn), lambda l, j: (l, 0, j)),
        out_shape=jax.ShapeDtypeStruct((DEPTH, N_MOD_ROWS, 6 * D_MODEL), F32),
        compiler_params=pltpu.CompilerParams(dimension_semantics=("parallel", "parallel")),
        name="ada_mod",
    )(cvecs, ada_w, ada_b.reshape(DEPTH, 1, 6 * D_MODEL))


def _proj_kernel(x_ref, mod_ref, g1_ref, wml_ref, wgt_ref, gbc_ref, wcm_ref, wmla_ref, wkr_ref,
                 cmg_ref, ws_ref, cmb_ref, qg_ref, kvg_ref, wuqa_ref, wuqb_ref, wk_ref, wv_ref, cos_ref, sin_ref,
                 ml_ref, gatest_ref, cm_ref, ckvn_ref, kr_ref, q_ref, k_ref, v_ref):
    x = x_ref[...]
    mod = mod_ref[0]
    sh1 = mod[:, 0:D_MODEL]
    sc1 = mod[:, D_MODEL:2 * D_MODEL]
    h = _rms(x, g1_ref[...]) * (1.0 + sc1) + sh1
    hb = h.astype(BF16)

    ml_ref[...] = _dot(hb, wml_ref[...])
    gatest_ref[...] = _dot_nt(wgt_ref[...], hb) + gbc_ref[...]

    cm = _dot(hb, wcm_ref[...])
    u = cm[:, 0:CM_WIDTH]
    vn = _rms(cm[:, CM_WIDTH:2 * CM_WIDTH], cmg_ref[...]).astype(BF16)
    lane_group = lax.broadcasted_iota(jnp.int32, (CHUNK, CM_WIDTH), 1) >> 6
    for c in range(TM // CHUNK):
        rows = slice(c * CHUNK, (c + 1) * CHUNK)
        vc = vn[rows]
        mixed = jnp.zeros((CHUNK, CM_WIDTH), F32)
        for g in range(CM_GROUPS):
            mixed = jnp.where(lane_group == g, _dot(ws_ref[g], vc), mixed)
        cm_ref[rows, :] = u[rows] * (mixed + cmb_ref[...])

    mla = _dot(hb, wmla_ref[...])
    qn = _rms(mla[:, 0:MLA_RANK], qg_ref[...]).astype(BF16)
    ckvn = _rms(mla[:, MLA_RANK:2 * MLA_RANK], kvg_ref[...])
    ckvn_ref[...] = ckvn
    kr_ref[...] = _dot(hb, wkr_ref[...])
    cos = cos_ref[...]
    sin = sin_ref[...]
    kr_rot = mla[:, 2 * MLA_RANK:2 * MLA_RANK + HEAD_PAD] * cos + mla[:, 2 * MLA_RANK + HEAD_PAD:] * sin
    qa = _dot(qn, wuqa_ref[...])
    qb = _dot(qn, wuqb_ref[...])
    ckvb = ckvn.astype(BF16)
    kp = _dot(ckvb, wk_ref[...])
    for hd in range(MLA_HEADS):
        cols = slice(hd * HEAD_PAD, (hd + 1) * HEAD_PAD)
        q_ref[hd] = (qa[:, cols] * cos + qb[:, cols] * sin).astype(BF16)
        k_ref[hd] = (kp[:, cols] + kr_rot).astype(BF16)
    v_ref[...] = _dot(ckvb, wv_ref[...]).astype(BF16)


def _proj_call(x, mod, lw, cos_t, sin_t):
    nblk = T_ALL // TM
    full = lambda shape: pl.BlockSpec(shape, lambda i: (0,) * len(shape))
    rows = lambda w: pl.BlockSpec((TM, w), lambda i: (i, 0))
    in_specs = [
        rows(D_MODEL),
        pl.BlockSpec((1, 1, 6 * D_MODEL), lambda i: (_mod_row_of_block(i, TM), 0, 0)),
        full((1, D_MODEL)),
        full((D_MODEL, 4 * ML_WIDTH)),
        full((16, D_MODEL)),
        full((16, 1)),
        full((D_MODEL, 2 * CM_WIDTH)),
        full((D_MODEL, 2 * MLA_RANK + 2 * HEAD_PAD)),
        full((D_MODEL, MLA_ROPE)),
        full((1, CM_WIDTH)),
        full((CM_GROUPS, CHUNK, CHUNK)),
        full((CHUNK, CM_WIDTH)),
        full((1, MLA_RANK)),
        full((1, MLA_RANK)),
        full((MLA_RANK, MLA_HEADS * HEAD_PAD)),
        full((MLA_RANK, MLA_HEADS * HEAD_PAD)),
        full((MLA_RANK, MLA_HEADS * HEAD_PAD)),
        full((MLA_RANK, MLA_HEADS * MLA_V)),
        rows(HEAD_PAD),
        rows(HEAD_PAD),
    ]
    out_specs = [
        rows(4 * ML_WIDTH),
        pl.BlockSpec((16, TM), lambda i: (0, i)),
        rows(CM_WIDTH),
        rows(MLA_RANK),
        rows(MLA_ROPE),
        pl.BlockSpec((MLA_HEADS, TM, HEAD_PAD), lambda i: (0, i, 0)),
        pl.BlockSpec((MLA_HEADS, TM, HEAD_PAD), lambda i: (0, i, 0)),
        rows(MLA_HEADS * MLA_V),
    ]
    out_shape = [
        jax.ShapeDtypeStruct((T_ALL, 4 * ML_WIDTH), F32),
        jax.ShapeDtypeStruct((16, T_ALL), F32),
        jax.ShapeDtypeStruct((T_ALL, CM_WIDTH), F32),
        jax.ShapeDtypeStruct((T_ALL, MLA_RANK), F32),
        jax.ShapeDtypeStruct((T_ALL, MLA_ROPE), F32),
        jax.ShapeDtypeStruct((MLA_HEADS, T_ALL, HEAD_PAD), BF16),
        jax.ShapeDtypeStruct((MLA_HEADS, T_ALL, HEAD_PAD), BF16),
        jax.ShapeDtypeStruct((T_ALL, MLA_HEADS * MLA_V), BF16),
    ]
    return pl.pallas_call(
        _proj_kernel,
        grid=(nblk,),
        in_specs=in_specs,
        out_specs=out_specs,
        out_shape=out_shape,
        compiler_params=pltpu.CompilerParams(dimension_semantics=("parallel",), vmem_limit_bytes=VMEM_LIMIT),
        name="proj_in",
    )(x, mod, lw["g1"], lw["wml"], lw["wgt"], lw["gb_col"], lw["wcm"], lw["wmla"],
      lw["wkr"], lw["cmg"], lw["ws"], lw["cmb"], lw["qg"], lw["kvg"], lw["wuqa"], lw["wuqb"], lw["wk"], lw["wv"],
      cos_t, sin_t)


def _mlstm_schedule():
    seq, fwd, bwd, first = [], [], [], []
    base = 0
    for s in range(N_CTX_SEQ + N_SMP_SEQ):
        nc = (CTX_LEN if s < N_CTX_SEQ else SMP_LEN) // CHUNK
        for j in range(nc):
            seq.append(s)
            fwd.append(base + j)
            bwd.append(base + nc - 1 - j)
            first.append(1 if j == 0 else 0)
        base += nc
    as_i32 = lambda a: jnp.asarray(np.asarray(a, np.int32))
    return as_i32(seq), as_i32(fwd), as_i32(bwd), as_i32(first)


def _scan_cummax(x, direction):
    L = CHUNK
    rows = x.shape[0]
    x = jnp.concatenate([x, x], axis=0)
    lane = lax.broadcasted_iota(jnp.int32, x.shape, 1)
    k = 1
    while k < L:
        if direction == 0:
            shifted = jnp.where(lane >= k, pltpu.roll(x, k, axis=1), NEG_INF)
        else:
            shifted = jnp.where(lane < L - k, pltpu.roll(x, L - k, axis=1), NEG_INF)
        x = jnp.maximum(x, shifted)
        k *= 2
    return x[0:rows]


def _rows_to_lane_cols(rows, eye, rep, pieces):
    x = jnp.concatenate([jnp.broadcast_to(rows[h:h + 1, :], (rep, CHUNK)) for h in range(ML_HEADS)], axis=0)
    out = None
    for _ in range(pieces):
        xb = x.astype(BF16)
        part = _dot_nt(eye, xb)
        out = part if out is None else out + part
        x = x - xb.astype(F32)
    return out


def _per_head_lanes(x512, lane64):
    lo = jnp.where(lane64, x512[:, 0:128], x512[:, 128:256])
    hi = jnp.where(lane64, x512[:, 256:384], x512[:, 384:512])
    return jnp.concatenate([lo, hi], axis=1)


def _mlstm_direction(ml, g_row, direction, c_ref, n_ref, m_ref):
    L = CHUNK
    t_idx = lax.broadcasted_iota(jnp.int32, (L, L), 0)
    s_idx = lax.broadcasted_iota(jnp.int32, (L, L), 1)
    visible = (s_idx <= t_idx) if direction == 0 else (s_idx >= t_idx)
    tri = jnp.where(visible, 1.0, 0.0).astype(BF16)
    eye = jnp.where(s_idx == t_idx, 1.0, 0.0).astype(BF16)
    lane64 = lax.broadcasted_iota(jnp.int32, (1, 128), 1) < ML_DIM
    head_of_lane = lax.broadcasted_iota(jnp.int32, (1, ML_WIDTH), 1) >> 6
    same_head = ((lax.broadcasted_iota(jnp.int32, (ML_WIDTH, ML_WIDTH), 0) >> 6)
                 == (lax.broadcasted_iota(jnp.int32, (ML_WIDTH, ML_WIDTH), 1) >> 6))

    i0 = 8 * direction
    i_row = g_row[i0:i0 + ML_HEADS, :]
    lf_row = _log_sigmoid(g_row[i0 + ML_HEADS:i0 + 2 * ML_HEADS, :])
    r1, r2, r3 = _split3(lf_row)
    b_row = _dot_nt(r1, tri) + _dot_nt(r2, tri) + _dot_nt(r3, tri)
    m_rep = m_ref[0, direction, 0:ML_HEADS, :]
    a_row = i_row - b_row
    g_row_ = jnp.maximum(m_rep, _scan_cummax(a_row, direction))
    b_end = jnp.sum(lf_row, axis=1, keepdims=True)
    log_w = b_end - b_row + i_row
    m_new = jnp.maximum(b_end + m_rep, jnp.max(log_w, axis=1, keepdims=True))
    w_k_row = jnp.exp(log_w - m_new)
    decay_rep = jnp.exp(b_end + m_rep - m_new)

    g512 = _rows_to_lane_cols(g_row_, eye, 128, 2)
    g_full = _per_head_lanes(g512, lane64)
    b_full = _rows_to_lane_cols(b_row, eye, ML_DIM, 2)
    wk_full = _rows_to_lane_cols(w_k_row, eye, ML_DIM, 1)
    m_full = _per_head_lanes(jnp.concatenate([m_rep[h:h + 1, :] for h in range(ML_HEADS)], axis=1), lane64)
    decay_full = _per_head_lanes(jnp.concatenate([decay_rep[h:h + 1, :] for h in range(ML_HEADS)], axis=1),
                                 lane64)

    q = ml[:, 0:ML_WIDTH]
    k = ml[:, ML_WIDTH:2 * ML_WIDTH] * (ML_DIM ** -0.5)
    v = ml[:, 2 * ML_WIDTH:3 * ML_WIDTH]
    qb = q.astype(BF16)
    kb = k.astype(BF16)
    vb = v.astype(BF16)
    num = jnp.zeros((L, ML_WIDTH), F32)
    rowsum = jnp.zeros((L, ML_WIDTH), F32)
    for hd in range(ML_HEADS):
        w_intra = jnp.where(visible, jnp.exp(a_row[hd:hd + 1, :] - g512[:, hd * 128:(hd + 1) * 128]), 0.0)
        q_h = jnp.where(head_of_lane == hd, q, 0.0).astype(BF16)
        sw = _dot_nt(q_h, kb) * w_intra
        num = jnp.where(head_of_lane == hd, _dot(sw.astype(BF16), vb), num)
        rowsum = jnp.where(head_of_lane == hd, jnp.sum(sw, axis=-1, keepdims=True), rowsum)

    C = c_ref[0, direction]
    n_row = n_ref[0, direction, 0:1, :]
    w_inter = jnp.exp(m_full - g_full)
    block_ones = jnp.where(same_head, 1.0, 0.0).astype(BF16)
    qn = _dot((q * n_row).astype(BF16), block_ones)
    num = num + w_inter * _dot(qb, C.astype(BF16))
    den = rowsum + w_inter * qn
    h_out = num / jnp.maximum(jnp.abs(den), jnp.exp(-(b_full + g_full)))

    kw = wk_full * k
    c_ref[0, direction] = decay_full * C + jnp.where(same_head, _dot_tn(kw.astype(BF16), vb), 0.0)
    n_ref[0, direction, 0:1, :] = decay_full * n_row + jnp.sum(kw, axis=0, keepdims=True)
    m_ref[0, direction, 0:ML_HEADS, :] = m_new
    return h_out


def _mlstm_kernel(seq_ref, fwd_ref, bwd_ref, first_ref, mlf_ref, mlb_ref, grf_ref, grb_ref,
                  c0_ref, n0_ref, m0_ref, hf_ref, hb_ref, c_ref, n_ref, m_ref):
    step = pl.program_id(0)

    @pl.when(first_ref[step] == 1)
    def _():
        c_ref[...] = c0_ref[...]
        n_ref[...] = n0_ref[...]
        m_ref[...] = m0_ref[...]

    hf_ref[...] = _mlstm_direction(mlf_ref[...], grf_ref[...], 0, c_ref, n_ref, m_ref)
    hb_ref[...] = _mlstm_direction(mlb_ref[...], grb_ref[...], 1, c_ref, n_ref, m_ref)


def _mlstm_call(ml, gates_t, c0, n0, m0):
    seq, fwd, bwd, first = _mlstm_schedule()
    nseq = N_CTX_SEQ + N_SMP_SEQ
    nsteps = int(seq.shape[0])
    f_rows = lambda w: pl.BlockSpec((CHUNK, w), lambda i, sq, fw, bw, fs: (fw[i], 0))
    b_rows = lambda w: pl.BlockSpec((CHUNK, w), lambda i, sq, fw, bw, fs: (bw[i], 0))
    c_spec = pl.BlockSpec((1, 2, ML_WIDTH, ML_WIDTH), lambda i, sq, fw, bw, fs: (sq[i], 0, 0, 0))
    n_spec = pl.BlockSpec((1, 2, 8, ML_WIDTH), lambda i, sq, fw, bw, fs: (sq[i], 0, 0, 0))
    m_spec = pl.BlockSpec((1, 2, 8, 128), lambda i, sq, fw, bw, fs: (sq[i], 0, 0, 0))
    grid_spec = pltpu.PrefetchScalarGridSpec(
        num_scalar_prefetch=4,
        grid=(nsteps,),
        in_specs=[
            f_rows(4 * ML_WIDTH), b_rows(4 * ML_WIDTH),
            pl.BlockSpec((16, CHUNK), lambda i, sq, fw, bw, fs: (0, fw[i])),
            pl.BlockSpec((16, CHUNK), lambda i, sq, fw, bw, fs: (0, bw[i])),
            c_spec, n_spec, m_spec,
        ],
        out_specs=[f_rows(ML_WIDTH), b_rows(ML_WIDTH), c_spec, n_spec, m_spec],
    )
    return pl.pallas_call(
        _mlstm_kernel,
        grid_spec=grid_spec,
        out_shape=[
            jax.ShapeDtypeStruct((T_ALL, ML_WIDTH), F32),
            jax.ShapeDtypeStruct((T_ALL, ML_WIDTH), F32),
            jax.ShapeDtypeStruct((nseq, 2, ML_WIDTH, ML_WIDTH), F32),
            jax.ShapeDtypeStruct((nseq, 2, 8, ML_WIDTH), F32),
            jax.ShapeDtypeStruct((nseq, 2, 8, 128), F32),
        ],
        compiler_params=pltpu.CompilerParams(dimension_semantics=("arbitrary",), vmem_limit_bytes=VMEM_LIMIT),
        name="mlstm",
    )(seq, fwd, bwd, first, ml, ml, gates_t, gates_t, c0, n0, m0)


def _cache_kv_kernel(ckv_ref, kr_ref, wk_ref, wv_ref, place_ref, k_ref, v_ref):
    ckvb = ckv_ref[...].astype(BF16)
    kp = _dot(ckvb, wk_ref[...])
    kr128 = _dot(kr_ref[...].astype(BF16), place_ref[...])
    for hd in range(MLA_HEADS):
        k_ref[hd] = (kp[:, hd * HEAD_PAD:(hd + 1) * HEAD_PAD] + kr128).astype(BF16)
    v_ref[...] = _dot(ckvb, wv_ref[...]).astype(BF16)


def _cache_kv_call(ckv, kr, wk, wv, place):
    n = N_SMP_SEQ * PAST_LEN
    tb = PAST_LEN
    return pl.pallas_call(
        _cache_kv_kernel,
        grid=(n // tb,),
        in_specs=[
            pl.BlockSpec((tb, MLA_RANK), lambda i: (i, 0)),
            pl.BlockSpec((tb, MLA_ROPE), lambda i: (i, 0)),
            pl.BlockSpec((MLA_RANK, MLA_HEADS * HEAD_PAD), lambda i: (0, 0)),
            pl.BlockSpec((MLA_RANK, MLA_HEADS * MLA_V), lambda i: (0, 0)),
            pl.BlockSpec((MLA_ROPE, HEAD_PAD), lambda i: (0, 0)),
        ],
        out_specs=[
            pl.BlockSpec((MLA_HEADS, tb, HEAD_PAD), lambda i: (0, i, 0)),
            pl.BlockSpec((tb, MLA_HEADS * MLA_V), lambda i: (i, 0)),
        ],
        out_shape=[
            jax.ShapeDtypeStruct((MLA_HEADS, n, HEAD_PAD), BF16),
            jax.ShapeDtypeStruct((n, MLA_HEADS * MLA_V), BF16),
        ],
        compiler_params=pltpu.CompilerParams(dimension_semantics=("parallel",)),
        name="cache_kv",
    )(ckv, kr, wk, wv, place)


def _attn_kernel(*refs, has_cache):
    if has_cache:
        q_ref, kn_ref, vn_ref, kc_ref, vc_ref, o_ref = refs
    else:
        q_ref, kn_ref, vn_ref, o_ref = refs
    scale = (MLA_NOPE + MLA_ROPE) ** -0.5
    outs = []
    for hd in range(MLA_HEADS):
        lanes = slice(hd * MLA_V, (hd + 1) * MLA_V)
        q = q_ref[hd]
        s_n = _dot_nt(q, kn_ref[hd])
        m = jnp.max(s_n, axis=-1, keepdims=True)
        if has_cache:
            s_c = _dot_nt(q, kc_ref[hd])
            m = jnp.maximum(m, jnp.max(s_c, axis=-1, keepdims=True))
        p_n = jnp.exp((s_n - m) * scale)
        l = jnp.sum(p_n, axis=-1, keepdims=True)
        o = _dot(p_n.astype(BF16), vn_ref[:, lanes])
        if has_cache:
            p_c = jnp.exp((s_c - m) * scale)
            l = l + jnp.sum(p_c, axis=-1, keepdims=True)
            o = o + _dot(p_c.astype(BF16), vc_ref[:, lanes])
        outs.append(o / l)
    o_ref[...] = jnp.concatenate(outs, axis=-1)


def _attn_ctx_call(q, k, v):
    tq = CTX_LEN
    return pl.pallas_call(
        functools.partial(_attn_kernel, has_cache=False),
        grid=(N_CTX_SEQ,),
        in_specs=[
            pl.BlockSpec((MLA_HEADS, tq, HEAD_PAD), lambda s: (0, s, 0)),
            pl.BlockSpec((MLA_HEADS, tq, HEAD_PAD), lambda s: (0, s, 0)),
            pl.BlockSpec((tq, MLA_HEADS * MLA_V), lambda s: (s, 0)),
        ],
        out_specs=pl.BlockSpec((tq, MLA_HEADS * MLA_V), lambda s: (s, 0)),
        out_shape=jax.ShapeDtypeStruct((T_CTX, MLA_HEADS * MLA_V), F32),
        compiler_params=pltpu.CompilerParams(dimension_semantics=("parallel",), vmem_limit_bytes=VMEM_LIMIT),
        name="attn_ctx",
    )(q, k, v)


def _attn_smp_call(q, k, v, kc, vc):
    tq = 256
    qb_per_seq = SMP_LEN // tq
    ctx_qb = T_CTX // tq
    ctx_kb = T_CTX // SMP_LEN
    return pl.pallas_call(
        functools.partial(_attn_kernel, has_cache=True),
        grid=(N_SMP_SEQ, qb_per_seq),
        in_specs=[
            pl.BlockSpec((MLA_HEADS, tq, HEAD_PAD), lambda b, i: (0, ctx_qb + b * qb_per_seq + i, 0)),
            pl.BlockSpec((MLA_HEADS, SMP_LEN, HEAD_PAD), lambda b, i: (0, ctx_kb + b, 0)),
            pl.BlockSpec((SMP_LEN, MLA_HEADS * MLA_V), lambda b, i: (ctx_kb + b, 0)),
            pl.BlockSpec((MLA_HEADS, PAST_LEN, HEAD_PAD), lambda b, i: (0, b, 0)),
            pl.BlockSpec((PAST_LEN, MLA_HEADS * MLA_V), lambda b, i: (b, 0)),
        ],
        out_specs=pl.BlockSpec((tq, MLA_HEADS * MLA_V), lambda b, i: (b * qb_per_seq + i, 0)),
        out_shape=jax.ShapeDtypeStruct((T_SMP, MLA_HEADS * MLA_V), F32),
        compiler_params=pltpu.CompilerParams(dimension_semantics=("parallel", "parallel"),
                                             vmem_limit_bytes=VMEM_LIMIT),
        name="attn_smp",
    )(q, k, v, kc, vc)


def _out_kernel(x_ref, mod_ref, ml_ref, hf_ref, hb_ref, cm_ref, attc_ref, atts_ref, wo_ref, g2_ref, wq_ref,
                sk_ref, x1_ref, h2t_ref, st_ref):
    is_ctx = pl.program_id(0) < T_CTX // TM
    att = jnp.where(is_ctx, attc_ref[...], atts_ref[...])
    mod = mod_ref[0]
    g1 = mod[:, 2 * D_MODEL:3 * D_MODEL]
    sh2 = mod[:, 3 * D_MODEL:4 * D_MODEL]
    sc2 = mod[:, 4 * D_MODEL:5 * D_MODEL]
    mlo = _sigmoid(ml_ref[:, 3 * ML_WIDTH:4 * ML_WIDTH]) * (hf_ref[...] + hb_ref[...])
    mix = (_dot(mlo.astype(BF16), wo_ref[0:ML_WIDTH, :])
           + _dot(cm_ref[...].astype(BF16), wo_ref[ML_WIDTH:ML_WIDTH + CM_WIDTH, :])
           + _dot(att.astype(BF16), wo_ref[ML_WIDTH + CM_WIDTH:, :]))
    x1 = x_ref[...] + g1 * mix
    x1_ref[...] = x1
    h2f = _rms(x1, g2_ref[...]) * (1.0 + sc2) + sh2
    h2t_ref[...] = h2f.T.astype(BF16)
    qp = _dot(h2f.astype(BF16), wq_ref[...]).astype(BF16)
    for hh in range(2 * PEER_HEADS):
        st_ref[hh] = _dot_nt(sk_ref[hh % 2], qp[:, hh * 128:(hh + 1) * 128])


def _out_call(x, mod, ml, hf, hb, cm, att_ctx, att_smp, lw):
    nblk = T_ALL // TM
    ctx_blk = T_CTX // TM
    rows = lambda w: pl.BlockSpec((TM, w), lambda i: (i, 0))
    full = lambda shape: pl.BlockSpec(shape, lambda i: (0,) * len(shape))
    att_w = MLA_HEADS * MLA_V
    attc_spec = pl.BlockSpec((TM, att_w), lambda i: (jnp.minimum(i, ctx_blk - 1), 0))
    atts_spec = pl.BlockSpec((TM, att_w), lambda i: (jnp.maximum(i - ctx_blk, 0), 0))
    return pl.pallas_call(
        _out_kernel,
        grid=(nblk,),
        in_specs=[
            rows(D_MODEL),
            pl.BlockSpec((1, 1, 6 * D_MODEL), lambda i: (_mod_row_of_block(i, TM), 0, 0)),
            rows(4 * ML_WIDTH), rows(ML_WIDTH), rows(ML_WIDTH), rows(CM_WIDTH), attc_spec, atts_spec,
            full((D_MODEL, D_MODEL)), full((1, D_MODEL)), full((D_MODEL, 2 * PEER_HEADS * 128)),
            full((2, PEER_NKEYS, 128)),
        ],
        out_specs=[rows(D_MODEL), pl.BlockSpec((D_MODEL, TM), lambda i: (0, i)),
                   pl.BlockSpec((2 * PEER_HEADS, PEER_NKEYS, TM), lambda i: (0, 0, i))],
        out_shape=[
            jax.ShapeDtypeStruct((T_ALL, D_MODEL), F32),
            jax.ShapeDtypeStruct((D_MODEL, T_ALL), BF16),
            jax.ShapeDtypeStruct((2 * PEER_HEADS, PEER_NKEYS, T_ALL), F32),
        ],
        compiler_params=pltpu.CompilerParams(dimension_semantics=("parallel",), vmem_limit_bytes=VMEM_LIMIT),
        name="proj_out",
    )(x, mod, ml, hf, hb, cm, att_ctx, att_smp, lw["wo"], lw["g2"], lw["wq"], lw["sk"])


def _top16_rows(s, want_rank=False):
    vals = []
    rank = jnp.full(s.shape, float(PEER_TOPK), F32) if want_rank else None
    for k in range(PEER_TOPK):
        m = jnp.max(s, axis=0, keepdims=True)
        vals.append(m)
        hit = s == m
        if want_rank:
            rank = jnp.where(hit, float(k), rank)
        s = jnp.where(hit, NEG_INF, s)
    return vals, rank


def _pack_rows_bf16(x):
    r, n = x.shape
    x4 = x.reshape(r // 16, 2, 8, n)
    lo = x4[:, 0].reshape(r // 2, n)
    hi = x4[:, 1].reshape(r // 2, n)
    return pltpu.bitcast(pltpu.pack_elementwise([lo, hi], packed_dtype=BF16), BF16)


def _dup_bf16_words(x):
    u = pltpu.bitcast(x.astype(BF16).astype(F32), jnp.int32)
    return u | lax.shift_right_logical(u, jnp.full(u.shape, 16, jnp.int32))


def _rows_to_array(rows, row_iota):
    arr = jnp.zeros(row_iota.shape, F32)
    for i, r in enumerate(rows):
        arr = jnp.where(row_iota == i, r, arr)
    return arr


def _topk_kernel(st_ref, e1_ref, cut_ref, e2_ref, r2_ref):
    n = st_ref.shape[-1]
    row16 = lax.broadcasted_iota(jnp.int32, (PEER_TOPK, n), 0)

    def head(hd, carry):
        s1 = st_ref[2 * hd]
        s2 = st_ref[2 * hd + 1]
        v1, _ = _top16_rows(s1)
        v2, rank2 = _top16_rows(s2, want_rank=True)
        v1arr = _rows_to_array(v1, row16)
        v2arr = _rows_to_array(v2, row16)
        pieces = [v1[0] + v2arr] + [v1[p] + v2arr[0:8] for p in range(1, PEER_TOPK)]
        c, _ = _top16_rows(jnp.concatenate(pieces, axis=0))
        tau = c[PEER_TOPK - 1]
        z = jnp.zeros((1, n), F32)
        for ck in c:
            z = z + jnp.exp(ck - c[0])
        cut = jnp.zeros(s1.shape, F32)
        for q in range(PEER_TOPK):
            sigma = jnp.min(jnp.where(v1arr + v2[q] >= tau, v1arr, jnp.inf), axis=0, keepdims=True)
            cut = cut + jnp.where(s1 >= sigma, 1.0, 0.0)
        e1_ref[hd] = _dup_bf16_words(jnp.exp(s1 - v1[0]) / z)
        cut_ref[hd] = _dup_bf16_words(cut)
        e2_ref[hd] = _pack_rows_bf16(jnp.exp(s2 - v2[0]))
        r2_ref[hd] = _pack_rows_bf16(rank2)
        return carry

    lax.fori_loop(0, PEER_HEADS, head, 0)


def _topk_call(st):
    tn = TN_TOPK
    spec = pl.BlockSpec((PEER_HEADS, PEER_NKEYS, tn), lambda i: (0, 0, i))
    stat = lambda dt: jax.ShapeDtypeStruct((PEER_HEADS, PEER_NKEYS, T_ALL), dt)
    return pl.pallas_call(
        _topk_kernel,
        grid=(T_ALL // tn,),
        in_specs=[pl.BlockSpec((2 * PEER_HEADS, PEER_NKEYS, tn), lambda i: (0, 0, i))],
        out_specs=[spec, spec, spec, spec],
        out_shape=[stat(jnp.int32), stat(jnp.int32), stat(BF16), stat(BF16)],
        compiler_params=pltpu.CompilerParams(dimension_semantics=("parallel",), vmem_limit_bytes=VMEM_LIMIT),
        name="peer_topk",
    )(st)


def _row_tile_bf16(row):
    blk = pltpu.bitcast(jnp.broadcast_to(row, (8, row.shape[1])), BF16)
    return jnp.concatenate([blk] * (PEER_NKEYS // 16), axis=0)


def _gelu_tanh_bf16(x):
    z2 = x * (1.5957691216057308 + 0.07135481627159584 * (x * x))
    return x / (1.0 + jnp.exp(-z2))


def _expert_kernel(h2t_ref, u_ref, vt_ref, e1_ref, cut_ref, e2_ref, r2_ref, x1_ref, mod_ref, fg_ref,
                   o_ref, acc_ref, a_ref, *, final_norm):
    j = pl.program_id(1)
    n_tiles = pl.num_programs(1)
    tm = h2t_ref.shape[1]
    a_per_tile = TE_EXP // PEER_NKEYS

    @pl.when(j == 0)
    def _():
        acc_ref[...] = jnp.zeros_like(acc_ref)

    for sub in range(TE_EXP // EXP_SUB):
        srows = slice(sub * EXP_SUB, (sub + 1) * EXP_SUB)
        s = _pack_rows_bf16(_dot(u_ref[srows, :], h2t_ref[...]))
        g = _gelu_tanh_bf16(s)
        for ai in range(sub * EXP_SUB // PEER_NKEYS, (sub + 1) * EXP_SUB // PEER_NKEYS):
            a = j * a_per_tile + ai
            gate = jnp.zeros((PEER_NKEYS, tm), BF16)
            for hd in range(PEER_HEADS):
                cut_a = _row_tile_bf16(cut_ref[hd, pl.ds(a, 1), :])
                e1_a = _row_tile_bf16(e1_ref[hd, pl.ds(a, 1), :])
                gate = gate + jnp.where(r2_ref[hd] < cut_a, e2_ref[hd], jnp.zeros((), BF16)) * e1_a
            grow = ai * PEER_NKEYS - sub * EXP_SUB
            a_ref[ai * PEER_NKEYS:(ai + 1) * PEER_NKEYS, :] = gate * g[grow:grow + PEER_NKEYS, :]
    acc_ref[...] += _dot(vt_ref[...], a_ref[...])

    @pl.when(j == n_tiles - 1)
    def _():
        g2 = mod_ref[0][:, 5 * D_MODEL:6 * D_MODEL]
        y = x1_ref[...] + g2 * acc_ref[...].T
        if final_norm:
            y = _rms(y, fg_ref[...])
        o_ref[...] = y


def _expert_call(h2t, u_bf, vt_bf, e1, cut, e2, r2, x1, mod, final_g, final_norm):
    tm, te = TM_EXP, TE_EXP
    stat = pl.BlockSpec((PEER_HEADS, PEER_NKEYS, tm), lambda i, j: (0, 0, i))
    return pl.pallas_call(
        functools.partial(_expert_kernel, final_norm=final_norm),
        grid=(T_ALL // tm, PEER_EXPERTS // te),
        in_specs=[
            pl.BlockSpec((D_MODEL, tm), lambda i, j: (0, i)),
            pl.BlockSpec((te, D_MODEL), lambda i, j: (j, 0)),
            pl.BlockSpec((D_MODEL, te), lambda i, j: (0, j)),
            stat, stat, stat, stat,
            pl.BlockSpec((tm, D_MODEL), lambda i, j: (i, 0)),
            pl.BlockSpec((1, 1, 6 * D_MODEL), lambda i, j: (_mod_row_of_block(i, tm), 0, 0)),
            pl.BlockSpec((1, D_MODEL), lambda i, j: (0, 0)),
        ],
        out_specs=pl.BlockSpec((tm, D_MODEL), lambda i, j: (i, 0)),
        out_shape=jax.ShapeDtypeStruct((T_ALL, D_MODEL), F32),
        scratch_shapes=[
            pltpu.VMEM((D_MODEL, tm), F32),
            pltpu.VMEM((te, tm), BF16),
        ],
        compiler_params=pltpu.CompilerParams(dimension_semantics=("parallel", "arbitrary"),
                                             vmem_limit_bytes=VMEM_LIMIT),
        name="peer_experts",
    )(h2t, u_bf, vt_bf, e1, cut, e2, r2, x1, mod, final_g)


def _rope_swap_cols(w):
    return jnp.concatenate([-w[:, 8:16], w[:, 0:8], -w[:, 24:32], w[:, 16:24]], axis=1)


def _pad_heads(parts, n_heads):
    k = next(p[0].shape[0] for p in parts if p[0] is not None)
    cols = []
    for hd in range(n_heads):
        for arr, w in parts:
            cols.append(jnp.zeros((k, w), F32) if arr is None else arr[:, hd * w:(hd + 1) * w])
    return jnp.concatenate(cols, axis=1)


def _rope_tables():
    pos = np.arange(SMP_LEN)
    freqs = ROPE_THETA ** (-np.arange(0, ROPE_AXIS, 2, dtype=np.float32) / ROPE_AXIS)
    ang_r = (pos // GRID_W).astype(np.float32)[:, None] * freqs
    ang_c = (pos % GRID_W).astype(np.float32)[:, None] * freqs
    ang = jnp.asarray(np.concatenate([ang_r, ang_r, ang_c, ang_c], axis=1).astype(np.float32))
    cos32 = jnp.cos(ang)
    sin32 = jnp.sin(ang)
    ones = jnp.ones((SMP_LEN, MLA_NOPE), F32)
    cos_s = jnp.concatenate([ones, cos32, ones[:, :HEAD_PAD - MLA_NOPE - MLA_ROPE]], axis=1)
    sin_s = jnp.concatenate([0 * ones, sin32, 0 * ones[:, :HEAD_PAD - MLA_NOPE - MLA_ROPE]], axis=1)
    cos_t = jnp.concatenate([jnp.ones((T_CTX, HEAD_PAD), F32)] + [cos_s] * N_SMP_SEQ, axis=0)
    sin_t = jnp.concatenate([jnp.zeros((T_CTX, HEAD_PAD), F32)] + [sin_s] * N_SMP_SEQ, axis=0)
    return cos_t, sin_t


def _layer_weights(l, norm1_g, w_in, mlstm_gate_b, cm_norm_g, cm_ws, cm_b, mla_q_norm_g, mla_w_uq, mla_kv_norm_g,
                   mla_w_ukv, w_out, norm2_g, peer_w_q, peer_subkeys, peer_u, peer_v):
    w = w_in[l]
    o_g = 4 * ML_WIDTH
    o_cm = o_g + 16
    o_cq = o_cm + 2 * CM_WIDTH
    o_ckv = o_cq + MLA_RANK
    o_kr = o_ckv + MLA_RANK
    w_kr = w[:, o_kr:o_kr + MLA_ROPE]
    zeros_l = jnp.zeros((D_MODEL, MLA_NOPE), F32)
    zeros_r = jnp.zeros((D_MODEL, HEAD_PAD - MLA_NOPE - MLA_ROPE), F32)
    kr128 = jnp.concatenate([zeros_l, w_kr, zeros_r], axis=1)
    krsw128 = jnp.concatenate([zeros_l, _rope_swap_cols(w_kr), zeros_r], axis=1)
    uq = mla_w_uq[l].reshape(MLA_RANK, MLA_HEADS, MLA_NOPE + MLA_ROPE)
    uq_nope = uq[:, :, :MLA_NOPE].reshape(MLA_RANK, -1)
    uq_rope = uq[:, :, MLA_NOPE:].reshape(MLA_RANK, -1)
    uq_rope_sw = jnp.concatenate(
        [_rope_swap_cols(uq_rope[:, hd * MLA_ROPE:(hd + 1) * MLA_ROPE]) for hd in range(MLA_HEADS)], axis=1)
    pad_w = HEAD_PAD - MLA_NOPE - MLA_ROPE
    ukv = mla_w_ukv[l].reshape(MLA_RANK, MLA_HEADS, MLA_NOPE + MLA_V)
    uk = ukv[:, :, :MLA_NOPE].reshape(MLA_RANK, -1)
    uv = ukv[:, :, MLA_NOPE:].reshape(MLA_RANK, -1)
    gb = mlstm_gate_b[l]
    return {
        "g1": norm1_g[l].reshape(1, D_MODEL),
        "wml": w[:, 0:o_g].astype(BF16),
        "wgt": w[:, o_g:o_cm].T.astype(BF16),
        "gb_col": gb.reshape(16, 1),
        "wcm": w[:, o_cm:o_cq].astype(BF16),
        "wmla": jnp.concatenate([w[:, o_cq:o_kr], kr128, krsw128], axis=1).astype(BF16),
        "wkr": w_kr.astype(BF16),
        "cmg": cm_norm_g[l].reshape(1, CM_WIDTH),
        "ws": cm_ws[l].astype(BF16),
        "cmb": jnp.repeat(cm_b[l].T, CM_WIDTH // CM_GROUPS, axis=1),
        "qg": mla_q_norm_g[l].reshape(1, MLA_RANK),
        "kvg": mla_kv_norm_g[l].reshape(1, MLA_RANK),
        "wuqa": _pad_heads([(uq_nope, MLA_NOPE), (uq_rope, MLA_ROPE), (None, pad_w)], MLA_HEADS).astype(BF16),
        "wuqb": _pad_heads([(None, MLA_NOPE), (uq_rope_sw, MLA_ROPE), (None, pad_w)], MLA_HEADS).astype(BF16),
        "wk": _pad_heads([(uk, MLA_NOPE), (None, HEAD_PAD - MLA_NOPE)], MLA_HEADS).astype(BF16),
        "wv": uv.astype(BF16),
        "wo": w_out[l].astype(BF16),
        "g2": norm2_g[l].reshape(1, D_MODEL),
        "wq": peer_w_q[l].astype(BF16),
        "sk": peer_subkeys[l].astype(BF16),
        "u": peer_u[l].astype(BF16),
        "vt": peer_v[l].reshape(PEER_EXPERTS // 16, 2, 8, D_MODEL).swapaxes(1, 2).reshape(PEER_EXPERTS, D_MODEL)
              .T.astype(BF16),
    }


def kernel(x_prompt, x_sample, c, cache_mla_ckv, cache_mla_krope, state_mlstm_C, state_mlstm_n, state_mlstm_m, c_ctx, norm1_g, ada_w, ada_b, w_in, mlstm_gate_b, cm_norm_g, cm_ws, cm_b, mla_q_norm_g, mla_w_uq, mla_kv_norm_g, mla_w_ukv, w_out, norm2_g, peer_w_q, peer_subkeys, peer_u, peer_v, final_g):
    x = jnp.concatenate([x_prompt.reshape(T_CTX, D_MODEL), x_sample.reshape(T_SMP, D_MODEL)], axis=0)
    cvecs = jnp.concatenate([c_ctx[None, :], c, jnp.zeros((N_MOD_ROWS - 1 - N_SMP_SEQ, D_MODEL), F32)], axis=0)
    mod_all = _ada_call(cvecs, ada_w, ada_b)
    cos_t, sin_t = _rope_tables()
    place = jnp.concatenate([jnp.zeros((MLA_ROPE, MLA_NOPE), F32), jnp.eye(MLA_ROPE, dtype=F32),
                             jnp.zeros((MLA_ROPE, HEAD_PAD - MLA_NOPE - MLA_ROPE), F32)], axis=1).astype(BF16)
    final_g2 = final_g.reshape(1, D_MODEL)

    ckvs, krs, Cs, ns, ms = [], [], [], [], []
    for l in range(DEPTH):
        lw = _layer_weights(l, norm1_g, w_in, mlstm_gate_b, cm_norm_g, cm_ws, cm_b, mla_q_norm_g, mla_w_uq,
                            mla_kv_norm_g, mla_w_ukv, w_out, norm2_g, peer_w_q, peer_subkeys, peer_u, peer_v)
        mod = mod_all[l].reshape(N_MOD_ROWS, 1, 6 * D_MODEL)
        ml, gates_t, cm, ckvn, kr, q, k, v = _proj_call(x, mod, lw, cos_t, sin_t)

        c_blk = jnp.einsum('bdhij,hg->bdhigj', state_mlstm_C[:, l], jnp.eye(ML_HEADS, dtype=F32))
        c0 = jnp.concatenate([jnp.zeros((N_CTX_SEQ, 2, ML_WIDTH, ML_WIDTH), F32),
                              c_blk.reshape(N_SMP_SEQ, 2, ML_WIDTH, ML_WIDTH)], axis=0)
        n0 = jnp.concatenate([jnp.zeros((N_CTX_SEQ, 2, 1, ML_WIDTH), F32),
                              state_mlstm_n[:, l].reshape(N_SMP_SEQ, 2, 1, ML_WIDTH)], axis=0)
        n0 = jnp.pad(n0, ((0, 0), (0, 0), (0, 7), (0, 0)))
        m0 = jnp.concatenate([jnp.zeros((N_CTX_SEQ, 2, ML_HEADS), F32), state_mlstm_m[:, l]], axis=0)
        m0 = jnp.pad(jnp.broadcast_to(m0[..., None], m0.shape + (128,)), ((0, 0), (0, 0), (0, 8 - ML_HEADS), (0, 0)))
        hf, hb, c_fin, n_fin, m_fin = _mlstm_call(ml, gates_t, c0, n0, m0)

        kc, vc = _cache_kv_call(cache_mla_ckv[:, l].reshape(N_SMP_SEQ * PAST_LEN, MLA_RANK),
                                cache_mla_krope[:, l].reshape(N_SMP_SEQ * PAST_LEN, MLA_ROPE),
                                lw["wk"], lw["wv"], place)
        att_ctx = _attn_ctx_call(q, k, v)
        att_smp = _attn_smp_call(q, k, v, kc, vc)

        x1, h2t, st = _out_call(x, mod, ml, hf, hb, cm, att_ctx, att_smp, lw)
        e1, cut, e2, r2 = _topk_call(st)
        x = _expert_call(h2t, lw["u"], lw["vt"], e1, cut, e2, r2, x1, mod, final_g2, final_norm=(l == DEPTH - 1))

        ckvs.append(ckvn[:T_CTX].reshape(N_CTX_SEQ, CTX_LEN, MLA_RANK))
        krs.append(kr[:T_CTX].reshape(N_CTX_SEQ, CTX_LEN, MLA_ROPE))
        Cs.append(jnp.stack([c_fin[:N_CTX_SEQ, :, hd * ML_DIM:(hd + 1) * ML_DIM, hd * ML_DIM:(hd + 1) * ML_DIM]
                             for hd in range(ML_HEADS)], axis=2))
        ns.append(n_fin[:N_CTX_SEQ, :, 0, :].reshape(N_CTX_SEQ, 2, ML_HEADS, ML_DIM))
        ms.append(m_fin[:N_CTX_SEQ, :, 0:ML_HEADS, 0])

    y_prompt = x[:T_CTX].reshape(N_CTX_SEQ, CTX_LEN, D_MODEL)
    y_sample = x[T_CTX:].reshape(N_SMP_SEQ, SMP_LEN, D_MODEL)
    return (y_prompt, y_sample, jnp.stack(ckvs, axis=1), jnp.stack(krs, axis=1), jnp.stack(Cs, axis=1),
            jnp.stack(ns, axis=1), jnp.stack(ms, axis=1))
```

```python
import functools

import numpy as np
import jax
import jax.numpy as jnp
from jax import lax
from jax.experimental import pallas as pl
from jax.experimental.pallas import tpu as pltpu

F32 = jnp.float32
BF16 = jnp.bfloat16

D_MODEL = 1024
N_CTX_SEQ = 16
CTX_LEN = 256
N_SMP_SEQ = 4
SMP_LEN = 2048
PAST_LEN = 256
DEPTH = 2
GRID_W = 64
EPS = 1e-6
T_CTX = N_CTX_SEQ * CTX_LEN
T_SMP = N_SMP_SEQ * SMP_LEN
T_ALL = T_CTX + T_SMP
N_MOD_ROWS = 8

ML_HEADS = 4
ML_DIM = 64
ML_WIDTH = 256
CHUNK = 128
CM_GROUPS = 4
CM_WIDTH = 256
MLA_HEADS = 8
MLA_NOPE = 64
MLA_ROPE = 32
MLA_V = 64
MLA_RANK = 256
HEAD_PAD = 128
ROPE_AXIS = 16
ROPE_THETA = 10000.0
PEER_HEADS = 8
PEER_NKEYS = 128
PEER_EXPERTS = PEER_NKEYS * PEER_NKEYS
PEER_TOPK = 16

TM = 256
TN_TOPK = 256
TM_EXP = 512
TE_EXP = 2048
EXP_SUB = 512
VMEM_LIMIT = 56 * 1024 * 1024

NEG_INF = float("-inf")


def _dot(a, b):
    return jnp.dot(a, b, preferred_element_type=F32)


def _dot_nt(a, b):
    return lax.dot_general(a, b, (((1,), (1,)), ((), ())), preferred_element_type=F32)


def _dot_tn(a, b):
    return lax.dot_general(a, b, (((0,), (0,)), ((), ())), preferred_element_type=F32)


def _split3(a):
    a1 = a.astype(BF16)
    r1 = a - a1.astype(F32)
    a2 = r1.astype(BF16)
    a3 = (r1 - a2.astype(F32)).astype(BF16)
    return a1, a2, a3


def _rms(x, g):
    return x * lax.rsqrt(jnp.mean(x * x, axis=-1, keepdims=True) + EPS) * g


def _sigmoid(x):
    return 1.0 / (1.0 + jnp.exp(-x))


def _log_sigmoid(x):
    return jnp.minimum(x, 0.0) - jnp.log(1.0 + jnp.exp(-jnp.abs(x)))


def _gelu_tanh(x):
    return 0.5 * x * (1.0 + jnp.tanh(0.7978845608028654 * (x + 0.044715 * (x * x * x))))


def _mod_row_of_block(i, rows_per_block):
    ctx_blocks = T_CTX // rows_per_block
    per_seq = SMP_LEN // rows_per_block
    return jnp.where(i < ctx_blocks, 0, 1 + (i - ctx_blocks) // per_seq)


def _ada_kernel(cv_ref, w_ref, b_ref, o_ref):
    cv = cv_ref[...]
    s = cv * _sigmoid(cv)
    w = w_ref[0]
    w1, w2, w3 = _split3(w)
    s1, s2, s3 = _split3(s)
    acc = _dot(s1, w1) + (_dot(s1, w2) + _dot(s2, w1)) + (_dot(s1, w3) + _dot(s2, w2) + _dot(s3, w1))
    o_ref[0] = acc + b_ref[0]


def _ada_call(cvecs, ada_w, ada_b):
    tn = 1024
    return pl.pallas_call(
        _ada_kernel,
        grid=(DEPTH, 6 * D_MODEL // tn),
        in_specs=[
            pl.BlockSpec((N_MOD_ROWS, D_MODEL), lambda l, j: (0, 0)),
            pl.BlockSpec((1, D_MODEL, tn), lambda l, j: (l, 0, j)),
            pl.BlockSpec((1, 1, tn), lambda l, j: (l, 0, j)),
        ],
        out_specs=pl.BlockSpec((1, N_MOD_ROWS, tn), lambda l, j: (l, 0, j)),
        out_shape=jax.ShapeDtypeStruct((DEPTH, N_MOD_ROWS, 6 * D_MODEL), F32),
        compiler_params=pltpu.CompilerParams(dimension_semantics=("parallel", "parallel")),
        name="ada_mod",
    )(cvecs, ada_w, ada_b.reshape(DEPTH, 1, 6 * D_MODEL))


def _x_specs(x_ctx, x_smp):
    ctx_blk = T_CTX // TM
    smp_off = x_smp.shape[0] // TM - T_SMP // TM
    return [pl.BlockSpec((TM, D_MODEL), lambda i: (jnp.minimum(i, ctx_blk - 1), 0)),
            pl.BlockSpec((TM, D_MODEL), lambda i: (jnp.maximum(i - ctx_blk, 0) + smp_off, 0))]


def _select_x(xc_ref, xs_ref):
    return jnp.where(pl.program_id(0) < T_CTX // TM, xc_ref[...], xs_ref[...])


def _proj_kernel(xc_ref, xs_ref, mod_ref, g1_ref, wml_ref, wgt_ref, gbc_ref, wcm_ref, wmla_ref, wkr_ref,
                 cmg_ref, ws_ref, cmb_ref, qg_ref, kvg_ref, wuqa_ref, wuqb_ref, wk_ref, wv_ref, cos_ref, sin_ref,
                 ml_ref, gatest_ref, cm_ref, ckvn_ref, kr_ref, q_ref, k_ref, v_ref):
    x = _select_x(xc_ref, xs_ref)
    mod = mod_ref[0]
    sh1 = mod[:, 0:D_MODEL]
    sc1 = mod[:, D_MODEL:2 * D_MODEL]
    h = _rms(x, g1_ref[...]) * (1.0 + sc1) + sh1
    hb = h.astype(BF16)

    ml_ref[...] = _dot(hb, wml_ref[...])
    gatest_ref[...] = _dot_nt(wgt_ref[...], hb) + gbc_ref[...]

    cm = _dot(hb, wcm_ref[...])
    u = cm[:, 0:CM_WIDTH]
    vn = _rms(cm[:, CM_WIDTH:2 * CM_WIDTH], cmg_ref[...]).astype(BF16)
    lane_group = lax.broadcasted_iota(jnp.int32, (CHUNK, CM_WIDTH), 1) >> 6
    for c in range(TM // CHUNK):
        rows = slice(c * CHUNK, (c + 1) * CHUNK)
        vc = vn[rows]
        mixed = jnp.zeros((CHUNK, CM_WIDTH), F32)
        for g in range(CM_GROUPS):
            mixed = jnp.where(lane_group == g, _dot(ws_ref[g], vc), mixed)
        cm_ref[rows, :] = u[rows] * (mixed + cmb_ref[...])

    mla = _dot(hb, wmla_ref[...])
    qn = _rms(mla[:, 0:MLA_RANK], qg_ref[...]).astype(BF16)
    ckvn = _rms(mla[:, MLA_RANK:2 * MLA_RANK], kvg_ref[...])
    ckvn_ref[...] = ckvn
    kr_ref[...] = _dot(hb, wkr_ref[...])
    cos = cos_ref[...]
    sin = sin_ref[...]
    kr_rot = mla[:, 2 * MLA_RANK:2 * MLA_RANK + HEAD_PAD] * cos + mla[:, 2 * MLA_RANK + HEAD_PAD:] * sin
    qa = _dot(qn, wuqa_ref[...])
    qb = _dot(qn, wuqb_ref[...])
    ckvb = ckvn.astype(BF16)
    kp = _dot(ckvb, wk_ref[...])
    for hd in range(MLA_HEADS):
        cols = slice(hd * HEAD_PAD, (hd + 1) * HEAD_PAD)
        q_ref[hd] = (qa[:, cols] * cos + qb[:, cols] * sin).astype(BF16)
        k_ref[hd] = (kp[:, cols] + kr_rot).astype(BF16)
    v_ref[...] = _dot(ckvb, wv_ref[...]).astype(BF16)


def _proj_call(x_ctx, x_smp, mod, lw, cos_t, sin_t):
    nblk = T_ALL // TM
    full = lambda shape: pl.BlockSpec(shape, lambda i: (0,) * len(shape))
    rows = lambda w: pl.BlockSpec((TM, w), lambda i: (i, 0))
    in_specs = _x_specs(x_ctx, x_smp) + [
        pl.BlockSpec((1, 1, 6 * D_MODEL), lambda i: (_mod_row_of_block(i, TM), 0, 0)),
        full((1, D_MODEL)),
        full((D_MODEL, 4 * ML_WIDTH)),
        full((16, D_MODEL)),
        full((16, 1)),
        full((D_MODEL, 2 * CM_WIDTH)),
        full((D_MODEL, 2 * MLA_RANK + 2 * HEAD_PAD)),
        full((D_MODEL, MLA_ROPE)),
        full((1, CM_WIDTH)),
        full((CM_GROUPS, CHUNK, CHUNK)),
        full((CHUNK, CM_WIDTH)),
        full((1, MLA_RANK)),
        full((1, MLA_RANK)),
        full((MLA_RANK, MLA_HEADS * HEAD_PAD)),
        full((MLA_RANK, MLA_HEADS * HEAD_PAD)),
        full((MLA_RANK, MLA_HEADS * HEAD_PAD)),
        full((MLA_RANK, MLA_HEADS * MLA_V)),
        rows(HEAD_PAD),
        rows(HEAD_PAD),
    ]
    out_specs = [
        rows(4 * ML_WIDTH),
        pl.BlockSpec((16, TM), lambda i: (0, i)),
        rows(CM_WIDTH),
        rows(MLA_RANK),
        rows(MLA_ROPE),
        pl.BlockSpec((MLA_HEADS, TM, HEAD_PAD), lambda i: (0, i, 0)),
        pl.BlockSpec((MLA_HEADS, TM, HEAD_PAD), lambda i: (0, i, 0)),
        rows(MLA_HEADS * MLA_V),
    ]
    out_shape = [
        jax.ShapeDtypeStruct((T_ALL, 4 * ML_WIDTH), F32),
        jax.ShapeDtypeStruct((16, T_ALL), F32),
        jax.ShapeDtypeStruct((T_ALL, CM_WIDTH), F32),
        jax.ShapeDtypeStruct((T_ALL, MLA_RANK), F32),
        jax.ShapeDtypeStruct((T_ALL, MLA_ROPE), F32),
        jax.ShapeDtypeStruct((MLA_HEADS, T_ALL, HEAD_PAD), BF16),
        jax.ShapeDtypeStruct((MLA_HEADS, T_ALL, HEAD_PAD), BF16),
        jax.ShapeDtypeStruct((T_ALL, MLA_HEADS * MLA_V), BF16),
    ]
    return pl.pallas_call(
        _proj_kernel,
        grid=(nblk,),
        in_specs=in_specs,
        out_specs=out_specs,
        out_shape=out_shape,
        compiler_params=pltpu.CompilerParams(dimension_semantics=("parallel",), vmem_limit_bytes=VMEM_LIMIT),
        name="proj_in",
    )(x_ctx, x_smp, mod, lw["g1"], lw["wml"], lw["wgt"], lw["gb_col"], lw["wcm"], lw["wmla"],
      lw["wkr"], lw["cmg"], lw["ws"], lw["cmb"], lw["qg"], lw["kvg"], lw["wuqa"], lw["wuqb"], lw["wk"], lw["wv"],
      cos_t, sin_t)


def _mlstm_schedule():
    seq, fwd, bwd, first = [], [], [], []
    base = 0
    for s in range(N_CTX_SEQ + N_SMP_SEQ):
        nc = (CTX_LEN if s < N_CTX_SEQ else SMP_LEN) // CHUNK
        for j in range(nc):
            seq.append(s)
            fwd.append(base + j)
            bwd.append(base + nc - 1 - j)
            first.append(1 if j == 0 else 0)
        base += nc
    as_i32 = lambda a: jnp.asarray(np.asarray(a, np.int32))
    return as_i32(seq), as_i32(fwd), as_i32(bwd), as_i32(first)


def _scan_cummax(x, direction):
    L = CHUNK
    rows = x.shape[0]
    x = jnp.concatenate([x, x], axis=0)
    lane = lax.broadcasted_iota(jnp.int32, x.shape, 1)
    k = 1
    while k < L:
        if direction == 0:
            shifted = jnp.where(lane >= k, pltpu.roll(x, k, axis=1), NEG_INF)
        else:
            shifted = jnp.where(lane < L - k, pltpu.roll(x, L - k, axis=1), NEG_INF)
        x = jnp.maximum(x, shifted)
        k *= 2
    return x[0:rows]


def _rows_to_lane_cols(rows, eye, rep, pieces):
    x = jnp.concatenate([jnp.broadcast_to(rows[h:h + 1, :], (rep, CHUNK)) for h in range(ML_HEADS)], axis=0)
    out = None
    for _ in range(pieces):
        xb = x.astype(BF16)
        part = _dot_nt(eye, xb)
        out = part if out is None else out + part
        x = x - xb.astype(F32)
    return out


def _per_head_lanes(x512, lane64):
    lo = jnp.where(lane64, x512[:, 0:128], x512[:, 128:256])
    hi = jnp.where(lane64, x512[:, 256:384], x512[:, 384:512])
    return jnp.concatenate([lo, hi], axis=1)


def _mlstm_direction(ml, g_row, direction, c_ref, n_ref, m_ref):
    L = CHUNK
    t_idx = lax.broadcasted_iota(jnp.int32, (L, L), 0)
    s_idx = lax.broadcasted_iota(jnp.int32, (L, L), 1)
    visible = (s_idx <= t_idx) if direction == 0 else (s_idx >= t_idx)
    tri = jnp.where(visible, 1.0, 0.0).astype(BF16)
    eye = jnp.where(s_idx == t_idx, 1.0, 0.0).astype(BF16)
    lane64 = lax.broadcasted_iota(jnp.int32, (1, 128), 1) < ML_DIM
    head_of_lane = lax.broadcasted_iota(jnp.int32, (1, ML_WIDTH), 1) >> 6
    same_head = ((lax.broadcasted_iota(jnp.int32, (ML_WIDTH, ML_WIDTH), 0) >> 6)
                 == (lax.broadcasted_iota(jnp.int32, (ML_WIDTH, ML_WIDTH), 1) >> 6))

    i0 = 8 * direction
    i_row = g_row[i0:i0 + ML_HEADS, :]
    lf_row = _log_sigmoid(g_row[i0 + ML_HEADS:i0 + 2 * ML_HEADS, :])
    r1, r2, r3 = _split3(lf_row)
    b_row = _dot_nt(r1, tri) + _dot_nt(r2, tri) + _dot_nt(r3, tri)
    m_rep = m_ref[0, direction, 0:ML_HEADS, :]
    a_row = i_row - b_row
    g_row_ = jnp.maximum(m_rep, _scan_cummax(a_row, direction))
    b_end = jnp.sum(lf_row, axis=1, keepdims=True)
    log_w = b_end - b_row + i_row
    m_new = jnp.maximum(b_end + m_rep, jnp.max(log_w, axis=1, keepdims=True))
    w_k_row = jnp.exp(log_w - m_new)
    decay_rep = jnp.exp(b_end + m_rep - m_new)

    g512 = _rows_to_lane_cols(g_row_, eye, 128, 2)
    g_full = _per_head_lanes(g512, lane64)
    b_full = _rows_to_lane_cols(b_row, eye, ML_DIM, 2)
    wk_full = _rows_to_lane_cols(w_k_row, eye, ML_DIM, 1)
    m_full = _per_head_lanes(jnp.concatenate([m_rep[h:h + 1, :] for h in range(ML_HEADS)], axis=1), lane64)
    decay_full = _per_head_lanes(jnp.concatenate([decay_rep[h:h + 1, :] for h in range(ML_HEADS)], axis=1),
                                 lane64)

    q = ml[:, 0:ML_WIDTH]
    k = ml[:, ML_WIDTH:2 * ML_WIDTH] * (ML_DIM ** -0.5)
    v = ml[:, 2 * ML_WIDTH:3 * ML_WIDTH]
    qb = q.astype(BF16)
    kb = k.astype(BF16)
    vb = v.astype(BF16)
    num = jnp.zeros((L, ML_WIDTH), F32)
    rowsum = jnp.zeros((L, ML_WIDTH), F32)
    for hd in range(ML_HEADS):
        w_intra = jnp.where(visible, jnp.exp(a_row[hd:hd + 1, :] - g512[:, hd * 128:(hd + 1) * 128]), 0.0)
        q_h = jnp.where(head_of_lane == hd, q, 0.0).astype(BF16)
        sw = _dot_nt(q_h, kb) * w_intra
        num = jnp.where(head_of_lane == hd, _dot(sw.astype(BF16), vb), num)
        rowsum = jnp.where(head_of_lane == hd, jnp.sum(sw, axis=-1, keepdims=True), rowsum)

    C = c_ref[0, direction]
    n_row = n_ref[0, direction, 0:1, :]
    w_inter = jnp.exp(m_full - g_full)
    block_ones = jnp.where(same_head, 1.0, 0.0).astype(BF16)
    qn = _dot((q * n_row).astype(BF16), block_ones)
    num = num + w_inter * _dot(qb, C.astype(BF16))
    den = rowsum + w_inter * qn
    h_out = num / jnp.maximum(jnp.abs(den), jnp.exp(-(b_full + g_full)))

    kw = wk_full * k
    c_ref[0, direction] = decay_full * C + jnp.where(same_head, _dot_tn(kw.astype(BF16), vb), 0.0)
    n_ref[0, direction, 0:1, :] = decay_full * n_row + jnp.sum(kw, axis=0, keepdims=True)
    m_ref[0, direction, 0:ML_HEADS, :] = m_new
    return h_out


def _mlstm_kernel(seq_ref, fwd_ref, bwd_ref, first_ref, mlf_ref, mlb_ref, grf_ref, grb_ref,
                  c0_ref, n0_ref, m0_ref, hf_ref, hb_ref, c_ref, n_ref, m_ref):
    step = pl.program_id(0)

    @pl.when(first_ref[step] == 1)
    def _():
        c_ref[...] = c0_ref[...]
        n_ref[...] = n0_ref[...]
        m_ref[...] = m0_ref[...]

    hf_ref[...] = _mlstm_direction(mlf_ref[...], grf_ref[...], 0, c_ref, n_ref, m_ref)
    hb_ref[...] = _mlstm_direction(mlb_ref[...], grb_ref[...], 1, c_ref, n_ref, m_ref)


def _mlstm_call(ml, gates_t, c0, n0, m0):
    seq, fwd, bwd, first = _mlstm_schedule()
    nseq = N_CTX_SEQ + N_SMP_SEQ
    nsteps = int(seq.shape[0])
    f_rows = lambda w: pl.BlockSpec((CHUNK, w), lambda i, sq, fw, bw, fs: (fw[i], 0))
    b_rows = lambda w: pl.BlockSpec((CHUNK, w), lambda i, sq, fw, bw, fs: (bw[i], 0))
    c_spec = pl.BlockSpec((1, 2, ML_WIDTH, ML_WIDTH), lambda i, sq, fw, bw, fs: (sq[i], 0, 0, 0))
    n_spec = pl.BlockSpec((1, 2, 8, ML_WIDTH), lambda i, sq, fw, bw, fs: (sq[i], 0, 0, 0))
    m_spec = pl.BlockSpec((1, 2, 8, 128), lambda i, sq, fw, bw, fs: (sq[i], 0, 0, 0))
    grid_spec = pltpu.PrefetchScalarGridSpec(
        num_scalar_prefetch=4,
        grid=(nsteps,),
        in_specs=[
            f_rows(4 * ML_WIDTH), b_rows(4 * ML_WIDTH),
            pl.BlockSpec((16, CHUNK), lambda i, sq, fw, bw, fs: (0, fw[i])),
            pl.BlockSpec((16, CHUNK), lambda i, sq, fw, bw, fs: (0, bw[i])),
            c_spec, n_spec, m_spec,
        ],
        out_specs=[f_rows(ML_WIDTH), b_rows(ML_WIDTH), c_spec, n_spec, m_spec],
    )
    return pl.pallas_call(
        _mlstm_kernel,
        grid_spec=grid_spec,
        out_shape=[
            jax.ShapeDtypeStruct((T_ALL, ML_WIDTH), F32),
            jax.ShapeDtypeStruct((T_ALL, ML_WIDTH), F32),
            jax.ShapeDtypeStruct((nseq, 2, ML_WIDTH, ML_WIDTH), F32),
            jax.ShapeDtypeStruct((nseq, 2, 8, ML_WIDTH), F32),
            jax.ShapeDtypeStruct((nseq, 2, 8, 128), F32),
        ],
        compiler_params=pltpu.CompilerParams(dimension_semantics=("arbitrary",), vmem_limit_bytes=VMEM_LIMIT),
        name="mlstm",
    )(seq, fwd, bwd, first, ml, ml, gates_t, gates_t, c0, n0, m0)


def _cache_kv_kernel(ckv_ref, kr_ref, wk_ref, wv_ref, place_ref, k_ref, v_ref):
    ckvb = ckv_ref[...].astype(BF16)
    kp = _dot(ckvb, wk_ref[...])
    kr128 = _dot(kr_ref[...].astype(BF16), place_ref[...])
    for hd in range(MLA_HEADS):
        k_ref[hd] = (kp[:, hd * HEAD_PAD:(hd + 1) * HEAD_PAD] + kr128).astype(BF16)
    v_ref[...] = _dot(ckvb, wv_ref[...]).astype(BF16)


def _cache_kv_call(ckv, kr, wk, wv, place):
    n = N_SMP_SEQ * PAST_LEN
    tb = PAST_LEN
    return pl.pallas_call(
        _cache_kv_kernel,
        grid=(n // tb,),
        in_specs=[
            pl.BlockSpec((tb, MLA_RANK), lambda i: (i, 0)),
            pl.BlockSpec((tb, MLA_ROPE), lambda i: (i, 0)),
            pl.BlockSpec((MLA_RANK, MLA_HEADS * HEAD_PAD), lambda i: (0, 0)),
            pl.BlockSpec((MLA_RANK, MLA_HEADS * MLA_V), lambda i: (0, 0)),
            pl.BlockSpec((MLA_ROPE, HEAD_PAD), lambda i: (0, 0)),
        ],
        out_specs=[
            pl.BlockSpec((MLA_HEADS, tb, HEAD_PAD), lambda i: (0, i, 0)),
            pl.BlockSpec((tb, MLA_HEADS * MLA_V), lambda i: (i, 0)),
        ],
        out_shape=[
            jax.ShapeDtypeStruct((MLA_HEADS, n, HEAD_PAD), BF16),
            jax.ShapeDtypeStruct((n, MLA_HEADS * MLA_V), BF16),
        ],
        compiler_params=pltpu.CompilerParams(dimension_semantics=("parallel",)),
        name="cache_kv",
    )(ckv, kr, wk, wv, place)


def _attn_kernel(*refs, has_cache):
    if has_cache:
        q_ref, kn_ref, vn_ref, kc_ref, vc_ref, o_ref = refs
    else:
        q_ref, kn_ref, vn_ref, o_ref = refs
    scale = (MLA_NOPE + MLA_ROPE) ** -0.5
    outs = []
    for hd in range(MLA_HEADS):
        lanes = slice(hd * MLA_V, (hd + 1) * MLA_V)
        q = q_ref[hd]
        s_n = _dot_nt(q, kn_ref[hd])
        m = jnp.max(s_n, axis=-1, keepdims=True)
        if has_cache:
            s_c = _dot_nt(q, kc_ref[hd])
            m = jnp.maximum(m, jnp.max(s_c, axis=-1, keepdims=True))
        p_n = jnp.exp((s_n - m) * scale)
        l = jnp.sum(p_n, axis=-1, keepdims=True)
        o = _dot(p_n.astype(BF16), vn_ref[:, lanes])
        if has_cache:
            p_c = jnp.exp((s_c - m) * scale)
            l = l + jnp.sum(p_c, axis=-1, keepdims=True)
            o = o + _dot(p_c.astype(BF16), vc_ref[:, lanes])
        outs.append(o / l)
    o_ref[...] = jnp.concatenate(outs, axis=-1)


def _attn_ctx_call(q, k, v):
    tq = CTX_LEN
    return pl.pallas_call(
        functools.partial(_attn_kernel, has_cache=False),
        grid=(N_CTX_SEQ,),
        in_specs=[
            pl.BlockSpec((MLA_HEADS, tq, HEAD_PAD), lambda s: (0, s, 0)),
            pl.BlockSpec((MLA_HEADS, tq, HEAD_PAD), lambda s: (0, s, 0)),
            pl.BlockSpec((tq, MLA_HEADS * MLA_V), lambda s: (s, 0)),
        ],
        out_specs=pl.BlockSpec((tq, MLA_HEADS * MLA_V), lambda s: (s, 0)),
        out_shape=jax.ShapeDtypeStruct((T_CTX, MLA_HEADS * MLA_V), F32),
        compiler_params=pltpu.CompilerParams(dimension_semantics=("parallel",), vmem_limit_bytes=VMEM_LIMIT),
        name="attn_ctx",
    )(q, k, v)


def _attn_smp_call(q, k, v, kc, vc):
    tq = 256
    qb_per_seq = SMP_LEN // tq
    ctx_qb = T_CTX // tq
    ctx_kb = T_CTX // SMP_LEN
    return pl.pallas_call(
        functools.partial(_attn_kernel, has_cache=True),
        grid=(N_SMP_SEQ, qb_per_seq),
        in_specs=[
            pl.BlockSpec((MLA_HEADS, tq, HEAD_PAD), lambda b, i: (0, ctx_qb + b * qb_per_seq + i, 0)),
            pl.BlockSpec((MLA_HEADS, SMP_LEN, HEAD_PAD), lambda b, i: (0, ctx_kb + b, 0)),
            pl.BlockSpec((SMP_LEN, MLA_HEADS * MLA_V), lambda b, i: (ctx_kb + b, 0)),
            pl.BlockSpec((MLA_HEADS, PAST_LEN, HEAD_PAD), lambda b, i: (0, b, 0)),
            pl.BlockSpec((PAST_LEN, MLA_HEADS * MLA_V), lambda b, i: (b, 0)),
        ],
        out_specs=pl.BlockSpec((tq, MLA_HEADS * MLA_V), lambda b, i: (b * qb_per_seq + i, 0)),
        out_shape=jax.ShapeDtypeStruct((T_SMP, MLA_HEADS * MLA_V), F32),
        compiler_params=pltpu.CompilerParams(dimension_semantics=("parallel", "parallel"),
                                             vmem_limit_bytes=VMEM_LIMIT),
        name="attn_smp",
    )(q, k, v, kc, vc)


def _out_kernel(xc_ref, xs_ref, mod_ref, ml_ref, hf_ref, hb_ref, cm_ref, attc_ref, atts_ref, wo_ref, g2_ref,
                wq_ref, sk_ref, x1_ref, h2t_ref, st_ref):
    is_ctx = pl.program_id(0) < T_CTX // TM
    att = jnp.where(is_ctx, attc_ref[...], atts_ref[...])
    mod = mod_ref[0]
    g1 = mod[:, 2 * D_MODEL:3 * D_MODEL]
    sh2 = mod[:, 3 * D_MODEL:4 * D_MODEL]
    sc2 = mod[:, 4 * D_MODEL:5 * D_MODEL]
    mlo = _sigmoid(ml_ref[:, 3 * ML_WIDTH:4 * ML_WIDTH]) * (hf_ref[...] + hb_ref[...])
    mix = (_dot(mlo.astype(BF16), wo_ref[0:ML_WIDTH, :])
           + _dot(cm_ref[...].astype(BF16), wo_ref[ML_WIDTH:ML_WIDTH + CM_WIDTH, :])
           + _dot(att.astype(BF16), wo_ref[ML_WIDTH + CM_WIDTH:, :]))
    x1 = _select_x(xc_ref, xs_ref) + g1 * mix
    x1_ref[...] = x1
    h2f = _rms(x1, g2_ref[...]) * (1.0 + sc2) + sh2
    h2t_ref[...] = h2f.T.astype(BF16)
    qp = _dot(h2f.astype(BF16), wq_ref[...]).astype(BF16)
    for hh in range(2 * PEER_HEADS):
        st_ref[hh] = _dot_nt(sk_ref[hh % 2], qp[:, hh * 128:(hh + 1) * 128])


def _out_call(x_ctx, x_smp, mod, ml, hf, hb, cm, att_ctx, att_smp, lw):
    nblk = T_ALL // TM
    ctx_blk = T_CTX // TM
    rows = lambda w: pl.BlockSpec((TM, w), lambda i: (i, 0))
    full = lambda shape: pl.BlockSpec(shape, lambda i: (0,) * len(shape))
    att_w = MLA_HEADS * MLA_V
    attc_spec = pl.BlockSpec((TM, att_w), lambda i: (jnp.minimum(i, ctx_blk - 1), 0))
    atts_spec = pl.BlockSpec((TM, att_w), lambda i: (jnp.maximum(i - ctx_blk, 0), 0))
    return pl.pallas_call(
        _out_kernel,
        grid=(nblk,),
        in_specs=_x_specs(x_ctx, x_smp) + [
            pl.BlockSpec((1, 1, 6 * D_MODEL), lambda i: (_mod_row_of_block(i, TM), 0, 0)),
            rows(4 * ML_WIDTH), rows(ML_WIDTH), rows(ML_WIDTH), rows(CM_WIDTH), attc_spec, atts_spec,
            full((D_MODEL, D_MODEL)), full((1, D_MODEL)), full((D_MODEL, 2 * PEER_HEADS * 128)),
            full((2, PEER_NKEYS, 128)),
        ],
        out_specs=[rows(D_MODEL), pl.BlockSpec((D_MODEL, TM), lambda i: (0, i)),
                   pl.BlockSpec((2 * PEER_HEADS, PEER_NKEYS, TM), lambda i: (0, 0, i))],
        out_shape=[
            jax.ShapeDtypeStruct((T_ALL, D_MODEL), F32),
            jax.ShapeDtypeStruct((D_MODEL, T_ALL), BF16),
            jax.ShapeDtypeStruct((2 * PEER_HEADS, PEER_NKEYS, T_ALL), F32),
        ],
        compiler_params=pltpu.CompilerParams(dimension_semantics=("parallel",), vmem_limit_bytes=VMEM_LIMIT),
        name="proj_out",
    )(x_ctx, x_smp, mod, ml, hf, hb, cm, att_ctx, att_smp, lw["wo"], lw["g2"], lw["wq"], lw["sk"])


def _sorting_network_pairs(n):
    pairs = []
    p = 1
    while p < n:
        k = p
        while k >= 1:
            for j in range(k % p, n - k, 2 * k):
                for i in range(min(k, n - j - k)):
                    if (i + j) // (2 * p) == (i + j + k) // (2 * p):
                        pairs.append((i + j, i + j + k))
            k //= 2
        p *= 2
    return pairs


_SORT16_PAIRS = _sorting_network_pairs(PEER_TOPK)


def _pop16(lists):
    lists = list(lists)
    vals = []
    for k in range(PEER_TOPK):
        m = jnp.max(lists[0], axis=0, keepdims=True)
        vals.append(m)
        if k == PEER_TOPK - 1:
            break
        hit = lists[0] == m
        for i in range(PEER_TOPK - 1 - k):
            lists[i] = jnp.where(hit, lists[i + 1], lists[i])
    return vals


def _top16_rows(s):
    tiles = [s[8 * j:8 * j + 8] for j in range(s.shape[0] // 8)]
    assert len(tiles) == PEER_TOPK
    for i, j in _SORT16_PAIRS:
        tiles[i], tiles[j] = jnp.maximum(tiles[i], tiles[j]), jnp.minimum(tiles[i], tiles[j])
    return _pop16(tiles)


def _count_steps(x, thresholds, below):
    r = jnp.zeros(x.shape, F32)
    for q, t in enumerate(thresholds):
        r = jnp.where((x < t) if below else (x >= t), float(q + 1), r)
    return r


def _pack_rows_bf16(x):
    r, n = x.shape
    x4 = x.reshape(r // 16, 2, 8, n)
    lo = x4[:, 0].reshape(r // 2, n)
    hi = x4[:, 1].reshape(r // 2, n)
    return pltpu.bitcast(pltpu.pack_elementwise([lo, hi], packed_dtype=BF16), BF16)


def _dup_bf16_words(x):
    u = pltpu.bitcast(x.astype(BF16).astype(F32), jnp.int32)
    return u | lax.shift_right_logical(u, jnp.full(u.shape, 16, jnp.int32))


def _rows_to_array(rows, row_iota):
    arr = jnp.zeros(row_iota.shape, F32)
    for i, r in enumerate(rows):
        arr = jnp.where(row_iota == i, r, arr)
    return arr


def _topk_kernel(st_ref, e1_ref, cut_ref, e2_ref, r2_ref):
    n = st_ref.shape[-1]
    row16 = lax.broadcasted_iota(jnp.int32, (PEER_TOPK, n), 0)

    def head(hd, carry):
        s1 = st_ref[2 * hd]
        s2 = st_ref[2 * hd + 1]
        v1 = _top16_rows(s1)
        v2 = _top16_rows(s2)
        rank2 = _count_steps(s2, v2, below=True)
        v1arr = _rows_to_array(v1, row16)
        c = _pop16([v1arr + v2[q] for q in range(PEER_TOPK)])
        tau = c[PEER_TOPK - 1]
        z = jnp.zeros((1, n), F32)
        for ck in c:
            z = z + jnp.exp(ck - c[0])
        sigma = [jnp.min(jnp.where(v1arr + v2[q] >= tau, v1arr, jnp.inf), axis=0, keepdims=True)
                 for q in range(PEER_TOPK)]
        cut = _count_steps(s1, sigma, below=False)
        e1_ref[hd] = _dup_bf16_words(jnp.exp(s1 - v1[0]) / z)
        cut_ref[hd] = _dup_bf16_words(cut)
        e2_ref[hd] = _pack_rows_bf16(jnp.exp(s2 - v2[0]))
        r2_ref[hd] = _pack_rows_bf16(rank2)
        return carry

    lax.fori_loop(0, PEER_HEADS, head, 0)


def _topk_call(st):
    tn = TN_TOPK
    spec = pl.BlockSpec((PEER_HEADS, PEER_NKEYS, tn), lambda i: (0, 0, i))
    stat = lambda dt: jax.ShapeDtypeStruct((PEER_HEADS, PEER_NKEYS, T_ALL), dt)
    return pl.pallas_call(
        _topk_kernel,
        grid=(T_ALL // tn,),
        in_specs=[pl.BlockSpec((2 * PEER_HEADS, PEER_NKEYS, tn), lambda i: (0, 0, i))],
        out_specs=[spec, spec, spec, spec],
        out_shape=[stat(jnp.int32), stat(jnp.int32), stat(BF16), stat(BF16)],
        compiler_params=pltpu.CompilerParams(dimension_semantics=("parallel",), vmem_limit_bytes=VMEM_LIMIT),
        name="peer_topk",
    )(st)


def _row_tile_bf16(row):
    blk = pltpu.bitcast(jnp.broadcast_to(row, (8, row.shape[1])), BF16)
    return jnp.concatenate([blk] * (PEER_NKEYS // 16), axis=0)


def _gelu_tanh_bf16(x):
    z2 = x * (1.5957691216057308 + 0.07135481627159584 * (x * x))
    return x / (1.0 + jnp.exp(-z2))


def _expert_kernel(h2t_ref, u_ref, vt_ref, e1_ref, cut_ref, e2_ref, r2_ref, x1_ref, mod_ref, fg_ref,
                   o_ref, acc_ref, a_ref, *, final_norm):
    j = pl.program_id(1)
    n_tiles = pl.num_programs(1)
    tm = h2t_ref.shape[1]
    a_per_tile = TE_EXP // PEER_NKEYS

    @pl.when(j == 0)
    def _():
        acc_ref[...] = jnp.zeros_like(acc_ref)

    for sub in range(TE_EXP // EXP_SUB):
        srows = slice(sub * EXP_SUB, (sub + 1) * EXP_SUB)
        s = _pack_rows_bf16(_dot(u_ref[srows, :], h2t_ref[...]))
        g = _gelu_tanh_bf16(s)
        for ai in range(sub * EXP_SUB // PEER_NKEYS, (sub + 1) * EXP_SUB // PEER_NKEYS):
            a = j * a_per_tile + ai
            gate = jnp.zeros((PEER_NKEYS, tm), BF16)
            for hd in range(PEER_HEADS):
                cut_a = _row_tile_bf16(cut_ref[hd, pl.ds(a, 1), :])
                e1_a = _row_tile_bf16(e1_ref[hd, pl.ds(a, 1), :])
                gate = gate + jnp.where(r2_ref[hd] < cut_a, e2_ref[hd], jnp.zeros((), BF16)) * e1_a
            grow = ai * PEER_NKEYS - sub * EXP_SUB
            a_ref[ai * PEER_NKEYS:(ai + 1) * PEER_NKEYS, :] = gate * g[grow:grow + PEER_NKEYS, :]
    acc_ref[...] += _dot(vt_ref[...], a_ref[...])

    @pl.when(j == n_tiles - 1)
    def _():
        g2 = mod_ref[0][:, 5 * D_MODEL:6 * D_MODEL]
        y = x1_ref[...] + g2 * acc_ref[...].T
        if final_norm:
            y = _rms(y, fg_ref[...])
        o_ref[...] = y


def _expert_call(h2t, u_bf, vt_bf, layer, e1, cut, e2, r2, x1, mod, final_g, final_norm, tok_start, tok_count):
    tm, te = TM_EXP, TE_EXP
    b0 = tok_start // tm
    stat = pl.BlockSpec((PEER_HEADS, PEER_NKEYS, tm), lambda i, j: (0, 0, b0 + i))
    return pl.pallas_call(
        functools.partial(_expert_kernel, final_norm=final_norm),
        grid=(tok_count // tm, PEER_EXPERTS // te),
        in_specs=[
            pl.BlockSpec((D_MODEL, tm), lambda i, j: (0, b0 + i)),
            pl.BlockSpec((None, te, D_MODEL), lambda i, j: (layer, j, 0)),
            pl.BlockSpec((None, D_MODEL, te), lambda i, j: (layer, 0, j)),
            stat, stat, stat, stat,
            pl.BlockSpec((tm, D_MODEL), lambda i, j: (b0 + i, 0)),
            pl.BlockSpec((1, 1, 6 * D_MODEL), lambda i, j: (_mod_row_of_block(b0 + i, tm), 0, 0)),
            pl.BlockSpec((1, D_MODEL), lambda i, j: (0, 0)),
        ],
        out_specs=pl.BlockSpec((tm, D_MODEL), lambda i, j: (i, 0)),
        out_shape=jax.ShapeDtypeStruct((tok_count, D_MODEL), F32),
        scratch_shapes=[
            pltpu.VMEM((D_MODEL, tm), F32),
            pltpu.VMEM((te, tm), BF16),
        ],
        compiler_params=pltpu.CompilerParams(dimension_semantics=("parallel", "arbitrary"),
                                             vmem_limit_bytes=VMEM_LIMIT),
        name="peer_experts",
    )(h2t, u_bf, vt_bf, e1, cut, e2, r2, x1, mod, final_g)


def _u_prep_kernel(u_ref, o_ref):
    o_ref[...] = u_ref[...].astype(BF16)


def _v_prep_kernel(v_ref, o_ref):
    o_ref[...] = _pack_rows_bf16(v_ref[...]).T


def _table_prep_call(peer_u, peer_v):
    te = TE_EXP
    grid = (DEPTH, PEER_EXPERTS // te)
    rows = pl.BlockSpec((None, te, D_MODEL), lambda l, j: (l, j, 0))
    params = pltpu.CompilerParams(dimension_semantics=("parallel", "parallel"), vmem_limit_bytes=VMEM_LIMIT)
    u_bf = pl.pallas_call(
        _u_prep_kernel, grid=grid, in_specs=[rows], out_specs=rows,
        out_shape=jax.ShapeDtypeStruct((DEPTH, PEER_EXPERTS, D_MODEL), BF16),
        compiler_params=params, name="peer_u_prep")(peer_u)
    vt_bf = pl.pallas_call(
        _v_prep_kernel, grid=grid, in_specs=[rows],
        out_specs=pl.BlockSpec((None, D_MODEL, te), lambda l, j: (l, 0, j)),
        out_shape=jax.ShapeDtypeStruct((DEPTH, D_MODEL, PEER_EXPERTS), BF16),
        compiler_params=params, name="peer_v_prep")(peer_v)
    return u_bf, vt_bf


def _rope_swap_cols(w):
    return jnp.concatenate([-w[:, 8:16], w[:, 0:8], -w[:, 24:32], w[:, 16:24]], axis=1)


def _pad_heads(parts, n_heads):
    k = next(p[0].shape[0] for p in parts if p[0] is not None)
    cols = []
    for hd in range(n_heads):
        for arr, w in parts:
            cols.append(jnp.zeros((k, w), F32) if arr is None else arr[:, hd * w:(hd + 1) * w])
    return jnp.concatenate(cols, axis=1)


def _rope_tables():
    pos = np.arange(SMP_LEN)
    freqs = ROPE_THETA ** (-np.arange(0, ROPE_AXIS, 2, dtype=np.float32) / ROPE_AXIS)
    ang_r = (pos // GRID_W).astype(np.float32)[:, None] * freqs
    ang_c = (pos % GRID_W).astype(np.float32)[:, None] * freqs
    ang = jnp.asarray(np.concatenate([ang_r, ang_r, ang_c, ang_c], axis=1).astype(np.float32))
    cos32 = jnp.cos(ang)
    sin32 = jnp.sin(ang)
    ones = jnp.ones((SMP_LEN, MLA_NOPE), F32)
    cos_s = jnp.concatenate([ones, cos32, ones[:, :HEAD_PAD - MLA_NOPE - MLA_ROPE]], axis=1)
    sin_s = jnp.concatenate([0 * ones, sin32, 0 * ones[:, :HEAD_PAD - MLA_NOPE - MLA_ROPE]], axis=1)
    cos_t = jnp.concatenate([jnp.ones((T_CTX, HEAD_PAD), F32)] + [cos_s] * N_SMP_SEQ, axis=0)
    sin_t = jnp.concatenate([jnp.zeros((T_CTX, HEAD_PAD), F32)] + [sin_s] * N_SMP_SEQ, axis=0)
    return cos_t, sin_t


def _layer_weights(l, norm1_g, w_in, mlstm_gate_b, cm_norm_g, cm_ws, cm_b, mla_q_norm_g, mla_w_uq, mla_kv_norm_g,
                   mla_w_ukv, w_out, norm2_g, peer_w_q, peer_subkeys):
    w = w_in[l]
    o_g = 4 * ML_WIDTH
    o_cm = o_g + 16
    o_cq = o_cm + 2 * CM_WIDTH
    o_ckv = o_cq + MLA_RANK
    o_kr = o_ckv + MLA_RANK
    w_kr = w[:, o_kr:o_kr + MLA_ROPE]
    zeros_l = jnp.zeros((D_MODEL, MLA_NOPE), F32)
    zeros_r = jnp.zeros((D_MODEL, HEAD_PAD - MLA_NOPE - MLA_ROPE), F32)
    kr128 = jnp.concatenate([zeros_l, w_kr, zeros_r], axis=1)
    krsw128 = jnp.concatenate([zeros_l, _rope_swap_cols(w_kr), zeros_r], axis=1)
    uq = mla_w_uq[l].reshape(MLA_RANK, MLA_HEADS, MLA_NOPE + MLA_ROPE)
    uq_nope = uq[:, :, :MLA_NOPE].reshape(MLA_RANK, -1)
    uq_rope = uq[:, :, MLA_NOPE:].reshape(MLA_RANK, -1)
    uq_rope_sw = jnp.concatenate(
        [_rope_swap_cols(uq_rope[:, hd * MLA_ROPE:(hd + 1) * MLA_ROPE]) for hd in range(MLA_HEADS)], axis=1)
    pad_w = HEAD_PAD - MLA_NOPE - MLA_ROPE
    ukv = mla_w_ukv[l].reshape(MLA_RANK, MLA_HEADS, MLA_NOPE + MLA_V)
    uk = ukv[:, :, :MLA_NOPE].reshape(MLA_RANK, -1)
    uv = ukv[:, :, MLA_NOPE:].reshape(MLA_RANK, -1)
    gb = mlstm_gate_b[l]
    return {
        "g1": norm1_g[l].reshape(1, D_MODEL),
        "wml": w[:, 0:o_g].astype(BF16),
        "wgt": w[:, o_g:o_cm].T.astype(BF16),
        "gb_col": gb.reshape(16, 1),
        "wcm": w[:, o_cm:o_cq].astype(BF16),
        "wmla": jnp.concatenate([w[:, o_cq:o_kr], kr128, krsw128], axis=1).astype(BF16),
        "wkr": w_kr.astype(BF16),
        "cmg": cm_norm_g[l].reshape(1, CM_WIDTH),
        "ws": cm_ws[l].astype(BF16),
        "cmb": jnp.repeat(cm_b[l].T, CM_WIDTH // CM_GROUPS, axis=1),
        "qg": mla_q_norm_g[l].reshape(1, MLA_RANK),
        "kvg": mla_kv_norm_g[l].reshape(1, MLA_RANK),
        "wuqa": _pad_heads([(uq_nope, MLA_NOPE), (uq_rope, MLA_ROPE), (None, pad_w)], MLA_HEADS).astype(BF16),
        "wuqb": _pad_heads([(None, MLA_NOPE), (uq_rope_sw, MLA_ROPE), (None, pad_w)], MLA_HEADS).astype(BF16),
        "wk": _pad_heads([(uk, MLA_NOPE), (None, HEAD_PAD - MLA_NOPE)], MLA_HEADS).astype(BF16),
        "wv": uv.astype(BF16),
        "wo": w_out[l].astype(BF16),
        "g2": norm2_g[l].reshape(1, D_MODEL),
        "wq": peer_w_q[l].astype(BF16),
        "sk": peer_subkeys[l].astype(BF16),
    }


def kernel(x_prompt, x_sample, c, cache_mla_ckv, cache_mla_krope, state_mlstm_C, state_mlstm_n, state_mlstm_m, c_ctx, norm1_g, ada_w, ada_b, w_in, mlstm_gate_b, cm_norm_g, cm_ws, cm_b, mla_q_norm_g, mla_w_uq, mla_kv_norm_g, mla_w_ukv, w_out, norm2_g, peer_w_q, peer_subkeys, peer_u, peer_v, final_g):
    x_ctx = x_prompt.reshape(T_CTX, D_MODEL)
    x_smp = x_sample.reshape(T_SMP, D_MODEL)
    cvecs = jnp.concatenate([c_ctx[None, :], c, jnp.zeros((N_MOD_ROWS - 1 - N_SMP_SEQ, D_MODEL), F32)], axis=0)
    mod_all = _ada_call(cvecs, ada_w, ada_b)
    cos_t, sin_t = _rope_tables()
    place = jnp.concatenate([jnp.zeros((MLA_ROPE, MLA_NOPE), F32), jnp.eye(MLA_ROPE, dtype=F32),
                             jnp.zeros((MLA_ROPE, HEAD_PAD - MLA_NOPE - MLA_ROPE), F32)], axis=1).astype(BF16)
    final_g2 = final_g.reshape(1, D_MODEL)
    u_bf, vt_bf = _table_prep_call(peer_u, peer_v)

    ckvs, krs, Cs, ns, ms = [], [], [], [], []
    for l in range(DEPTH):
        lw = _layer_weights(l, norm1_g, w_in, mlstm_gate_b, cm_norm_g, cm_ws, cm_b, mla_q_norm_g, mla_w_uq,
                            mla_kv_norm_g, mla_w_ukv, w_out, norm2_g, peer_w_q, peer_subkeys)
        mod = mod_all[l].reshape(N_MOD_ROWS, 1, 6 * D_MODEL)
        ml, gates_t, cm, ckvn, kr, q, k, v = _proj_call(x_ctx, x_smp, mod, lw, cos_t, sin_t)

        c_blk = jnp.einsum('bdhij,hg->bdhigj', state_mlstm_C[:, l], jnp.eye(ML_HEADS, dtype=F32))
        c0 = jnp.concatenate([jnp.zeros((N_CTX_SEQ, 2, ML_WIDTH, ML_WIDTH), F32),
                              c_blk.reshape(N_SMP_SEQ, 2, ML_WIDTH, ML_WIDTH)], axis=0)
        n0 = jnp.concatenate([jnp.zeros((N_CTX_SEQ, 2, 1, ML_WIDTH), F32),
                              state_mlstm_n[:, l].reshape(N_SMP_SEQ, 2, 1, ML_WIDTH)], axis=0)
        n0 = jnp.pad(n0, ((0, 0), (0, 0), (0, 7), (0, 0)))
        m0 = jnp.concatenate([jnp.zeros((N_CTX_SEQ, 2, ML_HEADS), F32), state_mlstm_m[:, l]], axis=0)
        m0 = jnp.pad(jnp.broadcast_to(m0[..., None], m0.shape + (128,)), ((0, 0), (0, 0), (0, 8 - ML_HEADS), (0, 0)))
        hf, hb, c_fin, n_fin, m_fin = _mlstm_call(ml, gates_t, c0, n0, m0)

        kc, vc = _cache_kv_call(cache_mla_ckv[:, l].reshape(N_SMP_SEQ * PAST_LEN, MLA_RANK),
                                cache_mla_krope[:, l].reshape(N_SMP_SEQ * PAST_LEN, MLA_ROPE),
                                lw["wk"], lw["wv"], place)
        att_ctx = _attn_ctx_call(q, k, v)
        att_smp = _attn_smp_call(q, k, v, kc, vc)

        x1, h2t, st = _out_call(x_ctx, x_smp, mod, ml, hf, hb, cm, att_ctx, att_smp, lw)
        e1, cut, e2, r2 = _topk_call(st)
        experts = functools.partial(_expert_call, h2t, u_bf, vt_bf, l, e1, cut, e2, r2, x1, mod, final_g2)
        if l < DEPTH - 1:
            x_ctx = x_smp = experts(False, 0, T_ALL)
        else:
            x_ctx = experts(True, 0, T_CTX)
            x_smp = experts(True, T_CTX, T_SMP)

        ckvs.append(ckvn[:T_CTX].reshape(N_CTX_SEQ, CTX_LEN, MLA_RANK))
        krs.append(kr[:T_CTX].reshape(N_CTX_SEQ, CTX_LEN, MLA_ROPE))
        Cs.append(jnp.stack([c_fin[:N_CTX_SEQ, :, hd * ML_DIM:(hd + 1) * ML_DIM, hd * ML_DIM:(hd + 1) * ML_DIM]
                             for hd in range(ML_HEADS)], axis=2))
        ns.append(n_fin[:N_CTX_SEQ, :, 0, :].reshape(N_CTX_SEQ, 2, ML_HEADS, ML_DIM))
        ms.append(m_fin[:N_CTX_SEQ, :, 0:ML_HEADS, 0])

    y_prompt = x_ctx.reshape(N_CTX_SEQ, CTX_LEN, D_MODEL)
    y_sample = x_smp.reshape(N_SMP_SEQ, SMP_LEN, D_MODEL)
    return (y_prompt, y_sample, jnp.stack(ckvs, axis=1), jnp.stack(krs, axis=1), jnp.stack(Cs, axis=1),
            jnp.stack(ns, axis=1), jnp.stack(ms, axis=1))
```

```python
import functools

import numpy as np
import jax
import jax.numpy as jnp
from jax import lax
from jax.experimental import pallas as pl
from jax.experimental.pallas import tpu as pltpu

F32 = jnp.float32
BF16 = jnp.bfloat16

D_MODEL = 1024
N_CTX_SEQ = 16
CTX_LEN = 256
N_SMP_SEQ = 4
SMP_LEN = 2048
PAST_LEN = 256
DEPTH = 2
GRID_W = 64
EPS = 1e-6
T_CTX = N_CTX_SEQ * CTX_LEN
T_SMP = N_SMP_SEQ * SMP_LEN
T_ALL = T_CTX + T_SMP
N_MOD_ROWS = 8

ML_HEADS = 4
ML_DIM = 64
ML_WIDTH = 256
CHUNK = 128
CM_GROUPS = 4
CM_WIDTH = 256
MLA_HEADS = 8
MLA_NOPE = 64
MLA_ROPE = 32
MLA_V = 64
MLA_RANK = 256
HEAD_PAD = 128
ROPE_AXIS = 16
ROPE_THETA = 10000.0
PEER_HEADS = 8
PEER_NKEYS = 128
PEER_EXPERTS = PEER_NKEYS * PEER_NKEYS
PEER_TOPK = 16

TM = 256
TN_TOPK = 256
TM_EXP = 512
TE_EXP = 2048
EXP_SUB = 512
ATTN_HEADS_IN_FLIGHT = 2
VMEM_LIMIT = 56 * 1024 * 1024

NEG_INF = float("-inf")


def _dot(a, b):
    return jnp.dot(a, b, preferred_element_type=F32)


def _dot_nt(a, b):
    return lax.dot_general(a, b, (((1,), (1,)), ((), ())), preferred_element_type=F32)


def _dot_tn(a, b):
    return lax.dot_general(a, b, (((0,), (0,)), ((), ())), preferred_element_type=F32)


def _split3(a):
    a1 = a.astype(BF16)
    r1 = a - a1.astype(F32)
    a2 = r1.astype(BF16)
    a3 = (r1 - a2.astype(F32)).astype(BF16)
    return a1, a2, a3


def _rms(x, g):
    return x * lax.rsqrt(jnp.mean(x * x, axis=-1, keepdims=True) + EPS) * g


def _sigmoid(x):
    return 1.0 / (1.0 + jnp.exp(-x))


def _log_sigmoid(x):
    return jnp.minimum(x, 0.0) - jnp.log(1.0 + jnp.exp(-jnp.abs(x)))


def _gelu_tanh(x):
    return 0.5 * x * (1.0 + jnp.tanh(0.7978845608028654 * (x + 0.044715 * (x * x * x))))


def _mod_row_of_block(i, rows_per_block):
    ctx_blocks = T_CTX // rows_per_block
    per_seq = SMP_LEN // rows_per_block
    return jnp.where(i < ctx_blocks, 0, 1 + (i - ctx_blocks) // per_seq)


def _ada_kernel(cv_ref, w_ref, b_ref, o_ref):
    cv = cv_ref[...]
    s = cv * _sigmoid(cv)
    w = w_ref[0]
    w1, w2, w3 = _split3(w)
    s1, s2, s3 = _split3(s)
    acc = _dot(s1, w1) + (_dot(s1, w2) + _dot(s2, w1)) + (_dot(s1, w3) + _dot(s2, w2) + _dot(s3, w1))
    o_ref[0] = acc + b_ref[0]


def _ada_call(cvecs, ada_w, ada_b):
    tn = 1024
    return pl.pallas_call(
        _ada_kernel,
        grid=(DEPTH, 6 * D_MODEL // tn),
        in_specs=[
            pl.BlockSpec((N_MOD_ROWS, D_MODEL), lambda l, j: (0, 0)),
            pl.BlockSpec((1, D_MODEL, tn), lambda l, j: (l, 0, j)),
            pl.BlockSpec((1, 1, tn), lambda l, j: (l, 0, j)),
        ],
        out_specs=pl.BlockSpec((1, N_MOD_ROWS, tn), lambda l, j: (l, 0, j)),
        out_shape=jax.ShapeDtypeStruct((DEPTH, N_MOD_ROWS, 6 * D_MODEL), F32),
        compiler_params=pltpu.CompilerParams(dimension_semantics=("parallel", "parallel")),
        name="ada_mod",
    )(cvecs, ada_w, ada_b.reshape(DEPTH, 1, 6 * D_MODEL))


def _x_specs(x_ctx, x_smp):
    ctx_blk = T_CTX // TM
    smp_off = x_smp.shape[0] // TM - T_SMP // TM
    return [pl.BlockSpec((TM, D_MODEL), lambda i: (jnp.minimum(i, ctx_blk - 1), 0)),
            pl.BlockSpec((TM, D_MODEL), lambda i: (jnp.maximum(i - ctx_blk, 0) + smp_off, 0))]


def _select_x(xc_ref, xs_ref):
    return jnp.where(pl.program_id(0) < T_CTX // TM, xc_ref[...], xs_ref[...])


def _proj_kernel(xc_ref, xs_ref, mod_ref, g1_ref, wml_ref, wgt_ref, gbc_ref, wcm_ref, wmla_ref, wkr_ref,
                 cmg_ref, ws_ref, cmb_ref, qg_ref, kvg_ref, wuqa_ref, wuqb_ref, wk_ref, wv_ref, cos_ref, sin_ref,
                 ml_ref, gatest_ref, cm_ref, ckvn_ref, kr_ref, q_ref, k_ref, v_ref):
    x = _select_x(xc_ref, xs_ref)
    mod = mod_ref[0]
    sh1 = mod[:, 0:D_MODEL]
    sc1 = mod[:, D_MODEL:2 * D_MODEL]
    h = _rms(x, g1_ref[...]) * (1.0 + sc1) + sh1
    hb = h.astype(BF16)

    ml_ref[...] = _dot(hb, wml_ref[...])
    gatest_ref[...] = _dot_nt(wgt_ref[...], hb) + gbc_ref[...]

    cm = _dot(hb, wcm_ref[...])
    u = cm[:, 0:CM_WIDTH]
    vn = _rms(cm[:, CM_WIDTH:2 * CM_WIDTH], cmg_ref[...]).astype(BF16)
    lane_group = lax.broadcasted_iota(jnp.int32, (CHUNK, CM_WIDTH), 1) >> 6
    for c in range(TM // CHUNK):
        rows = slice(c * CHUNK, (c + 1) * CHUNK)
        vc = vn[rows]
        mixed = jnp.zeros((CHUNK, CM_WIDTH), F32)
        for g in range(CM_GROUPS):
            mixed = jnp.where(lane_group == g, _dot(ws_ref[g], vc), mixed)
        cm_ref[rows, :] = u[rows] * (mixed + cmb_ref[...])

    mla = _dot(hb, wmla_ref[...])
    qn = _rms(mla[:, 0:MLA_RANK], qg_ref[...]).astype(BF16)
    ckvn = _rms(mla[:, MLA_RANK:2 * MLA_RANK], kvg_ref[...])
    ckvn_ref[...] = ckvn
    kr_ref[...] = _dot(hb, wkr_ref[...])
    cos = cos_ref[...]
    sin = sin_ref[...]
    kr_rot = mla[:, 2 * MLA_RANK:2 * MLA_RANK + HEAD_PAD] * cos + mla[:, 2 * MLA_RANK + HEAD_PAD:] * sin
    qa = _dot(qn, wuqa_ref[...])
    qb = _dot(qn, wuqb_ref[...])
    ckvb = ckvn.astype(BF16)
    kp = _dot(ckvb, wk_ref[...])
    for hd in range(MLA_HEADS):
        cols = slice(hd * HEAD_PAD, (hd + 1) * HEAD_PAD)
        q_ref[hd] = (qa[:, cols] * cos + qb[:, cols] * sin).astype(BF16)
        k_ref[hd] = (kp[:, cols] + kr_rot).astype(BF16)
    v_ref[...] = _dot(ckvb, wv_ref[...]).astype(BF16)


def _proj_call(x_ctx, x_smp, mod, lw, cos_t, sin_t):
    nblk = T_ALL // TM
    full = lambda shape: pl.BlockSpec(shape, lambda i: (0,) * len(shape))
    rows = lambda w: pl.BlockSpec((TM, w), lambda i: (i, 0))
    in_specs = _x_specs(x_ctx, x_smp) + [
        pl.BlockSpec((1, 1, 6 * D_MODEL), lambda i: (_mod_row_of_block(i, TM), 0, 0)),
        full((1, D_MODEL)),
        full((D_MODEL, 4 * ML_WIDTH)),
        full((16, D_MODEL)),
        full((16, 1)),
        full((D_MODEL, 2 * CM_WIDTH)),
        full((D_MODEL, 2 * MLA_RANK + 2 * HEAD_PAD)),
        full((D_MODEL, MLA_ROPE)),
        full((1, CM_WIDTH)),
        full((CM_GROUPS, CHUNK, CHUNK)),
        full((CHUNK, CM_WIDTH)),
        full((1, MLA_RANK)),
        full((1, MLA_RANK)),
        full((MLA_RANK, MLA_HEADS * HEAD_PAD)),
        full((MLA_RANK, MLA_HEADS * HEAD_PAD)),
        full((MLA_RANK, MLA_HEADS * HEAD_PAD)),
        full((MLA_RANK, MLA_HEADS * MLA_V)),
        rows(HEAD_PAD),
        rows(HEAD_PAD),
    ]
    out_specs = [
        rows(4 * ML_WIDTH),
        pl.BlockSpec((16, TM), lambda i: (0, i)),
        rows(CM_WIDTH),
        rows(MLA_RANK),
        rows(MLA_ROPE),
        pl.BlockSpec((MLA_HEADS, TM, HEAD_PAD), lambda i: (0, i, 0)),
        pl.BlockSpec((MLA_HEADS, TM, HEAD_PAD), lambda i: (0, i, 0)),
        rows(MLA_HEADS * MLA_V),
    ]
    out_shape = [
        jax.ShapeDtypeStruct((T_ALL, 4 * ML_WIDTH), F32),
        jax.ShapeDtypeStruct((16, T_ALL), F32),
        jax.ShapeDtypeStruct((T_ALL, CM_WIDTH), F32),
        jax.ShapeDtypeStruct((T_ALL, MLA_RANK), F32),
        jax.ShapeDtypeStruct((T_ALL, MLA_ROPE), F32),
        jax.ShapeDtypeStruct((MLA_HEADS, T_ALL, HEAD_PAD), BF16),
        jax.ShapeDtypeStruct((MLA_HEADS, T_ALL, HEAD_PAD), BF16),
        jax.ShapeDtypeStruct((T_ALL, MLA_HEADS * MLA_V), BF16),
    ]
    return pl.pallas_call(
        _proj_kernel,
        grid=(nblk,),
        in_specs=in_specs,
        out_specs=out_specs,
        out_shape=out_shape,
        compiler_params=pltpu.CompilerParams(dimension_semantics=("parallel",), vmem_limit_bytes=VMEM_LIMIT),
        name="proj_in",
    )(x_ctx, x_smp, mod, lw["g1"], lw["wml"], lw["wgt"], lw["gb_col"], lw["wcm"], lw["wmla"],
      lw["wkr"], lw["cmg"], lw["ws"], lw["cmb"], lw["qg"], lw["kvg"], lw["wuqa"], lw["wuqb"], lw["wk"], lw["wv"],
      cos_t, sin_t)


def _mlstm_schedule():
    seq, fwd, bwd, first = [], [], [], []
    base = 0
    for s in range(N_CTX_SEQ + N_SMP_SEQ):
        nc = (CTX_LEN if s < N_CTX_SEQ else SMP_LEN) // CHUNK
        for j in range(nc):
            seq.append(s)
            fwd.append(base + j)
            bwd.append(base + nc - 1 - j)
            first.append(1 if j == 0 else 0)
        base += nc
    as_i32 = lambda a: jnp.asarray(np.asarray(a, np.int32))
    return as_i32(seq), as_i32(fwd), as_i32(bwd), as_i32(first)


def _scan_cummax(x, direction):
    L = CHUNK
    rows = x.shape[0]
    x = jnp.concatenate([x, x], axis=0)
    lane = lax.broadcasted_iota(jnp.int32, x.shape, 1)
    k = 1
    while k < L:
        if direction == 0:
            shifted = jnp.where(lane >= k, pltpu.roll(x, k, axis=1), NEG_INF)
        else:
            shifted = jnp.where(lane < L - k, pltpu.roll(x, L - k, axis=1), NEG_INF)
        x = jnp.maximum(x, shifted)
        k *= 2
    return x[0:rows]


def _rows_to_lane_cols(rows, eye, rep, pieces):
    x = jnp.concatenate([jnp.broadcast_to(rows[h:h + 1, :], (rep, CHUNK)) for h in range(ML_HEADS)], axis=0)
    out = None
    for _ in range(pieces):
        xb = x.astype(BF16)
        part = _dot_nt(eye, xb)
        out = part if out is None else out + part
        x = x - xb.astype(F32)
    return out


def _per_head_lanes(x512, lane64):
    lo = jnp.where(lane64, x512[:, 0:128], x512[:, 128:256])
    hi = jnp.where(lane64, x512[:, 256:384], x512[:, 384:512])
    return jnp.concatenate([lo, hi], axis=1)


def _mlstm_direction(ml, g_row, direction, c_ref, n_ref, m_ref):
    L = CHUNK
    t_idx = lax.broadcasted_iota(jnp.int32, (L, L), 0)
    s_idx = lax.broadcasted_iota(jnp.int32, (L, L), 1)
    visible = (s_idx <= t_idx) if direction == 0 else (s_idx >= t_idx)
    tri = jnp.where(visible, 1.0, 0.0).astype(BF16)
    eye = jnp.where(s_idx == t_idx, 1.0, 0.0).astype(BF16)
    lane64 = lax.broadcasted_iota(jnp.int32, (1, 128), 1) < ML_DIM
    head_of_lane = lax.broadcasted_iota(jnp.int32, (1, ML_WIDTH), 1) >> 6
    same_head = ((lax.broadcasted_iota(jnp.int32, (ML_WIDTH, ML_WIDTH), 0) >> 6)
                 == (lax.broadcasted_iota(jnp.int32, (ML_WIDTH, ML_WIDTH), 1) >> 6))

    i0 = 8 * direction
    i_row = g_row[i0:i0 + ML_HEADS, :]
    lf_row = _log_sigmoid(g_row[i0 + ML_HEADS:i0 + 2 * ML_HEADS, :])
    r1, r2, r3 = _split3(lf_row)
    b_row = _dot_nt(r1, tri) + _dot_nt(r2, tri) + _dot_nt(r3, tri)
    yield None
    m_rep = m_ref[0, direction, 0:ML_HEADS, :]
    a_row = i_row - b_row
    g_row_ = jnp.maximum(m_rep, _scan_cummax(a_row, direction))
    yield None
    b_end = jnp.sum(lf_row, axis=1, keepdims=True)
    log_w = b_end - b_row + i_row
    m_new = jnp.maximum(b_end + m_rep, jnp.max(log_w, axis=1, keepdims=True))
    w_k_row = jnp.exp(log_w - m_new)
    decay_rep = jnp.exp(b_end + m_rep - m_new)

    yield None
    g512 = _rows_to_lane_cols(g_row_, eye, 128, 2)
    g_full = _per_head_lanes(g512, lane64)
    yield None
    b_full = _rows_to_lane_cols(b_row, eye, ML_DIM, 2)
    wk_full = _rows_to_lane_cols(w_k_row, eye, ML_DIM, 1)
    m_full = _per_head_lanes(jnp.concatenate([m_rep[h:h + 1, :] for h in range(ML_HEADS)], axis=1), lane64)
    decay_full = _per_head_lanes(jnp.concatenate([decay_rep[h:h + 1, :] for h in range(ML_HEADS)], axis=1),
                                 lane64)

    yield None
    q = ml[:, 0:ML_WIDTH]
    k = ml[:, ML_WIDTH:2 * ML_WIDTH] * (ML_DIM ** -0.5)
    v = ml[:, 2 * ML_WIDTH:3 * ML_WIDTH]
    qb = q.astype(BF16)
    kb = k.astype(BF16)
    vb = v.astype(BF16)
    num = jnp.zeros((L, ML_WIDTH), F32)
    rowsum = jnp.zeros((L, ML_WIDTH), F32)
    for hd in range(ML_HEADS):
        w_intra = jnp.where(visible, jnp.exp(a_row[hd:hd + 1, :] - g512[:, hd * 128:(hd + 1) * 128]), 0.0)
        q_h = jnp.where(head_of_lane == hd, q, 0.0).astype(BF16)
        sw = _dot_nt(q_h, kb) * w_intra
        num = jnp.where(head_of_lane == hd, _dot(sw.astype(BF16), vb), num)
        rowsum = jnp.where(head_of_lane == hd, jnp.sum(sw, axis=-1, keepdims=True), rowsum)
        yield None

    C = c_ref[0, direction]
    n_row = n_ref[0, direction, 0:1, :]
    w_inter = jnp.exp(m_full - g_full)
    block_ones = jnp.where(same_head, 1.0, 0.0).astype(BF16)
    qn = _dot((q * n_row).astype(BF16), block_ones)
    num = num + w_inter * _dot(qb, C.astype(BF16))
    den = rowsum + w_inter * qn
    h_out = num / jnp.maximum(jnp.abs(den), jnp.exp(-(b_full + g_full)))

    kw = wk_full * k
    c_ref[0, direction] = decay_full * C + jnp.where(same_head, _dot_tn(kw.astype(BF16), vb), 0.0)
    n_ref[0, direction, 0:1, :] = decay_full * n_row + jnp.sum(kw, axis=0, keepdims=True)
    m_ref[0, direction, 0:ML_HEADS, :] = m_new
    yield h_out


def _interleave(stage_generators):
    results = [None] * len(stage_generators)
    live = list(range(len(stage_generators)))
    while live:
        for idx in list(live):
            try:
                value = next(stage_generators[idx])
                if value is not None:
                    results[idx] = value
            except StopIteration:
                live.remove(idx)
    return results


def _mlstm_kernel(seq_ref, fwd_ref, bwd_ref, first_ref, mlf_ref, mlb_ref, grf_ref, grb_ref,
                  c0_ref, n0_ref, m0_ref, hf_ref, hb_ref, c_ref, n_ref, m_ref):
    step = pl.program_id(0)

    @pl.when(first_ref[step] == 1)
    def _():
        c_ref[...] = c0_ref[...]
        n_ref[...] = n0_ref[...]
        m_ref[...] = m0_ref[...]

    hf, hb = _interleave([_mlstm_direction(mlf_ref[...], grf_ref[...], 0, c_ref, n_ref, m_ref),
                          _mlstm_direction(mlb_ref[...], grb_ref[...], 1, c_ref, n_ref, m_ref)])
    hf_ref[...] = hf
    hb_ref[...] = hb


def _mlstm_call(ml, gates_t, c0, n0, m0):
    seq, fwd, bwd, first = _mlstm_schedule()
    nseq = N_CTX_SEQ + N_SMP_SEQ
    nsteps = int(seq.shape[0])
    f_rows = lambda w: pl.BlockSpec((CHUNK, w), lambda i, sq, fw, bw, fs: (fw[i], 0))
    b_rows = lambda w: pl.BlockSpec((CHUNK, w), lambda i, sq, fw, bw, fs: (bw[i], 0))
    c_spec = pl.BlockSpec((1, 2, ML_WIDTH, ML_WIDTH), lambda i, sq, fw, bw, fs: (sq[i], 0, 0, 0))
    n_spec = pl.BlockSpec((1, 2, 8, ML_WIDTH), lambda i, sq, fw, bw, fs: (sq[i], 0, 0, 0))
    m_spec = pl.BlockSpec((1, 2, 8, 128), lambda i, sq, fw, bw, fs: (sq[i], 0, 0, 0))
    grid_spec = pltpu.PrefetchScalarGridSpec(
        num_scalar_prefetch=4,
        grid=(nsteps,),
        in_specs=[
            f_rows(4 * ML_WIDTH), b_rows(4 * ML_WIDTH),
            pl.BlockSpec((16, CHUNK), lambda i, sq, fw, bw, fs: (0, fw[i])),
            pl.BlockSpec((16, CHUNK), lambda i, sq, fw, bw, fs: (0, bw[i])),
            c_spec, n_spec, m_spec,
        ],
        out_specs=[f_rows(ML_WIDTH), b_rows(ML_WIDTH), c_spec, n_spec, m_spec],
    )
    return pl.pallas_call(
        _mlstm_kernel,
        grid_spec=grid_spec,
        out_shape=[
            jax.ShapeDtypeStruct((T_ALL, ML_WIDTH), F32),
            jax.ShapeDtypeStruct((T_ALL, ML_WIDTH), F32),
            jax.ShapeDtypeStruct((nseq, 2, ML_WIDTH, ML_WIDTH), F32),
            jax.ShapeDtypeStruct((nseq, 2, 8, ML_WIDTH), F32),
            jax.ShapeDtypeStruct((nseq, 2, 8, 128), F32),
        ],
        compiler_params=pltpu.CompilerParams(dimension_semantics=("arbitrary",), vmem_limit_bytes=VMEM_LIMIT),
        name="mlstm",
    )(seq, fwd, bwd, first, ml, ml, gates_t, gates_t, c0, n0, m0)


def _cache_kv_kernel(ckv_ref, kr_ref, wk_ref, wv_ref, place_ref, k_ref, v_ref):
    ckvb = ckv_ref[...].astype(BF16)
    kp = _dot(ckvb, wk_ref[...])
    kr128 = _dot(kr_ref[...].astype(BF16), place_ref[...])
    for hd in range(MLA_HEADS):
        k_ref[hd] = (kp[:, hd * HEAD_PAD:(hd + 1) * HEAD_PAD] + kr128).astype(BF16)
    v_ref[...] = _dot(ckvb, wv_ref[...]).astype(BF16)


def _cache_kv_call(ckv, kr, wk, wv, place):
    n = N_SMP_SEQ * PAST_LEN
    tb = PAST_LEN
    return pl.pallas_call(
        _cache_kv_kernel,
        grid=(n // tb,),
        in_specs=[
            pl.BlockSpec((tb, MLA_RANK), lambda i: (i, 0)),
            pl.BlockSpec((tb, MLA_ROPE), lambda i: (i, 0)),
            pl.BlockSpec((MLA_RANK, MLA_HEADS * HEAD_PAD), lambda i: (0, 0)),
            pl.BlockSpec((MLA_RANK, MLA_HEADS * MLA_V), lambda i: (0, 0)),
            pl.BlockSpec((MLA_ROPE, HEAD_PAD), lambda i: (0, 0)),
        ],
        out_specs=[
            pl.BlockSpec((MLA_HEADS, tb, HEAD_PAD), lambda i: (0, i, 0)),
            pl.BlockSpec((tb, MLA_HEADS * MLA_V), lambda i: (i, 0)),
        ],
        out_shape=[
            jax.ShapeDtypeStruct((MLA_HEADS, n, HEAD_PAD), BF16),
            jax.ShapeDtypeStruct((n, MLA_HEADS * MLA_V), BF16),
        ],
        compiler_params=pltpu.CompilerParams(dimension_semantics=("parallel",)),
        name="cache_kv",
    )(ckv, kr, wk, wv, place)


def _attn_kernel(*refs, has_cache):
    if has_cache:
        q_ref, kn_ref, vn_ref, kc_ref, vc_ref, o_ref = refs
    else:
        q_ref, kn_ref, vn_ref, o_ref = refs
    scale = (MLA_NOPE + MLA_ROPE) ** -0.5

    def head(hd):
        lanes = slice(hd * MLA_V, (hd + 1) * MLA_V)
        q = q_ref[hd]
        s_n = _dot_nt(q, kn_ref[hd])
        if has_cache:
            s_c = _dot_nt(q, kc_ref[hd])
        yield None
        m = jnp.max(s_n, axis=-1, keepdims=True)
        if has_cache:
            m = jnp.maximum(m, jnp.max(s_c, axis=-1, keepdims=True))
        yield None
        p_n = jnp.exp((s_n - m) * scale)
        l = jnp.sum(p_n, axis=-1, keepdims=True)
        o = _dot(p_n.astype(BF16), vn_ref[:, lanes])
        if has_cache:
            p_c = jnp.exp((s_c - m) * scale)
            l = l + jnp.sum(p_c, axis=-1, keepdims=True)
            o = o + _dot(p_c.astype(BF16), vc_ref[:, lanes])
        yield o / l

    outs = []
    for hd in range(0, MLA_HEADS, ATTN_HEADS_IN_FLIGHT):
        outs += _interleave([head(hd + i) for i in range(ATTN_HEADS_IN_FLIGHT)])
    o_ref[...] = jnp.concatenate(outs, axis=-1)


def _attn_ctx_call(q, k, v):
    tq = CTX_LEN
    return pl.pallas_call(
        functools.partial(_attn_kernel, has_cache=False),
        grid=(N_CTX_SEQ,),
        in_specs=[
            pl.BlockSpec((MLA_HEADS, tq, HEAD_PAD), lambda s: (0, s, 0)),
            pl.BlockSpec((MLA_HEADS, tq, HEAD_PAD), lambda s: (0, s, 0)),
            pl.BlockSpec((tq, MLA_HEADS * MLA_V), lambda s: (s, 0)),
        ],
        out_specs=pl.BlockSpec((tq, MLA_HEADS * MLA_V), lambda s: (s, 0)),
        out_shape=jax.ShapeDtypeStruct((T_CTX, MLA_HEADS * MLA_V), F32),
        compiler_params=pltpu.CompilerParams(dimension_semantics=("parallel",), vmem_limit_bytes=VMEM_LIMIT),
        name="attn_ctx",
    )(q, k, v)


def _attn_smp_call(q, k, v, kc, vc):
    tq = 256
    qb_per_seq = SMP_LEN // tq
    ctx_qb = T_CTX // tq
    ctx_kb = T_CTX // SMP_LEN
    return pl.pallas_call(
        functools.partial(_attn_kernel, has_cache=True),
        grid=(N_SMP_SEQ, qb_per_seq),
        in_specs=[
            pl.BlockSpec((MLA_HEADS, tq, HEAD_PAD), lambda b, i: (0, ctx_qb + b * qb_per_seq + i, 0)),
            pl.BlockSpec((MLA_HEADS, SMP_LEN, HEAD_PAD), lambda b, i: (0, ctx_kb + b, 0)),
            pl.BlockSpec((SMP_LEN, MLA_HEADS * MLA_V), lambda b, i: (ctx_kb + b, 0)),
            pl.BlockSpec((MLA_HEADS, PAST_LEN, HEAD_PAD), lambda b, i: (0, b, 0)),
            pl.BlockSpec((PAST_LEN, MLA_HEADS * MLA_V), lambda b, i: (b, 0)),
        ],
        out_specs=pl.BlockSpec((tq, MLA_HEADS * MLA_V), lambda b, i: (b * qb_per_seq + i, 0)),
        out_shape=jax.ShapeDtypeStruct((T_SMP, MLA_HEADS * MLA_V), F32),
        compiler_params=pltpu.CompilerParams(dimension_semantics=("parallel", "parallel"),
                                             vmem_limit_bytes=VMEM_LIMIT),
        name="attn_smp",
    )(q, k, v, kc, vc)


def _out_kernel(xc_ref, xs_ref, mod_ref, ml_ref, hf_ref, hb_ref, cm_ref, attc_ref, atts_ref, wo_ref, g2_ref,
                wq_ref, sk_ref, x1_ref, h2t_ref, st_ref):
    is_ctx = pl.program_id(0) < T_CTX // TM
    att = jnp.where(is_ctx, attc_ref[...], atts_ref[...])
    mod = mod_ref[0]
    g1 = mod[:, 2 * D_MODEL:3 * D_MODEL]
    sh2 = mod[:, 3 * D_MODEL:4 * D_MODEL]
    sc2 = mod[:, 4 * D_MODEL:5 * D_MODEL]
    mlo = _sigmoid(ml_ref[:, 3 * ML_WIDTH:4 * ML_WIDTH]) * (hf_ref[...] + hb_ref[...])
    mix = (_dot(mlo.astype(BF16), wo_ref[0:ML_WIDTH, :])
           + _dot(cm_ref[...].astype(BF16), wo_ref[ML_WIDTH:ML_WIDTH + CM_WIDTH, :])
           + _dot(att.astype(BF16), wo_ref[ML_WIDTH + CM_WIDTH:, :]))
    x1 = _select_x(xc_ref, xs_ref) + g1 * mix
    x1_ref[...] = x1
    h2f = _rms(x1, g2_ref[...]) * (1.0 + sc2) + sh2
    h2t_ref[...] = h2f.T.astype(BF16)
    qp = _dot(h2f.astype(BF16), wq_ref[...]).astype(BF16)
    for hh in range(2 * PEER_HEADS):
        st_ref[hh] = _dot_nt(sk_ref[hh % 2], qp[:, hh * 128:(hh + 1) * 128])


def _out_call(x_ctx, x_smp, mod, ml, hf, hb, cm, att_ctx, att_smp, lw):
    nblk = T_ALL // TM
    ctx_blk = T_CTX // TM
    rows = lambda w: pl.BlockSpec((TM, w), lambda i: (i, 0))
    full = lambda shape: pl.BlockSpec(shape, lambda i: (0,) * len(shape))
    att_w = MLA_HEADS * MLA_V
    attc_spec = pl.BlockSpec((TM, att_w), lambda i: (jnp.minimum(i, ctx_blk - 1), 0))
    atts_spec = pl.BlockSpec((TM, att_w), lambda i: (jnp.maximum(i - ctx_blk, 0), 0))
    return pl.pallas_call(
        _out_kernel,
        grid=(nblk,),
        in_specs=_x_specs(x_ctx, x_smp) + [
            pl.BlockSpec((1, 1, 6 * D_MODEL), lambda i: (_mod_row_of_block(i, TM), 0, 0)),
            rows(4 * ML_WIDTH), rows(ML_WIDTH), rows(ML_WIDTH), rows(CM_WIDTH), attc_spec, atts_spec,
            full((D_MODEL, D_MODEL)), full((1, D_MODEL)), full((D_MODEL, 2 * PEER_HEADS * 128)),
            full((2, PEER_NKEYS, 128)),
        ],
        out_specs=[rows(D_MODEL), pl.BlockSpec((D_MODEL, TM), lambda i: (0, i)),
                   pl.BlockSpec((2 * PEER_HEADS, PEER_NKEYS, TM), lambda i: (0, 0, i))],
        out_shape=[
            jax.ShapeDtypeStruct((T_ALL, D_MODEL), F32),
            jax.ShapeDtypeStruct((D_MODEL, T_ALL), BF16),
            jax.ShapeDtypeStruct((2 * PEER_HEADS, PEER_NKEYS, T_ALL), F32),
        ],
        compiler_params=pltpu.CompilerParams(dimension_semantics=("parallel",), vmem_limit_bytes=VMEM_LIMIT),
        name="proj_out",
    )(x_ctx, x_smp, mod, ml, hf, hb, cm, att_ctx, att_smp, lw["wo"], lw["g2"], lw["wq"], lw["sk"])


def _sorting_network_pairs(n):
    pairs = []
    p = 1
    while p < n:
        k = p
        while k >= 1:
            for j in range(k % p, n - k, 2 * k):
                for i in range(min(k, n - j - k)):
                    if (i + j) // (2 * p) == (i + j + k) // (2 * p):
                        pairs.append((i + j, i + j + k))
            k //= 2
        p *= 2
    return pairs


_SORT16_PAIRS = _sorting_network_pairs(PEER_TOPK)


def _pop16(lists):
    lists = list(lists)
    vals = []
    for k in range(PEER_TOPK):
        m = jnp.max(lists[0], axis=0, keepdims=True)
        vals.append(m)
        if k == PEER_TOPK - 1:
            break
        hit = lists[0] == m
        for i in range(PEER_TOPK - 1 - k):
            lists[i] = jnp.where(hit, lists[i + 1], lists[i])
    return vals


def _top16_rows(s):
    tiles = [s[8 * j:8 * j + 8] for j in range(s.shape[0] // 8)]
    assert len(tiles) == PEER_TOPK
    for i, j in _SORT16_PAIRS:
        tiles[i], tiles[j] = jnp.maximum(tiles[i], tiles[j]), jnp.minimum(tiles[i], tiles[j])
    return _pop16(tiles)


def _count_steps(x, thresholds, below):
    r = jnp.zeros(x.shape, F32)
    for q, t in enumerate(thresholds):
        r = jnp.where((x < t) if below else (x >= t), float(q + 1), r)
    return r


def _pack_rows_bf16(x):
    r, n = x.shape
    x4 = x.reshape(r // 16, 2, 8, n)
    lo = x4[:, 0].reshape(r // 2, n)
    hi = x4[:, 1].reshape(r // 2, n)
    return pltpu.bitcast(pltpu.pack_elementwise([lo, hi], packed_dtype=BF16), BF16)


def _dup_bf16_words(x):
    u = pltpu.bitcast(x.astype(BF16).astype(F32), jnp.int32)
    return u | lax.shift_right_logical(u, jnp.full(u.shape, 16, jnp.int32))


def _rows_to_array(rows, row_iota):
    arr = jnp.zeros(row_iota.shape, F32)
    for i, r in enumerate(rows):
        arr = jnp.where(row_iota == i, r, arr)
    return arr


def _topk_kernel(st_ref, e1_ref, cut_ref, e2_ref, r2_ref):
    n = st_ref.shape[-1]
    row16 = lax.broadcasted_iota(jnp.int32, (PEER_TOPK, n), 0)

    def head(hd, carry):
        s1 = st_ref[2 * hd]
        s2 = st_ref[2 * hd + 1]
        v1 = _top16_rows(s1)
        v2 = _top16_rows(s2)
        rank2 = _count_steps(s2, v2, below=True)
        v1arr = _rows_to_array(v1, row16)
        c = _pop16([v1arr + v2[q] for q in range(PEER_TOPK)])
        tau = c[PEER_TOPK - 1]
        z = jnp.zeros((1, n), F32)
        for ck in c:
            z = z + jnp.exp(ck - c[0])
        sigma = [jnp.min(jnp.where(v1arr + v2[q] >= tau, v1arr, jnp.inf), axis=0, keepdims=True)
                 for q in range(PEER_TOPK)]
        cut = _count_steps(s1, sigma, below=False)
        e1_ref[hd] = _dup_bf16_words(jnp.exp(s1 - v1[0]) / z)
        cut_ref[hd] = _dup_bf16_words(cut)
        e2_ref[hd] = _pack_rows_bf16(jnp.exp(s2 - v2[0]))
        r2_ref[hd] = _pack_rows_bf16(rank2)
        return carry

    lax.fori_loop(0, PEER_HEADS, head, 0)


def _topk_call(st):
    tn = TN_TOPK
    spec = pl.BlockSpec((PEER_HEADS, PEER_NKEYS, tn), lambda i: (0, 0, i))
    stat = lambda dt: jax.ShapeDtypeStruct((PEER_HEADS, PEER_NKEYS, T_ALL), dt)
    return pl.pallas_call(
        _topk_kernel,
        grid=(T_ALL // tn,),
        in_specs=[pl.BlockSpec((2 * PEER_HEADS, PEER_NKEYS, tn), lambda i: (0, 0, i))],
        out_specs=[spec, spec, spec, spec],
        out_shape=[stat(jnp.int32), stat(jnp.int32), stat(BF16), stat(BF16)],
        compiler_params=pltpu.CompilerParams(dimension_semantics=("parallel",), vmem_limit_bytes=VMEM_LIMIT),
        name="peer_topk",
    )(st)


def _row_tile_bf16(row):
    blk = pltpu.bitcast(jnp.broadcast_to(row, (8, row.shape[1])), BF16)
    return jnp.concatenate([blk] * (PEER_NKEYS // 16), axis=0)


def _gelu_tanh_bf16(x):
    z2 = x * (1.5957691216057308 + 0.07135481627159584 * (x * x))
    return x / (1.0 + jnp.exp(-z2))


def _expert_kernel(h2t_ref, u_ref, vt_ref, e1_ref, cut_ref, e2_ref, r2_ref, x1_ref, mod_ref, fg_ref,
                   o_ref, acc_ref, a_ref, *, final_norm):
    j = pl.program_id(1)
    n_tiles = pl.num_programs(1)
    tm = h2t_ref.shape[1]
    a_per_tile = TE_EXP // PEER_NKEYS

    @pl.when(j == 0)
    def _():
        acc_ref[...] = jnp.zeros_like(acc_ref)

    def gate_tile(ai):
        a = j * a_per_tile + ai
        gate = jnp.zeros((PEER_NKEYS, tm), BF16)
        for hd in range(PEER_HEADS):
            cut_a = _row_tile_bf16(cut_ref[hd, pl.ds(a, 1), :])
            e1_a = _row_tile_bf16(e1_ref[hd, pl.ds(a, 1), :])
            gate = gate + jnp.where(r2_ref[hd] < cut_a, e2_ref[hd], jnp.zeros((), BF16)) * e1_a
        return gate

    a_per_sub = EXP_SUB // PEER_NKEYS
    n_sub = TE_EXP // EXP_SUB
    s = [None] * n_sub
    gates = [None] * n_sub
    s[0] = _dot(u_ref[0:EXP_SUB, :], h2t_ref[...])
    gates[0] = [gate_tile(ai) for ai in range(a_per_sub)]
    for sub in range(n_sub):
        if sub + 1 < n_sub:
            s[sub + 1] = _dot(u_ref[(sub + 1) * EXP_SUB:(sub + 2) * EXP_SUB, :], h2t_ref[...])
            gates[sub + 1] = [gate_tile((sub + 1) * a_per_sub + i) for i in range(a_per_sub)]
        g = _gelu_tanh_bf16(_pack_rows_bf16(s[sub]))
        for i in range(a_per_sub):
            r0 = sub * EXP_SUB + i * PEER_NKEYS
            a_ref[r0:r0 + PEER_NKEYS, :] = gates[sub][i] * g[i * PEER_NKEYS:(i + 1) * PEER_NKEYS, :]
    acc_ref[...] += _dot(vt_ref[...], a_ref[...])

    @pl.when(j == n_tiles - 1)
    def _():
        g2 = mod_ref[0][:, 5 * D_MODEL:6 * D_MODEL]
        y = x1_ref[...] + g2 * acc_ref[...].T
        if final_norm:
            y = _rms(y, fg_ref[...])
        o_ref[...] = y


def _expert_call(h2t, u_bf, vt_bf, layer, e1, cut, e2, r2, x1, mod, final_g, final_norm, tok_start, tok_count):
    tm, te = TM_EXP, TE_EXP
    b0 = tok_start // tm
    n_tiles = PEER_EXPERTS // te
    stat = pl.BlockSpec((PEER_HEADS, PEER_NKEYS, tm), lambda i, j: (0, 0, b0 + i))
    return pl.pallas_call(
        functools.partial(_expert_kernel, final_norm=final_norm),
        grid=(tok_count // tm, n_tiles),
        in_specs=[
            pl.BlockSpec((D_MODEL, tm), lambda i, j: (0, b0 + i)),
            pl.BlockSpec((None, te, D_MODEL), lambda i, j: (layer, j, 0)),
            pl.BlockSpec((None, D_MODEL, te), lambda i, j: (layer, 0, j)),
            stat, stat, stat, stat,
            pl.BlockSpec((tm, D_MODEL), lambda i, j: (b0 + i, 0)),
            pl.BlockSpec((1, 1, 6 * D_MODEL), lambda i, j: (_mod_row_of_block(b0 + i, tm), 0, 0)),
            pl.BlockSpec((1, D_MODEL), lambda i, j: (0, 0)),
        ],
        out_specs=pl.BlockSpec((tm, D_MODEL), lambda i, j: (i, 0)),
        out_shape=jax.ShapeDtypeStruct((tok_count, D_MODEL), F32),
        scratch_shapes=[
            pltpu.VMEM((D_MODEL, tm), F32),
            pltpu.VMEM((te, tm), BF16),
        ],
        compiler_params=pltpu.CompilerParams(dimension_semantics=("parallel", "arbitrary"),
                                             vmem_limit_bytes=VMEM_LIMIT),
        name="peer_experts",
    )(h2t, u_bf, vt_bf, e1, cut, e2, r2, x1, mod, final_g)


def _u_prep_kernel(u_ref, o_ref):
    o_ref[...] = u_ref[...].astype(BF16)


def _v_prep_kernel(v_ref, o_ref):
    o_ref[...] = _pack_rows_bf16(v_ref[...]).T


def _table_prep_call(peer_u, peer_v):
    te = TE_EXP
    grid = (DEPTH, PEER_EXPERTS // te)
    rows = pl.BlockSpec((None, te, D_MODEL), lambda l, j: (l, j, 0))
    params = pltpu.CompilerParams(dimension_semantics=("parallel", "parallel"), vmem_limit_bytes=VMEM_LIMIT)
    u_bf = pl.pallas_call(
        _u_prep_kernel, grid=grid, in_specs=[rows], out_specs=rows,
        out_shape=jax.ShapeDtypeStruct((DEPTH, PEER_EXPERTS, D_MODEL), BF16),
        compiler_params=params, name="peer_u_prep")(peer_u)
    vt_bf = pl.pallas_call(
        _v_prep_kernel, grid=grid, in_specs=[rows],
        out_specs=pl.BlockSpec((None, D_MODEL, te), lambda l, j: (l, 0, j)),
        out_shape=jax.ShapeDtypeStruct((DEPTH, D_MODEL, PEER_EXPERTS), BF16),
        compiler_params=params, name="peer_v_prep")(peer_v)
    return u_bf, vt_bf


def _rope_swap_cols(w):
    return jnp.concatenate([-w[:, 8:16], w[:, 0:8], -w[:, 24:32], w[:, 16:24]], axis=1)


def _pad_heads(parts, n_heads):
    k = next(p[0].shape[0] for p in parts if p[0] is not None)
    cols = []
    for hd in range(n_heads):
        for arr, w in parts:
            cols.append(jnp.zeros((k, w), F32) if arr is None else arr[:, hd * w:(hd + 1) * w])
    return jnp.concatenate(cols, axis=1)


def _rope_tables():
    pos = np.arange(SMP_LEN)
    freqs = ROPE_THETA ** (-np.arange(0, ROPE_AXIS, 2, dtype=np.float32) / ROPE_AXIS)
    ang_r = (pos // GRID_W).astype(np.float32)[:, None] * freqs
    ang_c = (pos % GRID_W).astype(np.float32)[:, None] * freqs
    ang = jnp.asarray(np.concatenate([ang_r, ang_r, ang_c, ang_c], axis=1).astype(np.float32))
    cos32 = jnp.cos(ang)
    sin32 = jnp.sin(ang)
    ones = jnp.ones((SMP_LEN, MLA_NOPE), F32)
    cos_s = jnp.concatenate([ones, cos32, ones[:, :HEAD_PAD - MLA_NOPE - MLA_ROPE]], axis=1)
    sin_s = jnp.concatenate([0 * ones, sin32, 0 * ones[:, :HEAD_PAD - MLA_NOPE - MLA_ROPE]], axis=1)
    cos_t = jnp.concatenate([jnp.ones((T_CTX, HEAD_PAD), F32)] + [cos_s] * N_SMP_SEQ, axis=0)
    sin_t = jnp.concatenate([jnp.zeros((T_CTX, HEAD_PAD), F32)] + [sin_s] * N_SMP_SEQ, axis=0)
    return cos_t, sin_t


def _layer_weights(l, norm1_g, w_in, mlstm_gate_b, cm_norm_g, cm_ws, cm_b, mla_q_norm_g, mla_w_uq, mla_kv_norm_g,
                   mla_w_ukv, w_out, norm2_g, peer_w_q, peer_subkeys):
    w = w_in[l]
    o_g = 4 * ML_WIDTH
    o_cm = o_g + 16
    o_cq = o_cm + 2 * CM_WIDTH
    o_ckv = o_cq + MLA_RANK
    o_kr = o_ckv + MLA_RANK
    w_kr = w[:, o_kr:o_kr + MLA_ROPE]
    zeros_l = jnp.zeros((D_MODEL, MLA_NOPE), F32)
    zeros_r = jnp.zeros((D_MODEL, HEAD_PAD - MLA_NOPE - MLA_ROPE), F32)
    kr128 = jnp.concatenate([zeros_l, w_kr, zeros_r], axis=1)
    krsw128 = jnp.concatenate([zeros_l, _rope_swap_cols(w_kr), zeros_r], axis=1)
    uq = mla_w_uq[l].reshape(MLA_RANK, MLA_HEADS, MLA_NOPE + MLA_ROPE)
    uq_nope = uq[:, :, :MLA_NOPE].reshape(MLA_RANK, -1)
    uq_rope = uq[:, :, MLA_NOPE:].reshape(MLA_RANK, -1)
    uq_rope_sw = jnp.concatenate(
        [_rope_swap_cols(uq_rope[:, hd * MLA_ROPE:(hd + 1) * MLA_ROPE]) for hd in range(MLA_HEADS)], axis=1)
    pad_w = HEAD_PAD - MLA_NOPE - MLA_ROPE
    ukv = mla_w_ukv[l].reshape(MLA_RANK, MLA_HEADS, MLA_NOPE + MLA_V)
    uk = ukv[:, :, :MLA_NOPE].reshape(MLA_RANK, -1)
    uv = ukv[:, :, MLA_NOPE:].reshape(MLA_RANK, -1)
    gb = mlstm_gate_b[l]
    return {
        "g1": norm1_g[l].reshape(1, D_MODEL),
        "wml": w[:, 0:o_g].astype(BF16),
        "wgt": w[:, o_g:o_cm].T.astype(BF16),
        "gb_col": gb.reshape(16, 1),
        "wcm": w[:, o_cm:o_cq].astype(BF16),
        "wmla": jnp.concatenate([w[:, o_cq:o_kr], kr128, krsw128], axis=1).astype(BF16),
        "wkr": w_kr.astype(BF16),
        "cmg": cm_norm_g[l].reshape(1, CM_WIDTH),
        "ws": cm_ws[l].astype(BF16),
        "cmb": jnp.repeat(cm_b[l].T, CM_WIDTH // CM_GROUPS, axis=1),
        "qg": mla_q_norm_g[l].reshape(1, MLA_RANK),
        "kvg": mla_kv_norm_g[l].reshape(1, MLA_RANK),
        "wuqa": _pad_heads([(uq_nope, MLA_NOPE), (uq_rope, MLA_ROPE), (None, pad_w)], MLA_HEADS).astype(BF16),
        "wuqb": _pad_heads([(None, MLA_NOPE), (uq_rope_sw, MLA_ROPE), (None, pad_w)], MLA_HEADS).astype(BF16),
        "wk": _pad_heads([(uk, MLA_NOPE), (None, HEAD_PAD - MLA_NOPE)], MLA_HEADS).astype(BF16),
        "wv": uv.astype(BF16),
        "wo": w_out[l].astype(BF16),
        "g2": norm2_g[l].reshape(1, D_MODEL),
        "wq": peer_w_q[l].astype(BF16),
        "sk": peer_subkeys[l].astype(BF16),
    }


def kernel(x_prompt, x_sample, c, cache_mla_ckv, cache_mla_krope, state_mlstm_C, state_mlstm_n, state_mlstm_m, c_ctx, norm1_g, ada_w, ada_b, w_in, mlstm_gate_b, cm_norm_g, cm_ws, cm_b, mla_q_norm_g, mla_w_uq, mla_kv_norm_g, mla_w_ukv, w_out, norm2_g, peer_w_q, peer_subkeys, peer_u, peer_v, final_g):
    x_ctx = x_prompt.reshape(T_CTX, D_MODEL)
    x_smp = x_sample.reshape(T_SMP, D_MODEL)
    cvecs = jnp.concatenate([c_ctx[None, :], c, jnp.zeros((N_MOD_ROWS - 1 - N_SMP_SEQ, D_MODEL), F32)], axis=0)
    mod_all = _ada_call(cvecs, ada_w, ada_b)
    cos_t, sin_t = _rope_tables()
    place = jnp.concatenate([jnp.zeros((MLA_ROPE, MLA_NOPE), F32), jnp.eye(MLA_ROPE, dtype=F32),
                             jnp.zeros((MLA_ROPE, HEAD_PAD - MLA_NOPE - MLA_ROPE), F32)], axis=1).astype(BF16)
    final_g2 = final_g.reshape(1, D_MODEL)
    u_bf, vt_bf = _table_prep_call(peer_u, peer_v)

    ckvs, krs, Cs, ns, ms = [], [], [], [], []
    for l in range(DEPTH):
        lw = _layer_weights(l, norm1_g, w_in, mlstm_gate_b, cm_norm_g, cm_ws, cm_b, mla_q_norm_g, mla_w_uq,
                            mla_kv_norm_g, mla_w_ukv, w_out, norm2_g, peer_w_q, peer_subkeys)
        mod = mod_all[l].reshape(N_MOD_ROWS, 1, 6 * D_MODEL)
        ml, gates_t, cm, ckvn, kr, q, k, v = _proj_call(x_ctx, x_smp, mod, lw, cos_t, sin_t)

        c_blk = jnp.einsum('bdhij,hg->bdhigj', state_mlstm_C[:, l], jnp.eye(ML_HEADS, dtype=F32))
        c0 = jnp.concatenate([jnp.zeros((N_CTX_SEQ, 2, ML_WIDTH, ML_WIDTH), F32),
                              c_blk.reshape(N_SMP_SEQ, 2, ML_WIDTH, ML_WIDTH)], axis=0)
        n0 = jnp.concatenate([jnp.zeros((N_CTX_SEQ, 2, 1, ML_WIDTH), F32),
                              state_mlstm_n[:, l].reshape(N_SMP_SEQ, 2, 1, ML_WIDTH)], axis=0)
        n0 = jnp.pad(n0, ((0, 0), (0, 0), (0, 7), (0, 0)))
        m0 = jnp.concatenate([jnp.zeros((N_CTX_SEQ, 2, ML_HEADS), F32), state_mlstm_m[:, l]], axis=0)
        m0 = jnp.pad(jnp.broadcast_to(m0[..., None], m0.shape + (128,)), ((0, 0), (0, 0), (0, 8 - ML_HEADS), (0, 0)))
        hf, hb, c_fin, n_fin, m_fin = _mlstm_call(ml, gates_t, c0, n0, m0)

        kc, vc = _cache_kv_call(cache_mla_ckv[:, l].reshape(N_SMP_SEQ * PAST_LEN, MLA_RANK),
                                cache_mla_krope[:, l].reshape(N_SMP_SEQ * PAST_LEN, MLA_ROPE),
                                lw["wk"], lw["wv"], place)
        att_ctx = _attn_ctx_call(q, k, v)
        att_smp = _attn_smp_call(q, k, v, kc, vc)

        x1, h2t, st = _out_call(x_ctx, x_smp, mod, ml, hf, hb, cm, att_ctx, att_smp, lw)
        e1, cut, e2, r2 = _topk_call(st)
        experts = functools.partial(_expert_call, h2t, u_bf, vt_bf, l, e1, cut, e2, r2, x1, mod, final_g2)
        if l < DEPTH - 1:
            x_ctx = x_smp = experts(False, 0, T_ALL)
        else:
            x_ctx = experts(True, 0, T_CTX)
            x_smp = experts(True, T_CTX, T_SMP)

        ckvs.append(ckvn[:T_CTX].reshape(N_CTX_SEQ, CTX_LEN, MLA_RANK))
        krs.append(kr[:T_CTX].reshape(N_CTX_SEQ, CTX_LEN, MLA_ROPE))
        Cs.append(jnp.stack([c_fin[:N_CTX_SEQ, :, hd * ML_DIM:(hd + 1) * ML_DIM, hd * ML_DIM:(hd + 1) * ML_DIM]
                             for hd in range(ML_HEADS)], axis=2))
        ns.append(n_fin[:N_CTX_SEQ, :, 0, :].reshape(N_CTX_SEQ, 2, ML_HEADS, ML_DIM))
        ms.append(m_fin[:N_CTX_SEQ, :, 0:ML_HEADS, 0])

    y_prompt = x_ctx.reshape(N_CTX_SEQ, CTX_LEN, D_MODEL)
    y_sample = x_smp.reshape(N_SMP_SEQ, SMP_LEN, D_MODEL)
    return (y_prompt, y_sample, jnp.stack(ckvs, axis=1), jnp.stack(krs, axis=1), jnp.stack(Cs, axis=1),
            jnp.stack(ns, axis=1), jnp.stack(ms, axis=1))
```

```python
import functools

import numpy as np
import jax
import jax.numpy as jnp
from jax import lax
from jax.experimental import pallas as pl
from jax.experimental.pallas import tpu as pltpu

F32 = jnp.float32
BF16 = jnp.bfloat16

D_MODEL = 1024
N_CTX_SEQ = 16
CTX_LEN = 256
N_SMP_SEQ = 4
SMP_LEN = 2048
PAST_LEN = 256
DEPTH = 2
GRID_W = 64
EPS = 1e-6
T_CTX = N_CTX_SEQ * CTX_LEN
T_SMP = N_SMP_SEQ * SMP_LEN
T_ALL = T_CTX + T_SMP
N_MOD_ROWS = 8

ML_HEADS = 4
ML_DIM = 64
ML_WIDTH = 256
CHUNK = 128
CM_GROUPS = 4
CM_WIDTH = 256
MLA_HEADS = 8
MLA_NOPE = 64
MLA_ROPE = 32
MLA_V = 64
MLA_RANK = 256
HEAD_PAD = 128
ROPE_AXIS = 16
ROPE_THETA = 10000.0
PEER_HEADS = 8
PEER_NKEYS = 128
PEER_EXPERTS = PEER_NKEYS * PEER_NKEYS
PEER_TOPK = 16

TM = 256
TN_TOPK = 256
TM_EXP = 512
TE_EXP = 4096
EXP_SUB = 512
TE_PREP = 2048
ATTN_HEADS_IN_FLIGHT = 2
VMEM_LIMIT = 56 * 1024 * 1024

NEG_INF = float("-inf")


def _dot(a, b):
    return jnp.dot(a, b, preferred_element_type=F32)


def _dot_nt(a, b):
    return lax.dot_general(a, b, (((1,), (1,)), ((), ())), preferred_element_type=F32)


def _dot_tn(a, b):
    return lax.dot_general(a, b, (((0,), (0,)), ((), ())), preferred_element_type=F32)


def _split3(a):
    a1 = a.astype(BF16)
    r1 = a - a1.astype(F32)
    a2 = r1.astype(BF16)
    a3 = (r1 - a2.astype(F32)).astype(BF16)
    return a1, a2, a3


def _rms(x, g):
    return x * lax.rsqrt(jnp.mean(x * x, axis=-1, keepdims=True) + EPS) * g


def _sigmoid(x):
    return 1.0 / (1.0 + jnp.exp(-x))


def _log_sigmoid(x):
    return jnp.minimum(x, 0.0) - jnp.log(1.0 + jnp.exp(-jnp.abs(x)))


def _gelu_tanh(x):
    return 0.5 * x * (1.0 + jnp.tanh(0.7978845608028654 * (x + 0.044715 * (x * x * x))))


def _mod_row_of_block(i, rows_per_block):
    ctx_blocks = T_CTX // rows_per_block
    per_seq = SMP_LEN // rows_per_block
    return jnp.where(i < ctx_blocks, 0, 1 + (i - ctx_blocks) // per_seq)


def _ada_kernel(cv_ref, w_ref, b_ref, o_ref):
    cv = cv_ref[...]
    s = cv * _sigmoid(cv)
    w = w_ref[0]
    w1, w2, w3 = _split3(w)
    s1, s2, s3 = _split3(s)
    acc = _dot(s1, w1) + (_dot(s1, w2) + _dot(s2, w1)) + (_dot(s1, w3) + _dot(s2, w2) + _dot(s3, w1))
    o_ref[0] = acc + b_ref[0]


def _ada_call(cvecs, ada_w, ada_b):
    tn = 1024
    return pl.pallas_call(
        _ada_kernel,
        grid=(DEPTH, 6 * D_MODEL // tn),
        in_specs=[
            pl.BlockSpec((N_MOD_ROWS, D_MODEL), lambda l, j: (0, 0)),
            pl.BlockSpec((1, D_MODEL, tn), lambda l, j: (l, 0, j)),
            pl.BlockSpec((1, 1, tn), lambda l, j: (l, 0, j)),
        ],
        out_specs=pl.BlockSpec((1, N_MOD_ROWS, tn), lambda l, j: (l, 0, j)),
        out_shape=jax.ShapeDtypeStruct((DEPTH, N_MOD_ROWS, 6 * D_MODEL), F32),
        compiler_params=pltpu.CompilerParams(dimension_semantics=("parallel", "parallel")),
        name="ada_mod",
    )(cvecs, ada_w, ada_b.reshape(DEPTH, 1, 6 * D_MODEL))


def _x_specs(x_ctx, x_smp, blk=lambda i: i):
    ctx_blk = T_CTX // TM
    smp_off = x_smp.shape[0] // TM - T_SMP // TM
    return [pl.BlockSpec((TM, D_MODEL), lambda i: (jnp.minimum(blk(i), ctx_blk - 1), 0)),
            pl.BlockSpec((TM, D_MODEL), lambda i: (jnp.maximum(blk(i) - ctx_blk, 0) + smp_off, 0))]


def _select_x(xc_ref, xs_ref):
    return jnp.where(pl.program_id(0) < T_CTX // TM, xc_ref[...], xs_ref[...])


def _proj_kernel(xc_ref, xs_ref, mod_ref, g1_ref, wml_ref, wgt_ref, gbc_ref, wcm_ref, wmla_ref, wkr_ref,
                 cmg_ref, ws_ref, cmb_ref, qg_ref, kvg_ref, wuqa_ref, wuqb_ref, wk_ref, wvt_ref, cos_ref, sin_ref,
                 ml_ref, gatest_ref, cm_ref, ckvn_ref, kr_ref, q_ref, k_ref, vt_ref):
    x = _select_x(xc_ref, xs_ref)
    mod = mod_ref[0]
    sh1 = mod[:, 0:D_MODEL]
    sc1 = mod[:, D_MODEL:2 * D_MODEL]
    h = _rms(x, g1_ref[...]) * (1.0 + sc1) + sh1
    hb = h.astype(BF16)

    ml_ref[...] = _dot(hb, wml_ref[...])
    gatest_ref[...] = _dot_nt(wgt_ref[...], hb) + gbc_ref[...]

    cm = _dot(hb, wcm_ref[...])
    u = cm[:, 0:CM_WIDTH]
    vn = _rms(cm[:, CM_WIDTH:2 * CM_WIDTH], cmg_ref[...]).astype(BF16)
    lane_group = lax.broadcasted_iota(jnp.int32, (CHUNK, CM_WIDTH), 1) >> 6
    for c in range(TM // CHUNK):
        rows = slice(c * CHUNK, (c + 1) * CHUNK)
        vc = vn[rows]
        mixed = jnp.zeros((CHUNK, CM_WIDTH), F32)
        for g in range(CM_GROUPS):
            mixed = jnp.where(lane_group == g, _dot(ws_ref[g], vc), mixed)
        cm_ref[rows, :] = u[rows] * (mixed + cmb_ref[...])

    mla = _dot(hb, wmla_ref[...])
    qn = _rms(mla[:, 0:MLA_RANK], qg_ref[...]).astype(BF16)
    ckvn = _rms(mla[:, MLA_RANK:2 * MLA_RANK], kvg_ref[...])
    ckvn_ref[...] = ckvn
    kr_ref[...] = _dot(hb, wkr_ref[...])
    cos = cos_ref[...]
    sin = sin_ref[...]
    kr_rot = mla[:, 2 * MLA_RANK:2 * MLA_RANK + HEAD_PAD] * cos + mla[:, 2 * MLA_RANK + HEAD_PAD:] * sin
    qa = _dot(qn, wuqa_ref[...])
    qb = _dot(qn, wuqb_ref[...])
    ckvb = ckvn.astype(BF16)
    kp = _dot(ckvb, wk_ref[...])
    for hd in range(MLA_HEADS):
        cols = slice(hd * HEAD_PAD, (hd + 1) * HEAD_PAD)
        q_ref[hd] = (qa[:, cols] * cos + qb[:, cols] * sin).astype(BF16)
        k_ref[hd] = (kp[:, cols] + kr_rot).astype(BF16)
    vt_ref[...] = _dot_nt(wvt_ref[...], ckvb).astype(BF16)


def _proj_call(x_ctx, x_smp, mod, lw, cos_t, sin_t):
    nblk = T_ALL // TM
    full = lambda shape: pl.BlockSpec(shape, lambda i: (0,) * len(shape))
    rows = lambda w: pl.BlockSpec((TM, w), lambda i: (i, 0))
    in_specs = _x_specs(x_ctx, x_smp) + [
        pl.BlockSpec((1, 1, 6 * D_MODEL), lambda i: (_mod_row_of_block(i, TM), 0, 0)),
        full((1, D_MODEL)),
        full((D_MODEL, 4 * ML_WIDTH)),
        full((16, D_MODEL)),
        full((16, 1)),
        full((D_MODEL, 2 * CM_WIDTH)),
        full((D_MODEL, 2 * MLA_RANK + 2 * HEAD_PAD)),
        full((D_MODEL, MLA_ROPE)),
        full((1, CM_WIDTH)),
        full((CM_GROUPS, CHUNK, CHUNK)),
        full((CHUNK, CM_WIDTH)),
        full((1, MLA_RANK)),
        full((1, MLA_RANK)),
        full((MLA_RANK, MLA_HEADS * HEAD_PAD)),
        full((MLA_RANK, MLA_HEADS * HEAD_PAD)),
        full((MLA_RANK, MLA_HEADS * HEAD_PAD)),
        full((MLA_HEADS * MLA_V, MLA_RANK)),
        rows(HEAD_PAD),
        rows(HEAD_PAD),
    ]
    out_specs = [
        rows(4 * ML_WIDTH),
        pl.BlockSpec((16, TM), lambda i: (0, i)),
        rows(CM_WIDTH),
        rows(MLA_RANK),
        rows(MLA_ROPE),
        pl.BlockSpec((MLA_HEADS, TM, HEAD_PAD), lambda i: (0, i, 0)),
        pl.BlockSpec((MLA_HEADS, TM, HEAD_PAD), lambda i: (0, i, 0)),
        pl.BlockSpec((MLA_HEADS * MLA_V, TM), lambda i: (0, i)),
    ]
    out_shape = [
        jax.ShapeDtypeStruct((T_ALL, 4 * ML_WIDTH), F32),
        jax.ShapeDtypeStruct((16, T_ALL), F32),
        jax.ShapeDtypeStruct((T_ALL, CM_WIDTH), F32),
        jax.ShapeDtypeStruct((T_ALL, MLA_RANK), F32),
        jax.ShapeDtypeStruct((T_ALL, MLA_ROPE), F32),
        jax.ShapeDtypeStruct((MLA_HEADS, T_ALL, HEAD_PAD), BF16),
        jax.ShapeDtypeStruct((MLA_HEADS, T_ALL, HEAD_PAD), BF16),
        jax.ShapeDtypeStruct((MLA_HEADS * MLA_V, T_ALL), BF16),
    ]
    return pl.pallas_call(
        _proj_kernel,
        grid=(nblk,),
        in_specs=in_specs,
        out_specs=out_specs,
        out_shape=out_shape,
        compiler_params=pltpu.CompilerParams(dimension_semantics=("parallel",), vmem_limit_bytes=VMEM_LIMIT),
        name="proj_in",
    )(x_ctx, x_smp, mod, lw["g1"], lw["wml"], lw["wgt"], lw["gb_col"], lw["wcm"], lw["wmla"],
      lw["wkr"], lw["cmg"], lw["ws"], lw["cmb"], lw["qg"], lw["kvg"], lw["wuqa"], lw["wuqb"], lw["wk"], lw["wvt"],
      cos_t, sin_t)


def _mlstm_schedule():
    seq, fwd, bwd, first = [], [], [], []
    base = 0
    for s in range(N_CTX_SEQ + N_SMP_SEQ):
        nc = (CTX_LEN if s < N_CTX_SEQ else SMP_LEN) // CHUNK
        for j in range(nc):
            seq.append(s)
            fwd.append(base + j)
            bwd.append(base + nc - 1 - j)
            first.append(1 if j == 0 else 0)
        base += nc
    as_i32 = lambda a: jnp.asarray(np.asarray(a, np.int32))
    return as_i32(seq), as_i32(fwd), as_i32(bwd), as_i32(first)


def _scan_cummax(x, direction):
    L = CHUNK
    rows = x.shape[0]
    x = jnp.concatenate([x, x], axis=0)
    lane = lax.broadcasted_iota(jnp.int32, x.shape, 1)
    k = 1
    while k < L:
        if direction == 0:
            shifted = jnp.where(lane >= k, pltpu.roll(x, k, axis=1), NEG_INF)
        else:
            shifted = jnp.where(lane < L - k, pltpu.roll(x, L - k, axis=1), NEG_INF)
        x = jnp.maximum(x, shifted)
        k *= 2
    return x[0:rows]


def _rows_to_lane_cols(rows, eye, rep, pieces):
    x = jnp.concatenate([jnp.broadcast_to(rows[h:h + 1, :], (rep, CHUNK)) for h in range(ML_HEADS)], axis=0)
    out = None
    for _ in range(pieces):
        xb = x.astype(BF16)
        part = _dot_nt(eye, xb)
        out = part if out is None else out + part
        x = x - xb.astype(F32)
    return out


def _per_head_lanes(x512, lane64):
    lo = jnp.where(lane64, x512[:, 0:128], x512[:, 128:256])
    hi = jnp.where(lane64, x512[:, 256:384], x512[:, 384:512])
    return jnp.concatenate([lo, hi], axis=1)


def _mlstm_direction(ml, g_row, direction, c_ref, n_ref, m_ref):
    L = CHUNK
    t_idx = lax.broadcasted_iota(jnp.int32, (L, L), 0)
    s_idx = lax.broadcasted_iota(jnp.int32, (L, L), 1)
    visible = (s_idx <= t_idx) if direction == 0 else (s_idx >= t_idx)
    tri = jnp.where(visible, 1.0, 0.0).astype(BF16)
    eye = jnp.where(s_idx == t_idx, 1.0, 0.0).astype(BF16)
    lane64 = lax.broadcasted_iota(jnp.int32, (1, 128), 1) < ML_DIM
    head_of_lane = lax.broadcasted_iota(jnp.int32, (1, ML_WIDTH), 1) >> 6
    same_head = ((lax.broadcasted_iota(jnp.int32, (ML_WIDTH, ML_WIDTH), 0) >> 6)
                 == (lax.broadcasted_iota(jnp.int32, (ML_WIDTH, ML_WIDTH), 1) >> 6))

    i0 = 8 * direction
    i_row = g_row[i0:i0 + ML_HEADS, :]
    lf_row = _log_sigmoid(g_row[i0 + ML_HEADS:i0 + 2 * ML_HEADS, :])
    r1, r2, r3 = _split3(lf_row)
    b_row = _dot_nt(r1, tri) + _dot_nt(r2, tri) + _dot_nt(r3, tri)
    yield None
    m_rep = m_ref[0, direction, 0:ML_HEADS, :]
    a_row = i_row - b_row
    g_row_ = jnp.maximum(m_rep, _scan_cummax(a_row, direction))
    yield None
    b_end = jnp.sum(lf_row, axis=1, keepdims=True)
    log_w = b_end - b_row + i_row
    m_new = jnp.maximum(b_end + m_rep, jnp.max(log_w, axis=1, keepdims=True))
    w_k_row = jnp.exp(log_w - m_new)
    decay_rep = jnp.exp(b_end + m_rep - m_new)

    yield None
    g512 = _rows_to_lane_cols(g_row_, eye, 128, 2)
    g_full = _per_head_lanes(g512, lane64)
    yield None
    b_full = _rows_to_lane_cols(b_row, eye, ML_DIM, 2)
    wk_full = _rows_to_lane_cols(w_k_row, eye, ML_DIM, 1)
    m_full = _per_head_lanes(jnp.concatenate([m_rep[h:h + 1, :] for h in range(ML_HEADS)], axis=1), lane64)
    decay_full = _per_head_lanes(jnp.concatenate([decay_rep[h:h + 1, :] for h in range(ML_HEADS)], axis=1),
                                 lane64)

    yield None
    q = ml[:, 0:ML_WIDTH]
    k = ml[:, ML_WIDTH:2 * ML_WIDTH] * (ML_DIM ** -0.5)
    v = ml[:, 2 * ML_WIDTH:3 * ML_WIDTH]
    qb = q.astype(BF16)
    kb = k.astype(BF16)
    vb = v.astype(BF16)
    num = jnp.zeros((L, ML_WIDTH), F32)
    rowsum = jnp.zeros((L, ML_WIDTH), F32)
    for hd in range(ML_HEADS):
        w_intra = jnp.where(visible, jnp.exp(a_row[hd:hd + 1, :] - g512[:, hd * 128:(hd + 1) * 128]), 0.0)
        q_h = jnp.where(head_of_lane == hd, q, 0.0).astype(BF16)
        sw = _dot_nt(q_h, kb) * w_intra
        num = jnp.where(head_of_lane == hd, _dot(sw.astype(BF16), vb), num)
        rowsum = jnp.where(head_of_lane == hd, jnp.sum(sw, axis=-1, keepdims=True), rowsum)
        yield None

    C = c_ref[0, direction]
    n_row = n_ref[0, direction, 0:1, :]
    w_inter = jnp.exp(m_full - g_full)
    block_ones = jnp.where(same_head, 1.0, 0.0).astype(BF16)
    qn = _dot((q * n_row).astype(BF16), block_ones)
    num = num + w_inter * _dot(qb, C.astype(BF16))
    den = rowsum + w_inter * qn
    h_out = num / jnp.maximum(jnp.abs(den), jnp.exp(-(b_full + g_full)))

    kw = wk_full * k
    c_ref[0, direction] = decay_full * C + jnp.where(same_head, _dot_tn(kw.astype(BF16), vb), 0.0)
    n_ref[0, direction, 0:1, :] = decay_full * n_row + jnp.sum(kw, axis=0, keepdims=True)
    m_ref[0, direction, 0:ML_HEADS, :] = m_new
    yield h_out


def _interleave(stage_generators):
    results = [None] * len(stage_generators)
    live = list(range(len(stage_generators)))
    while live:
        for idx in list(live):
            try:
                value = next(stage_generators[idx])
                if value is not None:
                    results[idx] = value
            except StopIteration:
                live.remove(idx)
    return results


def _mlstm_kernel(seq_ref, fwd_ref, bwd_ref, first_ref, mlf_ref, mlb_ref, grf_ref, grb_ref,
                  c0_ref, n0_ref, m0_ref, hf_ref, hb_ref, c_ref, n_ref, m_ref):
    step = pl.program_id(0)

    @pl.when(first_ref[step] == 1)
    def _():
        c_ref[...] = c0_ref[...]
        n_ref[...] = n0_ref[...]
        m_ref[...] = m0_ref[...]

    hf, hb = _interleave([_mlstm_direction(mlf_ref[...], grf_ref[...], 0, c_ref, n_ref, m_ref),
                          _mlstm_direction(mlb_ref[...], grb_ref[...], 1, c_ref, n_ref, m_ref)])
    hf_ref[...] = hf
    hb_ref[...] = hb


def _mlstm_call(ml, gates_t, c0, n0, m0):
    seq, fwd, bwd, first = _mlstm_schedule()
    nseq = N_CTX_SEQ + N_SMP_SEQ
    nsteps = int(seq.shape[0])
    f_rows = lambda w: pl.BlockSpec((CHUNK, w), lambda i, sq, fw, bw, fs: (fw[i], 0))
    b_rows = lambda w: pl.BlockSpec((CHUNK, w), lambda i, sq, fw, bw, fs: (bw[i], 0))
    c_spec = pl.BlockSpec((1, 2, ML_WIDTH, ML_WIDTH), lambda i, sq, fw, bw, fs: (sq[i], 0, 0, 0))
    n_spec = pl.BlockSpec((1, 2, 8, ML_WIDTH), lambda i, sq, fw, bw, fs: (sq[i], 0, 0, 0))
    m_spec = pl.BlockSpec((1, 2, 8, 128), lambda i, sq, fw, bw, fs: (sq[i], 0, 0, 0))
    grid_spec = pltpu.PrefetchScalarGridSpec(
        num_scalar_prefetch=4,
        grid=(nsteps,),
        in_specs=[
            f_rows(4 * ML_WIDTH), b_rows(4 * ML_WIDTH),
            pl.BlockSpec((16, CHUNK), lambda i, sq, fw, bw, fs: (0, fw[i])),
            pl.BlockSpec((16, CHUNK), lambda i, sq, fw, bw, fs: (0, bw[i])),
            c_spec, n_spec, m_spec,
        ],
        out_specs=[f_rows(ML_WIDTH), b_rows(ML_WIDTH), c_spec, n_spec, m_spec],
    )
    return pl.pallas_call(
        _mlstm_kernel,
        grid_spec=grid_spec,
        out_shape=[
            jax.ShapeDtypeStruct((T_ALL, ML_WIDTH), F32),
            jax.ShapeDtypeStruct((T_ALL, ML_WIDTH), F32),
            jax.ShapeDtypeStruct((nseq, 2, ML_WIDTH, ML_WIDTH), F32),
            jax.ShapeDtypeStruct((nseq, 2, 8, ML_WIDTH), F32),
            jax.ShapeDtypeStruct((nseq, 2, 8, 128), F32),
        ],
        compiler_params=pltpu.CompilerParams(dimension_semantics=("arbitrary",), vmem_limit_bytes=VMEM_LIMIT),
        name="mlstm",
    )(seq, fwd, bwd, first, ml, ml, gates_t, gates_t, c0, n0, m0)


def _cache_kv_kernel(ckv_ref, kr_ref, wk_ref, wvt_ref, place_ref, k_ref, vt_ref):
    ckvb = ckv_ref[...].astype(BF16)
    kp = _dot(ckvb, wk_ref[...])
    kr128 = _dot(kr_ref[...].astype(BF16), place_ref[...])
    for hd in range(MLA_HEADS):
        k_ref[hd] = (kp[:, hd * HEAD_PAD:(hd + 1) * HEAD_PAD] + kr128).astype(BF16)
    vt_ref[...] = _dot_nt(wvt_ref[...], ckvb).astype(BF16)


def _cache_kv_call(ckv, kr, wk, wvt, place):
    n = N_SMP_SEQ * PAST_LEN
    tb = PAST_LEN
    return pl.pallas_call(
        _cache_kv_kernel,
        grid=(n // tb,),
        in_specs=[
            pl.BlockSpec((tb, MLA_RANK), lambda i: (i, 0)),
            pl.BlockSpec((tb, MLA_ROPE), lambda i: (i, 0)),
            pl.BlockSpec((MLA_RANK, MLA_HEADS * HEAD_PAD), lambda i: (0, 0)),
            pl.BlockSpec((MLA_HEADS * MLA_V, MLA_RANK), lambda i: (0, 0)),
            pl.BlockSpec((MLA_ROPE, HEAD_PAD), lambda i: (0, 0)),
        ],
        out_specs=[
            pl.BlockSpec((MLA_HEADS, tb, HEAD_PAD), lambda i: (0, i, 0)),
            pl.BlockSpec((MLA_HEADS * MLA_V, tb), lambda i: (0, i)),
        ],
        out_shape=[
            jax.ShapeDtypeStruct((MLA_HEADS, n, HEAD_PAD), BF16),
            jax.ShapeDtypeStruct((MLA_HEADS * MLA_V, n), BF16),
        ],
        compiler_params=pltpu.CompilerParams(dimension_semantics=("parallel",)),
        name="cache_kv",
    )(ckv, kr, wk, wvt, place)


def _attn_kernel(*refs, has_cache):
    if has_cache:
        q_ref, kn_ref, vn_ref, kc_ref, vc_ref, o_ref = refs
    else:
        q_ref, kn_ref, vn_ref, o_ref = refs
    scale = (MLA_NOPE + MLA_ROPE) ** -0.5

    def head(hd):
        rows = slice(hd * MLA_V, (hd + 1) * MLA_V)
        q = q_ref[hd]
        s_n = _dot_nt(kn_ref[hd], q)
        if has_cache:
            s_c = _dot_nt(kc_ref[hd], q)
        yield None
        m = jnp.max(s_n, axis=0, keepdims=True)
        if has_cache:
            m = jnp.maximum(m, jnp.max(s_c, axis=0, keepdims=True))
        yield None
        p_n = jnp.exp((s_n - m) * scale)
        l = jnp.sum(p_n, axis=0, keepdims=True)
        o = _dot(vn_ref[rows, :], p_n.astype(BF16))
        if has_cache:
            p_c = jnp.exp((s_c - m) * scale)
            l = l + jnp.sum(p_c, axis=0, keepdims=True)
            o = o + _dot(vc_ref[rows, :], p_c.astype(BF16))
        yield o / l

    outs = []
    for hd in range(0, MLA_HEADS, ATTN_HEADS_IN_FLIGHT):
        outs += _interleave([head(hd + i) for i in range(ATTN_HEADS_IN_FLIGHT)])
    o_ref[...] = jnp.concatenate(outs, axis=0).T


def _attn_ctx_call(q, k, v):
    tq = CTX_LEN
    return pl.pallas_call(
        functools.partial(_attn_kernel, has_cache=False),
        grid=(N_CTX_SEQ,),
        in_specs=[
            pl.BlockSpec((MLA_HEADS, tq, HEAD_PAD), lambda s: (0, s, 0)),
            pl.BlockSpec((MLA_HEADS, tq, HEAD_PAD), lambda s: (0, s, 0)),
            pl.BlockSpec((MLA_HEADS * MLA_V, tq), lambda s: (0, s)),
        ],
        out_specs=pl.BlockSpec((tq, MLA_HEADS * MLA_V), lambda s: (s, 0)),
        out_shape=jax.ShapeDtypeStruct((T_CTX, MLA_HEADS * MLA_V), F32),
        compiler_params=pltpu.CompilerParams(dimension_semantics=("parallel",), vmem_limit_bytes=VMEM_LIMIT),
        name="attn_ctx",
    )(q, k, v)


def _attn_smp_call(q, k, v, kc, vc):
    tq = 256
    qb_per_seq = SMP_LEN // tq
    ctx_qb = T_CTX // tq
    ctx_kb = T_CTX // SMP_LEN
    return pl.pallas_call(
        functools.partial(_attn_kernel, has_cache=True),
        grid=(N_SMP_SEQ, qb_per_seq),
        in_specs=[
            pl.BlockSpec((MLA_HEADS, tq, HEAD_PAD), lambda b, i: (0, ctx_qb + b * qb_per_seq + i, 0)),
            pl.BlockSpec((MLA_HEADS, SMP_LEN, HEAD_PAD), lambda b, i: (0, ctx_kb + b, 0)),
            pl.BlockSpec((MLA_HEADS * MLA_V, SMP_LEN), lambda b, i: (0, ctx_kb + b)),
            pl.BlockSpec((MLA_HEADS, PAST_LEN, HEAD_PAD), lambda b, i: (0, b, 0)),
            pl.BlockSpec((MLA_HEADS * MLA_V, PAST_LEN), lambda b, i: (0, b)),
        ],
        out_specs=pl.BlockSpec((tq, MLA_HEADS * MLA_V), lambda b, i: (b * qb_per_seq + i, 0)),
        out_shape=jax.ShapeDtypeStruct((T_SMP, MLA_HEADS * MLA_V), F32),
        compiler_params=pltpu.CompilerParams(dimension_semantics=("parallel", "parallel"),
                                             vmem_limit_bytes=VMEM_LIMIT),
        name="attn_smp",
    )(q, k, v, kc, vc)


def _out_kernel(xc_ref, xs_ref, mod_ref, ml_ref, hf_ref, hb_ref, cm_ref, attc_ref, atts_ref, wo_ref, g2_ref,
                wq_ref, sk_ref, x1_ref, h2t_ref, st_ref):
    is_ctx = pl.program_id(0) < T_CTX // TM
    att = jnp.where(is_ctx, attc_ref[...], atts_ref[...])
    mod = mod_ref[0]
    g1 = mod[:, 2 * D_MODEL:3 * D_MODEL]
    sh2 = mod[:, 3 * D_MODEL:4 * D_MODEL]
    sc2 = mod[:, 4 * D_MODEL:5 * D_MODEL]
    mlo = _sigmoid(ml_ref[:, 3 * ML_WIDTH:4 * ML_WIDTH]) * (hf_ref[...] + hb_ref[...])
    mix = (_dot(mlo.astype(BF16), wo_ref[0:ML_WIDTH, :])
           + _dot(cm_ref[...].astype(BF16), wo_ref[ML_WIDTH:ML_WIDTH + CM_WIDTH, :])
           + _dot(att.astype(BF16), wo_ref[ML_WIDTH + CM_WIDTH:, :]))
    x1 = _select_x(xc_ref, xs_ref) + g1 * mix
    x1_ref[...] = x1
    h2f = _rms(x1, g2_ref[...]) * (1.0 + sc2) + sh2
    h2t_ref[...] = h2f.T.astype(BF16)
    qp = _dot(h2f.astype(BF16), wq_ref[...]).astype(BF16)
    for hh in range(2 * PEER_HEADS):
        st_ref[hh] = _dot_nt(sk_ref[hh % 2], qp[:, hh * 128:(hh + 1) * 128])


def _out_call(x_ctx, x_smp, mod, ml, hf, hb, cm, att_ctx, att_smp, lw):
    nblk = T_ALL // TM
    ctx_blk = T_CTX // TM
    rows = lambda w: pl.BlockSpec((TM, w), lambda i: (i, 0))
    full = lambda shape: pl.BlockSpec(shape, lambda i: (0,) * len(shape))
    att_w = MLA_HEADS * MLA_V
    attc_spec = pl.BlockSpec((TM, att_w), lambda i: (jnp.minimum(i, ctx_blk - 1), 0))
    atts_spec = pl.BlockSpec((TM, att_w), lambda i: (jnp.maximum(i - ctx_blk, 0), 0))
    return pl.pallas_call(
        _out_kernel,
        grid=(nblk,),
        in_specs=_x_specs(x_ctx, x_smp) + [
            pl.BlockSpec((1, 1, 6 * D_MODEL), lambda i: (_mod_row_of_block(i, TM), 0, 0)),
            rows(4 * ML_WIDTH), rows(ML_WIDTH), rows(ML_WIDTH), rows(CM_WIDTH), attc_spec, atts_spec,
            full((D_MODEL, D_MODEL)), full((1, D_MODEL)), full((D_MODEL, 2 * PEER_HEADS * 128)),
            full((2, PEER_NKEYS, 128)),
        ],
        out_specs=[rows(D_MODEL), pl.BlockSpec((D_MODEL, TM), lambda i: (0, i)),
                   pl.BlockSpec((2 * PEER_HEADS, PEER_NKEYS, TM), lambda i: (0, 0, i))],
        out_shape=[
            jax.ShapeDtypeStruct((T_ALL, D_MODEL), F32),
            jax.ShapeDtypeStruct((D_MODEL, T_ALL), BF16),
            jax.ShapeDtypeStruct((2 * PEER_HEADS, PEER_NKEYS, T_ALL), F32),
        ],
        compiler_params=pltpu.CompilerParams(dimension_semantics=("parallel",), vmem_limit_bytes=VMEM_LIMIT),
        name="proj_out",
    )(x_ctx, x_smp, mod, ml, hf, hb, cm, att_ctx, att_smp, lw["wo"], lw["g2"], lw["wq"], lw["sk"])


def _sorting_network_pairs(n):
    pairs = []
    p = 1
    while p < n:
        k = p
        while k >= 1:
            for j in range(k % p, n - k, 2 * k):
                for i in range(min(k, n - j - k)):
                    if (i + j) // (2 * p) == (i + j + k) // (2 * p):
                        pairs.append((i + j, i + j + k))
            k //= 2
        p *= 2
    return pairs


_SORT16_PAIRS = _sorting_network_pairs(PEER_TOPK)


def _pop16(lists):
    lists = list(lists)
    vals = []
    for k in range(PEER_TOPK):
        m = jnp.max(lists[0], axis=0, keepdims=True)
        vals.append(m)
        if k == PEER_TOPK - 1:
            break
        hit = lists[0] == m
        for i in range(PEER_TOPK - 1 - k):
            lists[i] = jnp.where(hit, lists[i + 1], lists[i])
    return vals


def _top16_rows(s):
    tiles = [s[8 * j:8 * j + 8] for j in range(s.shape[0] // 8)]
    assert len(tiles) == PEER_TOPK
    for i, j in _SORT16_PAIRS:
        tiles[i], tiles[j] = jnp.maximum(tiles[i], tiles[j]), jnp.minimum(tiles[i], tiles[j])
    return _pop16(tiles)


def _count_steps(x, thresholds, below):
    r = jnp.zeros(x.shape, F32)
    for q, t in enumerate(thresholds):
        r = jnp.where((x < t) if below else (x >= t), float(q + 1), r)
    return r


def _pack_rows_bf16(x):
    r, n = x.shape
    x4 = x.reshape(r // 16, 2, 8, n)
    lo = x4[:, 0].reshape(r // 2, n)
    hi = x4[:, 1].reshape(r // 2, n)
    return pltpu.bitcast(pltpu.pack_elementwise([lo, hi], packed_dtype=BF16), BF16)


def _dup_bf16_words(x):
    u = pltpu.bitcast(x.astype(BF16).astype(F32), jnp.int32)
    return u | lax.shift_right_logical(u, jnp.full(u.shape, 16, jnp.int32))


def _rows_to_array(rows, row_iota):
    arr = jnp.zeros(row_iota.shape, F32)
    for i, r in enumerate(rows):
        arr = jnp.where(row_iota == i, r, arr)
    return arr


def _topk_kernel(st_ref, e1_ref, cut_ref, e2_ref, r2_ref):
    n = st_ref.shape[-1]
    row16 = lax.broadcasted_iota(jnp.int32, (PEER_TOPK, n), 0)

    def head(hd, carry):
        s1 = st_ref[2 * hd]
        s2 = st_ref[2 * hd + 1]
        v1 = _top16_rows(s1)
        v2 = _top16_rows(s2)
        rank2 = _count_steps(s2, v2, below=True)
        v1arr = _rows_to_array(v1, row16)
        c = _pop16([v1arr + v2[q] for q in range(PEER_TOPK)])
        tau = c[PEER_TOPK - 1]
        z = jnp.zeros((1, n), F32)
        for ck in c:
            z = z + jnp.exp(ck - c[0])
        sigma = [jnp.min(jnp.where(v1arr + v2[q] >= tau, v1arr, jnp.inf), axis=0, keepdims=True)
                 for q in range(PEER_TOPK)]
        cut = _count_steps(s1, sigma, below=False)
        e1_ref[hd] = _dup_bf16_words(jnp.exp(s1 - v1[0]) / z)
        cut_ref[hd] = _dup_bf16_words(cut)
        e2_ref[hd] = _pack_rows_bf16(jnp.exp(s2 - v2[0]))
        r2_ref[hd] = _pack_rows_bf16(rank2)
        return carry

    lax.fori_loop(0, PEER_HEADS, head, 0)


def _topk_call(st):
    tn = TN_TOPK
    spec = pl.BlockSpec((PEER_HEADS, PEER_NKEYS, tn), lambda i: (0, 0, i))
    stat = lambda dt: jax.ShapeDtypeStruct((PEER_HEADS, PEER_NKEYS, T_ALL), dt)
    return pl.pallas_call(
        _topk_kernel,
        grid=(T_ALL // tn,),
        in_specs=[pl.BlockSpec((2 * PEER_HEADS, PEER_NKEYS, tn), lambda i: (0, 0, i))],
        out_specs=[spec, spec, spec, spec],
        out_shape=[stat(jnp.int32), stat(jnp.int32), stat(BF16), stat(BF16)],
        compiler_params=pltpu.CompilerParams(dimension_semantics=("parallel",), vmem_limit_bytes=VMEM_LIMIT),
        name="peer_topk",
    )(st)


def _row_tile_bf16(row):
    blk = pltpu.bitcast(jnp.broadcast_to(row, (8, row.shape[1])), BF16)
    return jnp.concatenate([blk] * (PEER_NKEYS // 16), axis=0)


def _gelu_tanh_bf16(x):
    z2 = x * (1.5957691216057308 + 0.07135481627159584 * (x * x))
    return x / (1.0 + jnp.exp(-z2))


def _expert_kernel(h2t_ref, u_ref, vt_ref, e1_ref, cut_ref, e2_ref, r2_ref, x1_ref, mod_ref, fg_ref,
                   o_ref, acc_ref, a_ref, *, final_norm):
    j = pl.program_id(1)
    n_tiles = pl.num_programs(1)
    tm = h2t_ref.shape[1]
    a_per_tile = TE_EXP // PEER_NKEYS

    @pl.when(j == 0)
    def _():
        acc_ref[...] = jnp.zeros_like(acc_ref)

    def gate_tile(ai):
        a = j * a_per_tile + ai
        gate = jnp.zeros((PEER_NKEYS, tm), BF16)
        for hd in range(PEER_HEADS):
            cut_a = _row_tile_bf16(cut_ref[hd, pl.ds(a, 1), :])
            e1_a = _row_tile_bf16(e1_ref[hd, pl.ds(a, 1), :])
            gate = gate + jnp.where(r2_ref[hd] < cut_a, e2_ref[hd], jnp.zeros((), BF16)) * e1_a
        return gate

    a_per_sub = EXP_SUB // PEER_NKEYS
    n_sub = TE_EXP // EXP_SUB
    s = [None] * n_sub
    gates = [None] * n_sub
    s[0] = _dot(u_ref[0:EXP_SUB, :], h2t_ref[...])
    gates[0] = [gate_tile(ai) for ai in range(a_per_sub)]
    for sub in range(n_sub):
        if sub + 1 < n_sub:
            s[sub + 1] = _dot(u_ref[(sub + 1) * EXP_SUB:(sub + 2) * EXP_SUB, :], h2t_ref[...])
            gates[sub + 1] = [gate_tile((sub + 1) * a_per_sub + i) for i in range(a_per_sub)]
        g = _gelu_tanh_bf16(_pack_rows_bf16(s[sub]))
        for i in range(a_per_sub):
            r0 = sub * EXP_SUB + i * PEER_NKEYS
            a_ref[r0:r0 + PEER_NKEYS, :] = gates[sub][i] * g[i * PEER_NKEYS:(i + 1) * PEER_NKEYS, :]
    acc_ref[...] += _dot(vt_ref[...], a_ref[...])

    @pl.when(j == n_tiles - 1)
    def _():
        g2 = mod_ref[0][:, 5 * D_MODEL:6 * D_MODEL]
        y = x1_ref[...] + g2 * acc_ref[...].T
        if final_norm:
            y = _rms(y, fg_ref[...])
        o_ref[...] = y


def _expert_call(h2t, u_bf, vt_bf, layer, e1, cut, e2, r2, x1, mod, final_g, final_norm, tok_start, tok_count):
    tm, te = TM_EXP, TE_EXP
    b0 = tok_start // tm
    n_tiles = PEER_EXPERTS // te
    once = dict(pipeline_mode=pl.Buffered(1))
    stat = pl.BlockSpec((PEER_HEADS, PEER_NKEYS, tm), lambda i, j: (0, 0, b0 + i), **once)
    return pl.pallas_call(
        functools.partial(_expert_kernel, final_norm=final_norm),
        grid=(tok_count // tm, n_tiles),
        in_specs=[
            pl.BlockSpec((D_MODEL, tm), lambda i, j: (0, b0 + i), **once),
            pl.BlockSpec((None, te, D_MODEL), lambda i, j: (layer, j, 0)),
            pl.BlockSpec((None, D_MODEL, te), lambda i, j: (layer, 0, j)),
            stat, stat, stat, stat,
            pl.BlockSpec((tm, D_MODEL), lambda i, j: (b0 + i, 0), **once),
            pl.BlockSpec((1, 1, 6 * D_MODEL), lambda i, j: (_mod_row_of_block(b0 + i, tm), 0, 0)),
            pl.BlockSpec((1, D_MODEL), lambda i, j: (0, 0)),
        ],
        out_specs=pl.BlockSpec((tm, D_MODEL), lambda i, j: (i, 0)),
        out_shape=jax.ShapeDtypeStruct((tok_count, D_MODEL), F32),
        scratch_shapes=[
            pltpu.VMEM((D_MODEL, tm), F32),
            pltpu.VMEM((te, tm), BF16),
        ],
        compiler_params=pltpu.CompilerParams(dimension_semantics=("parallel", "arbitrary"),
                                             vmem_limit_bytes=VMEM_LIMIT),
        name="peer_experts",
    )(h2t, u_bf, vt_bf, e1, cut, e2, r2, x1, mod, final_g)


def _u_prep_kernel(u_ref, o_ref):
    o_ref[...] = u_ref[...].astype(BF16)


def _v_prep_kernel(v_ref, o_ref):
    o_ref[...] = _pack_rows_bf16(v_ref[...]).T


def _table_prep_call(peer_u, peer_v):
    te = TE_PREP
    grid = (DEPTH, PEER_EXPERTS // te)
    rows = pl.BlockSpec((None, te, D_MODEL), lambda l, j: (l, j, 0))
    params = pltpu.CompilerParams(dimension_semantics=("parallel", "parallel"), vmem_limit_bytes=VMEM_LIMIT)
    u_bf = pl.pallas_call(
        _u_prep_kernel, grid=grid, in_specs=[rows], out_specs=rows,
        out_shape=jax.ShapeDtypeStruct((DEPTH, PEER_EXPERTS, D_MODEL), BF16),
        compiler_params=params, name="peer_u_prep")(peer_u)
    vt_bf = pl.pallas_call(
        _v_prep_kernel, grid=grid, in_specs=[rows],
        out_specs=pl.BlockSpec((None, D_MODEL, te), lambda l, j: (l, 0, j)),
        out_shape=jax.ShapeDtypeStruct((DEPTH, D_MODEL, PEER_EXPERTS), BF16),
        compiler_params=params, name="peer_v_prep")(peer_v)
    return u_bf, vt_bf


def _rope_swap_cols(w):
    return jnp.concatenate([-w[:, 8:16], w[:, 0:8], -w[:, 24:32], w[:, 16:24]], axis=1)


def _pad_heads(parts, n_heads):
    k = next(p[0].shape[0] for p in parts if p[0] is not None)
    cols = []
    for hd in range(n_heads):
        for arr, w in parts:
            cols.append(jnp.zeros((k, w), F32) if arr is None else arr[:, hd * w:(hd + 1) * w])
    return jnp.concatenate(cols, axis=1)


def _rope_tables():
    pos = np.arange(SMP_LEN)
    freqs = ROPE_THETA ** (-np.arange(0, ROPE_AXIS, 2, dtype=np.float32) / ROPE_AXIS)
    ang_r = (pos // GRID_W).astype(np.float32)[:, None] * freqs
    ang_c = (pos % GRID_W).astype(np.float32)[:, None] * freqs
    ang = jnp.asarray(np.concatenate([ang_r, ang_r, ang_c, ang_c], axis=1).astype(np.float32))
    cos32 = jnp.cos(ang)
    sin32 = jnp.sin(ang)
    ones = jnp.ones((SMP_LEN, MLA_NOPE), F32)
    cos_s = jnp.concatenate([ones, cos32, ones[:, :HEAD_PAD - MLA_NOPE - MLA_ROPE]], axis=1)
    sin_s = jnp.concatenate([0 * ones, sin32, 0 * ones[:, :HEAD_PAD - MLA_NOPE - MLA_ROPE]], axis=1)
    cos_t = jnp.concatenate([jnp.ones((T_CTX, HEAD_PAD), F32)] + [cos_s] * N_SMP_SEQ, axis=0)
    sin_t = jnp.concatenate([jnp.zeros((T_CTX, HEAD_PAD), F32)] + [sin_s] * N_SMP_SEQ, axis=0)
    return cos_t, sin_t


def _layer_weights(l, norm1_g, w_in, mlstm_gate_b, cm_norm_g, cm_ws, cm_b, mla_q_norm_g, mla_w_uq, mla_kv_norm_g,
                   mla_w_ukv, w_out, norm2_g, peer_w_q, peer_subkeys):
    w = w_in[l]
    o_g = 4 * ML_WIDTH
    o_cm = o_g + 16
    o_cq = o_cm + 2 * CM_WIDTH
    o_ckv = o_cq + MLA_RANK
    o_kr = o_ckv + MLA_RANK
    w_kr = w[:, o_kr:o_kr + MLA_ROPE]
    zeros_l = jnp.zeros((D_MODEL, MLA_NOPE), F32)
    zeros_r = jnp.zeros((D_MODEL, HEAD_PAD - MLA_NOPE - MLA_ROPE), F32)
    kr128 = jnp.concatenate([zeros_l, w_kr, zeros_r], axis=1)
    krsw128 = jnp.concatenate([zeros_l, _rope_swap_cols(w_kr), zeros_r], axis=1)
    uq = mla_w_uq[l].reshape(MLA_RANK, MLA_HEADS, MLA_NOPE + MLA_ROPE)
    uq_nope = uq[:, :, :MLA_NOPE].reshape(MLA_RANK, -1)
    uq_rope = uq[:, :, MLA_NOPE:].reshape(MLA_RANK, -1)
    uq_rope_sw = jnp.concatenate(
        [_rope_swap_cols(uq_rope[:, hd * MLA_ROPE:(hd + 1) * MLA_ROPE]) for hd in range(MLA_HEADS)], axis=1)
    pad_w = HEAD_PAD - MLA_NOPE - MLA_ROPE
    ukv = mla_w_ukv[l].reshape(MLA_RANK, MLA_HEADS, MLA_NOPE + MLA_V)
    uk = ukv[:, :, :MLA_NOPE].reshape(MLA_RANK, -1)
    uv = ukv[:, :, MLA_NOPE:].reshape(MLA_RANK, -1)
    gb = mlstm_gate_b[l]
    return {
        "g1": norm1_g[l].reshape(1, D_MODEL),
        "wml": w[:, 0:o_g].astype(BF16),
        "wgt": w[:, o_g:o_cm].T.astype(BF16),
        "gb_col": gb.reshape(16, 1),
        "wcm": w[:, o_cm:o_cq].astype(BF16),
        "wmla": jnp.concatenate([w[:, o_cq:o_kr], kr128, krsw128], axis=1).astype(BF16),
        "wkr": w_kr.astype(BF16),
        "cmg": cm_norm_g[l].reshape(1, CM_WIDTH),
        "ws": cm_ws[l].astype(BF16),
        "cmb": jnp.repeat(cm_b[l].T, CM_WIDTH // CM_GROUPS, axis=1),
        "qg": mla_q_norm_g[l].reshape(1, MLA_RANK),
        "kvg": mla_kv_norm_g[l].reshape(1, MLA_RANK),
        "wuqa": _pad_heads([(uq_nope, MLA_NOPE), (uq_rope, MLA_ROPE), (None, pad_w)], MLA_HEADS).astype(BF16),
        "wuqb": _pad_heads([(None, MLA_NOPE), (uq_rope_sw, MLA_ROPE), (None, pad_w)], MLA_HEADS).astype(BF16),
        "wk": _pad_heads([(uk, MLA_NOPE), (None, HEAD_PAD - MLA_NOPE)], MLA_HEADS).astype(BF16),
        "wvt": uv.T.astype(BF16),
        "wo": w_out[l].astype(BF16),
        "g2": norm2_g[l].reshape(1, D_MODEL),
        "wq": peer_w_q[l].astype(BF16),
        "sk": peer_subkeys[l].astype(BF16),
    }


def kernel(x_prompt, x_sample, c, cache_mla_ckv, cache_mla_krope, state_mlstm_C, state_mlstm_n, state_mlstm_m, c_ctx, norm1_g, ada_w, ada_b, w_in, mlstm_gate_b, cm_norm_g, cm_ws, cm_b, mla_q_norm_g, mla_w_uq, mla_kv_norm_g, mla_w_ukv, w_out, norm2_g, peer_w_q, peer_subkeys, peer_u, peer_v, final_g):
    x_ctx = x_prompt.reshape(T_CTX, D_MODEL)
    x_smp = x_sample.reshape(T_SMP, D_MODEL)
    cvecs = jnp.concatenate([c_ctx[None, :], c, jnp.zeros((N_MOD_ROWS - 1 - N_SMP_SEQ, D_MODEL), F32)], axis=0)
    mod_all = _ada_call(cvecs, ada_w, ada_b)
    cos_t, sin_t = _rope_tables()
    place = jnp.concatenate([jnp.zeros((MLA_ROPE, MLA_NOPE), F32), jnp.eye(MLA_ROPE, dtype=F32),
                             jnp.zeros((MLA_ROPE, HEAD_PAD - MLA_NOPE - MLA_ROPE), F32)], axis=1).astype(BF16)
    final_g2 = final_g.reshape(1, D_MODEL)
    u_bf, vt_bf = _table_prep_call(peer_u, peer_v)

    ckvs, krs, Cs, ns, ms = [], [], [], [], []
    for l in range(DEPTH):
        lw = _layer_weights(l, norm1_g, w_in, mlstm_gate_b, cm_norm_g, cm_ws, cm_b, mla_q_norm_g, mla_w_uq,
                            mla_kv_norm_g, mla_w_ukv, w_out, norm2_g, peer_w_q, peer_subkeys)
        mod = mod_all[l].reshape(N_MOD_ROWS, 1, 6 * D_MODEL)
        ml, gates_t, cm, ckvn, kr, q, k, v = _proj_call(x_ctx, x_smp, mod, lw, cos_t, sin_t)

        c_blk = jnp.einsum('bdhij,hg->bdhigj', state_mlstm_C[:, l], jnp.eye(ML_HEADS, dtype=F32))
        c0 = jnp.concatenate([jnp.zeros((N_CTX_SEQ, 2, ML_WIDTH, ML_WIDTH), F32),
                              c_blk.reshape(N_SMP_SEQ, 2, ML_WIDTH, ML_WIDTH)], axis=0)
        n0 = jnp.concatenate([jnp.zeros((N_CTX_SEQ, 2, 1, ML_WIDTH), F32),
                              state_mlstm_n[:, l].reshape(N_SMP_SEQ, 2, 1, ML_WIDTH)], axis=0)
        n0 = jnp.pad(n0, ((0, 0), (0, 0), (0, 7), (0, 0)))
        m0 = jnp.concatenate([jnp.zeros((N_CTX_SEQ, 2, ML_HEADS), F32), state_mlstm_m[:, l]], axis=0)
        m0 = jnp.pad(jnp.broadcast_to(m0[..., None], m0.shape + (128,)), ((0, 0), (0, 0), (0, 8 - ML_HEADS), (0, 0)))
        hf, hb, c_fin, n_fin, m_fin = _mlstm_call(ml, gates_t, c0, n0, m0)

        kc, vc = _cache_kv_call(cache_mla_ckv[:, l].reshape(N_SMP_SEQ * PAST_LEN, MLA_RANK),
                                cache_mla_krope[:, l].reshape(N_SMP_SEQ * PAST_LEN, MLA_ROPE),
                                lw["wk"], lw["wvt"], place)
        att_ctx = _attn_ctx_call(q, k, v)
        att_smp = _attn_smp_call(q, k, v, kc, vc)

        x1, h2t, st = _out_call(x_ctx, x_smp, mod, ml, hf, hb, cm, att_ctx, att_smp, lw)
        e1, cut, e2, r2 = _topk_call(st)
        experts = functools.partial(_expert_call, h2t, u_bf, vt_bf, l, e1, cut, e2, r2, x1, mod, final_g2)
        if l < DEPTH - 1:
            x_ctx = x_smp = experts(False, 0, T_ALL)
        else:
            x_ctx = experts(True, 0, T_CTX)
            x_smp = experts(True, T_CTX, T_SMP)

        ckvs.append(ckvn[:T_CTX].reshape(N_CTX_SEQ, CTX_LEN, MLA_RANK))
        krs.append(kr[:T_CTX].reshape(N_CTX_SEQ, CTX_LEN, MLA_ROPE))
        Cs.append(jnp.stack([c_fin[:N_CTX_SEQ, :, hd * ML_DIM:(hd + 1) * ML_DIM, hd * ML_DIM:(hd + 1) * ML_DIM]
                             for hd in range(ML_HEADS)], axis=2))
        ns.append(n_fin[:N_CTX_SEQ, :, 0, :].reshape(N_CTX_SEQ, 2, ML_HEADS, ML_DIM))
        ms.append(m_fin[:N_CTX_SEQ, :, 0:ML_HEADS, 0])

    y_prompt = x_ctx.reshape(N_CTX_SEQ, CTX_LEN, D_MODEL)
    y_sample = x_smp.reshape(N_SMP_SEQ, SMP_LEN, D_MODEL)
    return (y_prompt, y_sample, jnp.stack(ckvs, axis=1), jnp.stack(krs, axis=1), jnp.stack(Cs, axis=1),
            jnp.stack(ns, axis=1), jnp.stack(ms, axis=1))
```

```python
import functools

import numpy as np
import jax
import jax.numpy as jnp
from jax import lax
from jax.experimental import pallas as pl
from jax.experimental.pallas import tpu as pltpu

F32 = jnp.float32
BF16 = jnp.bfloat16

D_MODEL = 1024
N_CTX_SEQ = 16
CTX_LEN = 256
N_SMP_SEQ = 4
SMP_LEN = 2048
PAST_LEN = 256
DEPTH = 2
GRID_W = 64
EPS = 1e-6
T_CTX = N_CTX_SEQ * CTX_LEN
T_SMP = N_SMP_SEQ * SMP_LEN
T_ALL = T_CTX + T_SMP
N_MOD_ROWS = 8

ML_HEADS = 4
ML_DIM = 64
ML_WIDTH = 256
CHUNK = 128
CM_GROUPS = 4
CM_WIDTH = 256
MLA_HEADS = 8
MLA_NOPE = 64
MLA_ROPE = 32
MLA_V = 64
MLA_RANK = 256
HEAD_PAD = 128
ROPE_AXIS = 16
ROPE_THETA = 10000.0
PEER_HEADS = 8
PEER_NKEYS = 128
PEER_EXPERTS = PEER_NKEYS * PEER_NKEYS
PEER_TOPK = 16

TM = 256
TN_TOPK = 256
TM_EXP = 512
TE_EXP = 2048
EXP_SUB = 512
TE_PREP = 2048
ATTN_HEADS_IN_FLIGHT = 2
VMEM_LIMIT = 56 * 1024 * 1024

NEG_INF = float("-inf")


def _dot(a, b):
    return jnp.dot(a, b, preferred_element_type=F32)


def _dot_nt(a, b):
    return lax.dot_general(a, b, (((1,), (1,)), ((), ())), preferred_element_type=F32)


def _dot_tn(a, b):
    return lax.dot_general(a, b, (((0,), (0,)), ((), ())), preferred_element_type=F32)


def _split3(a):
    a1 = a.astype(BF16)
    r1 = a - a1.astype(F32)
    a2 = r1.astype(BF16)
    a3 = (r1 - a2.astype(F32)).astype(BF16)
    return a1, a2, a3


def _rms(x, g):
    return x * lax.rsqrt(jnp.mean(x * x, axis=-1, keepdims=True) + EPS) * g


def _sigmoid(x):
    return 1.0 / (1.0 + jnp.exp(-x))


def _log_sigmoid(x):
    return jnp.minimum(x, 0.0) - jnp.log(1.0 + jnp.exp(-jnp.abs(x)))


def _gelu_tanh(x):
    return 0.5 * x * (1.0 + jnp.tanh(0.7978845608028654 * (x + 0.044715 * (x * x * x))))


def _mod_row_of_block(i, rows_per_block):
    ctx_blocks = T_CTX // rows_per_block
    per_seq = SMP_LEN // rows_per_block
    return jnp.where(i < ctx_blocks, 0, 1 + (i - ctx_blocks) // per_seq)


def _ada_kernel(cv_ref, w_ref, b_ref, o_ref):
    cv = cv_ref[...]
    s = cv * _sigmoid(cv)
    w = w_ref[0]
    w1, w2, w3 = _split3(w)
    s1, s2, s3 = _split3(s)
    acc = _dot(s1, w1) + (_dot(s1, w2) + _dot(s2, w1)) + (_dot(s1, w3) + _dot(s2, w2) + _dot(s3, w1))
    o_ref[0] = acc + b_ref[0]


def _ada_call(cvecs, ada_w, ada_b):
    tn = 1024
    return pl.pallas_call(
        _ada_kernel,
        grid=(DEPTH, 6 * D_MODEL // tn),
        in_specs=[
            pl.BlockSpec((N_MOD_ROWS, D_MODEL), lambda l, j: (0, 0)),
            pl.BlockSpec((1, D_MODEL, tn), lambda l, j: (l, 0, j)),
            pl.BlockSpec((1, 1, tn), lambda l, j: (l, 0, j)),
        ],
        out_specs=pl.BlockSpec((1, N_MOD_ROWS, tn), lambda l, j: (l, 0, j)),
        out_shape=jax.ShapeDtypeStruct((DEPTH, N_MOD_ROWS, 6 * D_MODEL), F32),
        compiler_params=pltpu.CompilerParams(dimension_semantics=("parallel", "parallel")),
        name="ada_mod",
    )(cvecs, ada_w, ada_b.reshape(DEPTH, 1, 6 * D_MODEL))


def _x_specs(x_ctx, x_smp, blk=lambda i: i):
    ctx_blk = T_CTX // TM
    smp_off = x_smp.shape[0] // TM - T_SMP // TM
    return [pl.BlockSpec((TM, D_MODEL), lambda i: (jnp.minimum(blk(i), ctx_blk - 1), 0)),
            pl.BlockSpec((TM, D_MODEL), lambda i: (jnp.maximum(blk(i) - ctx_blk, 0) + smp_off, 0))]


def _select_x(xc_ref, xs_ref):
    return jnp.where(pl.program_id(0) < T_CTX // TM, xc_ref[...], xs_ref[...])


def _proj_kernel(xc_ref, xs_ref, mod_ref, g1_ref, wml_ref, wgt_ref, gbc_ref, wcm_ref, wmla_ref, wkr_ref,
                 cmg_ref, ws_ref, cmb_ref, qg_ref, kvg_ref, wuqa_ref, wuqb_ref, wk_ref, wv_ref, cos_ref, sin_ref,
                 ml_ref, gatest_ref, cm_ref, ckvn_ref, kr_ref, q_ref, k_ref, v_ref):
    x = _select_x(xc_ref, xs_ref)
    mod = mod_ref[0]
    sh1 = mod[:, 0:D_MODEL]
    sc1 = mod[:, D_MODEL:2 * D_MODEL]
    h = _rms(x, g1_ref[...]) * (1.0 + sc1) + sh1
    hb = h.astype(BF16)

    ml_ref[...] = _dot(hb, wml_ref[...])
    gatest_ref[...] = _dot_nt(wgt_ref[...], hb) + gbc_ref[...]

    cm = _dot(hb, wcm_ref[...])
    u = cm[:, 0:CM_WIDTH]
    vn = _rms(cm[:, CM_WIDTH:2 * CM_WIDTH], cmg_ref[...]).astype(BF16)
    lane_group = lax.broadcasted_iota(jnp.int32, (CHUNK, CM_WIDTH), 1) >> 6
    for c in range(TM // CHUNK):
        rows = slice(c * CHUNK, (c + 1) * CHUNK)
        vc = vn[rows]
        mixed = jnp.zeros((CHUNK, CM_WIDTH), F32)
        for g in range(CM_GROUPS):
            mixed = jnp.where(lane_group == g, _dot(ws_ref[g], vc), mixed)
        cm_ref[rows, :] = u[rows] * (mixed + cmb_ref[...])

    mla = _dot(hb, wmla_ref[...])
    qn = _rms(mla[:, 0:MLA_RANK], qg_ref[...]).astype(BF16)
    ckvn = _rms(mla[:, MLA_RANK:2 * MLA_RANK], kvg_ref[...])
    ckvn_ref[...] = ckvn
    kr_ref[...] = _dot(hb, wkr_ref[...])
    cos = cos_ref[...]
    sin = sin_ref[...]
    kr_rot = mla[:, 2 * MLA_RANK:2 * MLA_RANK + HEAD_PAD] * cos + mla[:, 2 * MLA_RANK + HEAD_PAD:] * sin
    qa = _dot(qn, wuqa_ref[...])
    qb = _dot(qn, wuqb_ref[...])
    ckvb = ckvn.astype(BF16)
    kp = _dot(ckvb, wk_ref[...])
    for hd in range(MLA_HEADS):
        cols = slice(hd * HEAD_PAD, (hd + 1) * HEAD_PAD)
        q_ref[hd] = (qa[:, cols] * cos + qb[:, cols] * sin).astype(BF16)
        k_ref[hd] = (kp[:, cols] + kr_rot).astype(BF16)
    v_ref[...] = _dot(ckvb, wv_ref[...]).astype(BF16)


def _proj_call(x_ctx, x_smp, mod, lw, cos_t, sin_t):
    nblk = T_ALL // TM
    full = lambda shape: pl.BlockSpec(shape, lambda i: (0,) * len(shape))
    rows = lambda w: pl.BlockSpec((TM, w), lambda i: (i, 0))
    in_specs = _x_specs(x_ctx, x_smp) + [
        pl.BlockSpec((1, 1, 6 * D_MODEL), lambda i: (_mod_row_of_block(i, TM), 0, 0)),
        full((1, D_MODEL)),
        full((D_MODEL, 4 * ML_WIDTH)),
        full((16, D_MODEL)),
        full((16, 1)),
        full((D_MODEL, 2 * CM_WIDTH)),
        full((D_MODEL, 2 * MLA_RANK + 2 * HEAD_PAD)),
        full((D_MODEL, MLA_ROPE)),
        full((1, CM_WIDTH)),
        full((CM_GROUPS, CHUNK, CHUNK)),
        full((CHUNK, CM_WIDTH)),
        full((1, MLA_RANK)),
        full((1, MLA_RANK)),
        full((MLA_RANK, MLA_HEADS * HEAD_PAD)),
        full((MLA_RANK, MLA_HEADS * HEAD_PAD)),
        full((MLA_RANK, MLA_HEADS * HEAD_PAD)),
        full((MLA_RANK, MLA_HEADS * MLA_V)),
        rows(HEAD_PAD),
        rows(HEAD_PAD),
    ]
    out_specs = [
        rows(4 * ML_WIDTH),
        pl.BlockSpec((16, TM), lambda i: (0, i)),
        rows(CM_WIDTH),
        rows(MLA_RANK),
        rows(MLA_ROPE),
        pl.BlockSpec((MLA_HEADS, TM, HEAD_PAD), lambda i: (0, i, 0)),
        pl.BlockSpec((MLA_HEADS, TM, HEAD_PAD), lambda i: (0, i, 0)),
        rows(MLA_HEADS * MLA_V),
    ]
    out_shape = [
        jax.ShapeDtypeStruct((T_ALL, 4 * ML_WIDTH), F32),
        jax.ShapeDtypeStruct((16, T_ALL), F32),
        jax.ShapeDtypeStruct((T_ALL, CM_WIDTH), F32),
        jax.ShapeDtypeStruct((T_ALL, MLA_RANK), F32),
        jax.ShapeDtypeStruct((T_ALL, MLA_ROPE), F32),
        jax.ShapeDtypeStruct((MLA_HEADS, T_ALL, HEAD_PAD), BF16),
        jax.ShapeDtypeStruct((MLA_HEADS, T_ALL, HEAD_PAD), BF16),
        jax.ShapeDtypeStruct((T_ALL, MLA_HEADS * MLA_V), BF16),
    ]
    return pl.pallas_call(
        _proj_kernel,
        grid=(nblk,),
        in_specs=in_specs,
        out_specs=out_specs,
        out_shape=out_shape,
        compiler_params=pltpu.CompilerParams(dimension_semantics=("parallel",), vmem_limit_bytes=VMEM_LIMIT),
        name="proj_in",
    )(x_ctx, x_smp, mod, lw["g1"], lw["wml"], lw["wgt"], lw["gb_col"], lw["wcm"], lw["wmla"],
      lw["wkr"], lw["cmg"], lw["ws"], lw["cmb"], lw["qg"], lw["kvg"], lw["wuqa"], lw["wuqb"], lw["wk"], lw["wv"],
      cos_t, sin_t)


def _mlstm_schedule():
    pair, fwd_a, bwd_a, fwd_b, bwd_b, first = [], [], [], [], [], []
    base = 0
    for p in range((N_CTX_SEQ + N_SMP_SEQ) // 2):
        nc = (CTX_LEN if 2 * p < N_CTX_SEQ else SMP_LEN) // CHUNK
        for j in range(nc):
            pair.append(p)
            fwd_a.append(base + j)
            bwd_a.append(base + nc - 1 - j)
            fwd_b.append(base + nc + j)
            bwd_b.append(base + 2 * nc - 1 - j)
            first.append(1 if j == 0 else 0)
        base += 2 * nc
    as_i32 = lambda a: jnp.asarray(np.asarray(a, np.int32))
    return tuple(as_i32(a) for a in (pair, fwd_a, bwd_a, fwd_b, bwd_b, first))


def _scan_cummax(x, direction):
    L = CHUNK
    rows = x.shape[0]
    x = jnp.concatenate([x, x], axis=0)
    lane = lax.broadcasted_iota(jnp.int32, x.shape, 1)
    k = 1
    while k < L:
        if direction == 0:
            shifted = jnp.where(lane >= k, pltpu.roll(x, k, axis=1), NEG_INF)
        else:
            shifted = jnp.where(lane < L - k, pltpu.roll(x, L - k, axis=1), NEG_INF)
        x = jnp.maximum(x, shifted)
        k *= 2
    return x[0:rows]


def _rows_to_lane_cols(rows, eye, rep, pieces):
    x = jnp.concatenate([jnp.broadcast_to(rows[h:h + 1, :], (rep, CHUNK)) for h in range(ML_HEADS)], axis=0)
    out = None
    for _ in range(pieces):
        xb = x.astype(BF16)
        part = _dot_nt(eye, xb)
        out = part if out is None else out + part
        x = x - xb.astype(F32)
    return out


def _per_head_lanes(x512, lane64):
    lo = jnp.where(lane64, x512[:, 0:128], x512[:, 128:256])
    hi = jnp.where(lane64, x512[:, 256:384], x512[:, 384:512])
    return jnp.concatenate([lo, hi], axis=1)


def _mlstm_direction(ml, g_row, direction, slot, c_ref, n_ref, m_ref):
    L = CHUNK
    t_idx = lax.broadcasted_iota(jnp.int32, (L, L), 0)
    s_idx = lax.broadcasted_iota(jnp.int32, (L, L), 1)
    visible = (s_idx <= t_idx) if direction == 0 else (s_idx >= t_idx)
    tri = jnp.where(visible, 1.0, 0.0).astype(BF16)
    eye = jnp.where(s_idx == t_idx, 1.0, 0.0).astype(BF16)
    lane64 = lax.broadcasted_iota(jnp.int32, (1, 128), 1) < ML_DIM
    head_of_lane = lax.broadcasted_iota(jnp.int32, (1, ML_WIDTH), 1) >> 6
    same_head = ((lax.broadcasted_iota(jnp.int32, (ML_WIDTH, ML_WIDTH), 0) >> 6)
                 == (lax.broadcasted_iota(jnp.int32, (ML_WIDTH, ML_WIDTH), 1) >> 6))

    i0 = 8 * direction
    i_row = g_row[i0:i0 + ML_HEADS, :]
    lf_row = _log_sigmoid(g_row[i0 + ML_HEADS:i0 + 2 * ML_HEADS, :])
    r1, r2, r3 = _split3(lf_row)
    b_row = _dot_nt(r1, tri) + _dot_nt(r2, tri) + _dot_nt(r3, tri)
    yield None
    m_rep = m_ref[slot, direction, 0:ML_HEADS, :]
    a_row = i_row - b_row
    g_row_ = jnp.maximum(m_rep, _scan_cummax(a_row, direction))
    yield None
    b_end = jnp.sum(lf_row, axis=1, keepdims=True)
    log_w = b_end - b_row + i_row
    m_new = jnp.maximum(b_end + m_rep, jnp.max(log_w, axis=1, keepdims=True))
    w_k_row = jnp.exp(log_w - m_new)
    decay_rep = jnp.exp(b_end + m_rep - m_new)

    yield None
    g512 = _rows_to_lane_cols(g_row_, eye, 128, 2)
    g_full = _per_head_lanes(g512, lane64)
    yield None
    b_full = _rows_to_lane_cols(b_row, eye, ML_DIM, 2)
    wk_full = _rows_to_lane_cols(w_k_row, eye, ML_DIM, 1)
    m_full = _per_head_lanes(jnp.concatenate([m_rep[h:h + 1, :] for h in range(ML_HEADS)], axis=1), lane64)
    decay_full = _per_head_lanes(jnp.concatenate([decay_rep[h:h + 1, :] for h in range(ML_HEADS)], axis=1),
                                 lane64)

    yield None
    q = ml[:, 0:ML_WIDTH]
    k = ml[:, ML_WIDTH:2 * ML_WIDTH] * (ML_DIM ** -0.5)
    v = ml[:, 2 * ML_WIDTH:3 * ML_WIDTH]
    qb = q.astype(BF16)
    kb = k.astype(BF16)
    vb = v.astype(BF16)
    num = jnp.zeros((L, ML_WIDTH), F32)
    rowsum = jnp.zeros((L, ML_WIDTH), F32)
    for hd in range(ML_HEADS):
        w_intra = jnp.where(visible, jnp.exp(a_row[hd:hd + 1, :] - g512[:, hd * 128:(hd + 1) * 128]), 0.0)
        q_h = jnp.where(head_of_lane == hd, q, 0.0).astype(BF16)
        sw = _dot_nt(q_h, kb) * w_intra
        num = jnp.where(head_of_lane == hd, _dot(sw.astype(BF16), vb), num)
        rowsum = jnp.where(head_of_lane == hd, jnp.sum(sw, axis=-1, keepdims=True), rowsum)
        yield None

    C = c_ref[slot, direction]
    n_row = n_ref[slot, direction, 0:1, :]
    w_inter = jnp.exp(m_full - g_full)
    block_ones = jnp.where(same_head, 1.0, 0.0).astype(BF16)
    qn = _dot((q * n_row).astype(BF16), block_ones)
    num = num + w_inter * _dot(qb, C.astype(BF16))
    den = rowsum + w_inter * qn
    h_out = num / jnp.maximum(jnp.abs(den), jnp.exp(-(b_full + g_full)))

    kw = wk_full * k
    c_ref[slot, direction] = decay_full * C + jnp.where(same_head, _dot_tn(kw.astype(BF16), vb), 0.0)
    n_ref[slot, direction, 0:1, :] = decay_full * n_row + jnp.sum(kw, axis=0, keepdims=True)
    m_ref[slot, direction, 0:ML_HEADS, :] = m_new
    yield h_out


def _interleave(stage_generators):
    results = [None] * len(stage_generators)
    live = list(range(len(stage_generators)))
    while live:
        for idx in list(live):
            try:
                value = next(stage_generators[idx])
                if value is not None:
                    results[idx] = value
            except StopIteration:
                live.remove(idx)
    return results


def _mlstm_kernel(pair_ref, fa_ref, ba_ref, fb_ref, bb_ref, first_ref,
                  mlfa_ref, mlba_ref, mlfb_ref, mlbb_ref, grfa_ref, grba_ref, grfb_ref, grbb_ref,
                  c0_ref, n0_ref, m0_ref, hf_ref, hb_ref, c_ref, n_ref, m_ref):
    step = pl.program_id(0)

    @pl.when(first_ref[step] == 1)
    def _():
        c_ref[...] = c0_ref[...]
        n_ref[...] = n0_ref[...]
        m_ref[...] = m0_ref[...]

    hfa, hba, hfb, hbb = _interleave([
        _mlstm_direction(mlfa_ref[...], grfa_ref[...], 0, 0, c_ref, n_ref, m_ref),
        _mlstm_direction(mlba_ref[...], grba_ref[...], 1, 0, c_ref, n_ref, m_ref),
        _mlstm_direction(mlfb_ref[...], grfb_ref[...], 0, 1, c_ref, n_ref, m_ref),
        _mlstm_direction(mlbb_ref[...], grbb_ref[...], 1, 1, c_ref, n_ref, m_ref)])
    hf_ref[0, 0] = hfa
    hf_ref[0, 1] = hfb
    hb_ref[0, 0] = hba
    hb_ref[0, 1] = hbb


def _mlstm_call(ml, gates_t, c0, n0, m0):
    sched = _mlstm_schedule()
    nseq = N_CTX_SEQ + N_SMP_SEQ
    nsteps = int(sched[0].shape[0])
    chunk_rows = lambda which: pl.BlockSpec((CHUNK, 4 * ML_WIDTH), lambda i, *s: (s[which][i], 0))
    gate_cols = lambda which: pl.BlockSpec((16, CHUNK), lambda i, *s: (0, s[which][i]))
    c_spec = pl.BlockSpec((2, 2, ML_WIDTH, ML_WIDTH), lambda i, *s: (s[0][i], 0, 0, 0))
    n_spec = pl.BlockSpec((2, 2, 8, ML_WIDTH), lambda i, *s: (s[0][i], 0, 0, 0))
    m_spec = pl.BlockSpec((2, 2, 8, 128), lambda i, *s: (s[0][i], 0, 0, 0))
    h_spec = pl.BlockSpec((1, 2, CHUNK, ML_WIDTH), lambda i, *s: (i, 0, 0, 0))
    grid_spec = pltpu.PrefetchScalarGridSpec(
        num_scalar_prefetch=6,
        grid=(nsteps,),
        in_specs=[chunk_rows(1), chunk_rows(2), chunk_rows(3), chunk_rows(4),
                  gate_cols(1), gate_cols(2), gate_cols(3), gate_cols(4), c_spec, n_spec, m_spec],
        out_specs=[h_spec, h_spec, c_spec, n_spec, m_spec],
    )
    return pl.pallas_call(
        _mlstm_kernel,
        grid_spec=grid_spec,
        out_shape=[
            jax.ShapeDtypeStruct((nsteps, 2, CHUNK, ML_WIDTH), F32),
            jax.ShapeDtypeStruct((nsteps, 2, CHUNK, ML_WIDTH), F32),
            jax.ShapeDtypeStruct((nseq, 2, ML_WIDTH, ML_WIDTH), F32),
            jax.ShapeDtypeStruct((nseq, 2, 8, ML_WIDTH), F32),
            jax.ShapeDtypeStruct((nseq, 2, 8, 128), F32),
        ],
        compiler_params=pltpu.CompilerParams(dimension_semantics=("arbitrary",), vmem_limit_bytes=VMEM_LIMIT),
        name="mlstm",
    )(*sched, ml, ml, ml, ml, gates_t, gates_t, gates_t, gates_t, c0, n0, m0)


def _h_pair_specs():
    ctx_blk = T_CTX // TM
    blk_per_seq = SMP_LEN // TM
    ctx_steps = N_CTX_SEQ // 2 * (CTX_LEN // CHUNK)

    def place(i):
        i_s = jnp.maximum(i - ctx_blk, 0)
        seq_s = i_s // blk_per_seq
        jb = i_s % blk_per_seq
        is_ctx = i < ctx_blk
        slot = jnp.where(is_ctx, i % 2, seq_s % 2)
        base = (ctx_steps + (seq_s // 2) * (SMP_LEN // CHUNK)) // 2
        fwd = jnp.where(is_ctx, i // 2, base + jb)
        bwd = jnp.where(is_ctx, i // 2, base + blk_per_seq - 1 - jb)
        return fwd, bwd, slot

    block = (2, None, CHUNK, ML_WIDTH)
    return (pl.BlockSpec(block, lambda i: (place(i)[0], place(i)[2], 0, 0)),
            pl.BlockSpec(block, lambda i: (place(i)[1], place(i)[2], 0, 0)))


def _cache_kv_kernel(ckv_ref, kr_ref, wk_ref, wv_ref, place_ref, k_ref, v_ref):
    ckvb = ckv_ref[...].astype(BF16)
    kp = _dot(ckvb, wk_ref[...])
    kr128 = _dot(kr_ref[...].astype(BF16), place_ref[...])
    for hd in range(MLA_HEADS):
        k_ref[hd] = (kp[:, hd * HEAD_PAD:(hd + 1) * HEAD_PAD] + kr128).astype(BF16)
    v_ref[...] = _dot(ckvb, wv_ref[...]).astype(BF16)


def _cache_kv_call(ckv, kr, wk, wv, place):
    n = N_SMP_SEQ * PAST_LEN
    tb = PAST_LEN
    return pl.pallas_call(
        _cache_kv_kernel,
        grid=(n // tb,),
        in_specs=[
            pl.BlockSpec((tb, MLA_RANK), lambda i: (i, 0)),
            pl.BlockSpec((tb, MLA_ROPE), lambda i: (i, 0)),
            pl.BlockSpec((MLA_RANK, MLA_HEADS * HEAD_PAD), lambda i: (0, 0)),
            pl.BlockSpec((MLA_RANK, MLA_HEADS * MLA_V), lambda i: (0, 0)),
            pl.BlockSpec((MLA_ROPE, HEAD_PAD), lambda i: (0, 0)),
        ],
        out_specs=[
            pl.BlockSpec((MLA_HEADS, tb, HEAD_PAD), lambda i: (0, i, 0)),
            pl.BlockSpec((tb, MLA_HEADS * MLA_V), lambda i: (i, 0)),
        ],
        out_shape=[
            jax.ShapeDtypeStruct((MLA_HEADS, n, HEAD_PAD), BF16),
            jax.ShapeDtypeStruct((n, MLA_HEADS * MLA_V), BF16),
        ],
        compiler_params=pltpu.CompilerParams(dimension_semantics=("parallel",)),
        name="cache_kv",
    )(ckv, kr, wk, wv, place)


def _attn_kernel(*refs, has_cache):
    if has_cache:
        q_ref, kn_ref, vn_ref, kc_ref, vc_ref, o_ref = refs
    else:
        q_ref, kn_ref, vn_ref, o_ref = refs
    scale = (MLA_NOPE + MLA_ROPE) ** -0.5

    def head(hd):
        lanes = slice(hd * MLA_V, (hd + 1) * MLA_V)
        q = q_ref[hd]
        s_n = _dot_nt(q, kn_ref[hd])
        if has_cache:
            s_c = _dot_nt(q, kc_ref[hd])
        yield None
        m = jnp.max(s_n, axis=-1, keepdims=True)
        if has_cache:
            m = jnp.maximum(m, jnp.max(s_c, axis=-1, keepdims=True))
        yield None
        p_n = jnp.exp((s_n - m) * scale)
        l = jnp.sum(p_n, axis=-1, keepdims=True)
        o = _dot(p_n.astype(BF16), vn_ref[:, lanes])
        if has_cache:
            p_c = jnp.exp((s_c - m) * scale)
            l = l + jnp.sum(p_c, axis=-1, keepdims=True)
            o = o + _dot(p_c.astype(BF16), vc_ref[:, lanes])
        yield o / l

    outs = []
    for hd in range(0, MLA_HEADS, ATTN_HEADS_IN_FLIGHT):
        outs += _interleave([head(hd + i) for i in range(ATTN_HEADS_IN_FLIGHT)])
    o_ref[...] = jnp.concatenate(outs, axis=-1)


def _attn_ctx_call(q, k, v):
    tq = CTX_LEN
    return pl.pallas_call(
        functools.partial(_attn_kernel, has_cache=False),
        grid=(N_CTX_SEQ,),
        in_specs=[
            pl.BlockSpec((MLA_HEADS, tq, HEAD_PAD), lambda s: (0, s, 0)),
            pl.BlockSpec((MLA_HEADS, tq, HEAD_PAD), lambda s: (0, s, 0)),
            pl.BlockSpec((tq, MLA_HEADS * MLA_V), lambda s: (s, 0)),
        ],
        out_specs=pl.BlockSpec((tq, MLA_HEADS * MLA_V), lambda s: (s, 0)),
        out_shape=jax.ShapeDtypeStruct((T_CTX, MLA_HEADS * MLA_V), F32),
        compiler_params=pltpu.CompilerParams(dimension_semantics=("parallel",), vmem_limit_bytes=VMEM_LIMIT),
        name="attn_ctx",
    )(q, k, v)


def _attn_smp_call(q, k, v, kc, vc):
    tq = 256
    qb_per_seq = SMP_LEN // tq
    ctx_qb = T_CTX // tq
    ctx_kb = T_CTX // SMP_LEN
    return pl.pallas_call(
        functools.partial(_attn_kernel, has_cache=True),
        grid=(N_SMP_SEQ, qb_per_seq),
        in_specs=[
            pl.BlockSpec((MLA_HEADS, tq, HEAD_PAD), lambda b, i: (0, ctx_qb + b * qb_per_seq + i, 0)),
            pl.BlockSpec((MLA_HEADS, SMP_LEN, HEAD_PAD), lambda b, i: (0, ctx_kb + b, 0)),
            pl.BlockSpec((SMP_LEN, MLA_HEADS * MLA_V), lambda b, i: (ctx_kb + b, 0)),
            pl.BlockSpec((MLA_HEADS, PAST_LEN, HEAD_PAD), lambda b, i: (0, b, 0)),
            pl.BlockSpec((PAST_LEN, MLA_HEADS * MLA_V), lambda b, i: (b, 0)),
        ],
        out_specs=pl.BlockSpec((tq, MLA_HEADS * MLA_V), lambda b, i: (b * qb_per_seq + i, 0)),
        out_shape=jax.ShapeDtypeStruct((T_SMP, MLA_HEADS * MLA_V), F32),
        compiler_params=pltpu.CompilerParams(dimension_semantics=("parallel", "parallel"),
                                             vmem_limit_bytes=VMEM_LIMIT),
        name="attn_smp",
    )(q, k, v, kc, vc)


def _out_kernel(xc_ref, xs_ref, mod_ref, ml_ref, hf_ref, hb_ref, cm_ref, attc_ref, atts_ref, wo_ref, g2_ref,
                wq_ref, sk_ref, x1_ref, h2t_ref, st_ref):
    is_ctx = pl.program_id(0) < T_CTX // TM
    att = jnp.where(is_ctx, attc_ref[...], atts_ref[...])
    mod = mod_ref[0]
    g1 = mod[:, 2 * D_MODEL:3 * D_MODEL]
    sh2 = mod[:, 3 * D_MODEL:4 * D_MODEL]
    sc2 = mod[:, 4 * D_MODEL:5 * D_MODEL]
    h_sum = jnp.concatenate([hf_ref[0] + hb_ref[1], hf_ref[1] + hb_ref[0]], axis=0)
    mlo = _sigmoid(ml_ref[:, 3 * ML_WIDTH:4 * ML_WIDTH]) * h_sum
    mix = (_dot(mlo.astype(BF16), wo_ref[0:ML_WIDTH, :])
           + _dot(cm_ref[...].astype(BF16), wo_ref[ML_WIDTH:ML_WIDTH + CM_WIDTH, :])
           + _dot(att.astype(BF16), wo_ref[ML_WIDTH + CM_WIDTH:, :]))
    x1 = _select_x(xc_ref, xs_ref) + g1 * mix
    x1_ref[...] = x1
    h2f = _rms(x1, g2_ref[...]) * (1.0 + sc2) + sh2
    h2t_ref[...] = h2f.T.astype(BF16)
    qp = _dot(h2f.astype(BF16), wq_ref[...]).astype(BF16)
    for hh in range(2 * PEER_HEADS):
        st_ref[hh] = _dot_nt(sk_ref[hh % 2], qp[:, hh * 128:(hh + 1) * 128])


def _out_call(x_ctx, x_smp, mod, ml, hf, hb, cm, att_ctx, att_smp, lw):
    nblk = T_ALL // TM
    ctx_blk = T_CTX // TM
    rows = lambda w: pl.BlockSpec((TM, w), lambda i: (i, 0))
    full = lambda shape: pl.BlockSpec(shape, lambda i: (0,) * len(shape))
    att_w = MLA_HEADS * MLA_V
    attc_spec = pl.BlockSpec((TM, att_w), lambda i: (jnp.minimum(i, ctx_blk - 1), 0))
    atts_spec = pl.BlockSpec((TM, att_w), lambda i: (jnp.maximum(i - ctx_blk, 0), 0))
    return pl.pallas_call(
        _out_kernel,
        grid=(nblk,),
        in_specs=_x_specs(x_ctx, x_smp) + [
            pl.BlockSpec((1, 1, 6 * D_MODEL), lambda i: (_mod_row_of_block(i, TM), 0, 0)),
            rows(4 * ML_WIDTH), *_h_pair_specs(), rows(CM_WIDTH), attc_spec, atts_spec,
            full((D_MODEL, D_MODEL)), full((1, D_MODEL)), full((D_MODEL, 2 * PEER_HEADS * 128)),
            full((2, PEER_NKEYS, 128)),
        ],
        out_specs=[rows(D_MODEL), pl.BlockSpec((D_MODEL, TM), lambda i: (0, i)),
                   pl.BlockSpec((2 * PEER_HEADS, PEER_NKEYS, TM), lambda i: (0, 0, i))],
        out_shape=[
            jax.ShapeDtypeStruct((T_ALL, D_MODEL), F32),
            jax.ShapeDtypeStruct((D_MODEL, T_ALL), BF16),
            jax.ShapeDtypeStruct((2 * PEER_HEADS, PEER_NKEYS, T_ALL), F32),
        ],
        compiler_params=pltpu.CompilerParams(dimension_semantics=("parallel",), vmem_limit_bytes=VMEM_LIMIT),
        name="proj_out",
    )(x_ctx, x_smp, mod, ml, hf, hb, cm, att_ctx, att_smp, lw["wo"], lw["g2"], lw["wq"], lw["sk"])


def _sorting_network_pairs(n):
    pairs = []
    p = 1
    while p < n:
        k = p
        while k >= 1:
            for j in range(k % p, n - k, 2 * k):
                for i in range(min(k, n - j - k)):
                    if (i + j) // (2 * p) == (i + j + k) // (2 * p):
                        pairs.append((i + j, i + j + k))
            k //= 2
        p *= 2
    return pairs


_SORT16_PAIRS = _sorting_network_pairs(PEER_TOPK)


def _pop16(lists):
    lists = list(lists)
    vals = []
    for k in range(PEER_TOPK):
        m = jnp.max(lists[0], axis=0, keepdims=True)
        vals.append(m)
        if k == PEER_TOPK - 1:
            break
        hit = lists[0] == m
        for i in range(PEER_TOPK - 1 - k):
            lists[i] = jnp.where(hit, lists[i + 1], lists[i])
    return vals


def _top16_rows(s):
    tiles = [s[8 * j:8 * j + 8] for j in range(s.shape[0] // 8)]
    assert len(tiles) == PEER_TOPK
    for i, j in _SORT16_PAIRS:
        tiles[i], tiles[j] = jnp.maximum(tiles[i], tiles[j]), jnp.minimum(tiles[i], tiles[j])
    return _pop16(tiles)


def _count_steps(x, thresholds, below):
    r = jnp.zeros(x.shape, F32)
    for q, t in enumerate(thresholds):
        r = jnp.where((x < t) if below else (x >= t), float(q + 1), r)
    return r


def _pack_rows_bf16(x):
    r, n = x.shape
    x4 = x.reshape(r // 16, 2, 8, n)
    lo = x4[:, 0].reshape(r // 2, n)
    hi = x4[:, 1].reshape(r // 2, n)
    return pltpu.bitcast(pltpu.pack_elementwise([lo, hi], packed_dtype=BF16), BF16)


def _dup_bf16_words(x):
    u = pltpu.bitcast(x.astype(BF16).astype(F32), jnp.int32)
    return u | lax.shift_right_logical(u, jnp.full(u.shape, 16, jnp.int32))


def _rows_to_array(rows, row_iota):
    arr = jnp.zeros(row_iota.shape, F32)
    for i, r in enumerate(rows):
        arr = jnp.where(row_iota == i, r, arr)
    return arr


def _topk_kernel(st_ref, e1_ref, cut_ref, e2_ref, r2_ref):
    n = st_ref.shape[-1]
    row16 = lax.broadcasted_iota(jnp.int32, (PEER_TOPK, n), 0)

    def head(hd, carry):
        s1 = st_ref[2 * hd]
        s2 = st_ref[2 * hd + 1]
        v1 = _top16_rows(s1)
        v2 = _top16_rows(s2)
        rank2 = _count_steps(s2, v2, below=True)
        v1arr = _rows_to_array(v1, row16)
        c = _pop16([v1arr + v2[q] for q in range(PEER_TOPK)])
        tau = c[PEER_TOPK - 1]
        z = jnp.zeros((1, n), F32)
        for ck in c:
            z = z + jnp.exp(ck - c[0])
        sigma = [jnp.min(jnp.where(v1arr + v2[q] >= tau, v1arr, jnp.inf), axis=0, keepdims=True)
                 for q in range(PEER_TOPK)]
        cut = _count_steps(s1, sigma, below=False)
        e1_ref[hd] = _dup_bf16_words(jnp.exp(s1 - v1[0]) / z)
        cut_ref[hd] = _dup_bf16_words(cut)
        e2_ref[hd] = _pack_rows_bf16(jnp.exp(s2 - v2[0]))
        r2_ref[hd] = _pack_rows_bf16(rank2)
        return carry

    lax.fori_loop(0, PEER_HEADS, head, 0)


def _topk_call(st):
    tn = TN_TOPK
    spec = pl.BlockSpec((PEER_HEADS, PEER_NKEYS, tn), lambda i: (0, 0, i))
    stat = lambda dt: jax.ShapeDtypeStruct((PEER_HEADS, PEER_NKEYS, T_ALL), dt)
    return pl.pallas_call(
        _topk_kernel,
        grid=(T_ALL // tn,),
        in_specs=[pl.BlockSpec((2 * PEER_HEADS, PEER_NKEYS, tn), lambda i: (0, 0, i))],
        out_specs=[spec, spec, spec, spec],
        out_shape=[stat(jnp.int32), stat(jnp.int32), stat(BF16), stat(BF16)],
        compiler_params=pltpu.CompilerParams(dimension_semantics=("parallel",), vmem_limit_bytes=VMEM_LIMIT),
        name="peer_topk",
    )(st)


def _row_tile_bf16(row):
    blk = pltpu.bitcast(jnp.broadcast_to(row, (8, row.shape[1])), BF16)
    return jnp.concatenate([blk] * (PEER_NKEYS // 16), axis=0)


def _gelu_tanh_bf16(x):
    z2 = x * (1.5957691216057308 + 0.07135481627159584 * (x * x))
    return x / (1.0 + jnp.exp(-z2))


def _expert_kernel(h2t_ref, u_ref, vt_ref, e1_ref, cut_ref, e2_ref, r2_ref, x1_ref, mod_ref, fg_ref,
                   o_ref, acc_ref, a_ref, *, final_norm):
    j = pl.program_id(1)
    n_tiles = pl.num_programs(1)
    tm = h2t_ref.shape[1]
    a_per_tile = TE_EXP // PEER_NKEYS

    @pl.when(j == 0)
    def _():
        acc_ref[...] = jnp.zeros_like(acc_ref)

    def gate_tile(ai):
        a = j * a_per_tile + ai
        gate = jnp.zeros((PEER_NKEYS, tm), BF16)
        for hd in range(PEER_HEADS):
            cut_a = _row_tile_bf16(cut_ref[hd, pl.ds(a, 1), :])
            e1_a = _row_tile_bf16(e1_ref[hd, pl.ds(a, 1), :])
            gate = gate + jnp.where(r2_ref[hd] < cut_a, e2_ref[hd], jnp.zeros((), BF16)) * e1_a
        return gate

    a_per_sub = EXP_SUB // PEER_NKEYS
    n_sub = TE_EXP // EXP_SUB
    s = [None] * n_sub
    gates = [None] * n_sub
    s[0] = _dot(u_ref[0:EXP_SUB, :], h2t_ref[...])
    gates[0] = [gate_tile(ai) for ai in range(a_per_sub)]
    for sub in range(n_sub):
        if sub + 1 < n_sub:
            s[sub + 1] = _dot(u_ref[(sub + 1) * EXP_SUB:(sub + 2) * EXP_SUB, :], h2t_ref[...])
            gates[sub + 1] = [gate_tile((sub + 1) * a_per_sub + i) for i in range(a_per_sub)]
        g = _gelu_tanh_bf16(_pack_rows_bf16(s[sub]))
        for i in range(a_per_sub):
            r0 = sub * EXP_SUB + i * PEER_NKEYS
            a_ref[r0:r0 + PEER_NKEYS, :] = gates[sub][i] * g[i * PEER_NKEYS:(i + 1) * PEER_NKEYS, :]
    acc_ref[...] += _dot(vt_ref[...], a_ref[...])

    @pl.when(j == n_tiles - 1)
    def _():
        g2 = mod_ref[0][:, 5 * D_MODEL:6 * D_MODEL]
        y = x1_ref[...] + g2 * acc_ref[...].T
        if final_norm:
            y = _rms(y, fg_ref[...])
        o_ref[...] = y


def _expert_call(h2t, u_bf, vt_bf, layer, e1, cut, e2, r2, x1, mod, final_g, final_norm, tok_start, tok_count):
    tm, te = TM_EXP, TE_EXP
    b0 = tok_start // tm
    n_tiles = PEER_EXPERTS // te
    stat = pl.BlockSpec((PEER_HEADS, PEER_NKEYS, tm), lambda i, j: (0, 0, b0 + i))
    return pl.pallas_call(
        functools.partial(_expert_kernel, final_norm=final_norm),
        grid=(tok_count // tm, n_tiles),
        in_specs=[
            pl.BlockSpec((D_MODEL, tm), lambda i, j: (0, b0 + i)),
            pl.BlockSpec((None, te, D_MODEL), lambda i, j: (layer, j, 0)),
            pl.BlockSpec((None, D_MODEL, te), lambda i, j: (layer, 0, j)),
            stat, stat, stat, stat,
            pl.BlockSpec((tm, D_MODEL), lambda i, j: (b0 + i, 0)),
            pl.BlockSpec((1, 1, 6 * D_MODEL), lambda i, j: (_mod_row_of_block(b0 + i, tm), 0, 0)),
            pl.BlockSpec((1, D_MODEL), lambda i, j: (0, 0)),
        ],
        out_specs=pl.BlockSpec((tm, D_MODEL), lambda i, j: (i, 0)),
        out_shape=jax.ShapeDtypeStruct((tok_count, D_MODEL), F32),
        scratch_shapes=[
            pltpu.VMEM((D_MODEL, tm), F32),
            pltpu.VMEM((te, tm), BF16),
        ],
        compiler_params=pltpu.CompilerParams(dimension_semantics=("parallel", "arbitrary"),
                                             vmem_limit_bytes=VMEM_LIMIT),
        name="peer_experts",
    )(h2t, u_bf, vt_bf, e1, cut, e2, r2, x1, mod, final_g)


def _u_prep_kernel(u_ref, o_ref):
    o_ref[...] = u_ref[...].astype(BF16)


def _v_prep_kernel(v_ref, o_ref):
    o_ref[...] = _pack_rows_bf16(v_ref[...]).T


def _table_prep_call(peer_u, peer_v):
    te = TE_PREP
    grid = (DEPTH, PEER_EXPERTS // te)
    rows = pl.BlockSpec((None, te, D_MODEL), lambda l, j: (l, j, 0))
    params = pltpu.CompilerParams(dimension_semantics=("parallel", "parallel"), vmem_limit_bytes=VMEM_LIMIT)
    u_bf = pl.pallas_call(
        _u_prep_kernel, grid=grid, in_specs=[rows], out_specs=rows,
        out_shape=jax.ShapeDtypeStruct((DEPTH, PEER_EXPERTS, D_MODEL), BF16),
        compiler_params=params, name="peer_u_prep")(peer_u)
    vt_bf = pl.pallas_call(
        _v_prep_kernel, grid=grid, in_specs=[rows],
        out_specs=pl.BlockSpec((None, D_MODEL, te), lambda l, j: (l, 0, j)),
        out_shape=jax.ShapeDtypeStruct((DEPTH, D_MODEL, PEER_EXPERTS), BF16),
        compiler_params=params, name="peer_v_prep")(peer_v)
    return u_bf, vt_bf


def _rope_swap_cols(w):
    return jnp.concatenate([-w[:, 8:16], w[:, 0:8], -w[:, 24:32], w[:, 16:24]], axis=1)


def _pad_heads(parts, n_heads):
    k = next(p[0].shape[0] for p in parts if p[0] is not None)
    cols = []
    for hd in range(n_heads):
        for arr, w in parts:
            cols.append(jnp.zeros((k, w), F32) if arr is None else arr[:, hd * w:(hd + 1) * w])
    return jnp.concatenate(cols, axis=1)


def _rope_tables():
    pos = np.arange(SMP_LEN)
    freqs = ROPE_THETA ** (-np.arange(0, ROPE_AXIS, 2, dtype=np.float32) / ROPE_AXIS)
    ang_r = (pos // GRID_W).astype(np.float32)[:, None] * freqs
    ang_c = (pos % GRID_W).astype(np.float32)[:, None] * freqs
    ang = jnp.asarray(np.concatenate([ang_r, ang_r, ang_c, ang_c], axis=1).astype(np.float32))
    cos32 = jnp.cos(ang)
    sin32 = jnp.sin(ang)
    ones = jnp.ones((SMP_LEN, MLA_NOPE), F32)
    cos_s = jnp.concatenate([ones, cos32, ones[:, :HEAD_PAD - MLA_NOPE - MLA_ROPE]], axis=1)
    sin_s = jnp.concatenate([0 * ones, sin32, 0 * ones[:, :HEAD_PAD - MLA_NOPE - MLA_ROPE]], axis=1)
    cos_t = jnp.concatenate([jnp.ones((T_CTX, HEAD_PAD), F32)] + [cos_s] * N_SMP_SEQ, axis=0)
    sin_t = jnp.concatenate([jnp.zeros((T_CTX, HEAD_PAD), F32)] + [sin_s] * N_SMP_SEQ, axis=0)
    return cos_t, sin_t


def _layer_weights(l, norm1_g, w_in, mlstm_gate_b, cm_norm_g, cm_ws, cm_b, mla_q_norm_g, mla_w_uq, mla_kv_norm_g,
                   mla_w_ukv, w_out, norm2_g, peer_w_q, peer_subkeys):
    w = w_in[l]
    o_g = 4 * ML_WIDTH
    o_cm = o_g + 16
    o_cq = o_cm + 2 * CM_WIDTH
    o_ckv = o_cq + MLA_RANK
    o_kr = o_ckv + MLA_RANK
    w_kr = w[:, o_kr:o_kr + MLA_ROPE]
    zeros_l = jnp.zeros((D_MODEL, MLA_NOPE), F32)
    zeros_r = jnp.zeros((D_MODEL, HEAD_PAD - MLA_NOPE - MLA_ROPE), F32)
    kr128 = jnp.concatenate([zeros_l, w_kr, zeros_r], axis=1)
    krsw128 = jnp.concatenate([zeros_l, _rope_swap_cols(w_kr), zeros_r], axis=1)
    uq = mla_w_uq[l].reshape(MLA_RANK, MLA_HEADS, MLA_NOPE + MLA_ROPE)
    uq_nope = uq[:, :, :MLA_NOPE].reshape(MLA_RANK, -1)
    uq_rope = uq[:, :, MLA_NOPE:].reshape(MLA_RANK, -1)
    uq_rope_sw = jnp.concatenate(
        [_rope_swap_cols(uq_rope[:, hd * MLA_ROPE:(hd + 1) * MLA_ROPE]) for hd in range(MLA_HEADS)], axis=1)
    pad_w = HEAD_PAD - MLA_NOPE - MLA_ROPE
    ukv = mla_w_ukv[l].reshape(MLA_RANK, MLA_HEADS, MLA_NOPE + MLA_V)
    uk = ukv[:, :, :MLA_NOPE].reshape(MLA_RANK, -1)
    uv = ukv[:, :, MLA_NOPE:].reshape(MLA_RANK, -1)
    gb = mlstm_gate_b[l]
    return {
        "g1": norm1_g[l].reshape(1, D_MODEL),
        "wml": w[:, 0:o_g].astype(BF16),
        "wgt": w[:, o_g:o_cm].T.astype(BF16),
        "gb_col": gb.reshape(16, 1),
        "wcm": w[:, o_cm:o_cq].astype(BF16),
        "wmla": jnp.concatenate([w[:, o_cq:o_kr], kr128, krsw128], axis=1).astype(BF16),
        "wkr": w_kr.astype(BF16),
        "cmg": cm_norm_g[l].reshape(1, CM_WIDTH),
        "ws": cm_ws[l].astype(BF16),
        "cmb": jnp.repeat(cm_b[l].T, CM_WIDTH // CM_GROUPS, axis=1),
        "qg": mla_q_norm_g[l].reshape(1, MLA_RANK),
        "kvg": mla_kv_norm_g[l].reshape(1, MLA_RANK),
        "wuqa": _pad_heads([(uq_nope, MLA_NOPE), (uq_rope, MLA_ROPE), (None, pad_w)], MLA_HEADS).astype(BF16),
        "wuqb": _pad_heads([(None, MLA_NOPE), (uq_rope_sw, MLA_ROPE), (None, pad_w)], MLA_HEADS).astype(BF16),
        "wk": _pad_heads([(uk, MLA_NOPE), (None, HEAD_PAD - MLA_NOPE)], MLA_HEADS).astype(BF16),
        "wv": uv.astype(BF16),
        "wo": w_out[l].astype(BF16),
        "g2": norm2_g[l].reshape(1, D_MODEL),
        "wq": peer_w_q[l].astype(BF16),
        "sk": peer_subkeys[l].astype(BF16),
    }


def kernel(x_prompt, x_sample, c, cache_mla_ckv, cache_mla_krope, state_mlstm_C, state_mlstm_n, state_mlstm_m, c_ctx, norm1_g, ada_w, ada_b, w_in, mlstm_gate_b, cm_norm_g, cm_ws, cm_b, mla_q_norm_g, mla_w_uq, mla_kv_norm_g, mla_w_ukv, w_out, norm2_g, peer_w_q, peer_subkeys, peer_u, peer_v, final_g):
    x_ctx = x_prompt.reshape(T_CTX, D_MODEL)
    x_smp = x_sample.reshape(T_SMP, D_MODEL)
    cvecs = jnp.concatenate([c_ctx[None, :], c, jnp.zeros((N_MOD_ROWS - 1 - N_SMP_SEQ, D_MODEL), F32)], axis=0)
    mod_all = _ada_call(cvecs, ada_w, ada_b)
    cos_t, sin_t = _rope_tables()
    place = jnp.concatenate([jnp.zeros((MLA_ROPE, MLA_NOPE), F32), jnp.eye(MLA_ROPE, dtype=F32),
                             jnp.zeros((MLA_ROPE, HEAD_PAD - MLA_NOPE - MLA_ROPE), F32)], axis=1).astype(BF16)
    final_g2 = final_g.reshape(1, D_MODEL)
    u_bf, vt_bf = _table_prep_call(peer_u, peer_v)

    ckvs, krs, Cs, ns, ms = [], [], [], [], []
    for l in range(DEPTH):
        lw = _layer_weights(l, norm1_g, w_in, mlstm_gate_b, cm_norm_g, cm_ws, cm_b, mla_q_norm_g, mla_w_uq,
                            mla_kv_norm_g, mla_w_ukv, w_out, norm2_g, peer_w_q, peer_subkeys)
        mod = mod_all[l].reshape(N_MOD_ROWS, 1, 6 * D_MODEL)
        ml, gates_t, cm, ckvn, kr, q, k, v = _proj_call(x_ctx, x_smp, mod, lw, cos_t, sin_t)

        c_blk = jnp.einsum('bdhij,hg->bdhigj', state_mlstm_C[:, l], jnp.eye(ML_HEADS, dtype=F32))
        c0 = jnp.concatenate([jnp.zeros((N_CTX_SEQ, 2, ML_WIDTH, ML_WIDTH), F32),
                              c_blk.reshape(N_SMP_SEQ, 2, ML_WIDTH, ML_WIDTH)], axis=0)
        n0 = jnp.concatenate([jnp.zeros((N_CTX_SEQ, 2, 1, ML_WIDTH), F32),
                              state_mlstm_n[:, l].reshape(N_SMP_SEQ, 2, 1, ML_WIDTH)], axis=0)
        n0 = jnp.pad(n0, ((0, 0), (0, 0), (0, 7), (0, 0)))
        m0 = jnp.concatenate([jnp.zeros((N_CTX_SEQ, 2, ML_HEADS), F32), state_mlstm_m[:, l]], axis=0)
        m0 = jnp.pad(jnp.broadcast_to(m0[..., None], m0.shape + (128,)), ((0, 0), (0, 0), (0, 8 - ML_HEADS), (0, 0)))
        hf, hb, c_fin, n_fin, m_fin = _mlstm_call(ml, gates_t, c0, n0, m0)

        kc, vc = _cache_kv_call(cache_mla_ckv[:, l].reshape(N_SMP_SEQ * PAST_LEN, MLA_RANK),
                                cache_mla_krope[:, l].reshape(N_SMP_SEQ * PAST_LEN, MLA_ROPE),
                                lw["wk"], lw["wv"], place)
        att_ctx = _attn_ctx_call(q, k, v)
        att_smp = _attn_smp_call(q, k, v, kc, vc)

        x1, h2t, st = _out_call(x_ctx, x_smp, mod, ml, hf, hb, cm, att_ctx, att_smp, lw)
        e1, cut, e2, r2 = _topk_call(st)
        experts = functools.partial(_expert_call, h2t, u_bf, vt_bf, l, e1, cut, e2, r2, x1, mod, final_g2)
        if l < DEPTH - 1:
            x_ctx = x_smp = experts(False, 0, T_ALL)
        else:
            x_ctx = experts(True, 0, T_CTX)
            x_smp = experts(True, T_CTX, T_SMP)

        ckvs.append(ckvn[:T_CTX].reshape(N_CTX_SEQ, CTX_LEN, MLA_RANK))
        krs.append(kr[:T_CTX].reshape(N_CTX_SEQ, CTX_LEN, MLA_ROPE))
        Cs.append(jnp.stack([c_fin[:N_CTX_SEQ, :, hd * ML_DIM:(hd + 1) * ML_DIM, hd * ML_DIM:(hd + 1) * ML_DIM]
                             for hd in range(ML_HEADS)], axis=2))
        ns.append(n_fin[:N_CTX_SEQ, :, 0, :].reshape(N_CTX_SEQ, 2, ML_HEADS, ML_DIM))
        ms.append(m_fin[:N_CTX_SEQ, :, 0:ML_HEADS, 0])

    y_prompt = x_ctx.reshape(N_CTX_SEQ, CTX_LEN, D_MODEL)
    y_sample = x_smp.reshape(N_SMP_SEQ, SMP_LEN, D_MODEL)
    return (y_prompt, y_sample, jnp.stack(ckvs, axis=1), jnp.stack(krs, axis=1), jnp.stack(Cs, axis=1),
            jnp.stack(ns, axis=1), jnp.stack(ms, axis=1))
```

```python
import functools

import numpy as np
import jax
import jax.numpy as jnp
from jax import lax
from jax.experimental import pallas as pl
from jax.experimental.pallas import tpu as pltpu

F32 = jnp.float32
BF16 = jnp.bfloat16

D_MODEL = 1024
N_CTX_SEQ = 16
CTX_LEN = 256
N_SMP_SEQ = 4
SMP_LEN = 2048
PAST_LEN = 256
DEPTH = 2
GRID_W = 64
EPS = 1e-6
T_CTX = N_CTX_SEQ * CTX_LEN
T_SMP = N_SMP_SEQ * SMP_LEN
T_ALL = T_CTX + T_SMP
N_MOD_ROWS = 8

ML_HEADS = 4
ML_DIM = 64
ML_WIDTH = 256
CHUNK = 128
CM_GROUPS = 4
CM_WIDTH = 256
MLA_HEADS = 8
MLA_NOPE = 64
MLA_ROPE = 32
MLA_V = 64
MLA_RANK = 256
HEAD_PAD = 128
ROPE_AXIS = 16
ROPE_THETA = 10000.0
PEER_HEADS = 8
PEER_NKEYS = 128
PEER_EXPERTS = PEER_NKEYS * PEER_NKEYS
PEER_TOPK = 16

TM = 256
TN_TOPK = 256
TM_EXP = 512
TE_EXP = 2048
EXP_SUB = 512
TE_PREP = 2048
ATTN_HEADS_IN_FLIGHT = 2
VMEM_LIMIT = 56 * 1024 * 1024

NEG_INF = float("-inf")


def _dot(a, b):
    return jnp.dot(a, b, preferred_element_type=F32)


def _dot_nt(a, b):
    return lax.dot_general(a, b, (((1,), (1,)), ((), ())), preferred_element_type=F32)


def _dot_tn(a, b):
    return lax.dot_general(a, b, (((0,), (0,)), ((), ())), preferred_element_type=F32)


def _split3(a):
    a1 = a.astype(BF16)
    r1 = a - a1.astype(F32)
    a2 = r1.astype(BF16)
    a3 = (r1 - a2.astype(F32)).astype(BF16)
    return a1, a2, a3


def _rms(x, g):
    return x * lax.rsqrt(jnp.mean(x * x, axis=-1, keepdims=True) + EPS) * g


def _sigmoid(x):
    return 1.0 / (1.0 + jnp.exp(-x))


def _log_sigmoid(x):
    return jnp.minimum(x, 0.0) - jnp.log(1.0 + jnp.exp(-jnp.abs(x)))


def _gelu_tanh(x):
    return 0.5 * x * (1.0 + jnp.tanh(0.7978845608028654 * (x + 0.044715 * (x * x * x))))


def _mod_row_of_block(i, rows_per_block):
    ctx_blocks = T_CTX // rows_per_block
    per_seq = SMP_LEN // rows_per_block
    return jnp.where(i < ctx_blocks, 0, 1 + (i - ctx_blocks) // per_seq)


def _ada_kernel(cv_ref, w_ref, b_ref, o_ref):
    cv = cv_ref[...]
    s = cv * _sigmoid(cv)
    w = w_ref[0]
    w1, w2, w3 = _split3(w)
    s1, s2, s3 = _split3(s)
    acc = _dot(s1, w1) + (_dot(s1, w2) + _dot(s2, w1)) + (_dot(s1, w3) + _dot(s2, w2) + _dot(s3, w1))
    o_ref[0] = acc + b_ref[0]


def _ada_call(cvecs, ada_w, ada_b):
    tn = 1024
    return pl.pallas_call(
        _ada_kernel,
        grid=(DEPTH, 6 * D_MODEL // tn),
        in_specs=[
            pl.BlockSpec((N_MOD_ROWS, D_MODEL), lambda l, j: (0, 0)),
            pl.BlockSpec((1, D_MODEL, tn), lambda l, j: (l, 0, j)),
            pl.BlockSpec((1, 1, tn), lambda l, j: (l, 0, j)),
        ],
        out_specs=pl.BlockSpec((1, N_MOD_ROWS, tn), lambda l, j: (l, 0, j)),
        out_shape=jax.ShapeDtypeStruct((DEPTH, N_MOD_ROWS, 6 * D_MODEL), F32),
        compiler_params=pltpu.CompilerParams(dimension_semantics=("parallel", "parallel")),
        name="ada_mod",
    )(cvecs, ada_w, ada_b.reshape(DEPTH, 1, 6 * D_MODEL))


def _x_specs(x_ctx, x_smp, blk=lambda i: i):
    ctx_blk = T_CTX // TM
    smp_off = x_smp.shape[0] // TM - T_SMP // TM
    return [pl.BlockSpec((TM, D_MODEL), lambda i: (jnp.minimum(blk(i), ctx_blk - 1), 0)),
            pl.BlockSpec((TM, D_MODEL), lambda i: (jnp.maximum(blk(i) - ctx_blk, 0) + smp_off, 0))]


def _select_x(xc_ref, xs_ref):
    return jnp.where(pl.program_id(0) < T_CTX // TM, xc_ref[...], xs_ref[...])


def _proj_kernel(xc_ref, xs_ref, mod_ref, g1_ref, wml_ref, wgt_ref, gbc_ref, wcm_ref, wmla_ref, wkr_ref,
                 cmg_ref, ws_ref, cmb_ref, qg_ref, kvg_ref, wuqa_ref, wuqb_ref, wk_ref, wv_ref, cos_ref, sin_ref,
                 ml_ref, gatest_ref, cm_ref, ckvn_ref, kr_ref, q_ref, k_ref, v_ref):
    x = _select_x(xc_ref, xs_ref)
    mod = mod_ref[0]
    sh1 = mod[:, 0:D_MODEL]
    sc1 = mod[:, D_MODEL:2 * D_MODEL]
    h = _rms(x, g1_ref[...]) * (1.0 + sc1) + sh1
    hb = h.astype(BF16)

    ml_ref[...] = _dot(hb, wml_ref[...])
    gatest_ref[...] = _dot_nt(wgt_ref[...], hb) + gbc_ref[...]

    cm = _dot(hb, wcm_ref[...])
    u = cm[:, 0:CM_WIDTH]
    vn = _rms(cm[:, CM_WIDTH:2 * CM_WIDTH], cmg_ref[...]).astype(BF16)
    lane_group = lax.broadcasted_iota(jnp.int32, (CHUNK, CM_WIDTH), 1) >> 6
    for c in range(TM // CHUNK):
        rows = slice(c * CHUNK, (c + 1) * CHUNK)
        vc = vn[rows]
        mixed = jnp.zeros((CHUNK, CM_WIDTH), F32)
        for g in range(CM_GROUPS):
            mixed = jnp.where(lane_group == g, _dot(ws_ref[g], vc), mixed)
        cm_ref[rows, :] = u[rows] * (mixed + cmb_ref[...])

    mla = _dot(hb, wmla_ref[...])
    qn = _rms(mla[:, 0:MLA_RANK], qg_ref[...]).astype(BF16)
    ckvn = _rms(mla[:, MLA_RANK:2 * MLA_RANK], kvg_ref[...])
    ckvn_ref[...] = ckvn
    kr_ref[...] = _dot(hb, wkr_ref[...])
    cos = cos_ref[...]
    sin = sin_ref[...]
    kr_rot = mla[:, 2 * MLA_RANK:2 * MLA_RANK + HEAD_PAD] * cos + mla[:, 2 * MLA_RANK + HEAD_PAD:] * sin
    qa = _dot(qn, wuqa_ref[...])
    qb = _dot(qn, wuqb_ref[...])
    ckvb = ckvn.astype(BF16)
    kp = _dot(ckvb, wk_ref[...])
    for hd in range(MLA_HEADS):
        cols = slice(hd * HEAD_PAD, (hd + 1) * HEAD_PAD)
        q_ref[hd] = (qa[:, cols] * cos + qb[:, cols] * sin).astype(BF16)
        k_ref[hd] = (kp[:, cols] + kr_rot).astype(BF16)
    v_ref[...] = _dot(ckvb, wv_ref[...]).astype(BF16)


def _proj_call(x_ctx, x_smp, mod, lw, cos_t, sin_t):
    nblk = T_ALL // TM
    full = lambda shape: pl.BlockSpec(shape, lambda i: (0,) * len(shape))
    rows = lambda w: pl.BlockSpec((TM, w), lambda i: (i, 0))
    in_specs = _x_specs(x_ctx, x_smp) + [
        pl.BlockSpec((1, 1, 6 * D_MODEL), lambda i: (_mod_row_of_block(i, TM), 0, 0)),
        full((1, D_MODEL)),
        full((D_MODEL, 4 * ML_WIDTH)),
        full((16, D_MODEL)),
        full((16, 1)),
        full((D_MODEL, 2 * CM_WIDTH)),
        full((D_MODEL, 2 * MLA_RANK + 2 * HEAD_PAD)),
        full((D_MODEL, MLA_ROPE)),
        full((1, CM_WIDTH)),
        full((CM_GROUPS, CHUNK, CHUNK)),
        full((CHUNK, CM_WIDTH)),
        full((1, MLA_RANK)),
        full((1, MLA_RANK)),
        full((MLA_RANK, MLA_HEADS * HEAD_PAD)),
        full((MLA_RANK, MLA_HEADS * HEAD_PAD)),
        full((MLA_RANK, MLA_HEADS * HEAD_PAD)),
        full((MLA_RANK, MLA_HEADS * MLA_V)),
        rows(HEAD_PAD),
        rows(HEAD_PAD),
    ]
    out_specs = [
        rows(4 * ML_WIDTH),
        pl.BlockSpec((16, TM), lambda i: (0, i)),
        rows(CM_WIDTH),
        rows(MLA_RANK),
        rows(MLA_ROPE),
        pl.BlockSpec((MLA_HEADS, TM, HEAD_PAD), lambda i: (0, i, 0)),
        pl.BlockSpec((MLA_HEADS, TM, HEAD_PAD), lambda i: (0, i, 0)),
        rows(MLA_HEADS * MLA_V),
    ]
    out_shape = [
        jax.ShapeDtypeStruct((T_ALL, 4 * ML_WIDTH), F32),
        jax.ShapeDtypeStruct((16, T_ALL), F32),
        jax.ShapeDtypeStruct((T_ALL, CM_WIDTH), F32),
        jax.ShapeDtypeStruct((T_ALL, MLA_RANK), F32),
        jax.ShapeDtypeStruct((T_ALL, MLA_ROPE), F32),
        jax.ShapeDtypeStruct((MLA_HEADS, T_ALL, HEAD_PAD), BF16),
        jax.ShapeDtypeStruct((MLA_HEADS, T_ALL, HEAD_PAD), BF16),
        jax.ShapeDtypeStruct((T_ALL, MLA_HEADS * MLA_V), BF16),
    ]
    return pl.pallas_call(
        _proj_kernel,
        grid=(nblk,),
        in_specs=in_specs,
        out_specs=out_specs,
        out_shape=out_shape,
        compiler_params=pltpu.CompilerParams(dimension_semantics=("parallel",), vmem_limit_bytes=VMEM_LIMIT),
        name="proj_in",
    )(x_ctx, x_smp, mod, lw["g1"], lw["wml"], lw["wgt"], lw["gb_col"], lw["wcm"], lw["wmla"],
      lw["wkr"], lw["cmg"], lw["ws"], lw["cmb"], lw["qg"], lw["kvg"], lw["wuqa"], lw["wuqb"], lw["wk"], lw["wv"],
      cos_t, sin_t)


def _mlstm_schedule():
    pair, fwd_a, bwd_a, fwd_b, bwd_b, first = [], [], [], [], [], []
    base = 0
    for p in range((N_CTX_SEQ + N_SMP_SEQ) // 2):
        nc = (CTX_LEN if 2 * p < N_CTX_SEQ else SMP_LEN) // CHUNK
        for j in range(nc):
            pair.append(p)
            fwd_a.append(base + j)
            bwd_a.append(base + nc - 1 - j)
            fwd_b.append(base + nc + j)
            bwd_b.append(base + 2 * nc - 1 - j)
            first.append(1 if j == 0 else 0)
        base += 2 * nc
    as_i32 = lambda a: jnp.asarray(np.asarray(a, np.int32))
    return tuple(as_i32(a) for a in (pair, fwd_a, bwd_a, fwd_b, bwd_b, first))


def _scan_cummax(x, direction):
    L = CHUNK
    rows = x.shape[0]
    x = jnp.concatenate([x, x], axis=0)
    lane = lax.broadcasted_iota(jnp.int32, x.shape, 1)
    k = 1
    while k < L:
        if direction == 0:
            shifted = jnp.where(lane >= k, pltpu.roll(x, k, axis=1), NEG_INF)
        else:
            shifted = jnp.where(lane < L - k, pltpu.roll(x, L - k, axis=1), NEG_INF)
        x = jnp.maximum(x, shifted)
        k *= 2
    return x[0:rows]


def _rows_to_lane_cols(rows, eye, rep, pieces):
    x = jnp.concatenate([jnp.broadcast_to(rows[h:h + 1, :], (rep, CHUNK)) for h in range(ML_HEADS)], axis=0)
    out = None
    for _ in range(pieces):
        xb = x.astype(BF16)
        part = _dot_nt(eye, xb)
        out = part if out is None else out + part
        x = x - xb.astype(F32)
    return out


def _per_head_lanes(x512, lane64):
    lo = jnp.where(lane64, x512[:, 0:128], x512[:, 128:256])
    hi = jnp.where(lane64, x512[:, 256:384], x512[:, 384:512])
    return jnp.concatenate([lo, hi], axis=1)


def _mlstm_direction(ml, g_row, direction, slot, c_ref, n_ref, m_ref):
    L = CHUNK
    t_idx = lax.broadcasted_iota(jnp.int32, (L, L), 0)
    s_idx = lax.broadcasted_iota(jnp.int32, (L, L), 1)
    visible = (s_idx <= t_idx) if direction == 0 else (s_idx >= t_idx)
    tri = jnp.where(visible, 1.0, 0.0).astype(BF16)
    eye = jnp.where(s_idx == t_idx, 1.0, 0.0).astype(BF16)
    lane64 = lax.broadcasted_iota(jnp.int32, (1, 128), 1) < ML_DIM
    head_of_lane = lax.broadcasted_iota(jnp.int32, (1, ML_WIDTH), 1) >> 6
    same_head = ((lax.broadcasted_iota(jnp.int32, (ML_WIDTH, ML_WIDTH), 0) >> 6)
                 == (lax.broadcasted_iota(jnp.int32, (ML_WIDTH, ML_WIDTH), 1) >> 6))

    i0 = 8 * direction
    i_row = g_row[i0:i0 + ML_HEADS, :]
    lf_row = _log_sigmoid(g_row[i0 + ML_HEADS:i0 + 2 * ML_HEADS, :])
    r1, r2, r3 = _split3(lf_row)
    b_row = _dot_nt(r1, tri) + _dot_nt(r2, tri) + _dot_nt(r3, tri)
    yield None
    m_rep = m_ref[slot, direction, 0:ML_HEADS, :]
    a_row = i_row - b_row
    g_row_ = jnp.maximum(m_rep, _scan_cummax(a_row, direction))
    yield None
    b_end = jnp.sum(lf_row, axis=1, keepdims=True)
    log_w = b_end - b_row + i_row
    m_new = jnp.maximum(b_end + m_rep, jnp.max(log_w, axis=1, keepdims=True))
    w_k_row = jnp.exp(log_w - m_new)
    decay_rep = jnp.exp(b_end + m_rep - m_new)

    yield None
    g512 = _rows_to_lane_cols(g_row_, eye, 128, 2)
    g_full = _per_head_lanes(g512, lane64)
    yield None
    b_full = _rows_to_lane_cols(b_row, eye, ML_DIM, 2)
    wk_full = _rows_to_lane_cols(w_k_row, eye, ML_DIM, 1)
    m_full = _per_head_lanes(jnp.concatenate([m_rep[h:h + 1, :] for h in range(ML_HEADS)], axis=1), lane64)
    decay_full = _per_head_lanes(jnp.concatenate([decay_rep[h:h + 1, :] for h in range(ML_HEADS)], axis=1),
                                 lane64)

    yield None
    q = ml[:, 0:ML_WIDTH]
    k = ml[:, ML_WIDTH:2 * ML_WIDTH] * (ML_DIM ** -0.5)
    v = ml[:, 2 * ML_WIDTH:3 * ML_WIDTH]
    qb = q.astype(BF16)
    kb = k.astype(BF16)
    vb = v.astype(BF16)
    num = jnp.zeros((L, ML_WIDTH), F32)
    rowsum = jnp.zeros((L, ML_WIDTH), F32)
    for hd in range(ML_HEADS):
        w_intra = jnp.where(visible, jnp.exp(a_row[hd:hd + 1, :] - g512[:, hd * 128:(hd + 1) * 128]), 0.0)
        q_h = jnp.where(head_of_lane == hd, q, 0.0).astype(BF16)
        sw = _dot_nt(q_h, kb) * w_intra
        num = jnp.where(head_of_lane == hd, _dot(sw.astype(BF16), vb), num)
        rowsum = jnp.where(head_of_lane == hd, jnp.sum(sw, axis=-1, keepdims=True), rowsum)
        yield None

    C = c_ref[slot, direction]
    n_row = n_ref[slot, direction, 0:1, :]
    w_inter = jnp.exp(m_full - g_full)
    block_ones = jnp.where(same_head, 1.0, 0.0).astype(BF16)
    qn = _dot((q * n_row).astype(BF16), block_ones)
    num = num + w_inter * _dot(qb, C.astype(BF16))
    den = rowsum + w_inter * qn
    h_out = num / jnp.maximum(jnp.abs(den), jnp.exp(-(b_full + g_full)))

    kw = wk_full * k
    c_ref[slot, direction] = decay_full * C + jnp.where(same_head, _dot_tn(kw.astype(BF16), vb), 0.0)
    n_ref[slot, direction, 0:1, :] = decay_full * n_row + jnp.sum(kw, axis=0, keepdims=True)
    m_ref[slot, direction, 0:ML_HEADS, :] = m_new
    yield h_out


def _interleave(stage_generators):
    results = [None] * len(stage_generators)
    live = list(range(len(stage_generators)))
    while live:
        for idx in list(live):
            try:
                value = next(stage_generators[idx])
                if value is not None:
                    results[idx] = value
            except StopIteration:
                live.remove(idx)
    return results


def _mlstm_kernel(pair_ref, fa_ref, ba_ref, fb_ref, bb_ref, first_ref,
                  mlfa_ref, mlba_ref, mlfb_ref, mlbb_ref, grfa_ref, grba_ref, grfb_ref, grbb_ref,
                  c0_ref, n0_ref, m0_ref, hf_ref, hb_ref, c_ref, n_ref, m_ref):
    step = pl.program_id(0)

    @pl.when(first_ref[step] == 1)
    def _():
        c_ref[...] = c0_ref[...]
        n_ref[...] = n0_ref[...]
        m_ref[...] = m0_ref[...]

    hfa, hba, hfb, hbb = _interleave([
        _mlstm_direction(mlfa_ref[...], grfa_ref[...], 0, 0, c_ref, n_ref, m_ref),
        _mlstm_direction(mlba_ref[...], grba_ref[...], 1, 0, c_ref, n_ref, m_ref),
        _mlstm_direction(mlfb_ref[...], grfb_ref[...], 0, 1, c_ref, n_ref, m_ref),
        _mlstm_direction(mlbb_ref[...], grbb_ref[...], 1, 1, c_ref, n_ref, m_ref)])
    hf_ref[0, 0] = hfa
    hf_ref[0, 1] = hfb
    hb_ref[0, 0] = hba
    hb_ref[0, 1] = hbb


def _mlstm_call(ml, gates_t, c0, n0, m0):
    sched = _mlstm_schedule()
    nseq = N_CTX_SEQ + N_SMP_SEQ
    nsteps = int(sched[0].shape[0])
    chunk_rows = lambda which: pl.BlockSpec((CHUNK, 4 * ML_WIDTH), lambda i, *s: (s[which][i], 0))
    gate_cols = lambda which: pl.BlockSpec((16, CHUNK), lambda i, *s: (0, s[which][i]))
    c_spec = pl.BlockSpec((2, 2, ML_WIDTH, ML_WIDTH), lambda i, *s: (s[0][i], 0, 0, 0))
    n_spec = pl.BlockSpec((2, 2, 8, ML_WIDTH), lambda i, *s: (s[0][i], 0, 0, 0))
    m_spec = pl.BlockSpec((2, 2, 8, 128), lambda i, *s: (s[0][i], 0, 0, 0))
    h_spec = pl.BlockSpec((1, 2, CHUNK, ML_WIDTH), lambda i, *s: (i, 0, 0, 0))
    grid_spec = pltpu.PrefetchScalarGridSpec(
        num_scalar_prefetch=6,
        grid=(nsteps,),
        in_specs=[chunk_rows(1), chunk_rows(2), chunk_rows(3), chunk_rows(4),
                  gate_cols(1), gate_cols(2), gate_cols(3), gate_cols(4), c_spec, n_spec, m_spec],
        out_specs=[h_spec, h_spec, c_spec, n_spec, m_spec],
    )
    return pl.pallas_call(
        _mlstm_kernel,
        grid_spec=grid_spec,
        out_shape=[
            jax.ShapeDtypeStruct((nsteps, 2, CHUNK, ML_WIDTH), F32),
            jax.ShapeDtypeStruct((nsteps, 2, CHUNK, ML_WIDTH), F32),
            jax.ShapeDtypeStruct((nseq, 2, ML_WIDTH, ML_WIDTH), F32),
            jax.ShapeDtypeStruct((nseq, 2, 8, ML_WIDTH), F32),
            jax.ShapeDtypeStruct((nseq, 2, 8, 128), F32),
        ],
        compiler_params=pltpu.CompilerParams(dimension_semantics=("arbitrary",), vmem_limit_bytes=VMEM_LIMIT),
        name="mlstm",
    )(*sched, ml, ml, ml, ml, gates_t, gates_t, gates_t, gates_t, c0, n0, m0)


def _h_pair_specs():
    ctx_blk = T_CTX // TM
    blk_per_seq = SMP_LEN // TM
    ctx_steps = N_CTX_SEQ // 2 * (CTX_LEN // CHUNK)

    def place(i):
        i_s = jnp.maximum(i - ctx_blk, 0)
        seq_s = i_s // blk_per_seq
        jb = i_s % blk_per_seq
        is_ctx = i < ctx_blk
        slot = jnp.where(is_ctx, i % 2, seq_s % 2)
        base = (ctx_steps + (seq_s // 2) * (SMP_LEN // CHUNK)) // 2
        fwd = jnp.where(is_ctx, i // 2, base + jb)
        bwd = jnp.where(is_ctx, i // 2, base + blk_per_seq - 1 - jb)
        return fwd, bwd, slot

    block = (2, None, CHUNK, ML_WIDTH)
    return (pl.BlockSpec(block, lambda i: (place(i)[0], place(i)[2], 0, 0)),
            pl.BlockSpec(block, lambda i: (place(i)[1], place(i)[2], 0, 0)))


def _cache_kv_kernel(ckv_ref, kr_ref, wk_ref, wv_ref, place_ref, k_ref, v_ref):
    ckvb = ckv_ref[...].astype(BF16)
    kp = _dot(ckvb, wk_ref[...])
    kr128 = _dot(kr_ref[...].astype(BF16), place_ref[...])
    for hd in range(MLA_HEADS):
        k_ref[hd] = (kp[:, hd * HEAD_PAD:(hd + 1) * HEAD_PAD] + kr128).astype(BF16)
    v_ref[...] = _dot(ckvb, wv_ref[...]).astype(BF16)


def _cache_kv_call(ckv, kr, wk, wv, place):
    n = N_SMP_SEQ * PAST_LEN
    tb = PAST_LEN
    return pl.pallas_call(
        _cache_kv_kernel,
        grid=(n // tb,),
        in_specs=[
            pl.BlockSpec((tb, MLA_RANK), lambda i: (i, 0)),
            pl.BlockSpec((tb, MLA_ROPE), lambda i: (i, 0)),
            pl.BlockSpec((MLA_RANK, MLA_HEADS * HEAD_PAD), lambda i: (0, 0)),
            pl.BlockSpec((MLA_RANK, MLA_HEADS * MLA_V), lambda i: (0, 0)),
            pl.BlockSpec((MLA_ROPE, HEAD_PAD), lambda i: (0, 0)),
        ],
        out_specs=[
            pl.BlockSpec((MLA_HEADS, tb, HEAD_PAD), lambda i: (0, i, 0)),
            pl.BlockSpec((tb, MLA_HEADS * MLA_V), lambda i: (i, 0)),
        ],
        out_shape=[
            jax.ShapeDtypeStruct((MLA_HEADS, n, HEAD_PAD), BF16),
            jax.ShapeDtypeStruct((n, MLA_HEADS * MLA_V), BF16),
        ],
        compiler_params=pltpu.CompilerParams(dimension_semantics=("parallel",)),
        name="cache_kv",
    )(ckv, kr, wk, wv, place)


def _attn_kernel(*refs, has_cache):
    if has_cache:
        q_ref, kn_ref, vn_ref, kc_ref, vc_ref, o_ref = refs
    else:
        q_ref, kn_ref, vn_ref, o_ref = refs
    scale = (MLA_NOPE + MLA_ROPE) ** -0.5

    def head(hd):
        lanes = slice(hd * MLA_V, (hd + 1) * MLA_V)
        q = q_ref[hd]
        s_n = _dot_nt(q, kn_ref[hd])
        if has_cache:
            s_c = _dot_nt(q, kc_ref[hd])
        yield None
        m = jnp.max(s_n, axis=-1, keepdims=True)
        if has_cache:
            m = jnp.maximum(m, jnp.max(s_c, axis=-1, keepdims=True))
        yield None
        p_n = jnp.exp((s_n - m) * scale)
        l = jnp.sum(p_n, axis=-1, keepdims=True)
        o = _dot(p_n.astype(BF16), vn_ref[:, lanes])
        if has_cache:
            p_c = jnp.exp((s_c - m) * scale)
            l = l + jnp.sum(p_c, axis=-1, keepdims=True)
            o = o + _dot(p_c.astype(BF16), vc_ref[:, lanes])
        yield o / l

    outs = []
    for hd in range(0, MLA_HEADS, ATTN_HEADS_IN_FLIGHT):
        outs += _interleave([head(hd + i) for i in range(ATTN_HEADS_IN_FLIGHT)])
    o_ref[...] = jnp.concatenate(outs, axis=-1)


def _attn_ctx_call(q, k, v):
    tq = CTX_LEN
    return pl.pallas_call(
        functools.partial(_attn_kernel, has_cache=False),
        grid=(N_CTX_SEQ,),
        in_specs=[
            pl.BlockSpec((MLA_HEADS, tq, HEAD_PAD), lambda s: (0, s, 0)),
            pl.BlockSpec((MLA_HEADS, tq, HEAD_PAD), lambda s: (0, s, 0)),
            pl.BlockSpec((tq, MLA_HEADS * MLA_V), lambda s: (s, 0)),
        ],
        out_specs=pl.BlockSpec((tq, MLA_HEADS * MLA_V), lambda s: (s, 0)),
        out_shape=jax.ShapeDtypeStruct((T_CTX, MLA_HEADS * MLA_V), F32),
        compiler_params=pltpu.CompilerParams(dimension_semantics=("parallel",), vmem_limit_bytes=VMEM_LIMIT),
        name="attn_ctx",
    )(q, k, v)


def _attn_smp_call(q, k, v, kc, vc):
    tq = 256
    qb_per_seq = SMP_LEN // tq
    ctx_qb = T_CTX // tq
    ctx_kb = T_CTX // SMP_LEN
    return pl.pallas_call(
        functools.partial(_attn_kernel, has_cache=True),
        grid=(N_SMP_SEQ, qb_per_seq),
        in_specs=[
            pl.BlockSpec((MLA_HEADS, tq, HEAD_PAD), lambda b, i: (0, ctx_qb + b * qb_per_seq + i, 0)),
            pl.BlockSpec((MLA_HEADS, SMP_LEN, HEAD_PAD), lambda b, i: (0, ctx_kb + b, 0)),
            pl.BlockSpec((SMP_LEN, MLA_HEADS * MLA_V), lambda b, i: (ctx_kb + b, 0)),
            pl.BlockSpec((MLA_HEADS, PAST_LEN, HEAD_PAD), lambda b, i: (0, b, 0)),
            pl.BlockSpec((PAST_LEN, MLA_HEADS * MLA_V), lambda b, i: (b, 0)),
        ],
        out_specs=pl.BlockSpec((tq, MLA_HEADS * MLA_V), lambda b, i: (b * qb_per_seq + i, 0)),
        out_shape=jax.ShapeDtypeStruct((T_SMP, MLA_HEADS * MLA_V), F32),
        compiler_params=pltpu.CompilerParams(dimension_semantics=("parallel", "parallel"),
                                             vmem_limit_bytes=VMEM_LIMIT),
        name="attn_smp",
    )(q, k, v, kc, vc)


def _out_kernel(xc_ref, xs_ref, mod_ref, ml_ref, hf_ref, hb_ref, cm_ref, attc_ref, atts_ref, wo_ref, g2_ref,
                wq_ref, sk_ref, x1_ref, h2t_ref, st_ref):
    is_ctx = pl.program_id(0) < T_CTX // TM
    att = jnp.where(is_ctx, attc_ref[...], atts_ref[...])
    mod = mod_ref[0]
    g1 = mod[:, 2 * D_MODEL:3 * D_MODEL]
    sh2 = mod[:, 3 * D_MODEL:4 * D_MODEL]
    sc2 = mod[:, 4 * D_MODEL:5 * D_MODEL]
    h_sum = jnp.concatenate([hf_ref[0] + hb_ref[1], hf_ref[1] + hb_ref[0]], axis=0)
    mlo = _sigmoid(ml_ref[:, 3 * ML_WIDTH:4 * ML_WIDTH]) * h_sum
    mix = (_dot(mlo.astype(BF16), wo_ref[0:ML_WIDTH, :])
           + _dot(cm_ref[...].astype(BF16), wo_ref[ML_WIDTH:ML_WIDTH + CM_WIDTH, :])
           + _dot(att.astype(BF16), wo_ref[ML_WIDTH + CM_WIDTH:, :]))
    x1 = _select_x(xc_ref, xs_ref) + g1 * mix
    x1_ref[...] = x1
    h2f = _rms(x1, g2_ref[...]) * (1.0 + sc2) + sh2
    h2t_ref[...] = h2f.T.astype(BF16)
    qp = _dot(h2f.astype(BF16), wq_ref[...]).astype(BF16)
    for hh in range(2 * PEER_HEADS):
        st_ref[hh] = _dot_nt(sk_ref[hh % 2], qp[:, hh * 128:(hh + 1) * 128])


def _out_call(x_ctx, x_smp, mod, ml, hf, hb, cm, att_ctx, att_smp, lw):
    nblk = T_ALL // TM
    ctx_blk = T_CTX // TM
    rows = lambda w: pl.BlockSpec((TM, w), lambda i: (i, 0))
    full = lambda shape: pl.BlockSpec(shape, lambda i: (0,) * len(shape))
    att_w = MLA_HEADS * MLA_V
    attc_spec = pl.BlockSpec((TM, att_w), lambda i: (jnp.minimum(i, ctx_blk - 1), 0))
    atts_spec = pl.BlockSpec((TM, att_w), lambda i: (jnp.maximum(i - ctx_blk, 0), 0))
    return pl.pallas_call(
        _out_kernel,
        grid=(nblk,),
        in_specs=_x_specs(x_ctx, x_smp) + [
            pl.BlockSpec((1, 1, 6 * D_MODEL), lambda i: (_mod_row_of_block(i, TM), 0, 0)),
            rows(4 * ML_WIDTH), *_h_pair_specs(), rows(CM_WIDTH), attc_spec, atts_spec,
            full((D_MODEL, D_MODEL)), full((1, D_MODEL)), full((D_MODEL, 2 * PEER_HEADS * 128)),
            full((2, PEER_NKEYS, 128)),
        ],
        out_specs=[rows(D_MODEL), pl.BlockSpec((D_MODEL, TM), lambda i: (0, i)),
                   pl.BlockSpec((2 * PEER_HEADS, PEER_NKEYS, TM), lambda i: (0, 0, i))],
        out_shape=[
            jax.ShapeDtypeStruct((T_ALL, D_MODEL), F32),
            jax.ShapeDtypeStruct((D_MODEL, T_ALL), BF16),
            jax.ShapeDtypeStruct((2 * PEER_HEADS, PEER_NKEYS, T_ALL), F32),
        ],
        compiler_params=pltpu.CompilerParams(dimension_semantics=("parallel",), vmem_limit_bytes=VMEM_LIMIT),
        name="proj_out",
    )(x_ctx, x_smp, mod, ml, hf, hb, cm, att_ctx, att_smp, lw["wo"], lw["g2"], lw["wq"], lw["sk"])


def _sorting_network_pairs(n):
    pairs = []
    p = 1
    while p < n:
        k = p
        while k >= 1:
            for j in range(k % p, n - k, 2 * k):
                for i in range(min(k, n - j - k)):
                    if (i + j) // (2 * p) == (i + j + k) // (2 * p):
                        pairs.append((i + j, i + j + k))
            k //= 2
        p *= 2
    return pairs


_SORT16_PAIRS = _sorting_network_pairs(PEER_TOPK)


def _pop16(lists):
    lists = list(lists)
    vals = []
    for k in range(PEER_TOPK):
        m = jnp.max(lists[0], axis=0, keepdims=True)
        vals.append(m)
        if k == PEER_TOPK - 1:
            break
        hit = lists[0] == m
        for i in range(PEER_TOPK - 1 - k):
            lists[i] = jnp.where(hit, lists[i + 1], lists[i])
    return vals


def _top16_rows(s):
    tiles = [s[8 * j:8 * j + 8] for j in range(s.shape[0] // 8)]
    assert len(tiles) == PEER_TOPK
    for i, j in _SORT16_PAIRS:
        tiles[i], tiles[j] = jnp.maximum(tiles[i], tiles[j]), jnp.minimum(tiles[i], tiles[j])
    return _pop16(tiles)


def _count_steps(x, thresholds, below):
    r = jnp.zeros(x.shape, F32)
    for q, t in enumerate(thresholds):
        r = jnp.where((x < t) if below else (x >= t), float(q + 1), r)
    return r


def _pack_rows_bf16(x):
    return pltpu.bitcast(_pack_rows_words(x), BF16)


def _pack_rows_words(x):
    r, n = x.shape
    x4 = x.reshape(r // 16, 2, 8, n)
    lo = x4[:, 0].reshape(r // 2, n)
    hi = x4[:, 1].reshape(r // 2, n)
    return pltpu.pack_elementwise([lo, hi], packed_dtype=BF16)


def _dup_bf16_words(x):
    u = pltpu.bitcast(x.astype(BF16).astype(F32), jnp.int32)
    return u | lax.shift_right_logical(u, jnp.full(u.shape, 16, jnp.int32))


def _rows_to_array(rows, row_iota):
    arr = jnp.zeros(row_iota.shape, F32)
    for i, r in enumerate(rows):
        arr = jnp.where(row_iota == i, r, arr)
    return arr


def _topk_kernel(st_ref, e1_ref, cut_ref, e2_ref, r2_ref):
    n = st_ref.shape[-1]
    row16 = lax.broadcasted_iota(jnp.int32, (PEER_TOPK, n), 0)

    def head(hd, carry):
        s1 = st_ref[2 * hd]
        s2 = st_ref[2 * hd + 1]
        v1 = _top16_rows(s1)
        v2 = _top16_rows(s2)
        rank2 = _count_steps(s2, v2, below=True)
        v1arr = _rows_to_array(v1, row16)
        c = _pop16([v1arr + v2[q] for q in range(PEER_TOPK)])
        tau = c[PEER_TOPK - 1]
        z = jnp.zeros((1, n), F32)
        for ck in c:
            z = z + jnp.exp(ck - c[0])
        sigma = [jnp.min(jnp.where(v1arr + v2[q] >= tau, v1arr, jnp.inf), axis=0, keepdims=True)
                 for q in range(PEER_TOPK)]
        cut = _count_steps(s1, sigma, below=False)
        e1_ref[hd] = _dup_bf16_words(jnp.exp(s1 - v1[0]) / z)
        cut_ref[hd] = _dup_bf16_words(cut)
        e2_ref[hd] = _pack_rows_words(jnp.exp(s2 - v2[0]))
        r2_ref[hd] = _pack_rows_words(rank2)
        return carry

    lax.fori_loop(0, PEER_HEADS, head, 0)


def _topk_call(st):
    tn = TN_TOPK
    spec = pl.BlockSpec((PEER_HEADS, PEER_NKEYS, tn), lambda i: (0, 0, i))
    word_spec = pl.BlockSpec((PEER_HEADS, PEER_NKEYS // 2, tn), lambda i: (0, 0, i))
    stat = lambda dt: jax.ShapeDtypeStruct((PEER_HEADS, PEER_NKEYS, T_ALL), dt)
    words = jax.ShapeDtypeStruct((PEER_HEADS, PEER_NKEYS // 2, T_ALL), jnp.uint32)
    return pl.pallas_call(
        _topk_kernel,
        grid=(T_ALL // tn,),
        in_specs=[pl.BlockSpec((2 * PEER_HEADS, PEER_NKEYS, tn), lambda i: (0, 0, i))],
        out_specs=[spec, spec, word_spec, word_spec],
        out_shape=[stat(jnp.int32), stat(jnp.int32), words, words],
        compiler_params=pltpu.CompilerParams(dimension_semantics=("parallel",), vmem_limit_bytes=VMEM_LIMIT),
        name="peer_topk",
    )(st)


def _row_tile_bf16(row):
    blk = pltpu.bitcast(jnp.broadcast_to(row, (8, row.shape[1])), BF16)
    return jnp.concatenate([blk] * (PEER_NKEYS // 16), axis=0)


def _gelu_tanh_bf16(x):
    log2e = 1.4426950408889634
    w = x * (-1.5957691216057308 * log2e - (0.07135481627159584 * log2e) * (x * x))
    return x / (1.0 + jnp.exp2(w))


def _expert_kernel(h2t_ref, u_ref, vt_ref, e1_ref, cut_ref, e2_ref, r2_ref, x1_ref, mod_ref, fg_ref,
                   o_ref, acc_ref, a_ref, *, final_norm):
    j = pl.program_id(1)
    n_tiles = pl.num_programs(1)
    tm = h2t_ref.shape[1]
    a_per_tile = TE_EXP // PEER_NKEYS

    @pl.when(j == 0)
    def _():
        acc_ref[...] = jnp.zeros_like(acc_ref)

    def gate_tile(ai):
        a = j * a_per_tile + ai
        gate = jnp.zeros((PEER_NKEYS, tm), BF16)
        for hd in range(PEER_HEADS):
            cut_a = _row_tile_bf16(cut_ref[hd, pl.ds(a, 1), :])
            e1_a = _row_tile_bf16(e1_ref[hd, pl.ds(a, 1), :])
            r2 = pltpu.bitcast(r2_ref[hd], BF16)
            e2 = pltpu.bitcast(e2_ref[hd], BF16)
            gate = gate + jnp.where(r2 < cut_a, e2, jnp.zeros((), BF16)) * e1_a
        return gate

    a_per_sub = EXP_SUB // PEER_NKEYS
    n_sub = TE_EXP // EXP_SUB
    s = [None] * n_sub
    gates = [None] * n_sub
    s[0] = _dot(u_ref[0:EXP_SUB, :], h2t_ref[...])
    gates[0] = [gate_tile(ai) for ai in range(a_per_sub)]
    for sub in range(n_sub):
        if sub + 1 < n_sub:
            s[sub + 1] = _dot(u_ref[(sub + 1) * EXP_SUB:(sub + 2) * EXP_SUB, :], h2t_ref[...])
            gates[sub + 1] = [gate_tile((sub + 1) * a_per_sub + i) for i in range(a_per_sub)]
        g = _gelu_tanh_bf16(_pack_rows_bf16(s[sub]))
        for i in range(a_per_sub):
            r0 = sub * EXP_SUB + i * PEER_NKEYS
            a_ref[r0:r0 + PEER_NKEYS, :] = gates[sub][i] * g[i * PEER_NKEYS:(i + 1) * PEER_NKEYS, :]
    acc_ref[...] += _dot(vt_ref[...], a_ref[...])

    @pl.when(j == n_tiles - 1)
    def _():
        g2 = mod_ref[0][:, 5 * D_MODEL:6 * D_MODEL]
        y = x1_ref[...] + g2 * acc_ref[...].T
        if final_norm:
            y = _rms(y, fg_ref[...])
        o_ref[...] = y


def _expert_call(h2t, u_bf, vt_bf, layer, e1, cut, e2, r2, x1, mod, final_g, final_norm, tok_start, tok_count):
    tm, te = TM_EXP, TE_EXP
    b0 = tok_start // tm
    n_tiles = PEER_EXPERTS // te
    stat = pl.BlockSpec((PEER_HEADS, PEER_NKEYS, tm), lambda i, j: (0, 0, b0 + i))
    stat_words = pl.BlockSpec((PEER_HEADS, PEER_NKEYS // 2, tm), lambda i, j: (0, 0, b0 + i))
    return pl.pallas_call(
        functools.partial(_expert_kernel, final_norm=final_norm),
        grid=(tok_count // tm, n_tiles),
        in_specs=[
            pl.BlockSpec((D_MODEL, tm), lambda i, j: (0, b0 + i)),
            pl.BlockSpec((None, te, D_MODEL), lambda i, j: (layer, j, 0)),
            pl.BlockSpec((None, D_MODEL, te), lambda i, j: (layer, 0, j)),
            stat, stat, stat_words, stat_words,
            pl.BlockSpec((tm, D_MODEL), lambda i, j: (b0 + i, 0)),
            pl.BlockSpec((1, 1, 6 * D_MODEL), lambda i, j: (_mod_row_of_block(b0 + i, tm), 0, 0)),
            pl.BlockSpec((1, D_MODEL), lambda i, j: (0, 0)),
        ],
        out_specs=pl.BlockSpec((tm, D_MODEL), lambda i, j: (i, 0)),
        out_shape=jax.ShapeDtypeStruct((tok_count, D_MODEL), F32),
        scratch_shapes=[
            pltpu.VMEM((D_MODEL, tm), F32),
            pltpu.VMEM((te, tm), BF16),
        ],
        compiler_params=pltpu.CompilerParams(dimension_semantics=("parallel", "arbitrary"),
                                             vmem_limit_bytes=VMEM_LIMIT),
        name="peer_experts",
    )(h2t, u_bf, vt_bf, e1, cut, e2, r2, x1, mod, final_g)


def _u_prep_kernel(u_ref, o_ref):
    o_ref[...] = u_ref[...].astype(BF16)


def _v_prep_kernel(v_ref, o_ref):
    o_ref[...] = _pack_rows_bf16(v_ref[...]).T


def _table_prep_call(peer_u, peer_v):
    te = TE_PREP
    grid = (DEPTH, PEER_EXPERTS // te)
    rows = pl.BlockSpec((None, te, D_MODEL), lambda l, j: (l, j, 0))
    params = pltpu.CompilerParams(dimension_semantics=("parallel", "parallel"), vmem_limit_bytes=VMEM_LIMIT)
    u_bf = pl.pallas_call(
        _u_prep_kernel, grid=grid, in_specs=[rows], out_specs=rows,
        out_shape=jax.ShapeDtypeStruct((DEPTH, PEER_EXPERTS, D_MODEL), BF16),
        compiler_params=params, name="peer_u_prep")(peer_u)
    vt_bf = pl.pallas_call(
        _v_prep_kernel, grid=grid, in_specs=[rows],
        out_specs=pl.BlockSpec((None, D_MODEL, te), lambda l, j: (l, 0, j)),
        out_shape=jax.ShapeDtypeStruct((DEPTH, D_MODEL, PEER_EXPERTS), BF16),
        compiler_params=params, name="peer_v_prep")(peer_v)
    return u_bf, vt_bf


def _rope_swap_cols(w):
    return jnp.concatenate([-w[:, 8:16], w[:, 0:8], -w[:, 24:32], w[:, 16:24]], axis=1)


def _pad_heads(parts, n_heads):
    k = next(p[0].shape[0] for p in parts if p[0] is not None)
    cols = []
    for hd in range(n_heads):
        for arr, w in parts:
            cols.append(jnp.zeros((k, w), F32) if arr is None else arr[:, hd * w:(hd + 1) * w])
    return jnp.concatenate(cols, axis=1)


def _rope_tables():
    pos = np.arange(SMP_LEN)
    freqs = ROPE_THETA ** (-np.arange(0, ROPE_AXIS, 2, dtype=np.float32) / ROPE_AXIS)
    ang_r = (pos // GRID_W).astype(np.float32)[:, None] * freqs
    ang_c = (pos % GRID_W).astype(np.float32)[:, None] * freqs
    ang = jnp.asarray(np.concatenate([ang_r, ang_r, ang_c, ang_c], axis=1).astype(np.float32))
    cos32 = jnp.cos(ang)
    sin32 = jnp.sin(ang)
    ones = jnp.ones((SMP_LEN, MLA_NOPE), F32)
    cos_s = jnp.concatenate([ones, cos32, ones[:, :HEAD_PAD - MLA_NOPE - MLA_ROPE]], axis=1)
    sin_s = jnp.concatenate([0 * ones, sin32, 0 * ones[:, :HEAD_PAD - MLA_NOPE - MLA_ROPE]], axis=1)
    cos_t = jnp.concatenate([jnp.ones((T_CTX, HEAD_PAD), F32)] + [cos_s] * N_SMP_SEQ, axis=0)
    sin_t = jnp.concatenate([jnp.zeros((T_CTX, HEAD_PAD), F32)] + [sin_s] * N_SMP_SEQ, axis=0)
    return cos_t, sin_t


def _layer_weights(l, norm1_g, w_in, mlstm_gate_b, cm_norm_g, cm_ws, cm_b, mla_q_norm_g, mla_w_uq, mla_kv_norm_g,
                   mla_w_ukv, w_out, norm2_g, peer_w_q, peer_subkeys):
    w = w_in[l]
    o_g = 4 * ML_WIDTH
    o_cm = o_g + 16
    o_cq = o_cm + 2 * CM_WIDTH
    o_ckv = o_cq + MLA_RANK
    o_kr = o_ckv + MLA_RANK
    w_kr = w[:, o_kr:o_kr + MLA_ROPE]
    zeros_l = jnp.zeros((D_MODEL, MLA_NOPE), F32)
    zeros_r = jnp.zeros((D_MODEL, HEAD_PAD - MLA_NOPE - MLA_ROPE), F32)
    kr128 = jnp.concatenate([zeros_l, w_kr, zeros_r], axis=1)
    krsw128 = jnp.concatenate([zeros_l, _rope_swap_cols(w_kr), zeros_r], axis=1)
    uq = mla_w_uq[l].reshape(MLA_RANK, MLA_HEADS, MLA_NOPE + MLA_ROPE)
    uq_nope = uq[:, :, :MLA_NOPE].reshape(MLA_RANK, -1)
    uq_rope = uq[:, :, MLA_NOPE:].reshape(MLA_RANK, -1)
    uq_rope_sw = jnp.concatenate(
        [_rope_swap_cols(uq_rope[:, hd * MLA_ROPE:(hd + 1) * MLA_ROPE]) for hd in range(MLA_HEADS)], axis=1)
    pad_w = HEAD_PAD - MLA_NOPE - MLA_ROPE
    ukv = mla_w_ukv[l].reshape(MLA_RANK, MLA_HEADS, MLA_NOPE + MLA_V)
    uk = ukv[:, :, :MLA_NOPE].reshape(MLA_RANK, -1)
    uv = ukv[:, :, MLA_NOPE:].reshape(MLA_RANK, -1)
    gb = mlstm_gate_b[l]
    return {
        "g1": norm1_g[l].reshape(1, D_MODEL),
        "wml": w[:, 0:o_g].astype(BF16),
        "wgt": w[:, o_g:o_cm].T.astype(BF16),
        "gb_col": gb.reshape(16, 1),
        "wcm": w[:, o_cm:o_cq].astype(BF16),
        "wmla": jnp.concatenate([w[:, o_cq:o_kr], kr128, krsw128], axis=1).astype(BF16),
        "wkr": w_kr.astype(BF16),
        "cmg": cm_norm_g[l].reshape(1, CM_WIDTH),
        "ws": cm_ws[l].astype(BF16),
        "cmb": jnp.repeat(cm_b[l].T, CM_WIDTH // CM_GROUPS, axis=1),
        "qg": mla_q_norm_g[l].reshape(1, MLA_RANK),
        "kvg": mla_kv_norm_g[l].reshape(1, MLA_RANK),
        "wuqa": _pad_heads([(uq_nope, MLA_NOPE), (uq_rope, MLA_ROPE), (None, pad_w)], MLA_HEADS).astype(BF16),
        "wuqb": _pad_heads([(None, MLA_NOPE), (uq_rope_sw, MLA_ROPE), (None, pad_w)], MLA_HEADS).astype(BF16),
        "wk": _pad_heads([(uk, MLA_NOPE), (None, HEAD_PAD - MLA_NOPE)], MLA_HEADS).astype(BF16),
        "wv": uv.astype(BF16),
        "wo": w_out[l].astype(BF16),
        "g2": norm2_g[l].reshape(1, D_MODEL),
        "wq": peer_w_q[l].astype(BF16),
        "sk": peer_subkeys[l].astype(BF16),
    }


def kernel(x_prompt, x_sample, c, cache_mla_ckv, cache_mla_krope, state_mlstm_C, state_mlstm_n, state_mlstm_m, c_ctx, norm1_g, ada_w, ada_b, w_in, mlstm_gate_b, cm_norm_g, cm_ws, cm_b, mla_q_norm_g, mla_w_uq, mla_kv_norm_g, mla_w_ukv, w_out, norm2_g, peer_w_q, peer_subkeys, peer_u, peer_v, final_g):
    x_ctx = x_prompt.reshape(T_CTX, D_MODEL)
    x_smp = x_sample.reshape(T_SMP, D_MODEL)
    cvecs = jnp.concatenate([c_ctx[None, :], c, jnp.zeros((N_MOD_ROWS - 1 - N_SMP_SEQ, D_MODEL), F32)], axis=0)
    mod_all = _ada_call(cvecs, ada_w, ada_b)
    cos_t, sin_t = _rope_tables()
    place = jnp.concatenate([jnp.zeros((MLA_ROPE, MLA_NOPE), F32), jnp.eye(MLA_ROPE, dtype=F32),
                             jnp.zeros((MLA_ROPE, HEAD_PAD - MLA_NOPE - MLA_ROPE), F32)], axis=1).astype(BF16)
    final_g2 = final_g.reshape(1, D_MODEL)
    u_bf, vt_bf = _table_prep_call(peer_u, peer_v)

    ckvs, krs, Cs, ns, ms = [], [], [], [], []
    for l in range(DEPTH):
        lw = _layer_weights(l, norm1_g, w_in, mlstm_gate_b, cm_norm_g, cm_ws, cm_b, mla_q_norm_g, mla_w_uq,
                            mla_kv_norm_g, mla_w_ukv, w_out, norm2_g, peer_w_q, peer_subkeys)
        mod = mod_all[l].reshape(N_MOD_ROWS, 1, 6 * D_MODEL)
        ml, gates_t, cm, ckvn, kr, q, k, v = _proj_call(x_ctx, x_smp, mod, lw, cos_t, sin_t)

        c_blk = jnp.einsum('bdhij,hg->bdhigj', state_mlstm_C[:, l], jnp.eye(ML_HEADS, dtype=F32))
        c0 = jnp.concatenate([jnp.zeros((N_CTX_SEQ, 2, ML_WIDTH, ML_WIDTH), F32),
                              c_blk.reshape(N_SMP_SEQ, 2, ML_WIDTH, ML_WIDTH)], axis=0)
        n0 = jnp.concatenate([jnp.zeros((N_CTX_SEQ, 2, 1, ML_WIDTH), F32),
                              state_mlstm_n[:, l].reshape(N_SMP_SEQ, 2, 1, ML_WIDTH)], axis=0)
        n0 = jnp.pad(n0, ((0, 0), (0, 0), (0, 7), (0, 0)))
        m0 = jnp.concatenate([jnp.zeros((N_CTX_SEQ, 2, ML_HEADS), F32), state_mlstm_m[:, l]], axis=0)
        m0 = jnp.pad(jnp.broadcast_to(m0[..., None], m0.shape + (128,)), ((0, 0), (0, 0), (0, 8 - ML_HEADS), (0, 0)))
        hf, hb, c_fin, n_fin, m_fin = _mlstm_call(ml, gates_t, c0, n0, m0)

        kc, vc = _cache_kv_call(cache_mla_ckv[:, l].reshape(N_SMP_SEQ * PAST_LEN, MLA_RANK),
                                cache_mla_krope[:, l].reshape(N_SMP_SEQ * PAST_LEN, MLA_ROPE),
                                lw["wk"], lw["wv"], place)
        att_ctx = _attn_ctx_call(q, k, v)
        att_smp = _attn_smp_call(q, k, v, kc, vc)

        x1, h2t, st = _out_call(x_ctx, x_smp, mod, ml, hf, hb, cm, att_ctx, att_smp, lw)
        e1, cut, e2, r2 = _topk_call(st)
        experts = functools.partial(_expert_call, h2t, u_bf, vt_bf, l, e1, cut, e2, r2, x1, mod, final_g2)
        if l < DEPTH - 1:
            x_ctx = x_smp = experts(False, 0, T_ALL)
        else:
            x_ctx = experts(True, 0, T_CTX)
            x_smp = experts(True, T_CTX, T_SMP)

        ckvs.append(ckvn[:T_CTX].reshape(N_CTX_SEQ, CTX_LEN, MLA_RANK))
        krs.append(kr[:T_CTX].reshape(N_CTX_SEQ, CTX_LEN, MLA_ROPE))
        Cs.append(jnp.stack([c_fin[:N_CTX_SEQ, :, hd * ML_DIM:(hd + 1) * ML_DIM, hd * ML_DIM:(hd + 1) * ML_DIM]
                             for hd in range(ML_HEADS)], axis=2))
        ns.append(n_fin[:N_CTX_SEQ, :, 0, :].reshape(N_CTX_SEQ, 2, ML_HEADS, ML_DIM))
        ms.append(m_fin[:N_CTX_SEQ, :, 0:ML_HEADS, 0])

    y_prompt = x_ctx.reshape(N_CTX_SEQ, CTX_LEN, D_MODEL)
    y_sample = x_smp.reshape(N_SMP_SEQ, SMP_LEN, D_MODEL)
    return (y_prompt, y_sample, jnp.stack(ckvs, axis=1), jnp.stack(krs, axis=1), jnp.stack(Cs, axis=1),
            jnp.stack(ns, axis=1), jnp.stack(ms, axis=1))
```

```python
import functools

import numpy as np
import jax
import jax.numpy as jnp
from jax import lax
from jax.experimental import pallas as pl
from jax.experimental.pallas import tpu as pltpu

F32 = jnp.float32
BF16 = jnp.bfloat16

D_MODEL = 1024
N_CTX_SEQ = 16
CTX_LEN = 256
N_SMP_SEQ = 4
SMP_LEN = 2048
PAST_LEN = 256
DEPTH = 2
GRID_W = 64
EPS = 1e-6
T_CTX = N_CTX_SEQ * CTX_LEN
T_SMP = N_SMP_SEQ * SMP_LEN
T_ALL = T_CTX + T_SMP
N_MOD_ROWS = 8

ML_HEADS = 4
ML_DIM = 64
ML_WIDTH = 256
CHUNK = 128
CM_GROUPS = 4
CM_WIDTH = 256
MLA_HEADS = 8
MLA_NOPE = 64
MLA_ROPE = 32
MLA_V = 64
MLA_RANK = 256
HEAD_PAD = 128
ROPE_AXIS = 16
ROPE_THETA = 10000.0
QK_SCALE_LOG2E = (MLA_NOPE + MLA_ROPE) ** -0.5 * 1.4426950408889634
PEER_HEADS = 8
PEER_NKEYS = 128
PEER_EXPERTS = PEER_NKEYS * PEER_NKEYS
PEER_TOPK = 16

TM = 256
TN_TOPK = 256
TM_EXP = 512
TE_EXP = 2048
EXP_SUB = 512
TE_PREP = 2048
ATTN_HEADS_IN_FLIGHT = 2
VMEM_LIMIT = 56 * 1024 * 1024

NEG_INF = float("-inf")


def _dot(a, b):
    return jnp.dot(a, b, preferred_element_type=F32)


def _dot_nt(a, b):
    return lax.dot_general(a, b, (((1,), (1,)), ((), ())), preferred_element_type=F32)


def _dot_tn(a, b):
    return lax.dot_general(a, b, (((0,), (0,)), ((), ())), preferred_element_type=F32)


def _split3(a):
    a1 = a.astype(BF16)
    r1 = a - a1.astype(F32)
    a2 = r1.astype(BF16)
    a3 = (r1 - a2.astype(F32)).astype(BF16)
    return a1, a2, a3


def _rms(x, g):
    return x * lax.rsqrt(jnp.mean(x * x, axis=-1, keepdims=True) + EPS) * g


def _sigmoid(x):
    return 1.0 / (1.0 + jnp.exp(-x))


def _log_sigmoid(x):
    return jnp.minimum(x, 0.0) - jnp.log(1.0 + jnp.exp(-jnp.abs(x)))


def _gelu_tanh(x):
    return 0.5 * x * (1.0 + jnp.tanh(0.7978845608028654 * (x + 0.044715 * (x * x * x))))


def _mod_row_of_block(i, rows_per_block):
    ctx_blocks = T_CTX // rows_per_block
    per_seq = SMP_LEN // rows_per_block
    return jnp.where(i < ctx_blocks, 0, 1 + (i - ctx_blocks) // per_seq)


def _ada_kernel(cv_ref, w_ref, b_ref, o_ref):
    cv = cv_ref[...]
    s = cv * _sigmoid(cv)
    w = w_ref[0]
    w1, w2, w3 = _split3(w)
    s1, s2, s3 = _split3(s)
    acc = _dot(s1, w1) + (_dot(s1, w2) + _dot(s2, w1)) + (_dot(s1, w3) + _dot(s2, w2) + _dot(s3, w1))
    o_ref[0] = acc + b_ref[0]


def _ada_call(cvecs, ada_w, ada_b):
    tn = 1024
    return pl.pallas_call(
        _ada_kernel,
        grid=(DEPTH, 6 * D_MODEL // tn),
        in_specs=[
            pl.BlockSpec((N_MOD_ROWS, D_MODEL), lambda l, j: (0, 0)),
            pl.BlockSpec((1, D_MODEL, tn), lambda l, j: (l, 0, j)),
            pl.BlockSpec((1, 1, tn), lambda l, j: (l, 0, j)),
        ],
        out_specs=pl.BlockSpec((1, N_MOD_ROWS, tn), lambda l, j: (l, 0, j)),
        out_shape=jax.ShapeDtypeStruct((DEPTH, N_MOD_ROWS, 6 * D_MODEL), F32),
        compiler_params=pltpu.CompilerParams(dimension_semantics=("parallel", "parallel")),
        name="ada_mod",
    )(cvecs, ada_w, ada_b.reshape(DEPTH, 1, 6 * D_MODEL))


def _x_specs(x_ctx, x_smp, blk=lambda i: i):
    ctx_blk = T_CTX // TM
    smp_off = x_smp.shape[0] // TM - T_SMP // TM
    return [pl.BlockSpec((TM, D_MODEL), lambda i: (jnp.minimum(blk(i), ctx_blk - 1), 0)),
            pl.BlockSpec((TM, D_MODEL), lambda i: (jnp.maximum(blk(i) - ctx_blk, 0) + smp_off, 0))]


def _select_x(xc_ref, xs_ref):
    return jnp.where(pl.program_id(0) < T_CTX // TM, xc_ref[...], xs_ref[...])


def _proj_kernel(xc_ref, xs_ref, mod_ref, g1_ref, wml_ref, wgt_ref, gbc_ref, wcm_ref, wmla_ref, wkr_ref,
                 cmg_ref, ws_ref, cmb_ref, qg_ref, kvg_ref, wuqa_ref, wuqb_ref, wk_ref, wv_ref, cos_ref, sin_ref,
                 ml_ref, gatest_ref, cm_ref, ckvn_ref, kr_ref, q_ref, k_ref, v_ref):
    x = _select_x(xc_ref, xs_ref)
    mod = mod_ref[0]
    sh1 = mod[:, 0:D_MODEL]
    sc1 = mod[:, D_MODEL:2 * D_MODEL]
    h = _rms(x, g1_ref[...]) * (1.0 + sc1) + sh1
    hb = h.astype(BF16)

    ml_ref[...] = _dot(hb, wml_ref[...])
    gatest_ref[...] = _dot_nt(wgt_ref[...], hb) + gbc_ref[...]

    cm = _dot(hb, wcm_ref[...])
    u = cm[:, 0:CM_WIDTH]
    vn = _rms(cm[:, CM_WIDTH:2 * CM_WIDTH], cmg_ref[...]).astype(BF16)
    lane_group = lax.broadcasted_iota(jnp.int32, (CHUNK, CM_WIDTH), 1) >> 6
    for c in range(TM // CHUNK):
        rows = slice(c * CHUNK, (c + 1) * CHUNK)
        vc = vn[rows]
        mixed = jnp.zeros((CHUNK, CM_WIDTH), F32)
        for g in range(CM_GROUPS):
            mixed = jnp.where(lane_group == g, _dot(ws_ref[g], vc), mixed)
        cm_ref[rows, :] = u[rows] * (mixed + cmb_ref[...])

    mla = _dot(hb, wmla_ref[...])
    qn = _rms(mla[:, 0:MLA_RANK], qg_ref[...]).astype(BF16)
    ckvn = _rms(mla[:, MLA_RANK:2 * MLA_RANK], kvg_ref[...])
    ckvn_ref[...] = ckvn
    kr_ref[...] = _dot(hb, wkr_ref[...])
    cos = cos_ref[...]
    sin = sin_ref[...]
    kr_rot = mla[:, 2 * MLA_RANK:2 * MLA_RANK + HEAD_PAD] * cos + mla[:, 2 * MLA_RANK + HEAD_PAD:] * sin
    qa = _dot(qn, wuqa_ref[...])
    qb = _dot(qn, wuqb_ref[...])
    ckvb = ckvn.astype(BF16)
    kp = _dot(ckvb, wk_ref[...])
    for hd in range(MLA_HEADS):
        cols = slice(hd * HEAD_PAD, (hd + 1) * HEAD_PAD)
        q_ref[hd] = ((qa[:, cols] * cos + qb[:, cols] * sin) * QK_SCALE_LOG2E).astype(BF16)
        k_ref[hd] = (kp[:, cols] + kr_rot).astype(BF16)
    v_ref[...] = _dot(ckvb, wv_ref[...]).astype(BF16)


def _proj_call(x_ctx, x_smp, mod, lw, cos_t, sin_t):
    nblk = T_ALL // TM
    full = lambda shape: pl.BlockSpec(shape, lambda i: (0,) * len(shape))
    rows = lambda w: pl.BlockSpec((TM, w), lambda i: (i, 0))
    in_specs = _x_specs(x_ctx, x_smp) + [
        pl.BlockSpec((1, 1, 6 * D_MODEL), lambda i: (_mod_row_of_block(i, TM), 0, 0)),
        full((1, D_MODEL)),
        full((D_MODEL, 4 * ML_WIDTH)),
        full((16, D_MODEL)),
        full((16, 1)),
        full((D_MODEL, 2 * CM_WIDTH)),
        full((D_MODEL, 2 * MLA_RANK + 2 * HEAD_PAD)),
        full((D_MODEL, MLA_ROPE)),
        full((1, CM_WIDTH)),
        full((CM_GROUPS, CHUNK, CHUNK)),
        full((CHUNK, CM_WIDTH)),
        full((1, MLA_RANK)),
        full((1, MLA_RANK)),
        full((MLA_RANK, MLA_HEADS * HEAD_PAD)),
        full((MLA_RANK, MLA_HEADS * HEAD_PAD)),
        full((MLA_RANK, MLA_HEADS * HEAD_PAD)),
        full((MLA_RANK, MLA_HEADS * MLA_V)),
        rows(HEAD_PAD),
        rows(HEAD_PAD),
    ]
    out_specs = [
        rows(4 * ML_WIDTH),
        pl.BlockSpec((16, TM), lambda i: (0, i)),
        rows(CM_WIDTH),
        rows(MLA_RANK),
        rows(MLA_ROPE),
        pl.BlockSpec((MLA_HEADS, TM, HEAD_PAD), lambda i: (0, i, 0)),
        pl.BlockSpec((MLA_HEADS, TM, HEAD_PAD), lambda i: (0, i, 0)),
        rows(MLA_HEADS * MLA_V),
    ]
    out_shape = [
        jax.ShapeDtypeStruct((T_ALL, 4 * ML_WIDTH), F32),
        jax.ShapeDtypeStruct((16, T_ALL), F32),
        jax.ShapeDtypeStruct((T_ALL, CM_WIDTH), F32),
        jax.ShapeDtypeStruct((T_ALL, MLA_RANK), F32),
        jax.ShapeDtypeStruct((T_ALL, MLA_ROPE), F32),
        jax.ShapeDtypeStruct((MLA_HEADS, T_ALL, HEAD_PAD), BF16),
        jax.ShapeDtypeStruct((MLA_HEADS, T_ALL, HEAD_PAD), BF16),
        jax.ShapeDtypeStruct((T_ALL, MLA_HEADS * MLA_V), BF16),
    ]
    return pl.pallas_call(
        _proj_kernel,
        grid=(nblk,),
        in_specs=in_specs,
        out_specs=out_specs,
        out_shape=out_shape,
        compiler_params=pltpu.CompilerParams(dimension_semantics=("parallel",), vmem_limit_bytes=VMEM_LIMIT),
        name="proj_in",
    )(x_ctx, x_smp, mod, lw["g1"], lw["wml"], lw["wgt"], lw["gb_col"], lw["wcm"], lw["wmla"],
      lw["wkr"], lw["cmg"], lw["ws"], lw["cmb"], lw["qg"], lw["kvg"], lw["wuqa"], lw["wuqb"], lw["wk"], lw["wv"],
      cos_t, sin_t)


def _mlstm_schedule():
    pair, fwd_a, bwd_a, fwd_b, bwd_b, first = [], [], [], [], [], []
    base = 0
    for p in range((N_CTX_SEQ + N_SMP_SEQ) // 2):
        nc = (CTX_LEN if 2 * p < N_CTX_SEQ else SMP_LEN) // CHUNK
        for j in range(nc):
            pair.append(p)
            fwd_a.append(base + j)
            bwd_a.append(base + nc - 1 - j)
            fwd_b.append(base + nc + j)
            bwd_b.append(base + 2 * nc - 1 - j)
            first.append(1 if j == 0 else 0)
        base += 2 * nc
    as_i32 = lambda a: jnp.asarray(np.asarray(a, np.int32))
    return tuple(as_i32(a) for a in (pair, fwd_a, bwd_a, fwd_b, bwd_b, first))


def _scan_cummax(x, direction):
    L = CHUNK
    rows = x.shape[0]
    x = jnp.concatenate([x, x], axis=0)
    lane = lax.broadcasted_iota(jnp.int32, x.shape, 1)
    k = 1
    while k < L:
        if direction == 0:
            shifted = jnp.where(lane >= k, pltpu.roll(x, k, axis=1), NEG_INF)
        else:
            shifted = jnp.where(lane < L - k, pltpu.roll(x, L - k, axis=1), NEG_INF)
        x = jnp.maximum(x, shifted)
        k *= 2
    return x[0:rows]


def _rows_to_lane_cols(rows, eye, rep, pieces):
    x = jnp.concatenate([jnp.broadcast_to(rows[h:h + 1, :], (rep, CHUNK)) for h in range(ML_HEADS)], axis=0)
    out = None
    for _ in range(pieces):
        xb = x.astype(BF16)
        part = _dot_nt(eye, xb)
        out = part if out is None else out + part
        x = x - xb.astype(F32)
    return out


def _per_head_lanes(x512, lane64):
    lo = jnp.where(lane64, x512[:, 0:128], x512[:, 128:256])
    hi = jnp.where(lane64, x512[:, 256:384], x512[:, 384:512])
    return jnp.concatenate([lo, hi], axis=1)


def _mlstm_direction(ml, g_row, direction, slot, c_ref, n_ref, m_ref):
    L = CHUNK
    t_idx = lax.broadcasted_iota(jnp.int32, (L, L), 0)
    s_idx = lax.broadcasted_iota(jnp.int32, (L, L), 1)
    visible = (s_idx <= t_idx) if direction == 0 else (s_idx >= t_idx)
    tri = jnp.where(visible, 1.0, 0.0).astype(BF16)
    eye = jnp.where(s_idx == t_idx, 1.0, 0.0).astype(BF16)
    lane64 = lax.broadcasted_iota(jnp.int32, (1, 128), 1) < ML_DIM
    head_of_lane = lax.broadcasted_iota(jnp.int32, (1, ML_WIDTH), 1) >> 6
    same_head = ((lax.broadcasted_iota(jnp.int32, (ML_WIDTH, ML_WIDTH), 0) >> 6)
                 == (lax.broadcasted_iota(jnp.int32, (ML_WIDTH, ML_WIDTH), 1) >> 6))

    i0 = 8 * direction
    i_row = g_row[i0:i0 + ML_HEADS, :]
    lf_row = _log_sigmoid(g_row[i0 + ML_HEADS:i0 + 2 * ML_HEADS, :])
    r1, r2, r3 = _split3(lf_row)
    b_row = _dot_nt(r1, tri) + _dot_nt(r2, tri) + _dot_nt(r3, tri)
    yield None
    m_rep = m_ref[slot, direction, 0:ML_HEADS, :]
    a_row = i_row - b_row
    g_row_ = jnp.maximum(m_rep, _scan_cummax(a_row, direction))
    yield None
    b_end = jnp.sum(lf_row, axis=1, keepdims=True)
    log_w = b_end - b_row + i_row
    m_new = jnp.maximum(b_end + m_rep, jnp.max(log_w, axis=1, keepdims=True))
    w_k_row = jnp.exp(log_w - m_new)
    decay_rep = jnp.exp(b_end + m_rep - m_new)

    yield None
    g512 = _rows_to_lane_cols(g_row_, eye, 128, 2)
    g_full = _per_head_lanes(g512, lane64)
    yield None
    b_full = _rows_to_lane_cols(b_row, eye, ML_DIM, 2)
    wk_full = _rows_to_lane_cols(w_k_row, eye, ML_DIM, 1)
    m_full = _per_head_lanes(jnp.concatenate([m_rep[h:h + 1, :] for h in range(ML_HEADS)], axis=1), lane64)
    decay_full = _per_head_lanes(jnp.concatenate([decay_rep[h:h + 1, :] for h in range(ML_HEADS)], axis=1),
                                 lane64)

    yield None
    q = ml[:, 0:ML_WIDTH]
    k = ml[:, ML_WIDTH:2 * ML_WIDTH] * (ML_DIM ** -0.5)
    v = ml[:, 2 * ML_WIDTH:3 * ML_WIDTH]
    qb = q.astype(BF16)
    kb = k.astype(BF16)
    vb = v.astype(BF16)
    num = jnp.zeros((L, ML_WIDTH), F32)
    rowsum = jnp.zeros((L, ML_WIDTH), F32)
    for hd in range(ML_HEADS):
        w_intra = jnp.where(visible, jnp.exp(a_row[hd:hd + 1, :] - g512[:, hd * 128:(hd + 1) * 128]), 0.0)
        q_h = jnp.where(head_of_lane == hd, q, 0.0).astype(BF16)
        sw = _dot_nt(q_h, kb) * w_intra
        num = jnp.where(head_of_lane == hd, _dot(sw.astype(BF16), vb), num)
        rowsum = jnp.where(head_of_lane == hd, jnp.sum(sw, axis=-1, keepdims=True), rowsum)
        yield None

    C = c_ref[slot, direction]
    n_row = n_ref[slot, direction, 0:1, :]
    w_inter = jnp.exp(m_full - g_full)
    block_ones = jnp.where(same_head, 1.0, 0.0).astype(BF16)
    qn = _dot((q * n_row).astype(BF16), block_ones)
    num = num + w_inter * _dot(qb, C.astype(BF16))
    den = rowsum + w_inter * qn
    h_out = num / jnp.maximum(jnp.abs(den), jnp.exp(-(b_full + g_full)))

    kw = wk_full * k
    c_ref[slot, direction] = decay_full * C + jnp.where(same_head, _dot_tn(kw.astype(BF16), vb), 0.0)
    n_ref[slot, direction, 0:1, :] = decay_full * n_row + jnp.sum(kw, axis=0, keepdims=True)
    m_ref[slot, direction, 0:ML_HEADS, :] = m_new
    yield h_out


def _interleave(stage_generators):
    results = [None] * len(stage_generators)
    live = list(range(len(stage_generators)))
    while live:
        for idx in list(live):
            try:
                value = next(stage_generators[idx])
                if value is not None:
                    results[idx] = value
            except StopIteration:
                live.remove(idx)
    return results


def _mlstm_kernel(pair_ref, fa_ref, ba_ref, fb_ref, bb_ref, first_ref,
                  mlfa_ref, mlba_ref, mlfb_ref, mlbb_ref, grfa_ref, grba_ref, grfb_ref, grbb_ref,
                  c0_ref, n0_ref, m0_ref, hf_ref, hb_ref, c_ref, n_ref, m_ref):
    step = pl.program_id(0)

    @pl.when(first_ref[step] == 1)
    def _():
        c_ref[...] = c0_ref[...]
        n_ref[...] = n0_ref[...]
        m_ref[...] = m0_ref[...]

    hfa, hba, hfb, hbb = _interleave([
        _mlstm_direction(mlfa_ref[...], grfa_ref[...], 0, 0, c_ref, n_ref, m_ref),
        _mlstm_direction(mlba_ref[...], grba_ref[...], 1, 0, c_ref, n_ref, m_ref),
        _mlstm_direction(mlfb_ref[...], grfb_ref[...], 0, 1, c_ref, n_ref, m_ref),
        _mlstm_direction(mlbb_ref[...], grbb_ref[...], 1, 1, c_ref, n_ref, m_ref)])
    hf_ref[0, 0] = hfa
    hf_ref[0, 1] = hfb
    hb_ref[0, 0] = hba
    hb_ref[0, 1] = hbb


def _mlstm_call(ml, gates_t, c0, n0, m0):
    sched = _mlstm_schedule()
    nseq = N_CTX_SEQ + N_SMP_SEQ
    nsteps = int(sched[0].shape[0])
    chunk_rows = lambda which: pl.BlockSpec((CHUNK, 4 * ML_WIDTH), lambda i, *s: (s[which][i], 0))
    gate_cols = lambda which: pl.BlockSpec((16, CHUNK), lambda i, *s: (0, s[which][i]))
    c_spec = pl.BlockSpec((2, 2, ML_WIDTH, ML_WIDTH), lambda i, *s: (s[0][i], 0, 0, 0))
    n_spec = pl.BlockSpec((2, 2, 8, ML_WIDTH), lambda i, *s: (s[0][i], 0, 0, 0))
    m_spec = pl.BlockSpec((2, 2, 8, 128), lambda i, *s: (s[0][i], 0, 0, 0))
    h_spec = pl.BlockSpec((1, 2, CHUNK, ML_WIDTH), lambda i, *s: (i, 0, 0, 0))
    grid_spec = pltpu.PrefetchScalarGridSpec(
        num_scalar_prefetch=6,
        grid=(nsteps,),
        in_specs=[chunk_rows(1), chunk_rows(2), chunk_rows(3), chunk_rows(4),
                  gate_cols(1), gate_cols(2), gate_cols(3), gate_cols(4), c_spec, n_spec, m_spec],
        out_specs=[h_spec, h_spec, c_spec, n_spec, m_spec],
    )
    return pl.pallas_call(
        _mlstm_kernel,
        grid_spec=grid_spec,
        out_shape=[
            jax.ShapeDtypeStruct((nsteps, 2, CHUNK, ML_WIDTH), F32),
            jax.ShapeDtypeStruct((nsteps, 2, CHUNK, ML_WIDTH), F32),
            jax.ShapeDtypeStruct((nseq, 2, ML_WIDTH, ML_WIDTH), F32),
            jax.ShapeDtypeStruct((nseq, 2, 8, ML_WIDTH), F32),
            jax.ShapeDtypeStruct((nseq, 2, 8, 128), F32),
        ],
        compiler_params=pltpu.CompilerParams(dimension_semantics=("arbitrary",), vmem_limit_bytes=VMEM_LIMIT),
        name="mlstm",
    )(*sched, ml, ml, ml, ml, gates_t, gates_t, gates_t, gates_t, c0, n0, m0)


def _h_pair_specs():
    ctx_blk = T_CTX // TM
    blk_per_seq = SMP_LEN // TM
    ctx_steps = N_CTX_SEQ // 2 * (CTX_LEN // CHUNK)

    def place(i):
        i_s = jnp.maximum(i - ctx_blk, 0)
        seq_s = i_s // blk_per_seq
        jb = i_s % blk_per_seq
        is_ctx = i < ctx_blk
        slot = jnp.where(is_ctx, i % 2, seq_s % 2)
        base = (ctx_steps + (seq_s // 2) * (SMP_LEN // CHUNK)) // 2
        fwd = jnp.where(is_ctx, i // 2, base + jb)
        bwd = jnp.where(is_ctx, i // 2, base + blk_per_seq - 1 - jb)
        return fwd, bwd, slot

    block = (2, None, CHUNK, ML_WIDTH)
    return (pl.BlockSpec(block, lambda i: (place(i)[0], place(i)[2], 0, 0)),
            pl.BlockSpec(block, lambda i: (place(i)[1], place(i)[2], 0, 0)))


def _cache_kv_kernel(ckv_ref, kr_ref, wk_ref, wv_ref, place_ref, k_ref, v_ref):
    ckvb = ckv_ref[...].astype(BF16)
    kp = _dot(ckvb, wk_ref[...])
    kr128 = _dot(kr_ref[...].astype(BF16), place_ref[...])
    for hd in range(MLA_HEADS):
        k_ref[hd] = (kp[:, hd * HEAD_PAD:(hd + 1) * HEAD_PAD] + kr128).astype(BF16)
    v_ref[...] = _dot(ckvb, wv_ref[...]).astype(BF16)


def _cache_kv_call(ckv, kr, wk, wv, place):
    n = N_SMP_SEQ * PAST_LEN
    tb = PAST_LEN
    return pl.pallas_call(
        _cache_kv_kernel,
        grid=(n // tb,),
        in_specs=[
            pl.BlockSpec((tb, MLA_RANK), lambda i: (i, 0)),
            pl.BlockSpec((tb, MLA_ROPE), lambda i: (i, 0)),
            pl.BlockSpec((MLA_RANK, MLA_HEADS * HEAD_PAD), lambda i: (0, 0)),
            pl.BlockSpec((MLA_RANK, MLA_HEADS * MLA_V), lambda i: (0, 0)),
            pl.BlockSpec((MLA_ROPE, HEAD_PAD), lambda i: (0, 0)),
        ],
        out_specs=[
            pl.BlockSpec((MLA_HEADS, tb, HEAD_PAD), lambda i: (0, i, 0)),
            pl.BlockSpec((tb, MLA_HEADS * MLA_V), lambda i: (i, 0)),
        ],
        out_shape=[
            jax.ShapeDtypeStruct((MLA_HEADS, n, HEAD_PAD), BF16),
            jax.ShapeDtypeStruct((n, MLA_HEADS * MLA_V), BF16),
        ],
        compiler_params=pltpu.CompilerParams(dimension_semantics=("parallel",)),
        name="cache_kv",
    )(ckv, kr, wk, wv, place)


def _attn_kernel(*refs, has_cache):
    if has_cache:
        q_ref, kn_ref, vn_ref, kc_ref, vc_ref, o_ref = refs
    else:
        q_ref, kn_ref, vn_ref, o_ref = refs

    def head(hd):
        lanes = slice(hd * MLA_V, (hd + 1) * MLA_V)
        q = q_ref[hd]
        s_n = _dot_nt(q, kn_ref[hd])
        if has_cache:
            s_c = _dot_nt(q, kc_ref[hd])
        yield None
        m = jnp.max(s_n, axis=-1, keepdims=True)
        if has_cache:
            m = jnp.maximum(m, jnp.max(s_c, axis=-1, keepdims=True))
        yield None
        p_n = jnp.exp2(s_n - m)
        l = jnp.sum(p_n, axis=-1, keepdims=True)
        o = _dot(p_n.astype(BF16), vn_ref[:, lanes])
        if has_cache:
            p_c = jnp.exp2(s_c - m)
            l = l + jnp.sum(p_c, axis=-1, keepdims=True)
            o = o + _dot(p_c.astype(BF16), vc_ref[:, lanes])
        yield o / l

    outs = []
    for hd in range(0, MLA_HEADS, ATTN_HEADS_IN_FLIGHT):
        outs += _interleave([head(hd + i) for i in range(ATTN_HEADS_IN_FLIGHT)])
    o_ref[...] = jnp.concatenate(outs, axis=-1)


def _attn_ctx_call(q, k, v):
    tq = CTX_LEN
    return pl.pallas_call(
        functools.partial(_attn_kernel, has_cache=False),
        grid=(N_CTX_SEQ,),
        in_specs=[
            pl.BlockSpec((MLA_HEADS, tq, HEAD_PAD), lambda s: (0, s, 0)),
            pl.BlockSpec((MLA_HEADS, tq, HEAD_PAD), lambda s: (0, s, 0)),
            pl.BlockSpec((tq, MLA_HEADS * MLA_V), lambda s: (s, 0)),
        ],
        out_specs=pl.BlockSpec((tq, MLA_HEADS * MLA_V), lambda s: (s, 0)),
        out_shape=jax.ShapeDtypeStruct((T_CTX, MLA_HEADS * MLA_V), F32),
        compiler_params=pltpu.CompilerParams(dimension_semantics=("parallel",), vmem_limit_bytes=VMEM_LIMIT),
        name="attn_ctx",
    )(q, k, v)


def _attn_smp_call(q, k, v, kc, vc):
    tq = 256
    qb_per_seq = SMP_LEN // tq
    ctx_qb = T_CTX // tq
    ctx_kb = T_CTX // SMP_LEN
    return pl.pallas_call(
        functools.partial(_attn_kernel, has_cache=True),
        grid=(N_SMP_SEQ, qb_per_seq),
        in_specs=[
            pl.BlockSpec((MLA_HEADS, tq, HEAD_PAD), lambda b, i: (0, ctx_qb + b * qb_per_seq + i, 0)),
            pl.BlockSpec((MLA_HEADS, SMP_LEN, HEAD_PAD), lambda b, i: (0, ctx_kb + b, 0)),
            pl.BlockSpec((SMP_LEN, MLA_HEADS * MLA_V), lambda b, i: (ctx_kb + b, 0)),
            pl.BlockSpec((MLA_HEADS, PAST_LEN, HEAD_PAD), lambda b, i: (0, b, 0)),
            pl.BlockSpec((PAST_LEN, MLA_HEADS * MLA_V), lambda b, i: (b, 0)),
        ],
        out_specs=pl.BlockSpec((tq, MLA_HEADS * MLA_V), lambda b, i: (b * qb_per_seq + i, 0)),
        out_shape=jax.ShapeDtypeStruct((T_SMP, MLA_HEADS * MLA_V), F32),
        compiler_params=pltpu.CompilerParams(dimension_semantics=("parallel", "parallel"),
                                             vmem_limit_bytes=VMEM_LIMIT),
        name="attn_smp",
    )(q, k, v, kc, vc)


def _out_kernel(xc_ref, xs_ref, mod_ref, ml_ref, hf_ref, hb_ref, cm_ref, attc_ref, atts_ref, wo_ref, g2_ref,
                wq_ref, sk_ref, x1_ref, h2t_ref, st_ref):
    is_ctx = pl.program_id(0) < T_CTX // TM
    att = jnp.where(is_ctx, attc_ref[...], atts_ref[...])
    mod = mod_ref[0]
    g1 = mod[:, 2 * D_MODEL:3 * D_MODEL]
    sh2 = mod[:, 3 * D_MODEL:4 * D_MODEL]
    sc2 = mod[:, 4 * D_MODEL:5 * D_MODEL]
    h_sum = jnp.concatenate([hf_ref[0] + hb_ref[1], hf_ref[1] + hb_ref[0]], axis=0)
    mlo = _sigmoid(ml_ref[:, 3 * ML_WIDTH:4 * ML_WIDTH]) * h_sum
    mix = (_dot(mlo.astype(BF16), wo_ref[0:ML_WIDTH, :])
           + _dot(cm_ref[...].astype(BF16), wo_ref[ML_WIDTH:ML_WIDTH + CM_WIDTH, :])
           + _dot(att.astype(BF16), wo_ref[ML_WIDTH + CM_WIDTH:, :]))
    x1 = _select_x(xc_ref, xs_ref) + g1 * mix
    x1_ref[...] = x1
    h2f = _rms(x1, g2_ref[...]) * (1.0 + sc2) + sh2
    h2t_ref[...] = h2f.T.astype(BF16)
    qp = _dot(h2f.astype(BF16), wq_ref[...]).astype(BF16)
    for hh in range(2 * PEER_HEADS):
        st_ref[hh] = _dot_nt(sk_ref[hh % 2], qp[:, hh * 128:(hh + 1) * 128])


def _out_call(x_ctx, x_smp, mod, ml, hf, hb, cm, att_ctx, att_smp, lw):
    nblk = T_ALL // TM
    ctx_blk = T_CTX // TM
    rows = lambda w: pl.BlockSpec((TM, w), lambda i: (i, 0))
    full = lambda shape: pl.BlockSpec(shape, lambda i: (0,) * len(shape))
    att_w = MLA_HEADS * MLA_V
    attc_spec = pl.BlockSpec((TM, att_w), lambda i: (jnp.minimum(i, ctx_blk - 1), 0))
    atts_spec = pl.BlockSpec((TM, att_w), lambda i: (jnp.maximum(i - ctx_blk, 0), 0))
    return pl.pallas_call(
        _out_kernel,
        grid=(nblk,),
        in_specs=_x_specs(x_ctx, x_smp) + [
            pl.BlockSpec((1, 1, 6 * D_MODEL), lambda i: (_mod_row_of_block(i, TM), 0, 0)),
            rows(4 * ML_WIDTH), *_h_pair_specs(), rows(CM_WIDTH), attc_spec, atts_spec,
            full((D_MODEL, D_MODEL)), full((1, D_MODEL)), full((D_MODEL, 2 * PEER_HEADS * 128)),
            full((2, PEER_NKEYS, 128)),
        ],
        out_specs=[rows(D_MODEL), pl.BlockSpec((D_MODEL, TM), lambda i: (0, i)),
                   pl.BlockSpec((2 * PEER_HEADS, PEER_NKEYS, TM), lambda i: (0, 0, i))],
        out_shape=[
            jax.ShapeDtypeStruct((T_ALL, D_MODEL), F32),
            jax.ShapeDtypeStruct((D_MODEL, T_ALL), BF16),
            jax.ShapeDtypeStruct((2 * PEER_HEADS, PEER_NKEYS, T_ALL), F32),
        ],
        compiler_params=pltpu.CompilerParams(dimension_semantics=("parallel",), vmem_limit_bytes=VMEM_LIMIT),
        name="proj_out",
    )(x_ctx, x_smp, mod, ml, hf, hb, cm, att_ctx, att_smp, lw["wo"], lw["g2"], lw["wq"], lw["sk"])


def _sorting_network_pairs(n):
    pairs = []
    p = 1
    while p < n:
        k = p
        while k >= 1:
            for j in range(k % p, n - k, 2 * k):
                for i in range(min(k, n - j - k)):
                    if (i + j) // (2 * p) == (i + j + k) // (2 * p):
                        pairs.append((i + j, i + j + k))
            k //= 2
        p *= 2
    return pairs


_SORT16_PAIRS = _sorting_network_pairs(PEER_TOPK)


def _pop16(lists):
    lists = list(lists)
    vals = []
    for k in range(PEER_TOPK):
        m = jnp.max(lists[0], axis=0, keepdims=True)
        vals.append(m)
        if k == PEER_TOPK - 1:
            break
        hit = lists[0] == m
        for i in range(PEER_TOPK - 1 - k):
            lists[i] = jnp.where(hit, lists[i + 1], lists[i])
    return vals


def _top16_rows(s):
    tiles = [s[8 * j:8 * j + 8] for j in range(s.shape[0] // 8)]
    assert len(tiles) == PEER_TOPK
    for i, j in _SORT16_PAIRS:
        tiles[i], tiles[j] = jnp.maximum(tiles[i], tiles[j]), jnp.minimum(tiles[i], tiles[j])
    return _pop16(tiles)


def _count_steps(x, thresholds, below):
    r = jnp.zeros(x.shape, F32)
    for q, t in enumerate(thresholds):
        r = jnp.where((x < t) if below else (x >= t), float(q + 1), r)
    return r


def _pack_rows_bf16(x):
    return pltpu.bitcast(_pack_rows_words(x), BF16)


def _pack_rows_words(x):
    r, n = x.shape
    x4 = x.reshape(r // 16, 2, 8, n)
    lo = x4[:, 0].reshape(r // 2, n)
    hi = x4[:, 1].reshape(r // 2, n)
    return pltpu.pack_elementwise([lo, hi], packed_dtype=BF16)


def _dup_bf16_words(x):
    u = pltpu.bitcast(x.astype(BF16).astype(F32), jnp.int32)
    return u | lax.shift_right_logical(u, jnp.full(u.shape, 16, jnp.int32))


def _rows_to_array(rows, row_iota):
    arr = jnp.zeros(row_iota.shape, F32)
    for i, r in enumerate(rows):
        arr = jnp.where(row_iota == i, r, arr)
    return arr


def _topk_kernel(st_ref, e1_ref, cut_ref, e2_ref, r2_ref):
    n = st_ref.shape[-1]
    row16 = lax.broadcasted_iota(jnp.int32, (PEER_TOPK, n), 0)

    def head(hd, carry):
        s1 = st_ref[2 * hd]
        s2 = st_ref[2 * hd + 1]
        v1 = _top16_rows(s1)
        v2 = _top16_rows(s2)
        rank2 = _count_steps(s2, v2, below=True)
        v1arr = _rows_to_array(v1, row16)
        c = _pop16([v1arr + v2[q] for q in range(PEER_TOPK)])
        tau = c[PEER_TOPK - 1]
        z = jnp.zeros((1, n), F32)
        for ck in c:
            z = z + jnp.exp(ck - c[0])
        sigma = [jnp.min(jnp.where(v1arr + v2[q] >= tau, v1arr, jnp.inf), axis=0, keepdims=True)
                 for q in range(PEER_TOPK)]
        cut = _count_steps(s1, sigma, below=False)
        e1_ref[hd] = _dup_bf16_words(jnp.exp(s1 - v1[0]) / z)
        cut_ref[hd] = _dup_bf16_words(cut)
        e2_ref[hd] = _pack_rows_words(jnp.exp(s2 - v2[0]))
        r2_ref[hd] = _pack_rows_words(rank2)
        return carry

    lax.fori_loop(0, PEER_HEADS, head, 0)


def _topk_call(st):
    tn = TN_TOPK
    spec = pl.BlockSpec((PEER_HEADS, PEER_NKEYS, tn), lambda i: (0, 0, i))
    word_spec = pl.BlockSpec((PEER_HEADS, PEER_NKEYS // 2, tn), lambda i: (0, 0, i))
    stat = lambda dt: jax.ShapeDtypeStruct((PEER_HEADS, PEER_NKEYS, T_ALL), dt)
    words = jax.ShapeDtypeStruct((PEER_HEADS, PEER_NKEYS // 2, T_ALL), jnp.uint32)
    return pl.pallas_call(
        _topk_kernel,
        grid=(T_ALL // tn,),
        in_specs=[pl.BlockSpec((2 * PEER_HEADS, PEER_NKEYS, tn), lambda i: (0, 0, i))],
        out_specs=[spec, spec, word_spec, word_spec],
        out_shape=[stat(jnp.int32), stat(jnp.int32), words, words],
        compiler_params=pltpu.CompilerParams(dimension_semantics=("parallel",), vmem_limit_bytes=VMEM_LIMIT),
        name="peer_topk",
    )(st)


def _row_tile_bf16(row):
    blk = pltpu.bitcast(jnp.broadcast_to(row, (8, row.shape[1])), BF16)
    return jnp.concatenate([blk] * (PEER_NKEYS // 16), axis=0)


def _gelu_tanh_bf16(x):
    log2e = 1.4426950408889634
    w = x * (-1.5957691216057308 * log2e - (0.07135481627159584 * log2e) * (x * x))
    return x / (1.0 + jnp.exp2(w))


def _expert_kernel(h2t_ref, u_ref, vt_ref, e1_ref, cut_ref, e2_ref, r2_ref, x1_ref, mod_ref, fg_ref,
                   o_ref, acc_ref, a_ref, *, final_norm):
    j = pl.program_id(1)
    n_tiles = pl.num_programs(1)
    tm = h2t_ref.shape[1]
    a_per_tile = TE_EXP // PEER_NKEYS

    @pl.when(j == 0)
    def _():
        acc_ref[...] = jnp.zeros_like(acc_ref)

    def gate_tile(ai):
        a = j * a_per_tile + ai
        gate = jnp.zeros((PEER_NKEYS, tm), BF16)
        for hd in range(PEER_HEADS):
            cut_a = _row_tile_bf16(cut_ref[hd, pl.ds(a, 1), :])
            e1_a = _row_tile_bf16(e1_ref[hd, pl.ds(a, 1), :])
            r2 = pltpu.bitcast(r2_ref[hd], BF16)
            e2 = pltpu.bitcast(e2_ref[hd], BF16)
            gate = gate + jnp.where(r2 < cut_a, e2, jnp.zeros((), BF16)) * e1_a
        return gate

    a_per_sub = EXP_SUB // PEER_NKEYS
    n_sub = TE_EXP // EXP_SUB
    s = [None] * n_sub
    gates = [None] * n_sub
    s[0] = _dot(u_ref[0:EXP_SUB, :], h2t_ref[...])
    gates[0] = [gate_tile(ai) for ai in range(a_per_sub)]
    for sub in range(n_sub):
        if sub + 1 < n_sub:
            s[sub + 1] = _dot(u_ref[(sub + 1) * EXP_SUB:(sub + 2) * EXP_SUB, :], h2t_ref[...])
            gates[sub + 1] = [gate_tile((sub + 1) * a_per_sub + i) for i in range(a_per_sub)]
        g = _gelu_tanh_bf16(_pack_rows_bf16(s[sub]))
        for i in range(a_per_sub):
            r0 = sub * EXP_SUB + i * PEER_NKEYS
            a_ref[r0:r0 + PEER_NKEYS, :] = gates[sub][i] * g[i * PEER_NKEYS:(i + 1) * PEER_NKEYS, :]
    acc_ref[...] += _dot(vt_ref[...], a_ref[...])

    @pl.when(j == n_tiles - 1)
    def _():
        g2 = mod_ref[0][:, 5 * D_MODEL:6 * D_MODEL]
        y = x1_ref[...] + g2 * acc_ref[...].T
        if final_norm:
            y = _rms(y, fg_ref[...])
        o_ref[...] = y


def _expert_call(h2t, u_bf, vt_bf, layer, e1, cut, e2, r2, x1, mod, final_g, final_norm, tok_start, tok_count):
    tm, te = TM_EXP, TE_EXP
    b0 = tok_start // tm
    n_tiles = PEER_EXPERTS // te
    stat = pl.BlockSpec((PEER_HEADS, PEER_NKEYS, tm), lambda i, j: (0, 0, b0 + i))
    stat_words = pl.BlockSpec((PEER_HEADS, PEER_NKEYS // 2, tm), lambda i, j: (0, 0, b0 + i))
    return pl.pallas_call(
        functools.partial(_expert_kernel, final_norm=final_norm),
        grid=(tok_count // tm, n_tiles),
        in_specs=[
            pl.BlockSpec((D_MODEL, tm), lambda i, j: (0, b0 + i)),
            pl.BlockSpec((None, te, D_MODEL), lambda i, j: (layer, j, 0)),
            pl.BlockSpec((None, D_MODEL, te), lambda i, j: (layer, 0, j)),
            stat, stat, stat_words, stat_words,
            pl.BlockSpec((tm, D_MODEL), lambda i, j: (b0 + i, 0)),
            pl.BlockSpec((1, 1, 6 * D_MODEL), lambda i, j: (_mod_row_of_block(b0 + i, tm), 0, 0)),
            pl.BlockSpec((1, D_MODEL), lambda i, j: (0, 0)),
        ],
        out_specs=pl.BlockSpec((tm, D_MODEL), lambda i, j: (i, 0)),
        out_shape=jax.ShapeDtypeStruct((tok_count, D_MODEL), F32),
        scratch_shapes=[
            pltpu.VMEM((D_MODEL, tm), F32),
            pltpu.VMEM((te, tm), BF16),
        ],
        compiler_params=pltpu.CompilerParams(dimension_semantics=("parallel", "arbitrary"),
                                             vmem_limit_bytes=VMEM_LIMIT),
        name="peer_experts",
    )(h2t, u_bf, vt_bf, e1, cut, e2, r2, x1, mod, final_g)


def _u_prep_kernel(u_ref, o_ref):
    o_ref[...] = u_ref[...].astype(BF16)


def _v_prep_kernel(v_ref, o_ref):
    o_ref[...] = _pack_rows_bf16(v_ref[...]).T


def _table_prep_call(peer_u, peer_v):
    te = TE_PREP
    grid = (DEPTH, PEER_EXPERTS // te)
    rows = pl.BlockSpec((None, te, D_MODEL), lambda l, j: (l, j, 0))
    params = pltpu.CompilerParams(dimension_semantics=("parallel", "parallel"), vmem_limit_bytes=VMEM_LIMIT)
    u_bf = pl.pallas_call(
        _u_prep_kernel, grid=grid, in_specs=[rows], out_specs=rows,
        out_shape=jax.ShapeDtypeStruct((DEPTH, PEER_EXPERTS, D_MODEL), BF16),
        compiler_params=params, name="peer_u_prep")(peer_u)
    vt_bf = pl.pallas_call(
        _v_prep_kernel, grid=grid, in_specs=[rows],
        out_specs=pl.BlockSpec((None, D_MODEL, te), lambda l, j: (l, 0, j)),
        out_shape=jax.ShapeDtypeStruct((DEPTH, D_MODEL, PEER_EXPERTS), BF16),
        compiler_params=params, name="peer_v_prep")(peer_v)
    return u_bf, vt_bf


def _rope_swap_cols(w):
    return jnp.concatenate([-w[:, 8:16], w[:, 0:8], -w[:, 24:32], w[:, 16:24]], axis=1)


def _pad_heads(parts, n_heads):
    k = next(p[0].shape[0] for p in parts if p[0] is not None)
    cols = []
    for hd in range(n_heads):
        for arr, w in parts:
            cols.append(jnp.zeros((k, w), F32) if arr is None else arr[:, hd * w:(hd + 1) * w])
    return jnp.concatenate(cols, axis=1)


def _rope_tables():
    pos = np.arange(SMP_LEN)
    freqs = ROPE_THETA ** (-np.arange(0, ROPE_AXIS, 2, dtype=np.float32) / ROPE_AXIS)
    ang_r = (pos // GRID_W).astype(np.float32)[:, None] * freqs
    ang_c = (pos % GRID_W).astype(np.float32)[:, None] * freqs
    ang = jnp.asarray(np.concatenate([ang_r, ang_r, ang_c, ang_c], axis=1).astype(np.float32))
    cos32 = jnp.cos(ang)
    sin32 = jnp.sin(ang)
    ones = jnp.ones((SMP_LEN, MLA_NOPE), F32)
    cos_s = jnp.concatenate([ones, cos32, ones[:, :HEAD_PAD - MLA_NOPE - MLA_ROPE]], axis=1)
    sin_s = jnp.concatenate([0 * ones, sin32, 0 * ones[:, :HEAD_PAD - MLA_NOPE - MLA_ROPE]], axis=1)
    cos_t = jnp.concatenate([jnp.ones((T_CTX, HEAD_PAD), F32)] + [cos_s] * N_SMP_SEQ, axis=0)
    sin_t = jnp.concatenate([jnp.zeros((T_CTX, HEAD_PAD), F32)] + [sin_s] * N_SMP_SEQ, axis=0)
    return cos_t, sin_t


def _layer_weights(l, norm1_g, w_in, mlstm_gate_b, cm_norm_g, cm_ws, cm_b, mla_q_norm_g, mla_w_uq, mla_kv_norm_g,
                   mla_w_ukv, w_out, norm2_g, peer_w_q, peer_subkeys):
    w = w_in[l]
    o_g = 4 * ML_WIDTH
    o_cm = o_g + 16
    o_cq = o_cm + 2 * CM_WIDTH
    o_ckv = o_cq + MLA_RANK
    o_kr = o_ckv + MLA_RANK
    w_kr = w[:, o_kr:o_kr + MLA_ROPE]
    zeros_l = jnp.zeros((D_MODEL, MLA_NOPE), F32)
    zeros_r = jnp.zeros((D_MODEL, HEAD_PAD - MLA_NOPE - MLA_ROPE), F32)
    kr128 = jnp.concatenate([zeros_l, w_kr, zeros_r], axis=1)
    krsw128 = jnp.concatenate([zeros_l, _rope_swap_cols(w_kr), zeros_r], axis=1)
    uq = mla_w_uq[l].reshape(MLA_RANK, MLA_HEADS, MLA_NOPE + MLA_ROPE)
    uq_nope = uq[:, :, :MLA_NOPE].reshape(MLA_RANK, -1)
    uq_rope = uq[:, :, MLA_NOPE:].reshape(MLA_RANK, -1)
    uq_rope_sw = jnp.concatenate(
        [_rope_swap_cols(uq_rope[:, hd * MLA_ROPE:(hd + 1) * MLA_ROPE]) for hd in range(MLA_HEADS)], axis=1)
    pad_w = HEAD_PAD - MLA_NOPE - MLA_ROPE
    ukv = mla_w_ukv[l].reshape(MLA_RANK, MLA_HEADS, MLA_NOPE + MLA_V)
    uk = ukv[:, :, :MLA_NOPE].reshape(MLA_RANK, -1)
    uv = ukv[:, :, MLA_NOPE:].reshape(MLA_RANK, -1)
    gb = mlstm_gate_b[l]
    return {
        "g1": norm1_g[l].reshape(1, D_MODEL),
        "wml": w[:, 0:o_g].astype(BF16),
        "wgt": w[:, o_g:o_cm].T.astype(BF16),
        "gb_col": gb.reshape(16, 1),
        "wcm": w[:, o_cm:o_cq].astype(BF16),
        "wmla": jnp.concatenate([w[:, o_cq:o_kr], kr128, krsw128], axis=1).astype(BF16),
        "wkr": w_kr.astype(BF16),
        "cmg": cm_norm_g[l].reshape(1, CM_WIDTH),
        "ws": cm_ws[l].astype(BF16),
        "cmb": jnp.repeat(cm_b[l].T, CM_WIDTH // CM_GROUPS, axis=1),
        "qg": mla_q_norm_g[l].reshape(1, MLA_RANK),
        "kvg": mla_kv_norm_g[l].reshape(1, MLA_RANK),
        "wuqa": _pad_heads([(uq_nope, MLA_NOPE), (uq_rope, MLA_ROPE), (None, pad_w)], MLA_HEADS).astype(BF16),
        "wuqb": _pad_heads([(None, MLA_NOPE), (uq_rope_sw, MLA_ROPE), (None, pad_w)], MLA_HEADS).astype(BF16),
        "wk": _pad_heads([(uk, MLA_NOPE), (None, HEAD_PAD - MLA_NOPE)], MLA_HEADS).astype(BF16),
        "wv": uv.astype(BF16),
        "wo": w_out[l].astype(BF16),
        "g2": norm2_g[l].reshape(1, D_MODEL),
        "wq": peer_w_q[l].astype(BF16),
        "sk": peer_subkeys[l].astype(BF16),
    }


def kernel(x_prompt, x_sample, c, cache_mla_ckv, cache_mla_krope, state_mlstm_C, state_mlstm_n, state_mlstm_m, c_ctx, norm1_g, ada_w, ada_b, w_in, mlstm_gate_b, cm_norm_g, cm_ws, cm_b, mla_q_norm_g, mla_w_uq, mla_kv_norm_g, mla_w_ukv, w_out, norm2_g, peer_w_q, peer_subkeys, peer_u, peer_v, final_g):
    x_ctx = x_prompt.reshape(T_CTX, D_MODEL)
    x_smp = x_sample.reshape(T_SMP, D_MODEL)
    cvecs = jnp.concatenate([c_ctx[None, :], c, jnp.zeros((N_MOD_ROWS - 1 - N_SMP_SEQ, D_MODEL), F32)], axis=0)
    mod_all = _ada_call(cvecs, ada_w, ada_b)
    cos_t, sin_t = _rope_tables()
    place = jnp.concatenate([jnp.zeros((MLA_ROPE, MLA_NOPE), F32), jnp.eye(MLA_ROPE, dtype=F32),
                             jnp.zeros((MLA_ROPE, HEAD_PAD - MLA_NOPE - MLA_ROPE), F32)], axis=1).astype(BF16)
    final_g2 = final_g.reshape(1, D_MODEL)
    u_bf, vt_bf = _table_prep_call(peer_u, peer_v)

    ckvs, krs, Cs, ns, ms = [], [], [], [], []
    for l in range(DEPTH):
        lw = _layer_weights(l, norm1_g, w_in, mlstm_gate_b, cm_norm_g, cm_ws, cm_b, mla_q_norm_g, mla_w_uq,
                            mla_kv_norm_g, mla_w_ukv, w_out, norm2_g, peer_w_q, peer_subkeys)
        mod = mod_all[l].reshape(N_MOD_ROWS, 1, 6 * D_MODEL)
        ml, gates_t, cm, ckvn, kr, q, k, v = _proj_call(x_ctx, x_smp, mod, lw, cos_t, sin_t)

        c_blk = jnp.einsum('bdhij,hg->bdhigj', state_mlstm_C[:, l], jnp.eye(ML_HEADS, dtype=F32))
        c0 = jnp.concatenate([jnp.zeros((N_CTX_SEQ, 2, ML_WIDTH, ML_WIDTH), F32),
                              c_blk.reshape(N_SMP_SEQ, 2, ML_WIDTH, ML_WIDTH)], axis=0)
        n0 = jnp.concatenate([jnp.zeros((N_CTX_SEQ, 2, 1, ML_WIDTH), F32),
                              state_mlstm_n[:, l].reshape(N_SMP_SEQ, 2, 1, ML_WIDTH)], axis=0)
        n0 = jnp.pad(n0, ((0, 0), (0, 0), (0, 7), (0, 0)))
        m0 = jnp.concatenate([jnp.zeros((N_CTX_SEQ, 2, ML_HEADS), F32), state_mlstm_m[:, l]], axis=0)
        m0 = jnp.pad(jnp.broadcast_to(m0[..., None], m0.shape + (128,)), ((0, 0), (0, 0), (0, 8 - ML_HEADS), (0, 0)))
        hf, hb, c_fin, n_fin, m_fin = _mlstm_call(ml, gates_t, c0, n0, m0)

        kc, vc = _cache_kv_call(cache_mla_ckv[:, l].reshape(N_SMP_SEQ * PAST_LEN, MLA_RANK),
                                cache_mla_krope[:, l].reshape(N_SMP_SEQ * PAST_LEN, MLA_ROPE),
                                lw["wk"], lw["wv"], place)
        att_ctx = _attn_ctx_call(q, k, v)
        att_smp = _attn_smp_call(q, k, v, kc, vc)

        x1, h2t, st = _out_call(x_ctx, x_smp, mod, ml, hf, hb, cm, att_ctx, att_smp, lw)
        e1, cut, e2, r2 = _topk_call(st)
        experts = functools.partial(_expert_call, h2t, u_bf, vt_bf, l, e1, cut, e2, r2, x1, mod, final_g2)
        if l < DEPTH - 1:
            x_ctx = x_smp = experts(False, 0, T_ALL)
        else:
            x_ctx = experts(True, 0, T_CTX)
            x_smp = experts(True, T_CTX, T_SMP)

        ckvs.append(ckvn[:T_CTX].reshape(N_CTX_SEQ, CTX_LEN, MLA_RANK))
        krs.append(kr[:T_CTX].reshape(N_CTX_SEQ, CTX_LEN, MLA_ROPE))
        Cs.append(jnp.stack([c_fin[:N_CTX_SEQ, :, hd * ML_DIM:(hd + 1) * ML_DIM, hd * ML_DIM:(hd + 1) * ML_DIM]
                             for hd in range(ML_HEADS)], axis=2))
        ns.append(n_fin[:N_CTX_SEQ, :, 0, :].reshape(N_CTX_SEQ, 2, ML_HEADS, ML_DIM))
        ms.append(m_fin[:N_CTX_SEQ, :, 0:ML_HEADS, 0])

    y_prompt = x_ctx.reshape(N_CTX_SEQ, CTX_LEN, D_MODEL)
    y_sample = x_smp.reshape(N_SMP_SEQ, SMP_LEN, D_MODEL)
    return (y_prompt, y_sample, jnp.stack(ckvs, axis=1), jnp.stack(krs, axis=1), jnp.stack(Cs, axis=1),
            jnp.stack(ns, axis=1), jnp.stack(ms, axis=1))
```

```python
import functools

import numpy as np
import jax
import jax.numpy as jnp
from jax import lax
from jax.experimental import pallas as pl
from jax.experimental.pallas import tpu as pltpu

F32 = jnp.float32
BF16 = jnp.bfloat16

D_MODEL = 1024
N_CTX_SEQ = 16
CTX_LEN = 256
N_SMP_SEQ = 4
SMP_LEN = 2048
PAST_LEN = 256
DEPTH = 2
GRID_W = 64
EPS = 1e-6
T_CTX = N_CTX_SEQ * CTX_LEN
T_SMP = N_SMP_SEQ * SMP_LEN
T_ALL = T_CTX + T_SMP
N_MOD_ROWS = 8

ML_HEADS = 4
ML_DIM = 64
ML_WIDTH = 256
CHUNK = 128
CM_GROUPS = 4
CM_WIDTH = 256
MLA_HEADS = 8
MLA_NOPE = 64
MLA_ROPE = 32
MLA_V = 64
MLA_RANK = 256
HEAD_PAD = 128
ROPE_AXIS = 16
ROPE_THETA = 10000.0
QK_SCALE_LOG2E = (MLA_NOPE + MLA_ROPE) ** -0.5 * 1.4426950408889634
PEER_HEADS = 8
PEER_NKEYS = 128
PEER_EXPERTS = PEER_NKEYS * PEER_NKEYS
PEER_TOPK = 16

TM = 256
TN_TOPK = 256
TM_EXP = 512
TE_EXP = 2048
EXP_SUB = 512
TE_PREP = 2048
ATTN_HEADS_IN_FLIGHT = 2
VMEM_LIMIT = 56 * 1024 * 1024

NEG_INF = float("-inf")


def _dot(a, b):
    return jnp.dot(a, b, preferred_element_type=F32)


def _dot_nt(a, b):
    return lax.dot_general(a, b, (((1,), (1,)), ((), ())), preferred_element_type=F32)


def _dot_tn(a, b):
    return lax.dot_general(a, b, (((0,), (0,)), ((), ())), preferred_element_type=F32)


def _split3(a):
    a1 = a.astype(BF16)
    r1 = a - a1.astype(F32)
    a2 = r1.astype(BF16)
    a3 = (r1 - a2.astype(F32)).astype(BF16)
    return a1, a2, a3


def _rms(x, g):
    return x * lax.rsqrt(jnp.mean(x * x, axis=-1, keepdims=True) + EPS) * g


def _sigmoid(x):
    return 1.0 / (1.0 + jnp.exp(-x))


def _log_sigmoid(x):
    return jnp.minimum(x, 0.0) - jnp.log(1.0 + jnp.exp(-jnp.abs(x)))


def _mod_row_of_block(i, rows_per_block):
    ctx_blocks = T_CTX // rows_per_block
    per_seq = SMP_LEN // rows_per_block
    return jnp.where(i < ctx_blocks, 0, 1 + (i - ctx_blocks) // per_seq)


def _ada_kernel(cv_ref, w_ref, b_ref, o_ref):
    cv = cv_ref[...]
    s = cv * _sigmoid(cv)
    w = w_ref[0]
    w1, w2, w3 = _split3(w)
    s1, s2, s3 = _split3(s)
    acc = _dot(s1, w1) + (_dot(s1, w2) + _dot(s2, w1)) + (_dot(s1, w3) + _dot(s2, w2) + _dot(s3, w1))
    o_ref[0] = acc + b_ref[0]


def _ada_call(cvecs, ada_w, ada_b):
    tn = 1024
    return pl.pallas_call(
        _ada_kernel,
        grid=(DEPTH, 6 * D_MODEL // tn),
        in_specs=[
            pl.BlockSpec((N_MOD_ROWS, D_MODEL), lambda l, j: (0, 0)),
            pl.BlockSpec((1, D_MODEL, tn), lambda l, j: (l, 0, j)),
            pl.BlockSpec((1, 1, tn), lambda l, j: (l, 0, j)),
        ],
        out_specs=pl.BlockSpec((1, N_MOD_ROWS, tn), lambda l, j: (l, 0, j)),
        out_shape=jax.ShapeDtypeStruct((DEPTH, N_MOD_ROWS, 6 * D_MODEL), F32),
        compiler_params=pltpu.CompilerParams(dimension_semantics=("parallel", "parallel")),
        name="ada_mod",
    )(cvecs, ada_w, ada_b.reshape(DEPTH, 1, 6 * D_MODEL))


def _x_specs(x_ctx, x_smp):
    ctx_blk = T_CTX // TM
    smp_off = x_smp.shape[0] // TM - T_SMP // TM
    return [pl.BlockSpec((TM, D_MODEL), lambda i: (jnp.minimum(i, ctx_blk - 1), 0)),
            pl.BlockSpec((TM, D_MODEL), lambda i: (jnp.maximum(i - ctx_blk, 0) + smp_off, 0))]


def _select_x(xc_ref, xs_ref):
    return jnp.where(pl.program_id(0) < T_CTX // TM, xc_ref[...], xs_ref[...])


def _proj_kernel(xc_ref, xs_ref, mod_ref, g1_ref, wml_ref, wgt_ref, gbc_ref, wcm_ref, wmla_ref, wkr_ref,
                 cmg_ref, ws_ref, cmb_ref, qg_ref, kvg_ref, wuqa_ref, wuqb_ref, wk_ref, wv_ref, cos_ref, sin_ref,
                 ml_ref, gatest_ref, cm_ref, ckvn_ref, kr_ref, q_ref, k_ref, v_ref):
    x = _select_x(xc_ref, xs_ref)
    mod = mod_ref[0]
    sh1 = mod[:, 0:D_MODEL]
    sc1 = mod[:, D_MODEL:2 * D_MODEL]
    h = _rms(x, g1_ref[...]) * (1.0 + sc1) + sh1
    hb = h.astype(BF16)

    def mlstm_inputs():
        ml_ref[...] = _dot(hb, wml_ref[...])
        yield None
        gatest_ref[...] = _dot_nt(wgt_ref[...], hb) + gbc_ref[...]
        yield None

    def spatial_gating():
        cm = _dot(hb, wcm_ref[...])
        yield None
        u = cm[:, 0:CM_WIDTH]
        vn = _rms(cm[:, CM_WIDTH:2 * CM_WIDTH], cmg_ref[...]).astype(BF16)
        lane_group = lax.broadcasted_iota(jnp.int32, (CHUNK, CM_WIDTH), 1) >> 6
        yield None
        for c in range(TM // CHUNK):
            rows = slice(c * CHUNK, (c + 1) * CHUNK)
            vc = vn[rows]
            mixed = jnp.zeros((CHUNK, CM_WIDTH), F32)
            for g in range(CM_GROUPS):
                mixed = jnp.where(lane_group == g, _dot(ws_ref[g], vc), mixed)
            cm_ref[rows, :] = u[rows] * (mixed + cmb_ref[...])
            yield None

    def latent_attention():
        mla = _dot(hb, wmla_ref[...])
        yield None
        qn = _rms(mla[:, 0:MLA_RANK], qg_ref[...]).astype(BF16)
        ckvn = _rms(mla[:, MLA_RANK:2 * MLA_RANK], kvg_ref[...])
        ckvn_ref[...] = ckvn
        kr_ref[...] = _dot(hb, wkr_ref[...])
        cos = cos_ref[...]
        sin = sin_ref[...]
        kr_rot = mla[:, 2 * MLA_RANK:2 * MLA_RANK + HEAD_PAD] * cos + mla[:, 2 * MLA_RANK + HEAD_PAD:] * sin
        yield None
        qa = _dot(qn, wuqa_ref[...])
        qb = _dot(qn, wuqb_ref[...])
        ckvb = ckvn.astype(BF16)
        kp = _dot(ckvb, wk_ref[...])
        yield None
        for hd in range(MLA_HEADS):
            cols = slice(hd * HEAD_PAD, (hd + 1) * HEAD_PAD)
            q_ref[hd] = ((qa[:, cols] * cos + qb[:, cols] * sin) * QK_SCALE_LOG2E).astype(BF16)
            k_ref[hd] = (kp[:, cols] + kr_rot).astype(BF16)
            if hd % 4 == 3:
                yield None
        v_ref[...] = _dot(ckvb, wv_ref[...]).astype(BF16)
        yield None

    _interleave([mlstm_inputs(), spatial_gating(), latent_attention()])


def _proj_call(x_ctx, x_smp, mod, lw, cos_t, sin_t):
    nblk = T_ALL // TM
    full = lambda shape: pl.BlockSpec(shape, lambda i: (0,) * len(shape))
    rows = lambda w: pl.BlockSpec((TM, w), lambda i: (i, 0))
    in_specs = _x_specs(x_ctx, x_smp) + [
        pl.BlockSpec((1, 1, 6 * D_MODEL), lambda i: (_mod_row_of_block(i, TM), 0, 0)),
        full((1, D_MODEL)),
        full((D_MODEL, 4 * ML_WIDTH)),
        full((16, D_MODEL)),
        full((16, 1)),
        full((D_MODEL, 2 * CM_WIDTH)),
        full((D_MODEL, 2 * MLA_RANK + 2 * HEAD_PAD)),
        full((D_MODEL, MLA_ROPE)),
        full((1, CM_WIDTH)),
        full((CM_GROUPS, CHUNK, CHUNK)),
        full((CHUNK, CM_WIDTH)),
        full((1, MLA_RANK)),
        full((1, MLA_RANK)),
        full((MLA_RANK, MLA_HEADS * HEAD_PAD)),
        full((MLA_RANK, MLA_HEADS * HEAD_PAD)),
        full((MLA_RANK, MLA_HEADS * HEAD_PAD)),
        full((MLA_RANK, MLA_HEADS * MLA_V)),
        rows(HEAD_PAD),
        rows(HEAD_PAD),
    ]
    out_specs = [
        rows(4 * ML_WIDTH),
        pl.BlockSpec((16, TM), lambda i: (0, i)),
        rows(CM_WIDTH),
        rows(MLA_RANK),
        rows(MLA_ROPE),
        pl.BlockSpec((MLA_HEADS, TM, HEAD_PAD), lambda i: (0, i, 0)),
        pl.BlockSpec((MLA_HEADS, TM, HEAD_PAD), lambda i: (0, i, 0)),
        rows(MLA_HEADS * MLA_V),
    ]
    out_shape = [
        jax.ShapeDtypeStruct((T_ALL, 4 * ML_WIDTH), F32),
        jax.ShapeDtypeStruct((16, T_ALL), F32),
        jax.ShapeDtypeStruct((T_ALL, CM_WIDTH), F32),
        jax.ShapeDtypeStruct((T_ALL, MLA_RANK), F32),
        jax.ShapeDtypeStruct((T_ALL, MLA_ROPE), F32),
        jax.ShapeDtypeStruct((MLA_HEADS, T_ALL, HEAD_PAD), BF16),
        jax.ShapeDtypeStruct((MLA_HEADS, T_ALL, HEAD_PAD), BF16),
        jax.ShapeDtypeStruct((T_ALL, MLA_HEADS * MLA_V), BF16),
    ]
    return pl.pallas_call(
        _proj_kernel,
        grid=(nblk,),
        in_specs=in_specs,
        out_specs=out_specs,
        out_shape=out_shape,
        compiler_params=pltpu.CompilerParams(dimension_semantics=("parallel",), vmem_limit_bytes=VMEM_LIMIT),
        name="proj_in",
    )(x_ctx, x_smp, mod, lw["g1"], lw["wml"], lw["wgt"], lw["gb_col"], lw["wcm"], lw["wmla"],
      lw["wkr"], lw["cmg"], lw["ws"], lw["cmb"], lw["qg"], lw["kvg"], lw["wuqa"], lw["wuqb"], lw["wk"], lw["wv"],
      cos_t, sin_t)


def _mlstm_schedule():
    pair, fwd_a, bwd_a, fwd_b, bwd_b, first = [], [], [], [], [], []
    base = 0
    for p in range((N_CTX_SEQ + N_SMP_SEQ) // 2):
        nc = (CTX_LEN if 2 * p < N_CTX_SEQ else SMP_LEN) // CHUNK
        for j in range(nc):
            pair.append(p)
            fwd_a.append(base + j)
            bwd_a.append(base + nc - 1 - j)
            fwd_b.append(base + nc + j)
            bwd_b.append(base + 2 * nc - 1 - j)
            first.append(1 if j == 0 else 0)
        base += 2 * nc
    as_i32 = lambda a: jnp.asarray(np.asarray(a, np.int32))
    return tuple(as_i32(a) for a in (pair, fwd_a, bwd_a, fwd_b, bwd_b, first))


def _scan_cummax(x, direction):
    L = CHUNK
    rows = x.shape[0]
    x = jnp.concatenate([x, x], axis=0)
    lane = lax.broadcasted_iota(jnp.int32, x.shape, 1)
    k = 1
    while k < L:
        if direction == 0:
            shifted = jnp.where(lane >= k, pltpu.roll(x, k, axis=1), NEG_INF)
        else:
            shifted = jnp.where(lane < L - k, pltpu.roll(x, L - k, axis=1), NEG_INF)
        x = jnp.maximum(x, shifted)
        k *= 2
    return x[0:rows]


def _rows_to_lane_cols(rows, eye, rep, pieces):
    x = jnp.concatenate([jnp.broadcast_to(rows[h:h + 1, :], (rep, CHUNK)) for h in range(ML_HEADS)], axis=0)
    out = None
    for _ in range(pieces):
        xb = x.astype(BF16)
        part = _dot_nt(eye, xb)
        out = part if out is None else out + part
        x = x - xb.astype(F32)
    return out


def _per_head_lanes(x512, lane64):
    lo = jnp.where(lane64, x512[:, 0:128], x512[:, 128:256])
    hi = jnp.where(lane64, x512[:, 256:384], x512[:, 384:512])
    return jnp.concatenate([lo, hi], axis=1)


def _mlstm_direction(ml, g_row, direction, slot, c_ref, n_ref, m_ref):
    L = CHUNK
    t_idx = lax.broadcasted_iota(jnp.int32, (L, L), 0)
    s_idx = lax.broadcasted_iota(jnp.int32, (L, L), 1)
    visible = (s_idx <= t_idx) if direction == 0 else (s_idx >= t_idx)
    tri = jnp.where(visible, 1.0, 0.0).astype(BF16)
    eye = jnp.where(s_idx == t_idx, 1.0, 0.0).astype(BF16)
    lane64 = lax.broadcasted_iota(jnp.int32, (1, 128), 1) < ML_DIM
    head_of_lane = lax.broadcasted_iota(jnp.int32, (1, ML_WIDTH), 1) >> 6
    same_head = ((lax.broadcasted_iota(jnp.int32, (ML_WIDTH, ML_WIDTH), 0) >> 6)
                 == (lax.broadcasted_iota(jnp.int32, (ML_WIDTH, ML_WIDTH), 1) >> 6))

    i0 = 8 * direction
    i_row = g_row[i0:i0 + ML_HEADS, :]
    lf_row = _log_sigmoid(g_row[i0 + ML_HEADS:i0 + 2 * ML_HEADS, :])
    r1, r2, r3 = _split3(lf_row)
    b_row = _dot_nt(r1, tri) + _dot_nt(r2, tri) + _dot_nt(r3, tri)
    yield None
    m_rep = m_ref[slot, direction, 0:ML_HEADS, :]
    a_row = i_row - b_row
    g_row_ = jnp.maximum(m_rep, _scan_cummax(a_row, direction))
    yield None
    b_end = jnp.sum(lf_row, axis=1, keepdims=True)
    log_w = b_end - b_row + i_row
    m_new = jnp.maximum(b_end + m_rep, jnp.max(log_w, axis=1, keepdims=True))
    w_k_row = jnp.exp(log_w - m_new)
    decay_rep = jnp.exp(b_end + m_rep - m_new)

    yield None
    g512 = _rows_to_lane_cols(g_row_, eye, 128, 2)
    g_full = _per_head_lanes(g512, lane64)
    yield None
    b_full = _rows_to_lane_cols(b_row, eye, ML_DIM, 2)
    wk_full = _rows_to_lane_cols(w_k_row, eye, ML_DIM, 1)
    m_full = _per_head_lanes(jnp.concatenate([m_rep[h:h + 1, :] for h in range(ML_HEADS)], axis=1), lane64)
    decay_full = _per_head_lanes(jnp.concatenate([decay_rep[h:h + 1, :] for h in range(ML_HEADS)], axis=1),
                                 lane64)

    yield None
    q = ml[:, 0:ML_WIDTH]
    k = ml[:, ML_WIDTH:2 * ML_WIDTH] * (ML_DIM ** -0.5)
    v = ml[:, 2 * ML_WIDTH:3 * ML_WIDTH]
    qb = q.astype(BF16)
    kb = k.astype(BF16)
    vb = v.astype(BF16)
    num = jnp.zeros((L, ML_WIDTH), F32)
    rowsum = jnp.zeros((L, ML_WIDTH), F32)
    for hd in range(ML_HEADS):
        w_intra = jnp.where(visible, jnp.exp(a_row[hd:hd + 1, :] - g512[:, hd * 128:(hd + 1) * 128]), 0.0)
        q_h = jnp.where(head_of_lane == hd, q, 0.0).astype(BF16)
        sw = _dot_nt(q_h, kb) * w_intra
        num = jnp.where(head_of_lane == hd, _dot(sw.astype(BF16), vb), num)
        rowsum = jnp.where(head_of_lane == hd, jnp.sum(sw, axis=-1, keepdims=True), rowsum)
        yield None

    C = c_ref[slot, direction]
    n_row = n_ref[slot, direction, 0:1, :]
    w_inter = jnp.exp(m_full - g_full)
    block_ones = jnp.where(same_head, 1.0, 0.0).astype(BF16)
    qn = _dot((q * n_row).astype(BF16), block_ones)
    num = num + w_inter * _dot(qb, C.astype(BF16))
    den = rowsum + w_inter * qn
    h_out = num / jnp.maximum(jnp.abs(den), jnp.exp(-(b_full + g_full)))

    kw = wk_full * k
    c_ref[slot, direction] = decay_full * C + jnp.where(same_head, _dot_tn(kw.astype(BF16), vb), 0.0)
    n_ref[slot, direction, 0:1, :] = decay_full * n_row + jnp.sum(kw, axis=0, keepdims=True)
    m_ref[slot, direction, 0:ML_HEADS, :] = m_new
    yield h_out


def _interleave(stage_generators):
    results = [None] * len(stage_generators)
    live = list(range(len(stage_generators)))
    while live:
        for idx in list(live):
            try:
                value = next(stage_generators[idx])
                if value is not None:
                    results[idx] = value
            except StopIteration:
                live.remove(idx)
    return results


def _mlstm_kernel(pair_ref, fa_ref, ba_ref, fb_ref, bb_ref, first_ref,
                  mlfa_ref, mlba_ref, mlfb_ref, mlbb_ref, grfa_ref, grba_ref, grfb_ref, grbb_ref,
                  c0_ref, n0_ref, m0_ref, hf_ref, hb_ref, c_ref, n_ref, m_ref):
    step = pl.program_id(0)

    @pl.when(first_ref[step] == 1)
    def _():
        c_ref[...] = c0_ref[...]
        n_ref[...] = n0_ref[...]
        m_ref[...] = m0_ref[...]

    hfa, hba, hfb, hbb = _interleave([
        _mlstm_direction(mlfa_ref[...], grfa_ref[...], 0, 0, c_ref, n_ref, m_ref),
        _mlstm_direction(mlba_ref[...], grba_ref[...], 1, 0, c_ref, n_ref, m_ref),
        _mlstm_direction(mlfb_ref[...], grfb_ref[...], 0, 1, c_ref, n_ref, m_ref),
        _mlstm_direction(mlbb_ref[...], grbb_ref[...], 1, 1, c_ref, n_ref, m_ref)])
    hf_ref[0, 0] = hfa
    hf_ref[0, 1] = hfb
    hb_ref[0, 0] = hba
    hb_ref[0, 1] = hbb


def _mlstm_call(ml, gates_t, c0, n0, m0):
    sched = _mlstm_schedule()
    nseq = N_CTX_SEQ + N_SMP_SEQ
    nsteps = int(sched[0].shape[0])
    chunk_rows = lambda which: pl.BlockSpec((CHUNK, 4 * ML_WIDTH), lambda i, *s: (s[which][i], 0))
    gate_cols = lambda which: pl.BlockSpec((16, CHUNK), lambda i, *s: (0, s[which][i]))
    c_spec = pl.BlockSpec((2, 2, ML_WIDTH, ML_WIDTH), lambda i, *s: (s[0][i], 0, 0, 0))
    n_spec = pl.BlockSpec((2, 2, 8, ML_WIDTH), lambda i, *s: (s[0][i], 0, 0, 0))
    m_spec = pl.BlockSpec((2, 2, 8, 128), lambda i, *s: (s[0][i], 0, 0, 0))
    h_spec = pl.BlockSpec((1, 2, CHUNK, ML_WIDTH), lambda i, *s: (i, 0, 0, 0))
    grid_spec = pltpu.PrefetchScalarGridSpec(
        num_scalar_prefetch=6,
        grid=(nsteps,),
        in_specs=[chunk_rows(1), chunk_rows(2), chunk_rows(3), chunk_rows(4),
                  gate_cols(1), gate_cols(2), gate_cols(3), gate_cols(4), c_spec, n_spec, m_spec],
        out_specs=[h_spec, h_spec, c_spec, n_spec, m_spec],
    )
    return pl.pallas_call(
        _mlstm_kernel,
        grid_spec=grid_spec,
        out_shape=[
            jax.ShapeDtypeStruct((nsteps, 2, CHUNK, ML_WIDTH), F32),
            jax.ShapeDtypeStruct((nsteps, 2, CHUNK, ML_WIDTH), F32),
            jax.ShapeDtypeStruct((nseq, 2, ML_WIDTH, ML_WIDTH), F32),
            jax.ShapeDtypeStruct((nseq, 2, 8, ML_WIDTH), F32),
            jax.ShapeDtypeStruct((nseq, 2, 8, 128), F32),
        ],
        compiler_params=pltpu.CompilerParams(dimension_semantics=("arbitrary",), vmem_limit_bytes=VMEM_LIMIT),
        name="mlstm",
    )(*sched, ml, ml, ml, ml, gates_t, gates_t, gates_t, gates_t, c0, n0, m0)


def _h_pair_specs():
    ctx_blk = T_CTX // TM
    blk_per_seq = SMP_LEN // TM
    ctx_steps = N_CTX_SEQ // 2 * (CTX_LEN // CHUNK)

    def place(i):
        i_s = jnp.maximum(i - ctx_blk, 0)
        seq_s = i_s // blk_per_seq
        jb = i_s % blk_per_seq
        is_ctx = i < ctx_blk
        slot = jnp.where(is_ctx, i % 2, seq_s % 2)
        base = (ctx_steps + (seq_s // 2) * (SMP_LEN // CHUNK)) // 2
        fwd = jnp.where(is_ctx, i // 2, base + jb)
        bwd = jnp.where(is_ctx, i // 2, base + blk_per_seq - 1 - jb)
        return fwd, bwd, slot

    block = (2, None, CHUNK, ML_WIDTH)
    return (pl.BlockSpec(block, lambda i: (place(i)[0], place(i)[2], 0, 0)),
            pl.BlockSpec(block, lambda i: (place(i)[1], place(i)[2], 0, 0)))


def _cache_kv_kernel(ckv_ref, kr_ref, wk_ref, wv_ref, place_ref, k_ref, v_ref):
    ckvb = ckv_ref[...].astype(BF16)
    kp = _dot(ckvb, wk_ref[...])
    kr128 = _dot(kr_ref[...].astype(BF16), place_ref[...])
    for hd in range(MLA_HEADS):
        k_ref[hd] = (kp[:, hd * HEAD_PAD:(hd + 1) * HEAD_PAD] + kr128).astype(BF16)
    v_ref[...] = _dot(ckvb, wv_ref[...]).astype(BF16)


def _cache_kv_call(ckv, kr, wk, wv, place):
    n = N_SMP_SEQ * PAST_LEN
    tb = PAST_LEN
    return pl.pallas_call(
        _cache_kv_kernel,
        grid=(n // tb,),
        in_specs=[
            pl.BlockSpec((tb, MLA_RANK), lambda i: (i, 0)),
            pl.BlockSpec((tb, MLA_ROPE), lambda i: (i, 0)),
            pl.BlockSpec((MLA_RANK, MLA_HEADS * HEAD_PAD), lambda i: (0, 0)),
            pl.BlockSpec((MLA_RANK, MLA_HEADS * MLA_V), lambda i: (0, 0)),
            pl.BlockSpec((MLA_ROPE, HEAD_PAD), lambda i: (0, 0)),
        ],
        out_specs=[
            pl.BlockSpec((MLA_HEADS, tb, HEAD_PAD), lambda i: (0, i, 0)),
            pl.BlockSpec((tb, MLA_HEADS * MLA_V), lambda i: (i, 0)),
        ],
        out_shape=[
            jax.ShapeDtypeStruct((MLA_HEADS, n, HEAD_PAD), BF16),
            jax.ShapeDtypeStruct((n, MLA_HEADS * MLA_V), BF16),
        ],
        compiler_params=pltpu.CompilerParams(dimension_semantics=("parallel",)),
        name="cache_kv",
    )(ckv, kr, wk, wv, place)


def _attn_kernel(*refs, has_cache):
    if has_cache:
        q_ref, kn_ref, vn_ref, kc_ref, vc_ref, o_ref = refs
    else:
        q_ref, kn_ref, vn_ref, o_ref = refs

    def head(hd):
        lanes = slice(hd * MLA_V, (hd + 1) * MLA_V)
        q = q_ref[hd]
        s_n = _dot_nt(q, kn_ref[hd])
        if has_cache:
            s_c = _dot_nt(q, kc_ref[hd])
        yield None
        m = jnp.max(s_n, axis=-1, keepdims=True)
        if has_cache:
            m = jnp.maximum(m, jnp.max(s_c, axis=-1, keepdims=True))
        yield None
        p_n = jnp.exp2(s_n - m)
        l = jnp.sum(p_n, axis=-1, keepdims=True)
        o = _dot(p_n.astype(BF16), vn_ref[:, lanes])
        if has_cache:
            p_c = jnp.exp2(s_c - m)
            l = l + jnp.sum(p_c, axis=-1, keepdims=True)
            o = o + _dot(p_c.astype(BF16), vc_ref[:, lanes])
        yield o / l

    outs = []
    for hd in range(0, MLA_HEADS, ATTN_HEADS_IN_FLIGHT):
        outs += _interleave([head(hd + i) for i in range(ATTN_HEADS_IN_FLIGHT)])
    o_ref[...] = jnp.concatenate(outs, axis=-1)


def _attn_ctx_call(q, k, v):
    tq = CTX_LEN
    return pl.pallas_call(
        functools.partial(_attn_kernel, has_cache=False),
        grid=(N_CTX_SEQ,),
        in_specs=[
            pl.BlockSpec((MLA_HEADS, tq, HEAD_PAD), lambda s: (0, s, 0)),
            pl.BlockSpec((MLA_HEADS, tq, HEAD_PAD), lambda s: (0, s, 0)),
            pl.BlockSpec((tq, MLA_HEADS * MLA_V), lambda s: (s, 0)),
        ],
        out_specs=pl.BlockSpec((tq, MLA_HEADS * MLA_V), lambda s: (s, 0)),
        out_shape=jax.ShapeDtypeStruct((T_CTX, MLA_HEADS * MLA_V), F32),
        compiler_params=pltpu.CompilerParams(dimension_semantics=("parallel",), vmem_limit_bytes=VMEM_LIMIT),
        name="attn_ctx",
    )(q, k, v)


def _attn_smp_call(q, k, v, kc, vc):
    tq = 256
    qb_per_seq = SMP_LEN // tq
    ctx_qb = T_CTX // tq
    ctx_kb = T_CTX // SMP_LEN
    return pl.pallas_call(
        functools.partial(_attn_kernel, has_cache=True),
        grid=(N_SMP_SEQ, qb_per_seq),
        in_specs=[
            pl.BlockSpec((MLA_HEADS, tq, HEAD_PAD), lambda b, i: (0, ctx_qb + b * qb_per_seq + i, 0)),
            pl.BlockSpec((MLA_HEADS, SMP_LEN, HEAD_PAD), lambda b, i: (0, ctx_kb + b, 0)),
            pl.BlockSpec((SMP_LEN, MLA_HEADS * MLA_V), lambda b, i: (ctx_kb + b, 0)),
            pl.BlockSpec((MLA_HEADS, PAST_LEN, HEAD_PAD), lambda b, i: (0, b, 0)),
            pl.BlockSpec((PAST_LEN, MLA_HEADS * MLA_V), lambda b, i: (b, 0)),
        ],
        out_specs=pl.BlockSpec((tq, MLA_HEADS * MLA_V), lambda b, i: (b * qb_per_seq + i, 0)),
        out_shape=jax.ShapeDtypeStruct((T_SMP, MLA_HEADS * MLA_V), F32),
        compiler_params=pltpu.CompilerParams(dimension_semantics=("parallel", "parallel"),
                                             vmem_limit_bytes=VMEM_LIMIT),
        name="attn_smp",
    )(q, k, v, kc, vc)


def _out_kernel(xc_ref, xs_ref, mod_ref, ml_ref, hf_ref, hb_ref, cm_ref, attc_ref, atts_ref, wo_ref, g2_ref,
                wq_ref, sk_ref, x1_ref, h2t_ref, st_ref):
    is_ctx = pl.program_id(0) < T_CTX // TM
    att = jnp.where(is_ctx, attc_ref[...], atts_ref[...])
    mod = mod_ref[0]
    g1 = mod[:, 2 * D_MODEL:3 * D_MODEL]
    sh2 = mod[:, 3 * D_MODEL:4 * D_MODEL]
    sc2 = mod[:, 4 * D_MODEL:5 * D_MODEL]
    h_sum = jnp.concatenate([hf_ref[0] + hb_ref[1], hf_ref[1] + hb_ref[0]], axis=0)
    mlo = _sigmoid(ml_ref[:, 3 * ML_WIDTH:4 * ML_WIDTH]) * h_sum
    mix = (_dot(mlo.astype(BF16), wo_ref[0:ML_WIDTH, :])
           + _dot(cm_ref[...].astype(BF16), wo_ref[ML_WIDTH:ML_WIDTH + CM_WIDTH, :])
           + _dot(att.astype(BF16), wo_ref[ML_WIDTH + CM_WIDTH:, :]))
    x1 = _select_x(xc_ref, xs_ref) + g1 * mix
    x1_ref[...] = x1
    h2f = _rms(x1, g2_ref[...]) * (1.0 + sc2) + sh2
    h2t_ref[...] = h2f.T.astype(BF16)
    qp = _dot(h2f.astype(BF16), wq_ref[...]).astype(BF16)
    for hh in range(2 * PEER_HEADS):
        st_ref[hh] = _dot_nt(sk_ref[hh % 2], qp[:, hh * 128:(hh + 1) * 128])


def _out_call(x_ctx, x_smp, mod, ml, hf, hb, cm, att_ctx, att_smp, lw):
    nblk = T_ALL // TM
    ctx_blk = T_CTX // TM
    rows = lambda w: pl.BlockSpec((TM, w), lambda i: (i, 0))
    full = lambda shape: pl.BlockSpec(shape, lambda i: (0,) * len(shape))
    att_w = MLA_HEADS * MLA_V
    attc_spec = pl.BlockSpec((TM, att_w), lambda i: (jnp.minimum(i, ctx_blk - 1), 0))
    atts_spec = pl.BlockSpec((TM, att_w), lambda i: (jnp.maximum(i - ctx_blk, 0), 0))
    return pl.pallas_call(
        _out_kernel,
        grid=(nblk,),
        in_specs=_x_specs(x_ctx, x_smp) + [
            pl.BlockSpec((1, 1, 6 * D_MODEL), lambda i: (_mod_row_of_block(i, TM), 0, 0)),
            rows(4 * ML_WIDTH), *_h_pair_specs(), rows(CM_WIDTH), attc_spec, atts_spec,
            full((D_MODEL, D_MODEL)), full((1, D_MODEL)), full((D_MODEL, 2 * PEER_HEADS * 128)),
            full((2, PEER_NKEYS, 128)),
        ],
        out_specs=[rows(D_MODEL), pl.BlockSpec((D_MODEL, TM), lambda i: (0, i)),
                   pl.BlockSpec((2 * PEER_HEADS, PEER_NKEYS, TM), lambda i: (0, 0, i))],
        out_shape=[
            jax.ShapeDtypeStruct((T_ALL, D_MODEL), F32),
            jax.ShapeDtypeStruct((D_MODEL, T_ALL), BF16),
            jax.ShapeDtypeStruct((2 * PEER_HEADS, PEER_NKEYS, T_ALL), F32),
        ],
        compiler_params=pltpu.CompilerParams(dimension_semantics=("parallel",), vmem_limit_bytes=VMEM_LIMIT),
        name="proj_out",
    )(x_ctx, x_smp, mod, ml, hf, hb, cm, att_ctx, att_smp, lw["wo"], lw["g2"], lw["wq"], lw["sk"])


def _sorting_network_pairs(n):
    pairs = []
    p = 1
    while p < n:
        k = p
        while k >= 1:
            for j in range(k % p, n - k, 2 * k):
                for i in range(min(k, n - j - k)):
                    if (i + j) // (2 * p) == (i + j + k) // (2 * p):
                        pairs.append((i + j, i + j + k))
            k //= 2
        p *= 2
    return pairs


_SORT16_PAIRS = _sorting_network_pairs(PEER_TOPK)


def _pop16(lists):
    lists = list(lists)
    vals = []
    for k in range(PEER_TOPK):
        m = jnp.max(lists[0], axis=0, keepdims=True)
        vals.append(m)
        if k == PEER_TOPK - 1:
            break
        hit = lists[0] == m
        for i in range(PEER_TOPK - 1 - k):
            lists[i] = jnp.where(hit, lists[i + 1], lists[i])
    return vals


def _top16_rows(s):
    tiles = [s[8 * j:8 * j + 8] for j in range(s.shape[0] // 8)]
    assert len(tiles) == PEER_TOPK
    for i, j in _SORT16_PAIRS:
        tiles[i], tiles[j] = jnp.maximum(tiles[i], tiles[j]), jnp.minimum(tiles[i], tiles[j])
    return _pop16(tiles)


def _count_steps(x, thresholds, below):
    r = jnp.zeros(x.shape, F32)
    for q, t in enumerate(thresholds):
        r = jnp.where((x < t) if below else (x >= t), float(q + 1), r)
    return r


def _pack_rows_bf16(x):
    return pltpu.bitcast(_pack_rows_words(x), BF16)


def _pack_rows_words(x):
    r, n = x.shape
    x4 = x.reshape(r // 16, 2, 8, n)
    lo = x4[:, 0].reshape(r // 2, n)
    hi = x4[:, 1].reshape(r // 2, n)
    return pltpu.pack_elementwise([lo, hi], packed_dtype=BF16)


def _dup_bf16_words(x):
    u = pltpu.bitcast(x.astype(BF16).astype(F32), jnp.int32)
    return u | lax.shift_right_logical(u, jnp.full(u.shape, 16, jnp.int32))


def _rows_to_array(rows, row_iota):
    arr = jnp.zeros(row_iota.shape, F32)
    for i, r in enumerate(rows):
        arr = jnp.where(row_iota == i, r, arr)
    return arr


def _topk_kernel(st_ref, e1_ref, cut_ref, e2_ref, r2_ref):
    n = st_ref.shape[-1]
    row16 = lax.broadcasted_iota(jnp.int32, (PEER_TOPK, n), 0)

    def head(hd, carry):
        s1 = st_ref[2 * hd]
        s2 = st_ref[2 * hd + 1]
        v1 = _top16_rows(s1)
        v2 = _top16_rows(s2)
        rank2 = _count_steps(s2, v2, below=True)
        v1arr = _rows_to_array(v1, row16)
        c = _pop16([v1arr + v2[q] for q in range(PEER_TOPK)])
        tau = c[PEER_TOPK - 1]
        z = jnp.zeros((1, n), F32)
        for ck in c:
            z = z + jnp.exp(ck - c[0])
        sigma = [jnp.min(jnp.where(v1arr + v2[q] >= tau, v1arr, jnp.inf), axis=0, keepdims=True)
                 for q in range(PEER_TOPK)]
        cut = _count_steps(s1, sigma, below=False)
        e1_ref[hd] = _dup_bf16_words(jnp.exp(s1 - v1[0]) / z)
        cut_ref[hd] = _dup_bf16_words(cut)
        e2_ref[hd] = _pack_rows_words(jnp.exp(s2 - v2[0]))
        r2_ref[hd] = _pack_rows_words(rank2)
        return carry

    lax.fori_loop(0, PEER_HEADS, head, 0)


def _topk_call(st):
    tn = TN_TOPK
    spec = pl.BlockSpec((PEER_HEADS, PEER_NKEYS, tn), lambda i: (0, 0, i))
    word_spec = pl.BlockSpec((PEER_HEADS, PEER_NKEYS // 2, tn), lambda i: (0, 0, i))
    stat = lambda dt: jax.ShapeDtypeStruct((PEER_HEADS, PEER_NKEYS, T_ALL), dt)
    words = jax.ShapeDtypeStruct((PEER_HEADS, PEER_NKEYS // 2, T_ALL), jnp.uint32)
    return pl.pallas_call(
        _topk_kernel,
        grid=(T_ALL // tn,),
        in_specs=[pl.BlockSpec((2 * PEER_HEADS, PEER_NKEYS, tn), lambda i: (0, 0, i))],
        out_specs=[spec, spec, word_spec, word_spec],
        out_shape=[stat(jnp.int32), stat(jnp.int32), words, words],
        compiler_params=pltpu.CompilerParams(dimension_semantics=("parallel",), vmem_limit_bytes=VMEM_LIMIT),
        name="peer_topk",
    )(st)


def _row_tile_bf16(row):
    blk = pltpu.bitcast(jnp.broadcast_to(row, (8, row.shape[1])), BF16)
    return jnp.concatenate([blk] * (PEER_NKEYS // 16), axis=0)


def _gelu_tanh_bf16(x):
    log2e = 1.4426950408889634
    w = x * (-1.5957691216057308 * log2e - (0.07135481627159584 * log2e) * (x * x))
    return x / (1.0 + jnp.exp2(w))


def _expert_kernel(h2t_ref, u_ref, vt_ref, e1_ref, cut_ref, e2_ref, r2_ref, x1_ref, mod_ref, fg_ref,
                   o_ref, acc_ref, a_ref, *, final_norm):
    j = pl.program_id(1)
    n_tiles = pl.num_programs(1)
    tm = h2t_ref.shape[1]
    a_per_tile = TE_EXP // PEER_NKEYS

    @pl.when(j == 0)
    def _():
        acc_ref[...] = jnp.zeros_like(acc_ref)

    def gate_tile(ai):
        a = j * a_per_tile + ai
        gate = jnp.zeros((PEER_NKEYS, tm), BF16)
        for hd in range(PEER_HEADS):
            cut_a = _row_tile_bf16(cut_ref[hd, pl.ds(a, 1), :])
            e1_a = _row_tile_bf16(e1_ref[hd, pl.ds(a, 1), :])
            r2 = pltpu.bitcast(r2_ref[hd], BF16)
            e2 = pltpu.bitcast(e2_ref[hd], BF16)
            gate = gate + jnp.where(r2 < cut_a, e2, jnp.zeros((), BF16)) * e1_a
        return gate

    a_per_sub = EXP_SUB // PEER_NKEYS
    n_sub = TE_EXP // EXP_SUB
    s = [None] * n_sub
    gates = [None] * n_sub
    s[0] = _dot(u_ref[0:EXP_SUB, :], h2t_ref[...])
    gates[0] = [gate_tile(ai) for ai in range(a_per_sub)]
    for sub in range(n_sub):
        if sub + 1 < n_sub:
            s[sub + 1] = _dot(u_ref[(sub + 1) * EXP_SUB:(sub + 2) * EXP_SUB, :], h2t_ref[...])
            gates[sub + 1] = [gate_tile((sub + 1) * a_per_sub + i) for i in range(a_per_sub)]
        g = _gelu_tanh_bf16(_pack_rows_bf16(s[sub]))
        for i in range(a_per_sub):
            r0 = sub * EXP_SUB + i * PEER_NKEYS
            a_ref[r0:r0 + PEER_NKEYS, :] = gates[sub][i] * g[i * PEER_NKEYS:(i + 1) * PEER_NKEYS, :]
    acc_ref[...] += _dot(vt_ref[...], a_ref[...])

    @pl.when(j == n_tiles - 1)
    def _():
        g2 = mod_ref[0][:, 5 * D_MODEL:6 * D_MODEL]
        y = x1_ref[...] + g2 * acc_ref[...].T
        if final_norm:
            y = _rms(y, fg_ref[...])
        o_ref[...] = y


def _expert_call(h2t, u_bf, vt_bf, layer, e1, cut, e2, r2, x1, mod, final_g, final_norm, tok_start, tok_count):
    tm, te = TM_EXP, TE_EXP
    b0 = tok_start // tm
    n_tiles = PEER_EXPERTS // te
    stat = pl.BlockSpec((PEER_HEADS, PEER_NKEYS, tm), lambda i, j: (0, 0, b0 + i))
    stat_words = pl.BlockSpec((PEER_HEADS, PEER_NKEYS // 2, tm), lambda i, j: (0, 0, b0 + i))
    return pl.pallas_call(
        functools.partial(_expert_kernel, final_norm=final_norm),
        grid=(tok_count // tm, n_tiles),
        in_specs=[
            pl.BlockSpec((D_MODEL, tm), lambda i, j: (0, b0 + i)),
            pl.BlockSpec((None, te, D_MODEL), lambda i, j: (layer, j, 0)),
            pl.BlockSpec((None, D_MODEL, te), lambda i, j: (layer, 0, j)),
            stat, stat, stat_words, stat_words,
            pl.BlockSpec((tm, D_MODEL), lambda i, j: (b0 + i, 0)),
            pl.BlockSpec((1, 1, 6 * D_MODEL), lambda i, j: (_mod_row_of_block(b0 + i, tm), 0, 0)),
            pl.BlockSpec((1, D_MODEL), lambda i, j: (0, 0)),
        ],
        out_specs=pl.BlockSpec((tm, D_MODEL), lambda i, j: (i, 0)),
        out_shape=jax.ShapeDtypeStruct((tok_count, D_MODEL), F32),
        scratch_shapes=[
            pltpu.VMEM((D_MODEL, tm), F32),
            pltpu.VMEM((te, tm), BF16),
        ],
        compiler_params=pltpu.CompilerParams(dimension_semantics=("parallel", "arbitrary"),
                                             vmem_limit_bytes=VMEM_LIMIT),
        name="peer_experts",
    )(h2t, u_bf, vt_bf, e1, cut, e2, r2, x1, mod, final_g)


def _u_prep_kernel(u_ref, o_ref):
    o_ref[...] = u_ref[...].astype(BF16)


def _v_prep_kernel(v_ref, o_ref):
    o_ref[...] = _pack_rows_bf16(v_ref[...]).T


def _table_prep_call(peer_u, peer_v):
    te = TE_PREP
    grid = (DEPTH, PEER_EXPERTS // te)
    rows = pl.BlockSpec((None, te, D_MODEL), lambda l, j: (l, j, 0))
    params = pltpu.CompilerParams(dimension_semantics=("parallel", "parallel"), vmem_limit_bytes=VMEM_LIMIT)
    u_bf = pl.pallas_call(
        _u_prep_kernel, grid=grid, in_specs=[rows], out_specs=rows,
        out_shape=jax.ShapeDtypeStruct((DEPTH, PEER_EXPERTS, D_MODEL), BF16),
        compiler_params=params, name="peer_u_prep")(peer_u)
    vt_bf = pl.pallas_call(
        _v_prep_kernel, grid=grid, in_specs=[rows],
        out_specs=pl.BlockSpec((None, D_MODEL, te), lambda l, j: (l, 0, j)),
        out_shape=jax.ShapeDtypeStruct((DEPTH, D_MODEL, PEER_EXPERTS), BF16),
        compiler_params=params, name="peer_v_prep")(peer_v)
    return u_bf, vt_bf


def _rope_swap_cols(w):
    return jnp.concatenate([-w[:, 8:16], w[:, 0:8], -w[:, 24:32], w[:, 16:24]], axis=1)


def _pad_heads(parts, n_heads):
    k = next(p[0].shape[0] for p in parts if p[0] is not None)
    cols = []
    for hd in range(n_heads):
        for arr, w in parts:
            cols.append(jnp.zeros((k, w), F32) if arr is None else arr[:, hd * w:(hd + 1) * w])
    return jnp.concatenate(cols, axis=1)


def _rope_tables():
    pos = np.arange(SMP_LEN)
    freqs = ROPE_THETA ** (-np.arange(0, ROPE_AXIS, 2, dtype=np.float32) / ROPE_AXIS)
    ang_r = (pos // GRID_W).astype(np.float32)[:, None] * freqs
    ang_c = (pos % GRID_W).astype(np.float32)[:, None] * freqs
    ang = jnp.asarray(np.concatenate([ang_r, ang_r, ang_c, ang_c], axis=1).astype(np.float32))
    cos32 = jnp.cos(ang)
    sin32 = jnp.sin(ang)
    ones = jnp.ones((SMP_LEN, MLA_NOPE), F32)
    cos_s = jnp.concatenate([ones, cos32, ones[:, :HEAD_PAD - MLA_NOPE - MLA_ROPE]], axis=1)
    sin_s = jnp.concatenate([0 * ones, sin32, 0 * ones[:, :HEAD_PAD - MLA_NOPE - MLA_ROPE]], axis=1)
    cos_t = jnp.concatenate([jnp.ones((T_CTX, HEAD_PAD), F32)] + [cos_s] * N_SMP_SEQ, axis=0)
    sin_t = jnp.concatenate([jnp.zeros((T_CTX, HEAD_PAD), F32)] + [sin_s] * N_SMP_SEQ, axis=0)
    return cos_t, sin_t


def _layer_weights(l, norm1_g, w_in, mlstm_gate_b, cm_norm_g, cm_ws, cm_b, mla_q_norm_g, mla_w_uq, mla_kv_norm_g,
                   mla_w_ukv, w_out, norm2_g, peer_w_q, peer_subkeys):
    w = w_in[l]
    o_g = 4 * ML_WIDTH
    o_cm = o_g + 16
    o_cq = o_cm + 2 * CM_WIDTH
    o_ckv = o_cq + MLA_RANK
    o_kr = o_ckv + MLA_RANK
    w_kr = w[:, o_kr:o_kr + MLA_ROPE]
    zeros_l = jnp.zeros((D_MODEL, MLA_NOPE), F32)
    zeros_r = jnp.zeros((D_MODEL, HEAD_PAD - MLA_NOPE - MLA_ROPE), F32)
    kr128 = jnp.concatenate([zeros_l, w_kr, zeros_r], axis=1)
    krsw128 = jnp.concatenate([zeros_l, _rope_swap_cols(w_kr), zeros_r], axis=1)
    uq = mla_w_uq[l].reshape(MLA_RANK, MLA_HEADS, MLA_NOPE + MLA_ROPE)
    uq_nope = uq[:, :, :MLA_NOPE].reshape(MLA_RANK, -1)
    uq_rope = uq[:, :, MLA_NOPE:].reshape(MLA_RANK, -1)
    uq_rope_sw = jnp.concatenate(
        [_rope_swap_cols(uq_rope[:, hd * MLA_ROPE:(hd + 1) * MLA_ROPE]) for hd in range(MLA_HEADS)], axis=1)
    pad_w = HEAD_PAD - MLA_NOPE - MLA_ROPE
    ukv = mla_w_ukv[l].reshape(MLA_RANK, MLA_HEADS, MLA_NOPE + MLA_V)
    uk = ukv[:, :, :MLA_NOPE].reshape(MLA_RANK, -1)
    uv = ukv[:, :, MLA_NOPE:].reshape(MLA_RANK, -1)
    gb = mlstm_gate_b[l]
    return {
        "g1": norm1_g[l].reshape(1, D_MODEL),
        "wml": w[:, 0:o_g].astype(BF16),
        "wgt": w[:, o_g:o_cm].T.astype(BF16),
        "gb_col": gb.reshape(16, 1),
        "wcm": w[:, o_cm:o_cq].astype(BF16),
        "wmla": jnp.concatenate([w[:, o_cq:o_kr], kr128, krsw128], axis=1).astype(BF16),
        "wkr": w_kr.astype(BF16),
        "cmg": cm_norm_g[l].reshape(1, CM_WIDTH),
        "ws": cm_ws[l].astype(BF16),
        "cmb": jnp.repeat(cm_b[l].T, CM_WIDTH // CM_GROUPS, axis=1),
        "qg": mla_q_norm_g[l].reshape(1, MLA_RANK),
        "kvg": mla_kv_norm_g[l].reshape(1, MLA_RANK),
        "wuqa": _pad_heads([(uq_nope, MLA_NOPE), (uq_rope, MLA_ROPE), (None, pad_w)], MLA_HEADS).astype(BF16),
        "wuqb": _pad_heads([(None, MLA_NOPE), (uq_rope_sw, MLA_ROPE), (None, pad_w)], MLA_HEADS).astype(BF16),
        "wk": _pad_heads([(uk, MLA_NOPE), (None, HEAD_PAD - MLA_NOPE)], MLA_HEADS).astype(BF16),
        "wv": uv.astype(BF16),
        "wo": w_out[l].astype(BF16),
        "g2": norm2_g[l].reshape(1, D_MODEL),
        "wq": peer_w_q[l].astype(BF16),
        "sk": peer_subkeys[l].astype(BF16),
    }


def kernel(x_prompt, x_sample, c, cache_mla_ckv, cache_mla_krope, state_mlstm_C, state_mlstm_n, state_mlstm_m, c_ctx, norm1_g, ada_w, ada_b, w_in, mlstm_gate_b, cm_norm_g, cm_ws, cm_b, mla_q_norm_g, mla_w_uq, mla_kv_norm_g, mla_w_ukv, w_out, norm2_g, peer_w_q, peer_subkeys, peer_u, peer_v, final_g):
    x_ctx = x_prompt.reshape(T_CTX, D_MODEL)
    x_smp = x_sample.reshape(T_SMP, D_MODEL)
    cvecs = jnp.concatenate([c_ctx[None, :], c, jnp.zeros((N_MOD_ROWS - 1 - N_SMP_SEQ, D_MODEL), F32)], axis=0)
    mod_all = _ada_call(cvecs, ada_w, ada_b)
    cos_t, sin_t = _rope_tables()
    place = jnp.concatenate([jnp.zeros((MLA_ROPE, MLA_NOPE), F32), jnp.eye(MLA_ROPE, dtype=F32),
                             jnp.zeros((MLA_ROPE, HEAD_PAD - MLA_NOPE - MLA_ROPE), F32)], axis=1).astype(BF16)
    final_g2 = final_g.reshape(1, D_MODEL)
    u_bf, vt_bf = _table_prep_call(peer_u, peer_v)

    ckvs, krs, Cs, ns, ms = [], [], [], [], []
    for l in range(DEPTH):
        lw = _layer_weights(l, norm1_g, w_in, mlstm_gate_b, cm_norm_g, cm_ws, cm_b, mla_q_norm_g, mla_w_uq,
                            mla_kv_norm_g, mla_w_ukv, w_out, norm2_g, peer_w_q, peer_subkeys)
        mod = mod_all[l].reshape(N_MOD_ROWS, 1, 6 * D_MODEL)
        ml, gates_t, cm, ckvn, kr, q, k, v = _proj_call(x_ctx, x_smp, mod, lw, cos_t, sin_t)

        c_blk = jnp.einsum('bdhij,hg->bdhigj', state_mlstm_C[:, l], jnp.eye(ML_HEADS, dtype=F32))
        c0 = jnp.concatenate([jnp.zeros((N_CTX_SEQ, 2, ML_WIDTH, ML_WIDTH), F32),
                              c_blk.reshape(N_SMP_SEQ, 2, ML_WIDTH, ML_WIDTH)], axis=0)
        n0 = jnp.concatenate([jnp.zeros((N_CTX_SEQ, 2, 1, ML_WIDTH), F32),
                              state_mlstm_n[:, l].reshape(N_SMP_SEQ, 2, 1, ML_WIDTH)], axis=0)
        n0 = jnp.pad(n0, ((0, 0), (0, 0), (0, 7), (0, 0)))
        m0 = jnp.concatenate([jnp.zeros((N_CTX_SEQ, 2, ML_HEADS), F32), state_mlstm_m[:, l]], axis=0)
        m0 = jnp.pad(jnp.broadcast_to(m0[..., None], m0.shape + (128,)), ((0, 0), (0, 0), (0, 8 - ML_HEADS), (0, 0)))
        hf, hb, c_fin, n_fin, m_fin = _mlstm_call(ml, gates_t, c0, n0, m0)

        kc, vc = _cache_kv_call(cache_mla_ckv[:, l].reshape(N_SMP_SEQ * PAST_LEN, MLA_RANK),
                                cache_mla_krope[:, l].reshape(N_SMP_SEQ * PAST_LEN, MLA_ROPE),
                                lw["wk"], lw["wv"], place)
        att_ctx = _attn_ctx_call(q, k, v)
        att_smp = _attn_smp_call(q, k, v, kc, vc)

        x1, h2t, st = _out_call(x_ctx, x_smp, mod, ml, hf, hb, cm, att_ctx, att_smp, lw)
        e1, cut, e2, r2 = _topk_call(st)
        experts = functools.partial(_expert_call, h2t, u_bf, vt_bf, l, e1, cut, e2, r2, x1, mod, final_g2)
        if l < DEPTH - 1:
            x_ctx = x_smp = experts(False, 0, T_ALL)
        else:
            x_ctx = experts(True, 0, T_CTX)
            x_smp = experts(True, T_CTX, T_SMP)

        ckvs.append(ckvn[:T_CTX].reshape(N_CTX_SEQ, CTX_LEN, MLA_RANK))
        krs.append(kr[:T_CTX].reshape(N_CTX_SEQ, CTX_LEN, MLA_ROPE))
        Cs.append(jnp.stack([c_fin[:N_CTX_SEQ, :, hd * ML_DIM:(hd + 1) * ML_DIM, hd * ML_DIM:(hd + 1) * ML_DIM]
                             for hd in range(ML_HEADS)], axis=2))
        ns.append(n_fin[:N_CTX_SEQ, :, 0, :].reshape(N_CTX_SEQ, 2, ML_HEADS, ML_DIM))
        ms.append(m_fin[:N_CTX_SEQ, :, 0:ML_HEADS, 0])

    y_prompt = x_ctx.reshape(N_CTX_SEQ, CTX_LEN, D_MODEL)
    y_sample = x_smp.reshape(N_SMP_SEQ, SMP_LEN, D_MODEL)
    return (y_prompt, y_sample, jnp.stack(ckvs, axis=1), jnp.stack(krs, axis=1), jnp.stack(Cs, axis=1),
            jnp.stack(ns, axis=1), jnp.stack(ms, axis=1))
```

```python
import functools

import numpy as np
import jax
import jax.numpy as jnp
from jax import lax
from jax.experimental import pallas as pl
from jax.experimental.pallas import tpu as pltpu

F32 = jnp.float32
BF16 = jnp.bfloat16

D_MODEL = 1024
N_CTX_SEQ = 16
CTX_LEN = 256
N_SMP_SEQ = 4
SMP_LEN = 2048
PAST_LEN = 256
DEPTH = 2
GRID_W = 64
EPS = 1e-6
T_CTX = N_CTX_SEQ * CTX_LEN
T_SMP = N_SMP_SEQ * SMP_LEN
T_ALL = T_CTX + T_SMP
N_MOD_ROWS = 8

ML_HEADS = 4
ML_DIM = 64
ML_WIDTH = 256
CHUNK = 128
CM_GROUPS = 4
CM_WIDTH = 256
MLA_HEADS = 8
MLA_NOPE = 64
MLA_ROPE = 32
MLA_V = 64
MLA_RANK = 256
HEAD_PAD = 128
ROPE_AXIS = 16
ROPE_THETA = 10000.0
QK_SCALE_LOG2E = (MLA_NOPE + MLA_ROPE) ** -0.5 * 1.4426950408889634
PEER_HEADS = 8
PEER_NKEYS = 128
PEER_EXPERTS = PEER_NKEYS * PEER_NKEYS
PEER_TOPK = 16

TM = 256
TN_TOPK = 256
TM_EXP = 512
TE_EXP = 2048
EXP_SUB = 512
TE_PREP = 2048
ATTN_HEADS_IN_FLIGHT = 2
VMEM_LIMIT = 56 * 1024 * 1024

NEG_INF = float("-inf")


def _dot(a, b):
    return jnp.dot(a, b, preferred_element_type=F32)


def _dot_nt(a, b):
    return lax.dot_general(a, b, (((1,), (1,)), ((), ())), preferred_element_type=F32)


def _dot_tn(a, b):
    return lax.dot_general(a, b, (((0,), (0,)), ((), ())), preferred_element_type=F32)


def _split3(a):
    a1 = a.astype(BF16)
    r1 = a - a1.astype(F32)
    a2 = r1.astype(BF16)
    a3 = (r1 - a2.astype(F32)).astype(BF16)
    return a1, a2, a3


def _rms(x, g):
    return x * lax.rsqrt(jnp.mean(x * x, axis=-1, keepdims=True) + EPS) * g


def _sigmoid(x):
    return 1.0 / (1.0 + jnp.exp(-x))


def _log_sigmoid(x):
    return jnp.minimum(x, 0.0) - jnp.log(1.0 + jnp.exp(-jnp.abs(x)))


def _mod_row_of_block(i, rows_per_block):
    ctx_blocks = T_CTX // rows_per_block
    per_seq = SMP_LEN // rows_per_block
    return jnp.where(i < ctx_blocks, 0, 1 + (i - ctx_blocks) // per_seq)


def _ada_kernel(cv_ref, w_ref, b_ref, o_ref):
    cv = cv_ref[...]
    s = cv * _sigmoid(cv)
    w = w_ref[0]
    w1, w2, w3 = _split3(w)
    s1, s2, s3 = _split3(s)
    acc = _dot(s1, w1) + (_dot(s1, w2) + _dot(s2, w1)) + (_dot(s1, w3) + _dot(s2, w2) + _dot(s3, w1))
    o_ref[0] = acc + b_ref[0]


def _ada_call(cvecs, ada_w, ada_b):
    tn = 1024
    return pl.pallas_call(
        _ada_kernel,
        grid=(DEPTH, 6 * D_MODEL // tn),
        in_specs=[
            pl.BlockSpec((N_MOD_ROWS, D_MODEL), lambda l, j: (0, 0)),
            pl.BlockSpec((1, D_MODEL, tn), lambda l, j: (l, 0, j)),
            pl.BlockSpec((1, 1, tn), lambda l, j: (l, 0, j)),
        ],
        out_specs=pl.BlockSpec((1, N_MOD_ROWS, tn), lambda l, j: (l, 0, j)),
        out_shape=jax.ShapeDtypeStruct((DEPTH, N_MOD_ROWS, 6 * D_MODEL), F32),
        compiler_params=pltpu.CompilerParams(dimension_semantics=("parallel", "parallel")),
        name="ada_mod",
    )(cvecs, ada_w, ada_b.reshape(DEPTH, 1, 6 * D_MODEL))


def _x_specs(x_ctx, x_smp):
    ctx_blk = T_CTX // TM
    smp_off = x_smp.shape[0] // TM - T_SMP // TM
    return [pl.BlockSpec((TM, D_MODEL), lambda i: (jnp.minimum(i, ctx_blk - 1), 0)),
            pl.BlockSpec((TM, D_MODEL), lambda i: (jnp.maximum(i - ctx_blk, 0) + smp_off, 0))]


def _select_x(xc_ref, xs_ref):
    return jnp.where(pl.program_id(0) < T_CTX // TM, xc_ref[...], xs_ref[...])


def _proj_kernel(xc_ref, xs_ref, mod_ref, g1_ref, wml_ref, wgt_ref, gbc_ref, wcm_ref, wmla_ref, wkr_ref,
                 cmg_ref, ws_ref, cmb_ref, qg_ref, kvg_ref, wuqa_ref, wuqb_ref, wk_ref, wv_ref, cos_ref, sin_ref,
                 ml_ref, gatest_ref, cm_ref, ckvn_ref, kr_ref, q_ref, k_ref, v_ref):
    x = _select_x(xc_ref, xs_ref)
    mod = mod_ref[0]
    sh1 = mod[:, 0:D_MODEL]
    sc1 = mod[:, D_MODEL:2 * D_MODEL]
    h = _rms(x, g1_ref[...]) * (1.0 + sc1) + sh1
    hb = h.astype(BF16)

    def mlstm_inputs():
        ml_ref[...] = _dot(hb, wml_ref[...])
        yield None
        gatest_ref[...] = _dot_nt(wgt_ref[...], hb) + gbc_ref[...]
        yield None

    def spatial_gating():
        cm = _dot(hb, wcm_ref[...])
        yield None
        u = cm[:, 0:CM_WIDTH]
        vn = _rms(cm[:, CM_WIDTH:2 * CM_WIDTH], cmg_ref[...]).astype(BF16)
        lane_group = lax.broadcasted_iota(jnp.int32, (CHUNK, CM_WIDTH), 1) >> 6
        yield None
        for c in range(TM // CHUNK):
            rows = slice(c * CHUNK, (c + 1) * CHUNK)
            vc = vn[rows]
            mixed = jnp.zeros((CHUNK, CM_WIDTH), F32)
            for g in range(CM_GROUPS):
                mixed = jnp.where(lane_group == g, _dot(ws_ref[g], vc), mixed)
            cm_ref[rows, :] = u[rows] * (mixed + cmb_ref[...])
            yield None

    def latent_attention():
        mla = _dot(hb, wmla_ref[...])
        yield None
        qn = _rms(mla[:, 0:MLA_RANK], qg_ref[...]).astype(BF16)
        ckvn = _rms(mla[:, MLA_RANK:2 * MLA_RANK], kvg_ref[...])
        ckvn_ref[...] = ckvn
        kr_ref[...] = _dot(hb, wkr_ref[...])
        cos = cos_ref[...]
        sin = sin_ref[...]
        kr_rot = mla[:, 2 * MLA_RANK:2 * MLA_RANK + HEAD_PAD] * cos + mla[:, 2 * MLA_RANK + HEAD_PAD:] * sin
        yield None
        qa = _dot(qn, wuqa_ref[...])
        qb = _dot(qn, wuqb_ref[...])
        ckvb = ckvn.astype(BF16)
        kp = _dot(ckvb, wk_ref[...])
        yield None
        for hd in range(MLA_HEADS):
            cols = slice(hd * HEAD_PAD, (hd + 1) * HEAD_PAD)
            q_ref[hd] = ((qa[:, cols] * cos + qb[:, cols] * sin) * QK_SCALE_LOG2E).astype(BF16)
            k_ref[hd] = (kp[:, cols] + kr_rot).astype(BF16)
            if hd % 4 == 3:
                yield None
        v_ref[...] = _dot(ckvb, wv_ref[...]).astype(BF16)
        yield None

    _interleave([mlstm_inputs(), spatial_gating(), latent_attention()])


def _proj_call(x_ctx, x_smp, mod, lw, cos_t, sin_t):
    nblk = T_ALL // TM
    full = lambda shape: pl.BlockSpec(shape, lambda i: (0,) * len(shape))
    rows = lambda w: pl.BlockSpec((TM, w), lambda i: (i, 0))
    in_specs = _x_specs(x_ctx, x_smp) + [
        pl.BlockSpec((1, 1, 6 * D_MODEL), lambda i: (_mod_row_of_block(i, TM), 0, 0)),
        full((1, D_MODEL)),
        full((D_MODEL, 4 * ML_WIDTH)),
        full((16, D_MODEL)),
        full((16, 1)),
        full((D_MODEL, 2 * CM_WIDTH)),
        full((D_MODEL, 2 * MLA_RANK + 2 * HEAD_PAD)),
        full((D_MODEL, MLA_ROPE)),
        full((1, CM_WIDTH)),
        full((CM_GROUPS, CHUNK, CHUNK)),
        full((CHUNK, CM_WIDTH)),
        full((1, MLA_RANK)),
        full((1, MLA_RANK)),
        full((MLA_RANK, MLA_HEADS * HEAD_PAD)),
        full((MLA_RANK, MLA_HEADS * HEAD_PAD)),
        full((MLA_RANK, MLA_HEADS * HEAD_PAD)),
        full((MLA_RANK, MLA_HEADS * MLA_V)),
        rows(HEAD_PAD),
        rows(HEAD_PAD),
    ]
    out_specs = [
        rows(4 * ML_WIDTH),
        pl.BlockSpec((16, TM), lambda i: (0, i)),
        rows(CM_WIDTH),
        rows(MLA_RANK),
        rows(MLA_ROPE),
        pl.BlockSpec((MLA_HEADS, TM, HEAD_PAD), lambda i: (0, i, 0)),
        pl.BlockSpec((MLA_HEADS, TM, HEAD_PAD), lambda i: (0, i, 0)),
        rows(MLA_HEADS * MLA_V),
    ]
    out_shape = [
        jax.ShapeDtypeStruct((T_ALL, 4 * ML_WIDTH), F32),
        jax.ShapeDtypeStruct((16, T_ALL), F32),
        jax.ShapeDtypeStruct((T_ALL, CM_WIDTH), F32),
        jax.ShapeDtypeStruct((T_ALL, MLA_RANK), F32),
        jax.ShapeDtypeStruct((T_ALL, MLA_ROPE), F32),
        jax.ShapeDtypeStruct((MLA_HEADS, T_ALL, HEAD_PAD), BF16),
        jax.ShapeDtypeStruct((MLA_HEADS, T_ALL, HEAD_PAD), BF16),
        jax.ShapeDtypeStruct((T_ALL, MLA_HEADS * MLA_V), BF16),
    ]
    return pl.pallas_call(
        _proj_kernel,
        grid=(nblk,),
        in_specs=in_specs,
        out_specs=out_specs,
        out_shape=out_shape,
        compiler_params=pltpu.CompilerParams(dimension_semantics=("parallel",), vmem_limit_bytes=VMEM_LIMIT),
        name="proj_in",
    )(x_ctx, x_smp, mod, lw["g1"], lw["wml"], lw["wgt"], lw["gb_col"], lw["wcm"], lw["wmla"],
      lw["wkr"], lw["cmg"], lw["ws"], lw["cmb"], lw["qg"], lw["kvg"], lw["wuqa"], lw["wuqb"], lw["wk"], lw["wv"],
      cos_t, sin_t)


def _mlstm_schedule():
    pair, fwd_a, bwd_a, fwd_b, bwd_b, first = [], [], [], [], [], []
    base = 0
    for p in range((N_CTX_SEQ + N_SMP_SEQ) // 2):
        nc = (CTX_LEN if 2 * p < N_CTX_SEQ else SMP_LEN) // CHUNK
        for j in range(nc):
            pair.append(p)
            fwd_a.append(base + j)
            bwd_a.append(base + nc - 1 - j)
            fwd_b.append(base + nc + j)
            bwd_b.append(base + 2 * nc - 1 - j)
            first.append(1 if j == 0 else 0)
        base += 2 * nc
    as_i32 = lambda a: jnp.asarray(np.asarray(a, np.int32))
    return tuple(as_i32(a) for a in (pair, fwd_a, bwd_a, fwd_b, bwd_b, first))


def _scan_cummax(x, direction):
    L = CHUNK
    rows = x.shape[0]
    x = jnp.concatenate([x, x], axis=0)
    lane = lax.broadcasted_iota(jnp.int32, x.shape, 1)
    k = 1
    while k < L:
        if direction == 0:
            shifted = jnp.where(lane >= k, pltpu.roll(x, k, axis=1), NEG_INF)
        else:
            shifted = jnp.where(lane < L - k, pltpu.roll(x, L - k, axis=1), NEG_INF)
        x = jnp.maximum(x, shifted)
        k *= 2
    return x[0:rows]


def _rows_to_lane_cols(rows, eye, rep, pieces):
    x = jnp.concatenate([jnp.broadcast_to(rows[h:h + 1, :], (rep, CHUNK)) for h in range(ML_HEADS)], axis=0)
    out = None
    for _ in range(pieces):
        xb = x.astype(BF16)
        part = _dot_nt(eye, xb)
        out = part if out is None else out + part
        x = x - xb.astype(F32)
    return out


def _per_head_lanes(x512, lane64):
    lo = jnp.where(lane64, x512[:, 0:128], x512[:, 128:256])
    hi = jnp.where(lane64, x512[:, 256:384], x512[:, 384:512])
    return jnp.concatenate([lo, hi], axis=1)


def _mlstm_direction(ml, g_row, direction, slot, c_ref, n_ref, m_ref):
    L = CHUNK
    t_idx = lax.broadcasted_iota(jnp.int32, (L, L), 0)
    s_idx = lax.broadcasted_iota(jnp.int32, (L, L), 1)
    visible = (s_idx <= t_idx) if direction == 0 else (s_idx >= t_idx)
    tri = jnp.where(visible, 1.0, 0.0).astype(BF16)
    eye = jnp.where(s_idx == t_idx, 1.0, 0.0).astype(BF16)
    lane64 = lax.broadcasted_iota(jnp.int32, (1, 128), 1) < ML_DIM
    head_of_lane = lax.broadcasted_iota(jnp.int32, (1, ML_WIDTH), 1) >> 6
    same_head = ((lax.broadcasted_iota(jnp.int32, (ML_WIDTH, ML_WIDTH), 0) >> 6)
                 == (lax.broadcasted_iota(jnp.int32, (ML_WIDTH, ML_WIDTH), 1) >> 6))

    i0 = 8 * direction
    i_row = g_row[i0:i0 + ML_HEADS, :]
    lf_row = _log_sigmoid(g_row[i0 + ML_HEADS:i0 + 2 * ML_HEADS, :])
    r1, r2, r3 = _split3(lf_row)
    b_row = _dot_nt(r1, tri) + _dot_nt(r2, tri) + _dot_nt(r3, tri)
    yield None
    m_rep = m_ref[slot, direction, 0:ML_HEADS, :]
    a_row = i_row - b_row
    g_row_ = jnp.maximum(m_rep, _scan_cummax(a_row, direction))
    yield None
    b_end = jnp.sum(lf_row, axis=1, keepdims=True)
    log_w = b_end - b_row + i_row
    m_new = jnp.maximum(b_end + m_rep, jnp.max(log_w, axis=1, keepdims=True))
    w_k_row = jnp.exp(log_w - m_new)
    decay_rep = jnp.exp(b_end + m_rep - m_new)

    yield None
    g512 = _rows_to_lane_cols(g_row_, eye, 128, 2)
    g_full = _per_head_lanes(g512, lane64)
    yield None
    b_full = _rows_to_lane_cols(b_row, eye, ML_DIM, 2)
    wk_full = _rows_to_lane_cols(w_k_row, eye, ML_DIM, 1)
    m_full = _per_head_lanes(jnp.concatenate([m_rep[h:h + 1, :] for h in range(ML_HEADS)], axis=1), lane64)
    decay_full = _per_head_lanes(jnp.concatenate([decay_rep[h:h + 1, :] for h in range(ML_HEADS)], axis=1),
                                 lane64)

    yield None
    q = ml[:, 0:ML_WIDTH]
    k = ml[:, ML_WIDTH:2 * ML_WIDTH] * (ML_DIM ** -0.5)
    v = ml[:, 2 * ML_WIDTH:3 * ML_WIDTH]
    qb = q.astype(BF16)
    kb = k.astype(BF16)
    vb = v.astype(BF16)
    num = jnp.zeros((L, ML_WIDTH), F32)
    rowsum = jnp.zeros((L, ML_WIDTH), F32)
    for hd in range(ML_HEADS):
        w_intra = jnp.where(visible, jnp.exp(a_row[hd:hd + 1, :] - g512[:, hd * 128:(hd + 1) * 128]), 0.0)
        q_h = jnp.where(head_of_lane == hd, q, 0.0).astype(BF16)
        sw = _dot_nt(q_h, kb) * w_intra
        num = jnp.where(head_of_lane == hd, _dot(sw.astype(BF16), vb), num)
        rowsum = jnp.where(head_of_lane == hd, jnp.sum(sw, axis=-1, keepdims=True), rowsum)
        yield None

    C = c_ref[slot, direction]
    n_row = n_ref[slot, direction, 0:1, :]
    w_inter = jnp.exp(m_full - g_full)
    block_ones = jnp.where(same_head, 1.0, 0.0).astype(BF16)
    qn = _dot((q * n_row).astype(BF16), block_ones)
    num = num + w_inter * _dot(qb, C.astype(BF16))
    den = rowsum + w_inter * qn
    h_out = num / jnp.maximum(jnp.abs(den), jnp.exp(-(b_full + g_full)))

    kw = wk_full * k
    c_ref[slot, direction] = decay_full * C + jnp.where(same_head, _dot_tn(kw.astype(BF16), vb), 0.0)
    n_ref[slot, direction, 0:1, :] = decay_full * n_row + jnp.sum(kw, axis=0, keepdims=True)
    m_ref[slot, direction, 0:ML_HEADS, :] = m_new
    yield h_out


def _interleave(stage_generators):
    results = [None] * len(stage_generators)
    live = list(range(len(stage_generators)))
    while live:
        for idx in list(live):
            try:
                value = next(stage_generators[idx])
                if value is not None:
                    results[idx] = value
            except StopIteration:
                live.remove(idx)
    return results


def _mlstm_kernel(pair_ref, fa_ref, ba_ref, fb_ref, bb_ref, first_ref,
                  mlfa_ref, mlba_ref, mlfb_ref, mlbb_ref, grfa_ref, grba_ref, grfb_ref, grbb_ref,
                  c0_ref, n0_ref, m0_ref, hf_ref, hb_ref, c_ref, n_ref, m_ref):
    step = pl.program_id(0)

    @pl.when(first_ref[step] == 1)
    def _():
        c_ref[...] = c0_ref[...]
        n_ref[...] = n0_ref[...]
        m_ref[...] = m0_ref[...]

    hfa, hba, hfb, hbb = _interleave([
        _mlstm_direction(mlfa_ref[...], grfa_ref[...], 0, 0, c_ref, n_ref, m_ref),
        _mlstm_direction(mlba_ref[...], grba_ref[...], 1, 0, c_ref, n_ref, m_ref),
        _mlstm_direction(mlfb_ref[...], grfb_ref[...], 0, 1, c_ref, n_ref, m_ref),
        _mlstm_direction(mlbb_ref[...], grbb_ref[...], 1, 1, c_ref, n_ref, m_ref)])
    hf_ref[0, 0] = hfa
    hf_ref[0, 1] = hfb
    hb_ref[0, 0] = hba
    hb_ref[0, 1] = hbb


def _mlstm_call(ml, gates_t, c0, n0, m0):
    sched = _mlstm_schedule()
    nseq = N_CTX_SEQ + N_SMP_SEQ
    nsteps = int(sched[0].shape[0])
    chunk_rows = lambda which: pl.BlockSpec((CHUNK, 4 * ML_WIDTH), lambda i, *s: (s[which][i], 0))
    gate_cols = lambda which: pl.BlockSpec((16, CHUNK), lambda i, *s: (0, s[which][i]))
    c_spec = pl.BlockSpec((2, 2, ML_WIDTH, ML_WIDTH), lambda i, *s: (s[0][i], 0, 0, 0))
    n_spec = pl.BlockSpec((2, 2, 8, ML_WIDTH), lambda i, *s: (s[0][i], 0, 0, 0))
    m_spec = pl.BlockSpec((2, 2, 8, 128), lambda i, *s: (s[0][i], 0, 0, 0))
    h_spec = pl.BlockSpec((1, 2, CHUNK, ML_WIDTH), lambda i, *s: (i, 0, 0, 0))
    grid_spec = pltpu.PrefetchScalarGridSpec(
        num_scalar_prefetch=6,
        grid=(nsteps,),
        in_specs=[chunk_rows(1), chunk_rows(2), chunk_rows(3), chunk_rows(4),
                  gate_cols(1), gate_cols(2), gate_cols(3), gate_cols(4), c_spec, n_spec, m_spec],
        out_specs=[h_spec, h_spec, c_spec, n_spec, m_spec],
    )
    return pl.pallas_call(
        _mlstm_kernel,
        grid_spec=grid_spec,
        out_shape=[
            jax.ShapeDtypeStruct((nsteps, 2, CHUNK, ML_WIDTH), F32),
            jax.ShapeDtypeStruct((nsteps, 2, CHUNK, ML_WIDTH), F32),
            jax.ShapeDtypeStruct((nseq, 2, ML_WIDTH, ML_WIDTH), F32),
            jax.ShapeDtypeStruct((nseq, 2, 8, ML_WIDTH), F32),
            jax.ShapeDtypeStruct((nseq, 2, 8, 128), F32),
        ],
        compiler_params=pltpu.CompilerParams(dimension_semantics=("arbitrary",), vmem_limit_bytes=VMEM_LIMIT),
        name="mlstm",
    )(*sched, ml, ml, ml, ml, gates_t, gates_t, gates_t, gates_t, c0, n0, m0)


def _h_pair_specs():
    ctx_blk = T_CTX // TM
    blk_per_seq = SMP_LEN // TM
    ctx_steps = N_CTX_SEQ // 2 * (CTX_LEN // CHUNK)

    def place(i):
        i_s = jnp.maximum(i - ctx_blk, 0)
        seq_s = i_s // blk_per_seq
        jb = i_s % blk_per_seq
        is_ctx = i < ctx_blk
        slot = jnp.where(is_ctx, i % 2, seq_s % 2)
        base = (ctx_steps + (seq_s // 2) * (SMP_LEN // CHUNK)) // 2
        fwd = jnp.where(is_ctx, i // 2, base + jb)
        bwd = jnp.where(is_ctx, i // 2, base + blk_per_seq - 1 - jb)
        return fwd, bwd, slot

    block = (2, None, CHUNK, ML_WIDTH)
    return (pl.BlockSpec(block, lambda i: (place(i)[0], place(i)[2], 0, 0)),
            pl.BlockSpec(block, lambda i: (place(i)[1], place(i)[2], 0, 0)))


def _cache_kv_kernel(ckv_ref, kr_ref, wk_ref, wv_ref, place_ref, k_ref, v_ref):
    ckvb = ckv_ref[...].astype(BF16)
    kp = _dot(ckvb, wk_ref[...])
    kr128 = _dot(kr_ref[...].astype(BF16), place_ref[...])
    for hd in range(MLA_HEADS):
        k_ref[hd] = (kp[:, hd * HEAD_PAD:(hd + 1) * HEAD_PAD] + kr128).astype(BF16)
    v_ref[...] = _dot(ckvb, wv_ref[...]).astype(BF16)


def _cache_kv_call(ckv, kr, wk, wv, place):
    n = N_SMP_SEQ * PAST_LEN
    tb = PAST_LEN
    return pl.pallas_call(
        _cache_kv_kernel,
        grid=(n // tb,),
        in_specs=[
            pl.BlockSpec((tb, MLA_RANK), lambda i: (i, 0)),
            pl.BlockSpec((tb, MLA_ROPE), lambda i: (i, 0)),
            pl.BlockSpec((MLA_RANK, MLA_HEADS * HEAD_PAD), lambda i: (0, 0)),
            pl.BlockSpec((MLA_RANK, MLA_HEADS * MLA_V), lambda i: (0, 0)),
            pl.BlockSpec((MLA_ROPE, HEAD_PAD), lambda i: (0, 0)),
        ],
        out_specs=[
            pl.BlockSpec((MLA_HEADS, tb, HEAD_PAD), lambda i: (0, i, 0)),
            pl.BlockSpec((tb, MLA_HEADS * MLA_V), lambda i: (i, 0)),
        ],
        out_shape=[
            jax.ShapeDtypeStruct((MLA_HEADS, n, HEAD_PAD), BF16),
            jax.ShapeDtypeStruct((n, MLA_HEADS * MLA_V), BF16),
        ],
        compiler_params=pltpu.CompilerParams(dimension_semantics=("parallel",)),
        name="cache_kv",
    )(ckv, kr, wk, wv, place)


def _attn_kernel(*refs, has_cache):
    if has_cache:
        q_ref, kn_ref, vn_ref, kc_ref, vc_ref, o_ref = refs
    else:
        q_ref, kn_ref, vn_ref, o_ref = refs

    def head(hd):
        lanes = slice(hd * MLA_V, (hd + 1) * MLA_V)
        q = q_ref[hd]
        s_n = _dot_nt(q, kn_ref[hd])
        if has_cache:
            s_c = _dot_nt(q, kc_ref[hd])
        yield None
        m = jnp.max(s_n, axis=-1, keepdims=True)
        if has_cache:
            m = jnp.maximum(m, jnp.max(s_c, axis=-1, keepdims=True))
        yield None
        p_n = jnp.exp2(s_n - m)
        l = jnp.sum(p_n, axis=-1, keepdims=True)
        o = _dot(p_n.astype(BF16), vn_ref[:, lanes])
        if has_cache:
            p_c = jnp.exp2(s_c - m)
            l = l + jnp.sum(p_c, axis=-1, keepdims=True)
            o = o + _dot(p_c.astype(BF16), vc_ref[:, lanes])
        yield o / l

    outs = []
    for hd in range(0, MLA_HEADS, ATTN_HEADS_IN_FLIGHT):
        outs += _interleave([head(hd + i) for i in range(ATTN_HEADS_IN_FLIGHT)])
    o_ref[...] = jnp.concatenate(outs, axis=-1)


def _attn_ctx_call(q, k, v):
    tq = CTX_LEN
    return pl.pallas_call(
        functools.partial(_attn_kernel, has_cache=False),
        grid=(N_CTX_SEQ,),
        in_specs=[
            pl.BlockSpec((MLA_HEADS, tq, HEAD_PAD), lambda s: (0, s, 0)),
            pl.BlockSpec((MLA_HEADS, tq, HEAD_PAD), lambda s: (0, s, 0)),
            pl.BlockSpec((tq, MLA_HEADS * MLA_V), lambda s: (s, 0)),
        ],
        out_specs=pl.BlockSpec((tq, MLA_HEADS * MLA_V), lambda s: (s, 0)),
        out_shape=jax.ShapeDtypeStruct((T_CTX, MLA_HEADS * MLA_V), F32),
        compiler_params=pltpu.CompilerParams(dimension_semantics=("parallel",), vmem_limit_bytes=VMEM_LIMIT),
        name="attn_ctx",
    )(q, k, v)


def _attn_smp_call(q, k, v, kc, vc):
    tq = 256
    qb_per_seq = SMP_LEN // tq
    ctx_qb = T_CTX // tq
    ctx_kb = T_CTX // SMP_LEN
    return pl.pallas_call(
        functools.partial(_attn_kernel, has_cache=True),
        grid=(N_SMP_SEQ, qb_per_seq),
        in_specs=[
            pl.BlockSpec((MLA_HEADS, tq, HEAD_PAD), lambda b, i: (0, ctx_qb + b * qb_per_seq + i, 0)),
            pl.BlockSpec((MLA_HEADS, SMP_LEN, HEAD_PAD), lambda b, i: (0, ctx_kb + b, 0)),
            pl.BlockSpec((SMP_LEN, MLA_HEADS * MLA_V), lambda b, i: (ctx_kb + b, 0)),
            pl.BlockSpec((MLA_HEADS, PAST_LEN, HEAD_PAD), lambda b, i: (0, b, 0)),
            pl.BlockSpec((PAST_LEN, MLA_HEADS * MLA_V), lambda b, i: (b, 0)),
        ],
        out_specs=pl.BlockSpec((tq, MLA_HEADS * MLA_V), lambda b, i: (b * qb_per_seq + i, 0)),
        out_shape=jax.ShapeDtypeStruct((T_SMP, MLA_HEADS * MLA_V), F32),
        compiler_params=pltpu.CompilerParams(dimension_semantics=("parallel", "parallel"),
                                             vmem_limit_bytes=VMEM_LIMIT),
        name="attn_smp",
    )(q, k, v, kc, vc)


def _out_kernel(xc_ref, xs_ref, mod_ref, ml_ref, hf_ref, hb_ref, cm_ref, attc_ref, atts_ref, wo_ref, g2_ref,
                wq_ref, sk_ref, x1_ref, h2t_ref, st_ref):
    is_ctx = pl.program_id(0) < T_CTX // TM
    att = jnp.where(is_ctx, attc_ref[...], atts_ref[...])
    mod = mod_ref[0]
    g1 = mod[:, 2 * D_MODEL:3 * D_MODEL]
    sh2 = mod[:, 3 * D_MODEL:4 * D_MODEL]
    sc2 = mod[:, 4 * D_MODEL:5 * D_MODEL]
    h_sum = jnp.concatenate([hf_ref[0] + hb_ref[1], hf_ref[1] + hb_ref[0]], axis=0)
    mlo = _sigmoid(ml_ref[:, 3 * ML_WIDTH:4 * ML_WIDTH]) * h_sum
    mix = (_dot(mlo.astype(BF16), wo_ref[0:ML_WIDTH, :])
           + _dot(cm_ref[...].astype(BF16), wo_ref[ML_WIDTH:ML_WIDTH + CM_WIDTH, :])
           + _dot(att.astype(BF16), wo_ref[ML_WIDTH + CM_WIDTH:, :]))
    x1 = _select_x(xc_ref, xs_ref) + g1 * mix
    x1_ref[...] = x1
    h2f = _rms(x1, g2_ref[...]) * (1.0 + sc2) + sh2
    h2t_ref[...] = h2f.T.astype(BF16)
    qp = _dot(h2f.astype(BF16), wq_ref[...]).astype(BF16)
    for hh in range(2 * PEER_HEADS):
        st_ref[hh] = _dot_nt(sk_ref[hh % 2], qp[:, hh * 128:(hh + 1) * 128])


def _out_call(x_ctx, x_smp, mod, ml, hf, hb, cm, att_ctx, att_smp, lw):
    nblk = T_ALL // TM
    ctx_blk = T_CTX // TM
    rows = lambda w: pl.BlockSpec((TM, w), lambda i: (i, 0))
    full = lambda shape: pl.BlockSpec(shape, lambda i: (0,) * len(shape))
    att_w = MLA_HEADS * MLA_V
    attc_spec = pl.BlockSpec((TM, att_w), lambda i: (jnp.minimum(i, ctx_blk - 1), 0))
    atts_spec = pl.BlockSpec((TM, att_w), lambda i: (jnp.maximum(i - ctx_blk, 0), 0))
    return pl.pallas_call(
        _out_kernel,
        grid=(nblk,),
        in_specs=_x_specs(x_ctx, x_smp) + [
            pl.BlockSpec((1, 1, 6 * D_MODEL), lambda i: (_mod_row_of_block(i, TM), 0, 0)),
            rows(4 * ML_WIDTH), *_h_pair_specs(), rows(CM_WIDTH), attc_spec, atts_spec,
            full((D_MODEL, D_MODEL)), full((1, D_MODEL)), full((D_MODEL, 2 * PEER_HEADS * 128)),
            full((2, PEER_NKEYS, 128)),
        ],
        out_specs=[rows(D_MODEL), pl.BlockSpec((D_MODEL, TM), lambda i: (0, i)),
                   pl.BlockSpec((2 * PEER_HEADS, PEER_NKEYS, TM), lambda i: (0, 0, i))],
        out_shape=[
            jax.ShapeDtypeStruct((T_ALL, D_MODEL), F32),
            jax.ShapeDtypeStruct((D_MODEL, T_ALL), BF16),
            jax.ShapeDtypeStruct((2 * PEER_HEADS, PEER_NKEYS, T_ALL), F32),
        ],
        compiler_params=pltpu.CompilerParams(dimension_semantics=("parallel",), vmem_limit_bytes=VMEM_LIMIT),
        name="proj_out",
    )(x_ctx, x_smp, mod, ml, hf, hb, cm, att_ctx, att_smp, lw["wo"], lw["g2"], lw["wq"], lw["sk"])


def _sorting_network_pairs(n):
    pairs = []
    p = 1
    while p < n:
        k = p
        while k >= 1:
            for j in range(k % p, n - k, 2 * k):
                for i in range(min(k, n - j - k)):
                    if (i + j) // (2 * p) == (i + j + k) // (2 * p):
                        pairs.append((i + j, i + j + k))
            k //= 2
        p *= 2
    return pairs


_SORT16_PAIRS = _sorting_network_pairs(PEER_TOPK)


def _pop16(lists, singles=None):
    lists = list(lists)
    vals = []
    for k in range(PEER_TOPK):
        heads = lists[0] if singles is None else jnp.maximum(lists[0], singles)
        m = jnp.max(heads, axis=0, keepdims=True)
        vals.append(m)
        if k == PEER_TOPK - 1:
            break
        hit = lists[0] == m
        for i in range(PEER_TOPK - 1 - k):
            lists[i] = jnp.where(hit, lists[i + 1], lists[i])
        if singles is not None:
            singles = jnp.where(singles == m, NEG_INF, singles)
    return vals


def _top16_rows(s):
    tiles = [s[8 * j:8 * j + 8] for j in range(s.shape[0] // 8)]
    assert len(tiles) == PEER_TOPK
    for i, j in _SORT16_PAIRS:
        tiles[i], tiles[j] = jnp.maximum(tiles[i], tiles[j]), jnp.minimum(tiles[i], tiles[j])
    return _pop16(tiles)


def _count_steps(x, thresholds, below):
    r = jnp.zeros(x.shape, F32)
    for q, t in enumerate(thresholds):
        r = jnp.where((x < t) if below else (x >= t), float(q + 1), r)
    return r


def _pack_rows_bf16(x):
    return pltpu.bitcast(_pack_rows_words(x), BF16)


def _pack_rows_words(x):
    r, n = x.shape
    x4 = x.reshape(r // 16, 2, 8, n)
    lo = x4[:, 0].reshape(r // 2, n)
    hi = x4[:, 1].reshape(r // 2, n)
    return pltpu.pack_elementwise([lo, hi], packed_dtype=BF16)


def _dup_bf16_words(x):
    return pltpu.pack_elementwise([x, x], packed_dtype=BF16)


def _rows_to_array(rows, row_iota):
    arr = jnp.zeros(row_iota.shape, F32)
    for i, r in enumerate(rows):
        arr = jnp.where(row_iota == i, r, arr)
    return arr


def _topk_kernel(st_ref, e1_ref, cut_ref, e2_ref, r2_ref):
    n = st_ref.shape[-1]
    row8 = lax.broadcasted_iota(jnp.int32, (8, n), 0)

    def head(hd, carry):
        s1 = st_ref[2 * hd]
        s2 = st_ref[2 * hd + 1]
        v1 = _top16_rows(s1)
        v2 = _top16_rows(s2)
        rank2 = _count_steps(s2, v2, below=True)
        v1_top = _rows_to_array(v1[0:8], row8)
        v1_bot = _rows_to_array(v1[8:PEER_TOPK], row8)
        c = _pop16([v1_top + v2[q] for q in range(PEER_TOPK)], singles=v1_bot + v2[0])
        tau = c[PEER_TOPK - 1]
        z = jnp.zeros((1, n), F32)
        for ck in c:
            z = z + jnp.exp(ck - c[0])

        def smallest_qualifying(v1_rows, q):
            return jnp.min(jnp.where(v1_rows + v2[q] >= tau, v1_rows, jnp.inf), axis=0, keepdims=True)

        sigma = [jnp.minimum(smallest_qualifying(v1_top, 0), smallest_qualifying(v1_bot, 0))]
        sigma += [smallest_qualifying(v1_top, q) for q in range(1, PEER_TOPK)]
        cut = _count_steps(s1, sigma, below=False)
        e1_ref[hd] = _dup_bf16_words(jnp.exp(s1 - v1[0]) / z)
        cut_ref[hd] = _dup_bf16_words(cut)
        e2_ref[hd] = _pack_rows_words(jnp.exp(s2 - v2[0]))
        r2_ref[hd] = _pack_rows_words(rank2)
        return carry

    lax.fori_loop(0, PEER_HEADS, head, 0)


def _topk_call(st):
    tn = TN_TOPK
    spec = pl.BlockSpec((PEER_HEADS, PEER_NKEYS, tn), lambda i: (0, 0, i))
    word_spec = pl.BlockSpec((PEER_HEADS, PEER_NKEYS // 2, tn), lambda i: (0, 0, i))
    stat = lambda dt: jax.ShapeDtypeStruct((PEER_HEADS, PEER_NKEYS, T_ALL), dt)
    words = jax.ShapeDtypeStruct((PEER_HEADS, PEER_NKEYS // 2, T_ALL), jnp.uint32)
    return pl.pallas_call(
        _topk_kernel,
        grid=(T_ALL // tn,),
        in_specs=[pl.BlockSpec((2 * PEER_HEADS, PEER_NKEYS, tn), lambda i: (0, 0, i))],
        out_specs=[spec, spec, word_spec, word_spec],
        out_shape=[stat(jnp.uint32), stat(jnp.uint32), words, words],
        compiler_params=pltpu.CompilerParams(dimension_semantics=("parallel",), vmem_limit_bytes=VMEM_LIMIT),
        name="peer_topk",
    )(st)


def _row_tile_bf16(row):
    blk = pltpu.bitcast(jnp.broadcast_to(row, (8, row.shape[1])), BF16)
    return jnp.concatenate([blk] * (PEER_NKEYS // 16), axis=0)


def _gelu_tanh_bf16(x):
    log2e = 1.4426950408889634
    w = x * (-1.5957691216057308 * log2e - (0.07135481627159584 * log2e) * (x * x))
    return x / (1.0 + jnp.exp2(w))


def _expert_kernel(h2t_ref, u_ref, vt_ref, e1_ref, cut_ref, e2_ref, r2_ref, x1_ref, mod_ref, fg_ref,
                   o_ref, acc_ref, a_ref, *, final_norm):
    j = pl.program_id(1)
    n_tiles = pl.num_programs(1)
    tm = h2t_ref.shape[1]
    a_per_tile = TE_EXP // PEER_NKEYS

    @pl.when(j == 0)
    def _():
        acc_ref[...] = jnp.zeros_like(acc_ref)

    def gate_tile(ai):
        a = j * a_per_tile + ai
        gate = jnp.zeros((PEER_NKEYS, tm), BF16)
        for hd in range(PEER_HEADS):
            cut_a = _row_tile_bf16(cut_ref[hd, pl.ds(a, 1), :])
            e1_a = _row_tile_bf16(e1_ref[hd, pl.ds(a, 1), :])
            r2 = pltpu.bitcast(r2_ref[hd], BF16)
            e2 = pltpu.bitcast(e2_ref[hd], BF16)
            gate = gate + jnp.where(r2 < cut_a, e2, jnp.zeros((), BF16)) * e1_a
        return gate

    a_per_sub = EXP_SUB // PEER_NKEYS
    n_sub = TE_EXP // EXP_SUB
    s = [None] * n_sub
    gates = [None] * n_sub
    s[0] = _dot(u_ref[0:EXP_SUB, :], h2t_ref[...])
    gates[0] = [gate_tile(ai) for ai in range(a_per_sub)]
    for sub in range(n_sub):
        if sub + 1 < n_sub:
            s[sub + 1] = _dot(u_ref[(sub + 1) * EXP_SUB:(sub + 2) * EXP_SUB, :], h2t_ref[...])
            gates[sub + 1] = [gate_tile((sub + 1) * a_per_sub + i) for i in range(a_per_sub)]
        g = _gelu_tanh_bf16(_pack_rows_bf16(s[sub]))
        for i in range(a_per_sub):
            r0 = sub * EXP_SUB + i * PEER_NKEYS
            a_ref[r0:r0 + PEER_NKEYS, :] = gates[sub][i] * g[i * PEER_NKEYS:(i + 1) * PEER_NKEYS, :]
    acc_ref[...] += _dot(vt_ref[...], a_ref[...])

    @pl.when(j == n_tiles - 1)
    def _():
        g2 = mod_ref[0][:, 5 * D_MODEL:6 * D_MODEL]
        y = x1_ref[...] + g2 * acc_ref[...].T
        if final_norm:
            y = _rms(y, fg_ref[...])
        o_ref[...] = y


def _expert_call(h2t, u_bf, vt_bf, layer, e1, cut, e2, r2, x1, mod, final_g, final_norm, tok_start, tok_count):
    tm, te = TM_EXP, TE_EXP
    b0 = tok_start // tm
    n_tiles = PEER_EXPERTS // te
    stat = pl.BlockSpec((PEER_HEADS, PEER_NKEYS, tm), lambda i, j: (0, 0, b0 + i))
    stat_words = pl.BlockSpec((PEER_HEADS, PEER_NKEYS // 2, tm), lambda i, j: (0, 0, b0 + i))
    return pl.pallas_call(
        functools.partial(_expert_kernel, final_norm=final_norm),
        grid=(tok_count // tm, n_tiles),
        in_specs=[
            pl.BlockSpec((D_MODEL, tm), lambda i, j: (0, b0 + i)),
            pl.BlockSpec((None, te, D_MODEL), lambda i, j: (layer, j, 0)),
            pl.BlockSpec((None, D_MODEL, te), lambda i, j: (layer, 0, j)),
            stat, stat, stat_words, stat_words,
            pl.BlockSpec((tm, D_MODEL), lambda i, j: (b0 + i, 0)),
            pl.BlockSpec((1, 1, 6 * D_MODEL), lambda i, j: (_mod_row_of_block(b0 + i, tm), 0, 0)),
            pl.BlockSpec((1, D_MODEL), lambda i, j: (0, 0)),
        ],
        out_specs=pl.BlockSpec((tm, D_MODEL), lambda i, j: (i, 0)),
        out_shape=jax.ShapeDtypeStruct((tok_count, D_MODEL), F32),
        scratch_shapes=[
            pltpu.VMEM((D_MODEL, tm), F32),
            pltpu.VMEM((te, tm), BF16),
        ],
        compiler_params=pltpu.CompilerParams(dimension_semantics=("parallel", "arbitrary"),
                                             vmem_limit_bytes=VMEM_LIMIT),
        name="peer_experts",
    )(h2t, u_bf, vt_bf, e1, cut, e2, r2, x1, mod, final_g)


def _u_prep_kernel(u_ref, o_ref):
    o_ref[...] = u_ref[...].astype(BF16)


def _v_prep_kernel(v_ref, o_ref):
    o_ref[...] = _pack_rows_bf16(v_ref[...]).T


def _table_prep_call(peer_u, peer_v):
    te = TE_PREP
    grid = (DEPTH, PEER_EXPERTS // te)
    rows = pl.BlockSpec((None, te, D_MODEL), lambda l, j: (l, j, 0))
    params = pltpu.CompilerParams(dimension_semantics=("parallel", "parallel"), vmem_limit_bytes=VMEM_LIMIT)
    u_bf = pl.pallas_call(
        _u_prep_kernel, grid=grid, in_specs=[rows], out_specs=rows,
        out_shape=jax.ShapeDtypeStruct((DEPTH, PEER_EXPERTS, D_MODEL), BF16),
        compiler_params=params, name="peer_u_prep")(peer_u)
    vt_bf = pl.pallas_call(
        _v_prep_kernel, grid=grid, in_specs=[rows],
        out_specs=pl.BlockSpec((None, D_MODEL, te), lambda l, j: (l, 0, j)),
        out_shape=jax.ShapeDtypeStruct((DEPTH, D_MODEL, PEER_EXPERTS), BF16),
        compiler_params=params, name="peer_v_prep")(peer_v)
    return u_bf, vt_bf


def _rope_swap_cols(w):
    return jnp.concatenate([-w[:, 8:16], w[:, 0:8], -w[:, 24:32], w[:, 16:24]], axis=1)


def _pad_heads(parts, n_heads):
    k = next(p[0].shape[0] for p in parts if p[0] is not None)
    cols = []
    for hd in range(n_heads):
        for arr, w in parts:
            cols.append(jnp.zeros((k, w), F32) if arr is None else arr[:, hd * w:(hd + 1) * w])
    return jnp.concatenate(cols, axis=1)


def _rope_tables():
    pos = np.arange(SMP_LEN)
    freqs = ROPE_THETA ** (-np.arange(0, ROPE_AXIS, 2, dtype=np.float32) / ROPE_AXIS)
    ang_r = (pos // GRID_W).astype(np.float32)[:, None] * freqs
    ang_c = (pos % GRID_W).astype(np.float32)[:, None] * freqs
    ang = jnp.asarray(np.concatenate([ang_r, ang_r, ang_c, ang_c], axis=1).astype(np.float32))
    cos32 = jnp.cos(ang)
    sin32 = jnp.sin(ang)
    ones = jnp.ones((SMP_LEN, MLA_NOPE), F32)
    cos_s = jnp.concatenate([ones, cos32, ones[:, :HEAD_PAD - MLA_NOPE - MLA_ROPE]], axis=1)
    sin_s = jnp.concatenate([0 * ones, sin32, 0 * ones[:, :HEAD_PAD - MLA_NOPE - MLA_ROPE]], axis=1)
    cos_t = jnp.concatenate([jnp.ones((T_CTX, HEAD_PAD), F32)] + [cos_s] * N_SMP_SEQ, axis=0)
    sin_t = jnp.concatenate([jnp.zeros((T_CTX, HEAD_PAD), F32)] + [sin_s] * N_SMP_SEQ, axis=0)
    return cos_t, sin_t


def _layer_weights(l, norm1_g, w_in, mlstm_gate_b, cm_norm_g, cm_ws, cm_b, mla_q_norm_g, mla_w_uq, mla_kv_norm_g,
                   mla_w_ukv, w_out, norm2_g, peer_w_q, peer_subkeys):
    w = w_in[l]
    o_g = 4 * ML_WIDTH
    o_cm = o_g + 16
    o_cq = o_cm + 2 * CM_WIDTH
    o_ckv = o_cq + MLA_RANK
    o_kr = o_ckv + MLA_RANK
    w_kr = w[:, o_kr:o_kr + MLA_ROPE]
    zeros_l = jnp.zeros((D_MODEL, MLA_NOPE), F32)
    zeros_r = jnp.zeros((D_MODEL, HEAD_PAD - MLA_NOPE - MLA_ROPE), F32)
    kr128 = jnp.concatenate([zeros_l, w_kr, zeros_r], axis=1)
    krsw128 = jnp.concatenate([zeros_l, _rope_swap_cols(w_kr), zeros_r], axis=1)
    uq = mla_w_uq[l].reshape(MLA_RANK, MLA_HEADS, MLA_NOPE + MLA_ROPE)
    uq_nope = uq[:, :, :MLA_NOPE].reshape(MLA_RANK, -1)
    uq_rope = uq[:, :, MLA_NOPE:].reshape(MLA_RANK, -1)
    uq_rope_sw = jnp.concatenate(
        [_rope_swap_cols(uq_rope[:, hd * MLA_ROPE:(hd + 1) * MLA_ROPE]) for hd in range(MLA_HEADS)], axis=1)
    pad_w = HEAD_PAD - MLA_NOPE - MLA_ROPE
    ukv = mla_w_ukv[l].reshape(MLA_RANK, MLA_HEADS, MLA_NOPE + MLA_V)
    uk = ukv[:, :, :MLA_NOPE].reshape(MLA_RANK, -1)
    uv = ukv[:, :, MLA_NOPE:].reshape(MLA_RANK, -1)
    gb = mlstm_gate_b[l]
    return {
        "g1": norm1_g[l].reshape(1, D_MODEL),
        "wml": w[:, 0:o_g].astype(BF16),
        "wgt": w[:, o_g:o_cm].T.astype(BF16),
        "gb_col": gb.reshape(16, 1),
        "wcm": w[:, o_cm:o_cq].astype(BF16),
        "wmla": jnp.concatenate([w[:, o_cq:o_kr], kr128, krsw128], axis=1).astype(BF16),
        "wkr": w_kr.astype(BF16),
        "cmg": cm_norm_g[l].reshape(1, CM_WIDTH),
        "ws": cm_ws[l].astype(BF16),
        "cmb": jnp.repeat(cm_b[l].T, CM_WIDTH // CM_GROUPS, axis=1),
        "qg": mla_q_norm_g[l].reshape(1, MLA_RANK),
        "kvg": mla_kv_norm_g[l].reshape(1, MLA_RANK),
        "wuqa": _pad_heads([(uq_nope, MLA_NOPE), (uq_rope, MLA_ROPE), (None, pad_w)], MLA_HEADS).astype(BF16),
        "wuqb": _pad_heads([(None, MLA_NOPE), (uq_rope_sw, MLA_ROPE), (None, pad_w)], MLA_HEADS).astype(BF16),
        "wk": _pad_heads([(uk, MLA_NOPE), (None, HEAD_PAD - MLA_NOPE)], MLA_HEADS).astype(BF16),
        "wv": uv.astype(BF16),
        "wo": w_out[l].astype(BF16),
        "g2": norm2_g[l].reshape(1, D_MODEL),
        "wq": peer_w_q[l].astype(BF16),
        "sk": peer_subkeys[l].astype(BF16),
    }


def kernel(x_prompt, x_sample, c, cache_mla_ckv, cache_mla_krope, state_mlstm_C, state_mlstm_n, state_mlstm_m, c_ctx, norm1_g, ada_w, ada_b, w_in, mlstm_gate_b, cm_norm_g, cm_ws, cm_b, mla_q_norm_g, mla_w_uq, mla_kv_norm_g, mla_w_ukv, w_out, norm2_g, peer_w_q, peer_subkeys, peer_u, peer_v, final_g):
    x_ctx = x_prompt.reshape(T_CTX, D_MODEL)
    x_smp = x_sample.reshape(T_SMP, D_MODEL)
    cvecs = jnp.concatenate([c_ctx[None, :], c, jnp.zeros((N_MOD_ROWS - 1 - N_SMP_SEQ, D_MODEL), F32)], axis=0)
    mod_all = _ada_call(cvecs, ada_w, ada_b)
    cos_t, sin_t = _rope_tables()
    place = jnp.concatenate([jnp.zeros((MLA_ROPE, MLA_NOPE), F32), jnp.eye(MLA_ROPE, dtype=F32),
                             jnp.zeros((MLA_ROPE, HEAD_PAD - MLA_NOPE - MLA_ROPE), F32)], axis=1).astype(BF16)
    final_g2 = final_g.reshape(1, D_MODEL)
    u_bf, vt_bf = _table_prep_call(peer_u, peer_v)

    ckvs, krs, Cs, ns, ms = [], [], [], [], []
    for l in range(DEPTH):
        lw = _layer_weights(l, norm1_g, w_in, mlstm_gate_b, cm_norm_g, cm_ws, cm_b, mla_q_norm_g, mla_w_uq,
                            mla_kv_norm_g, mla_w_ukv, w_out, norm2_g, peer_w_q, peer_subkeys)
        mod = mod_all[l].reshape(N_MOD_ROWS, 1, 6 * D_MODEL)
        ml, gates_t, cm, ckvn, kr, q, k, v = _proj_call(x_ctx, x_smp, mod, lw, cos_t, sin_t)

        c_blk = jnp.einsum('bdhij,hg->bdhigj', state_mlstm_C[:, l], jnp.eye(ML_HEADS, dtype=F32))
        c0 = jnp.concatenate([jnp.zeros((N_CTX_SEQ, 2, ML_WIDTH, ML_WIDTH), F32),
                              c_blk.reshape(N_SMP_SEQ, 2, ML_WIDTH, ML_WIDTH)], axis=0)
        n0 = jnp.concatenate([jnp.zeros((N_CTX_SEQ, 2, 1, ML_WIDTH), F32),
                              state_mlstm_n[:, l].reshape(N_SMP_SEQ, 2, 1, ML_WIDTH)], axis=0)
        n0 = jnp.pad(n0, ((0, 0), (0, 0), (0, 7), (0, 0)))
        m0 = jnp.concatenate([jnp.zeros((N_CTX_SEQ, 2, ML_HEADS), F32), state_mlstm_m[:, l]], axis=0)
        m0 = jnp.pad(jnp.broadcast_to(m0[..., None], m0.shape + (128,)), ((0, 0), (0, 0), (0, 8 - ML_HEADS), (0, 0)))
        hf, hb, c_fin, n_fin, m_fin = _mlstm_call(ml, gates_t, c0, n0, m0)

        kc, vc = _cache_kv_call(cache_mla_ckv[:, l].reshape(N_SMP_SEQ * PAST_LEN, MLA_RANK),
                                cache_mla_krope[:, l].reshape(N_SMP_SEQ * PAST_LEN, MLA_ROPE),
                                lw["wk"], lw["wv"], place)
        att_ctx = _attn_ctx_call(q, k, v)
        att_smp = _attn_smp_call(q, k, v, kc, vc)

        x1, h2t, st = _out_call(x_ctx, x_smp, mod, ml, hf, hb, cm, att_ctx, att_smp, lw)
        e1, cut, e2, r2 = _topk_call(st)
        experts = functools.partial(_expert_call, h2t, u_bf, vt_bf, l, e1, cut, e2, r2, x1, mod, final_g2)
        if l < DEPTH - 1:
            x_ctx = x_smp = experts(False, 0, T_ALL)
        else:
            x_ctx = experts(True, 0, T_CTX)
            x_smp = experts(True, T_CTX, T_SMP)

        ckvs.append(ckvn[:T_CTX].reshape(N_CTX_SEQ, CTX_LEN, MLA_RANK))
        krs.append(kr[:T_CTX].reshape(N_CTX_SEQ, CTX_LEN, MLA_ROPE))
        Cs.append(jnp.stack([c_fin[:N_CTX_SEQ, :, hd * ML_DIM:(hd + 1) * ML_DIM, hd * ML_DIM:(hd + 1) * ML_DIM]
                             for hd in range(ML_HEADS)], axis=2))
        ns.append(n_fin[:N_CTX_SEQ, :, 0, :].reshape(N_CTX_SEQ, 2, ML_HEADS, ML_DIM))
        ms.append(m_fin[:N_CTX_SEQ, :, 0:ML_HEADS, 0])

    y_prompt = x_ctx.reshape(N_CTX_SEQ, CTX_LEN, D_MODEL)
    y_sample = x_smp.reshape(N_SMP_SEQ, SMP_LEN, D_MODEL)
    return (y_prompt, y_sample, jnp.stack(ckvs, axis=1), jnp.stack(krs, axis=1), jnp.stack(Cs, axis=1),
            jnp.stack(ns, axis=1), jnp.stack(ms, axis=1))
```

```python
import functools

import numpy as np
import jax
import jax.numpy as jnp
from jax import lax
from jax.experimental import pallas as pl
from jax.experimental.pallas import tpu as pltpu

F32 = jnp.float32
BF16 = jnp.bfloat16

D_MODEL = 1024
N_CTX_SEQ = 16
CTX_LEN = 256
N_SMP_SEQ = 4
SMP_LEN = 2048
PAST_LEN = 256
DEPTH = 2
GRID_W = 64
EPS = 1e-6
T_CTX = N_CTX_SEQ * CTX_LEN
T_SMP = N_SMP_SEQ * SMP_LEN
T_ALL = T_CTX + T_SMP
N_MOD_ROWS = 8

ML_HEADS = 4
ML_DIM = 64
ML_WIDTH = 256
CHUNK = 128
CM_GROUPS = 4
CM_WIDTH = 256
MLA_HEADS = 8
MLA_NOPE = 64
MLA_ROPE = 32
MLA_V = 64
MLA_RANK = 256
HEAD_PAD = 128
ROPE_AXIS = 16
ROPE_THETA = 10000.0
QK_SCALE_LOG2E = (MLA_NOPE + MLA_ROPE) ** -0.5 * 1.4426950408889634
PEER_HEADS = 8
PEER_NKEYS = 128
PEER_EXPERTS = PEER_NKEYS * PEER_NKEYS
PEER_TOPK = 16

TM = 256
TN_TOPK = 256
TM_EXP = 512
TE_EXP = 2048
EXP_SUB = 512
TE_PREP = 2048
ATTN_HEADS_IN_FLIGHT = 2
VMEM_LIMIT = 56 * 1024 * 1024

NEG_INF = float("-inf")


def _dot(a, b):
    return jnp.dot(a, b, preferred_element_type=F32)


def _dot_nt(a, b):
    return lax.dot_general(a, b, (((1,), (1,)), ((), ())), preferred_element_type=F32)


def _dot_tn(a, b):
    return lax.dot_general(a, b, (((0,), (0,)), ((), ())), preferred_element_type=F32)


def _split3(a):
    a1 = a.astype(BF16)
    r1 = a - a1.astype(F32)
    a2 = r1.astype(BF16)
    a3 = (r1 - a2.astype(F32)).astype(BF16)
    return a1, a2, a3


def _rms(x, g):
    return x * lax.rsqrt(jnp.mean(x * x, axis=-1, keepdims=True) + EPS) * g


def _sigmoid(x):
    return 1.0 / (1.0 + jnp.exp(-x))


def _log_sigmoid(x):
    return jnp.minimum(x, 0.0) - jnp.log(1.0 + jnp.exp(-jnp.abs(x)))


def _mod_row_of_block(i, rows_per_block):
    ctx_blocks = T_CTX // rows_per_block
    per_seq = SMP_LEN // rows_per_block
    return jnp.where(i < ctx_blocks, 0, 1 + (i - ctx_blocks) // per_seq)


def _ada_kernel(cv_ref, w_ref, b_ref, o_ref):
    cv = cv_ref[...]
    s = cv * _sigmoid(cv)
    w = w_ref[0]
    w1, w2, w3 = _split3(w)
    s1, s2, s3 = _split3(s)
    acc = _dot(s1, w1) + (_dot(s1, w2) + _dot(s2, w1)) + (_dot(s1, w3) + _dot(s2, w2) + _dot(s3, w1))
    o_ref[0] = acc + b_ref[0]


def _ada_call(cvecs, ada_w, ada_b):
    tn = 1024
    return pl.pallas_call(
        _ada_kernel,
        grid=(DEPTH, 6 * D_MODEL // tn),
        in_specs=[
            pl.BlockSpec((N_MOD_ROWS, D_MODEL), lambda l, j: (0, 0)),
            pl.BlockSpec((1, D_MODEL, tn), lambda l, j: (l, 0, j)),
            pl.BlockSpec((1, 1, tn), lambda l, j: (l, 0, j)),
        ],
        out_specs=pl.BlockSpec((1, N_MOD_ROWS, tn), lambda l, j: (l, 0, j)),
        out_shape=jax.ShapeDtypeStruct((DEPTH, N_MOD_ROWS, 6 * D_MODEL), F32),
        compiler_params=pltpu.CompilerParams(dimension_semantics=("parallel", "parallel")),
        name="ada_mod",
    )(cvecs, ada_w, ada_b.reshape(DEPTH, 1, 6 * D_MODEL))


def _x_specs(x_ctx, x_smp):
    ctx_blk = T_CTX // TM
    smp_off = x_smp.shape[0] // TM - T_SMP // TM
    return [pl.BlockSpec((TM, D_MODEL), lambda i: (jnp.minimum(i, ctx_blk - 1), 0)),
            pl.BlockSpec((TM, D_MODEL), lambda i: (jnp.maximum(i - ctx_blk, 0) + smp_off, 0))]


def _select_x(xc_ref, xs_ref):
    return jnp.where(pl.program_id(0) < T_CTX // TM, xc_ref[...], xs_ref[...])


def _proj_kernel(xc_ref, xs_ref, mod_ref, g1_ref, wml_ref, wgt_ref, gbc_ref, wcm_ref, wmla_ref, wkr_ref,
                 cmg_ref, ws_ref, cmb_ref, qg_ref, kvg_ref, wuqa_ref, wuqb_ref, wk_ref, wv_ref, cos_ref, sin_ref,
                 ml_ref, gatest_ref, cm_ref, ckvn_ref, kr_ref, q_ref, k_ref, v_ref):
    x = _select_x(xc_ref, xs_ref)
    mod = mod_ref[0]
    sh1 = mod[:, 0:D_MODEL]
    sc1 = mod[:, D_MODEL:2 * D_MODEL]
    h = _rms(x, g1_ref[...]) * (1.0 + sc1) + sh1
    hb = h.astype(BF16)

    def mlstm_inputs():
        ml_ref[...] = _dot(hb, wml_ref[...])
        yield None
        gatest_ref[...] = _dot_nt(wgt_ref[...], hb) + gbc_ref[...]
        yield None

    def spatial_gating():
        cm = _dot(hb, wcm_ref[...])
        yield None
        u = cm[:, 0:CM_WIDTH]
        vn = _rms(cm[:, CM_WIDTH:2 * CM_WIDTH], cmg_ref[...]).astype(BF16)
        lane_group = lax.broadcasted_iota(jnp.int32, (CHUNK, CM_WIDTH), 1) >> 6
        yield None
        for c in range(TM // CHUNK):
            rows = slice(c * CHUNK, (c + 1) * CHUNK)
            vc = vn[rows]
            mixed = jnp.zeros((CHUNK, CM_WIDTH), F32)
            for g in range(CM_GROUPS):
                mixed = jnp.where(lane_group == g, _dot(ws_ref[g], vc), mixed)
            cm_ref[rows, :] = u[rows] * (mixed + cmb_ref[...])
            yield None

    def latent_attention():
        mla = _dot(hb, wmla_ref[...])
        yield None
        qn = _rms(mla[:, 0:MLA_RANK], qg_ref[...]).astype(BF16)
        ckvn = _rms(mla[:, MLA_RANK:2 * MLA_RANK], kvg_ref[...])
        ckvn_ref[...] = ckvn
        kr_ref[...] = _dot(hb, wkr_ref[...])
        cos = cos_ref[...]
        sin = sin_ref[...]
        kr_rot = mla[:, 2 * MLA_RANK:2 * MLA_RANK + HEAD_PAD] * cos + mla[:, 2 * MLA_RANK + HEAD_PAD:] * sin
        yield None
        qa = _dot(qn, wuqa_ref[...])
        qb = _dot(qn, wuqb_ref[...])
        ckvb = ckvn.astype(BF16)
        kp = _dot(ckvb, wk_ref[...])
        yield None
        for hd in range(MLA_HEADS):
            cols = slice(hd * HEAD_PAD, (hd + 1) * HEAD_PAD)
            q_ref[hd] = ((qa[:, cols] * cos + qb[:, cols] * sin) * QK_SCALE_LOG2E).astype(BF16)
            k_ref[hd] = (kp[:, cols] + kr_rot).astype(BF16)
            if hd % 4 == 3:
                yield None
        v_ref[...] = _dot(ckvb, wv_ref[...]).astype(BF16)
        yield None

    _interleave([mlstm_inputs(), spatial_gating(), latent_attention()])


def _proj_call(x_ctx, x_smp, mod, lw, cos_t, sin_t):
    nblk = T_ALL // TM
    full = lambda shape: pl.BlockSpec(shape, lambda i: (0,) * len(shape))
    rows = lambda w: pl.BlockSpec((TM, w), lambda i: (i, 0))
    in_specs = _x_specs(x_ctx, x_smp) + [
        pl.BlockSpec((1, 1, 6 * D_MODEL), lambda i: (_mod_row_of_block(i, TM), 0, 0)),
        full((1, D_MODEL)),
        full((D_MODEL, 4 * ML_WIDTH)),
        full((16, D_MODEL)),
        full((16, 1)),
        full((D_MODEL, 2 * CM_WIDTH)),
        full((D_MODEL, 2 * MLA_RANK + 2 * HEAD_PAD)),
        full((D_MODEL, MLA_ROPE)),
        full((1, CM_WIDTH)),
        full((CM_GROUPS, CHUNK, CHUNK)),
        full((CHUNK, CM_WIDTH)),
        full((1, MLA_RANK)),
        full((1, MLA_RANK)),
        full((MLA_RANK, MLA_HEADS * HEAD_PAD)),
        full((MLA_RANK, MLA_HEADS * HEAD_PAD)),
        full((MLA_RANK, MLA_HEADS * HEAD_PAD)),
        full((MLA_RANK, MLA_HEADS * MLA_V)),
        rows(HEAD_PAD),
        rows(HEAD_PAD),
    ]
    out_specs = [
        rows(4 * ML_WIDTH),
        pl.BlockSpec((16, TM), lambda i: (0, i)),
        rows(CM_WIDTH),
        rows(MLA_RANK),
        rows(MLA_ROPE),
        pl.BlockSpec((MLA_HEADS, TM, HEAD_PAD), lambda i: (0, i, 0)),
        pl.BlockSpec((MLA_HEADS, TM, HEAD_PAD), lambda i: (0, i, 0)),
        rows(MLA_HEADS * MLA_V),
    ]
    out_shape = [
        jax.ShapeDtypeStruct((T_ALL, 4 * ML_WIDTH), F32),
        jax.ShapeDtypeStruct((16, T_ALL), F32),
        jax.ShapeDtypeStruct((T_ALL, CM_WIDTH), F32),
        jax.ShapeDtypeStruct((T_ALL, MLA_RANK), F32),
        jax.ShapeDtypeStruct((T_ALL, MLA_ROPE), F32),
        jax.ShapeDtypeStruct((MLA_HEADS, T_ALL, HEAD_PAD), BF16),
        jax.ShapeDtypeStruct((MLA_HEADS, T_ALL, HEAD_PAD), BF16),
        jax.ShapeDtypeStruct((T_ALL, MLA_HEADS * MLA_V), BF16),
    ]
    return pl.pallas_call(
        _proj_kernel,
        grid=(nblk,),
        in_specs=in_specs,
        out_specs=out_specs,
        out_shape=out_shape,
        compiler_params=pltpu.CompilerParams(dimension_semantics=("parallel",), vmem_limit_bytes=VMEM_LIMIT),
        name="proj_in",
    )(x_ctx, x_smp, mod, lw["g1"], lw["wml"], lw["wgt"], lw["gb_col"], lw["wcm"], lw["wmla"],
      lw["wkr"], lw["cmg"], lw["ws"], lw["cmb"], lw["qg"], lw["kvg"], lw["wuqa"], lw["wuqb"], lw["wk"], lw["wv"],
      cos_t, sin_t)


def _mlstm_schedule():
    pair, fwd_a, bwd_a, fwd_b, bwd_b, first = [], [], [], [], [], []
    base = 0
    for p in range((N_CTX_SEQ + N_SMP_SEQ) // 2):
        nc = (CTX_LEN if 2 * p < N_CTX_SEQ else SMP_LEN) // CHUNK
        for j in range(nc):
            pair.append(p)
            fwd_a.append(base + j)
            bwd_a.append(base + nc - 1 - j)
            fwd_b.append(base + nc + j)
            bwd_b.append(base + 2 * nc - 1 - j)
            first.append(1 if j == 0 else 0)
        base += 2 * nc
    as_i32 = lambda a: jnp.asarray(np.asarray(a, np.int32))
    return tuple(as_i32(a) for a in (pair, fwd_a, bwd_a, fwd_b, bwd_b, first))


def _scan_cummax(x, direction):
    L = CHUNK
    rows = x.shape[0]
    x = jnp.concatenate([x, x], axis=0)
    lane = lax.broadcasted_iota(jnp.int32, x.shape, 1)
    k = 1
    while k < L:
        if direction == 0:
            shifted = jnp.where(lane >= k, pltpu.roll(x, k, axis=1), NEG_INF)
        else:
            shifted = jnp.where(lane < L - k, pltpu.roll(x, L - k, axis=1), NEG_INF)
        x = jnp.maximum(x, shifted)
        k *= 2
    return x[0:rows]


def _rows_to_lane_cols(rows, eye, rep, pieces):
    x = jnp.concatenate([jnp.broadcast_to(rows[h:h + 1, :], (rep, CHUNK)) for h in range(ML_HEADS)], axis=0)
    out = None
    for _ in range(pieces):
        xb = x.astype(BF16)
        part = _dot_nt(eye, xb)
        out = part if out is None else out + part
        x = x - xb.astype(F32)
    return out


def _per_head_lanes(x512, lane64):
    lo = jnp.where(lane64, x512[:, 0:128], x512[:, 128:256])
    hi = jnp.where(lane64, x512[:, 256:384], x512[:, 384:512])
    return jnp.concatenate([lo, hi], axis=1)


def _mlstm_direction(ml, g_row, direction, slot, c_ref, n_ref, m_ref):
    L = CHUNK
    t_idx = lax.broadcasted_iota(jnp.int32, (L, L), 0)
    s_idx = lax.broadcasted_iota(jnp.int32, (L, L), 1)
    visible = (s_idx <= t_idx) if direction == 0 else (s_idx >= t_idx)
    tri = jnp.where(visible, 1.0, 0.0).astype(BF16)
    eye = jnp.where(s_idx == t_idx, 1.0, 0.0).astype(BF16)
    lane64 = lax.broadcasted_iota(jnp.int32, (1, 128), 1) < ML_DIM
    head_of_lane = lax.broadcasted_iota(jnp.int32, (1, ML_WIDTH), 1) >> 6
    same_head = ((lax.broadcasted_iota(jnp.int32, (ML_WIDTH, ML_WIDTH), 0) >> 6)
                 == (lax.broadcasted_iota(jnp.int32, (ML_WIDTH, ML_WIDTH), 1) >> 6))

    i0 = 8 * direction
    i_row = g_row[i0:i0 + ML_HEADS, :]
    lf_row = _log_sigmoid(g_row[i0 + ML_HEADS:i0 + 2 * ML_HEADS, :])
    r1, r2, r3 = _split3(lf_row)
    b_row = _dot_nt(r1, tri) + _dot_nt(r2, tri) + _dot_nt(r3, tri)
    yield None
    m_rep = m_ref[slot, direction, 0:ML_HEADS, :]
    a_row = i_row - b_row
    g_row_ = jnp.maximum(m_rep, _scan_cummax(a_row, direction))
    yield None
    b_end = jnp.sum(lf_row, axis=1, keepdims=True)
    log_w = b_end - b_row + i_row
    m_new = jnp.maximum(b_end + m_rep, jnp.max(log_w, axis=1, keepdims=True))
    w_k_row = jnp.exp(log_w - m_new)
    decay_rep = jnp.exp(b_end + m_rep - m_new)

    yield None
    g512 = _rows_to_lane_cols(g_row_, eye, 128, 2)
    g_full = _per_head_lanes(g512, lane64)
    yield None
    b_full = _rows_to_lane_cols(b_row, eye, ML_DIM, 2)
    wk_full = _rows_to_lane_cols(w_k_row, eye, ML_DIM, 1)
    m_full = _per_head_lanes(jnp.concatenate([m_rep[h:h + 1, :] for h in range(ML_HEADS)], axis=1), lane64)
    decay_full = _per_head_lanes(jnp.concatenate([decay_rep[h:h + 1, :] for h in range(ML_HEADS)], axis=1),
                                 lane64)

    yield None
    q = ml[:, 0:ML_WIDTH]
    k = ml[:, ML_WIDTH:2 * ML_WIDTH] * (ML_DIM ** -0.5)
    v = ml[:, 2 * ML_WIDTH:3 * ML_WIDTH]
    qb = q.astype(BF16)
    kb = k.astype(BF16)
    vb = v.astype(BF16)
    num = jnp.zeros((L, ML_WIDTH), F32)
    rowsum = jnp.zeros((L, ML_WIDTH), F32)
    for hd in range(ML_HEADS):
        w_intra = jnp.where(visible, jnp.exp(a_row[hd:hd + 1, :] - g512[:, hd * 128:(hd + 1) * 128]), 0.0)
        q_h = jnp.where(head_of_lane == hd, q, 0.0).astype(BF16)
        sw = _dot_nt(q_h, kb) * w_intra
        num = jnp.where(head_of_lane == hd, _dot(sw.astype(BF16), vb), num)
        rowsum = jnp.where(head_of_lane == hd, jnp.sum(sw, axis=-1, keepdims=True), rowsum)
        yield None

    C = c_ref[slot, direction]
    n_row = n_ref[slot, direction, 0:1, :]
    w_inter = jnp.exp(m_full - g_full)
    block_ones = jnp.where(same_head, 1.0, 0.0).astype(BF16)
    qn = _dot((q * n_row).astype(BF16), block_ones)
    num = num + w_inter * _dot(qb, C.astype(BF16))
    den = rowsum + w_inter * qn
    h_out = num / jnp.maximum(jnp.abs(den), jnp.exp(-(b_full + g_full)))

    kw = wk_full * k
    c_ref[slot, direction] = decay_full * C + jnp.where(same_head, _dot_tn(kw.astype(BF16), vb), 0.0)
    n_ref[slot, direction, 0:1, :] = decay_full * n_row + jnp.sum(kw, axis=0, keepdims=True)
    m_ref[slot, direction, 0:ML_HEADS, :] = m_new
    yield h_out


def _interleave(stage_generators):
    results = [None] * len(stage_generators)
    live = list(range(len(stage_generators)))
    while live:
        for idx in list(live):
            try:
                value = next(stage_generators[idx])
                if value is not None:
                    results[idx] = value
            except StopIteration:
                live.remove(idx)
    return results


def _mlstm_kernel(pair_ref, fa_ref, ba_ref, fb_ref, bb_ref, first_ref,
                  mlfa_ref, mlba_ref, mlfb_ref, mlbb_ref, grfa_ref, grba_ref, grfb_ref, grbb_ref,
                  c0_ref, n0_ref, m0_ref, hf_ref, hb_ref, c_ref, n_ref, m_ref):
    step = pl.program_id(0)

    @pl.when(first_ref[step] == 1)
    def _():
        c_ref[...] = c0_ref[...]
        n_ref[...] = n0_ref[...]
        m_ref[...] = m0_ref[...]

    hfa, hba, hfb, hbb = _interleave([
        _mlstm_direction(mlfa_ref[...], grfa_ref[...], 0, 0, c_ref, n_ref, m_ref),
        _mlstm_direction(mlba_ref[...], grba_ref[...], 1, 0, c_ref, n_ref, m_ref),
        _mlstm_direction(mlfb_ref[...], grfb_ref[...], 0, 1, c_ref, n_ref, m_ref),
        _mlstm_direction(mlbb_ref[...], grbb_ref[...], 1, 1, c_ref, n_ref, m_ref)])
    hf_ref[0, 0] = hfa
    hf_ref[0, 1] = hfb
    hb_ref[0, 0] = hba
    hb_ref[0, 1] = hbb


def _mlstm_call(ml, gates_t, c0, n0, m0):
    sched = _mlstm_schedule()
    nseq = N_CTX_SEQ + N_SMP_SEQ
    nsteps = int(sched[0].shape[0])
    chunk_rows = lambda which: pl.BlockSpec((CHUNK, 4 * ML_WIDTH), lambda i, *s: (s[which][i], 0))
    gate_cols = lambda which: pl.BlockSpec((16, CHUNK), lambda i, *s: (0, s[which][i]))
    c_spec = pl.BlockSpec((2, 2, ML_WIDTH, ML_WIDTH), lambda i, *s: (s[0][i], 0, 0, 0))
    n_spec = pl.BlockSpec((2, 2, 8, ML_WIDTH), lambda i, *s: (s[0][i], 0, 0, 0))
    m_spec = pl.BlockSpec((2, 2, 8, 128), lambda i, *s: (s[0][i], 0, 0, 0))
    h_spec = pl.BlockSpec((1, 2, CHUNK, ML_WIDTH), lambda i, *s: (i, 0, 0, 0))
    grid_spec = pltpu.PrefetchScalarGridSpec(
        num_scalar_prefetch=6,
        grid=(nsteps,),
        in_specs=[chunk_rows(1), chunk_rows(2), chunk_rows(3), chunk_rows(4),
                  gate_cols(1), gate_cols(2), gate_cols(3), gate_cols(4), c_spec, n_spec, m_spec],
        out_specs=[h_spec, h_spec, c_spec, n_spec, m_spec],
    )
    return pl.pallas_call(
        _mlstm_kernel,
        grid_spec=grid_spec,
        out_shape=[
            jax.ShapeDtypeStruct((nsteps, 2, CHUNK, ML_WIDTH), F32),
            jax.ShapeDtypeStruct((nsteps, 2, CHUNK, ML_WIDTH), F32),
            jax.ShapeDtypeStruct((nseq, 2, ML_WIDTH, ML_WIDTH), F32),
            jax.ShapeDtypeStruct((nseq, 2, 8, ML_WIDTH), F32),
            jax.ShapeDtypeStruct((nseq, 2, 8, 128), F32),
        ],
        compiler_params=pltpu.CompilerParams(dimension_semantics=("arbitrary",), vmem_limit_bytes=VMEM_LIMIT),
        name="mlstm",
    )(*sched, ml, ml, ml, ml, gates_t, gates_t, gates_t, gates_t, c0, n0, m0)


def _h_pair_specs():
    ctx_blk = T_CTX // TM
    blk_per_seq = SMP_LEN // TM
    ctx_steps = N_CTX_SEQ // 2 * (CTX_LEN // CHUNK)

    def place(i):
        i_s = jnp.maximum(i - ctx_blk, 0)
        seq_s = i_s // blk_per_seq
        jb = i_s % blk_per_seq
        is_ctx = i < ctx_blk
        slot = jnp.where(is_ctx, i % 2, seq_s % 2)
        base = (ctx_steps + (seq_s // 2) * (SMP_LEN // CHUNK)) // 2
        fwd = jnp.where(is_ctx, i // 2, base + jb)
        bwd = jnp.where(is_ctx, i // 2, base + blk_per_seq - 1 - jb)
        return fwd, bwd, slot

    block = (2, None, CHUNK, ML_WIDTH)
    return (pl.BlockSpec(block, lambda i: (place(i)[0], place(i)[2], 0, 0)),
            pl.BlockSpec(block, lambda i: (place(i)[1], place(i)[2], 0, 0)))


def _cache_kv_kernel(ckv_ref, kr_ref, wk_ref, wv_ref, place_ref, k_ref, v_ref):
    ckvb = ckv_ref[...].astype(BF16)
    kp = _dot(ckvb, wk_ref[...])
    kr128 = _dot(kr_ref[...].astype(BF16), place_ref[...])
    for hd in range(MLA_HEADS):
        k_ref[hd] = (kp[:, hd * HEAD_PAD:(hd + 1) * HEAD_PAD] + kr128).astype(BF16)
    v_ref[...] = _dot(ckvb, wv_ref[...]).astype(BF16)


def _cache_kv_call(ckv, kr, wk, wv, place):
    n = N_SMP_SEQ * PAST_LEN
    tb = PAST_LEN
    return pl.pallas_call(
        _cache_kv_kernel,
        grid=(n // tb,),
        in_specs=[
            pl.BlockSpec((tb, MLA_RANK), lambda i: (i, 0)),
            pl.BlockSpec((tb, MLA_ROPE), lambda i: (i, 0)),
            pl.BlockSpec((MLA_RANK, MLA_HEADS * HEAD_PAD), lambda i: (0, 0)),
            pl.BlockSpec((MLA_RANK, MLA_HEADS * MLA_V), lambda i: (0, 0)),
            pl.BlockSpec((MLA_ROPE, HEAD_PAD), lambda i: (0, 0)),
        ],
        out_specs=[
            pl.BlockSpec((MLA_HEADS, tb, HEAD_PAD), lambda i: (0, i, 0)),
            pl.BlockSpec((tb, MLA_HEADS * MLA_V), lambda i: (i, 0)),
        ],
        out_shape=[
            jax.ShapeDtypeStruct((MLA_HEADS, n, HEAD_PAD), BF16),
            jax.ShapeDtypeStruct((n, MLA_HEADS * MLA_V), BF16),
        ],
        compiler_params=pltpu.CompilerParams(dimension_semantics=("parallel",)),
        name="cache_kv",
    )(ckv, kr, wk, wv, place)


def _attn_kernel(*refs, has_cache):
    if has_cache:
        q_ref, kn_ref, vn_ref, kc_ref, vc_ref, o_ref = refs
    else:
        q_ref, kn_ref, vn_ref, o_ref = refs

    def head(hd):
        lanes = slice(hd * MLA_V, (hd + 1) * MLA_V)
        q = q_ref[hd]
        s_n = _dot_nt(q, kn_ref[hd])
        if has_cache:
            s_c = _dot_nt(q, kc_ref[hd])
        yield None
        m = jnp.max(s_n, axis=-1, keepdims=True)
        if has_cache:
            m = jnp.maximum(m, jnp.max(s_c, axis=-1, keepdims=True))
        yield None
        p_n = jnp.exp2(s_n - m)
        l = jnp.sum(p_n, axis=-1, keepdims=True)
        o = _dot(p_n.astype(BF16), vn_ref[:, lanes])
        if has_cache:
            p_c = jnp.exp2(s_c - m)
            l = l + jnp.sum(p_c, axis=-1, keepdims=True)
            o = o + _dot(p_c.astype(BF16), vc_ref[:, lanes])
        yield o / l

    outs = []
    for hd in range(0, MLA_HEADS, ATTN_HEADS_IN_FLIGHT):
        outs += _interleave([head(hd + i) for i in range(ATTN_HEADS_IN_FLIGHT)])
    o_ref[...] = jnp.concatenate(outs, axis=-1)


def _attn_ctx_call(q, k, v):
    tq = CTX_LEN
    return pl.pallas_call(
        functools.partial(_attn_kernel, has_cache=False),
        grid=(N_CTX_SEQ,),
        in_specs=[
            pl.BlockSpec((MLA_HEADS, tq, HEAD_PAD), lambda s: (0, s, 0)),
            pl.BlockSpec((MLA_HEADS, tq, HEAD_PAD), lambda s: (0, s, 0)),
            pl.BlockSpec((tq, MLA_HEADS * MLA_V), lambda s: (s, 0)),
        ],
        out_specs=pl.BlockSpec((tq, MLA_HEADS * MLA_V), lambda s: (s, 0)),
        out_shape=jax.ShapeDtypeStruct((T_CTX, MLA_HEADS * MLA_V), F32),
        compiler_params=pltpu.CompilerParams(dimension_semantics=("parallel",), vmem_limit_bytes=VMEM_LIMIT),
        name="attn_ctx",
    )(q, k, v)


def _attn_smp_call(q, k, v, kc, vc):
    tq = 256
    qb_per_seq = SMP_LEN // tq
    ctx_qb = T_CTX // tq
    ctx_kb = T_CTX // SMP_LEN
    return pl.pallas_call(
        functools.partial(_attn_kernel, has_cache=True),
        grid=(N_SMP_SEQ, qb_per_seq),
        in_specs=[
            pl.BlockSpec((MLA_HEADS, tq, HEAD_PAD), lambda b, i: (0, ctx_qb + b * qb_per_seq + i, 0)),
            pl.BlockSpec((MLA_HEADS, SMP_LEN, HEAD_PAD), lambda b, i: (0, ctx_kb + b, 0)),
            pl.BlockSpec((SMP_LEN, MLA_HEADS * MLA_V), lambda b, i: (ctx_kb + b, 0)),
            pl.BlockSpec((MLA_HEADS, PAST_LEN, HEAD_PAD), lambda b, i: (0, b, 0)),
            pl.BlockSpec((PAST_LEN, MLA_HEADS * MLA_V), lambda b, i: (b, 0)),
        ],
        out_specs=pl.BlockSpec((tq, MLA_HEADS * MLA_V), lambda b, i: (b * qb_per_seq + i, 0)),
        out_shape=jax.ShapeDtypeStruct((T_SMP, MLA_HEADS * MLA_V), F32),
        compiler_params=pltpu.CompilerParams(dimension_semantics=("parallel", "parallel"),
                                             vmem_limit_bytes=VMEM_LIMIT),
        name="attn_smp",
    )(q, k, v, kc, vc)


def _out_kernel(xc_ref, xs_ref, mod_ref, ml_ref, hf_ref, hb_ref, cm_ref, attc_ref, atts_ref, wo_ref, g2_ref,
                wq_ref, sk_ref, x1_ref, h2t_ref, st_ref):
    is_ctx = pl.program_id(0) < T_CTX // TM
    att = jnp.where(is_ctx, attc_ref[...], atts_ref[...])
    mod = mod_ref[0]
    g1 = mod[:, 2 * D_MODEL:3 * D_MODEL]
    sh2 = mod[:, 3 * D_MODEL:4 * D_MODEL]
    sc2 = mod[:, 4 * D_MODEL:5 * D_MODEL]
    h_sum = jnp.concatenate([hf_ref[0] + hb_ref[1], hf_ref[1] + hb_ref[0]], axis=0)
    mlo = _sigmoid(ml_ref[:, 3 * ML_WIDTH:4 * ML_WIDTH]) * h_sum
    mix = (_dot(mlo.astype(BF16), wo_ref[0:ML_WIDTH, :])
           + _dot(cm_ref[...].astype(BF16), wo_ref[ML_WIDTH:ML_WIDTH + CM_WIDTH, :])
           + _dot(att.astype(BF16), wo_ref[ML_WIDTH + CM_WIDTH:, :]))
    x1 = _select_x(xc_ref, xs_ref) + g1 * mix
    x1_ref[...] = x1
    h2f = _rms(x1, g2_ref[...]) * (1.0 + sc2) + sh2
    h2t_ref[...] = h2f.T.astype(BF16)
    qp = _dot(h2f.astype(BF16), wq_ref[...]).astype(BF16)
    for hh in range(2 * PEER_HEADS):
        st_ref[hh] = _dot_nt(sk_ref[hh % 2], qp[:, hh * 128:(hh + 1) * 128])


def _out_call(x_ctx, x_smp, mod, ml, hf, hb, cm, att_ctx, att_smp, lw):
    nblk = T_ALL // TM
    ctx_blk = T_CTX // TM
    rows = lambda w: pl.BlockSpec((TM, w), lambda i: (i, 0))
    full = lambda shape: pl.BlockSpec(shape, lambda i: (0,) * len(shape))
    att_w = MLA_HEADS * MLA_V
    attc_spec = pl.BlockSpec((TM, att_w), lambda i: (jnp.minimum(i, ctx_blk - 1), 0))
    atts_spec = pl.BlockSpec((TM, att_w), lambda i: (jnp.maximum(i - ctx_blk, 0), 0))
    return pl.pallas_call(
        _out_kernel,
        grid=(nblk,),
        in_specs=_x_specs(x_ctx, x_smp) + [
            pl.BlockSpec((1, 1, 6 * D_MODEL), lambda i: (_mod_row_of_block(i, TM), 0, 0)),
            rows(4 * ML_WIDTH), *_h_pair_specs(), rows(CM_WIDTH), attc_spec, atts_spec,
            full((D_MODEL, D_MODEL)), full((1, D_MODEL)), full((D_MODEL, 2 * PEER_HEADS * 128)),
            full((2, PEER_NKEYS, 128)),
        ],
        out_specs=[rows(D_MODEL), pl.BlockSpec((D_MODEL, TM), lambda i: (0, i)),
                   pl.BlockSpec((2 * PEER_HEADS, PEER_NKEYS, TM), lambda i: (0, 0, i))],
        out_shape=[
            jax.ShapeDtypeStruct((T_ALL, D_MODEL), F32),
            jax.ShapeDtypeStruct((D_MODEL, T_ALL), BF16),
            jax.ShapeDtypeStruct((2 * PEER_HEADS, PEER_NKEYS, T_ALL), F32),
        ],
        compiler_params=pltpu.CompilerParams(dimension_semantics=("parallel",), vmem_limit_bytes=VMEM_LIMIT),
        name="proj_out",
    )(x_ctx, x_smp, mod, ml, hf, hb, cm, att_ctx, att_smp, lw["wo"], lw["g2"], lw["wq"], lw["sk"])


def _sorting_network_pairs(n):
    pairs = []
    p = 1
    while p < n:
        k = p
        while k >= 1:
            for j in range(k % p, n - k, 2 * k):
                for i in range(min(k, n - j - k)):
                    if (i + j) // (2 * p) == (i + j + k) // (2 * p):
                        pairs.append((i + j, i + j + k))
            k //= 2
        p *= 2
    return pairs


_SORT16_PAIRS = _sorting_network_pairs(PEER_TOPK)


def _pop16(lists, singles=None):
    lists = list(lists)
    vals = []
    for k in range(PEER_TOPK):
        heads = lists[0] if singles is None else jnp.maximum(lists[0], singles)
        m = jnp.max(heads, axis=0, keepdims=True)
        vals.append(m)
        if k == PEER_TOPK - 1:
            break
        hit = lists[0] == m
        for i in range(PEER_TOPK - 1 - k):
            lists[i] = jnp.where(hit, lists[i + 1], lists[i])
        if singles is not None:
            singles = jnp.where(singles == m, NEG_INF, singles)
    return vals


def _top16_rows(s):
    tiles = [s[8 * j:8 * j + 8] for j in range(s.shape[0] // 8)]
    assert len(tiles) == PEER_TOPK
    for i, j in _SORT16_PAIRS:
        tiles[i], tiles[j] = jnp.maximum(tiles[i], tiles[j]), jnp.minimum(tiles[i], tiles[j])
    return _pop16(tiles)


def _count_steps(x, thresholds, below):
    r = jnp.zeros(x.shape, F32)
    for q, t in enumerate(thresholds):
        r = jnp.where((x < t) if below else (x >= t), float(q + 1), r)
    return r


def _pack_rows_bf16(x):
    return pltpu.bitcast(_pack_rows_words(x), BF16)


def _pack_rows_words(x):
    r, n = x.shape
    x4 = x.reshape(r // 16, 2, 8, n)
    lo = x4[:, 0].reshape(r // 2, n)
    hi = x4[:, 1].reshape(r // 2, n)
    return pltpu.pack_elementwise([lo, hi], packed_dtype=BF16)


def _dup_bf16_words(x):
    return pltpu.pack_elementwise([x, x], packed_dtype=BF16)


def _rows_to_array(rows, row_iota):
    arr = jnp.zeros(row_iota.shape, F32)
    for i, r in enumerate(rows):
        arr = jnp.where(row_iota == i, r, arr)
    return arr


def _topk_kernel(st_ref, e1_ref, cut_ref, e2_ref, r2_ref):
    n = st_ref.shape[-1]
    row8 = lax.broadcasted_iota(jnp.int32, (8, n), 0)

    def head(hd, carry):
        s1 = st_ref[2 * hd]
        s2 = st_ref[2 * hd + 1]
        v1 = _top16_rows(s1)
        v2 = _top16_rows(s2)
        rank2 = _count_steps(s2, v2, below=True)
        v1_top = _rows_to_array(v1[0:8], row8)
        v1_bot = _rows_to_array(v1[8:PEER_TOPK], row8)
        c = _pop16([v1_top + v2[q] for q in range(PEER_TOPK)], singles=v1_bot + v2[0])
        tau = c[PEER_TOPK - 1]
        z = jnp.zeros((1, n), F32)
        for ck in c:
            z = z + jnp.exp(ck - c[0])

        def smallest_qualifying(v1_rows, q):
            return jnp.min(jnp.where(v1_rows + v2[q] >= tau, v1_rows, jnp.inf), axis=0, keepdims=True)

        sigma = [jnp.minimum(smallest_qualifying(v1_top, 0), smallest_qualifying(v1_bot, 0))]
        sigma += [smallest_qualifying(v1_top, q) for q in range(1, 8)]
        extra = jnp.zeros((1, n), F32)
        for q in range(8, PEER_TOPK):
            extra = extra + jnp.where(v1[0] + v2[q] >= tau, 1.0, 0.0)
        cut = _count_steps(s1, sigma, below=False) + jnp.where(s1 >= v1[0], extra, 0.0)
        e1_ref[hd] = _dup_bf16_words(jnp.exp(s1 - v1[0]) / z)
        cut_ref[hd] = _dup_bf16_words(cut)
        e2_ref[hd] = _pack_rows_words(jnp.exp(s2 - v2[0]))
        r2_ref[hd] = _pack_rows_words(rank2)
        return carry

    lax.fori_loop(0, PEER_HEADS, head, 0)


def _topk_call(st):
    tn = TN_TOPK
    spec = pl.BlockSpec((PEER_HEADS, PEER_NKEYS, tn), lambda i: (0, 0, i))
    word_spec = pl.BlockSpec((PEER_HEADS, PEER_NKEYS // 2, tn), lambda i: (0, 0, i))
    stat = lambda dt: jax.ShapeDtypeStruct((PEER_HEADS, PEER_NKEYS, T_ALL), dt)
    words = jax.ShapeDtypeStruct((PEER_HEADS, PEER_NKEYS // 2, T_ALL), jnp.uint32)
    return pl.pallas_call(
        _topk_kernel,
        grid=(T_ALL // tn,),
        in_specs=[pl.BlockSpec((2 * PEER_HEADS, PEER_NKEYS, tn), lambda i: (0, 0, i))],
        out_specs=[spec, spec, word_spec, word_spec],
        out_shape=[stat(jnp.uint32), stat(jnp.uint32), words, words],
        compiler_params=pltpu.CompilerParams(dimension_semantics=("parallel",), vmem_limit_bytes=VMEM_LIMIT),
        name="peer_topk",
    )(st)


def _row_tile_bf16(row):
    blk = pltpu.bitcast(jnp.broadcast_to(row, (8, row.shape[1])), BF16)
    return jnp.concatenate([blk] * (PEER_NKEYS // 16), axis=0)


def _gelu_tanh_bf16(x):
    log2e = 1.4426950408889634
    w = x * (-1.5957691216057308 * log2e - (0.07135481627159584 * log2e) * (x * x))
    return x / (1.0 + jnp.exp2(w))


def _expert_kernel(h2t_ref, u_ref, vt_ref, e1_ref, cut_ref, e2_ref, r2_ref, x1_ref, mod_ref, fg_ref,
                   o_ref, acc_ref, a_ref, *, final_norm):
    j = pl.program_id(1)
    n_tiles = pl.num_programs(1)
    tm = h2t_ref.shape[1]
    a_per_tile = TE_EXP // PEER_NKEYS

    @pl.when(j == 0)
    def _():
        acc_ref[...] = jnp.zeros_like(acc_ref)

    def gate_tile(ai):
        a = j * a_per_tile + ai
        gate = jnp.zeros((PEER_NKEYS, tm), BF16)
        for hd in range(PEER_HEADS):
            cut_a = _row_tile_bf16(cut_ref[hd, pl.ds(a, 1), :])
            e1_a = _row_tile_bf16(e1_ref[hd, pl.ds(a, 1), :])
            r2 = pltpu.bitcast(r2_ref[hd], BF16)
            e2 = pltpu.bitcast(e2_ref[hd], BF16)
            gate = gate + jnp.where(r2 < cut_a, e2, jnp.zeros((), BF16)) * e1_a
        return gate

    a_per_sub = EXP_SUB // PEER_NKEYS
    n_sub = TE_EXP // EXP_SUB
    s = [None] * n_sub
    gates = [None] * n_sub
    s[0] = _dot(u_ref[0:EXP_SUB, :], h2t_ref[...])
    gates[0] = [gate_tile(ai) for ai in range(a_per_sub)]
    for sub in range(n_sub):
        if sub + 1 < n_sub:
            s[sub + 1] = _dot(u_ref[(sub + 1) * EXP_SUB:(sub + 2) * EXP_SUB, :], h2t_ref[...])
            gates[sub + 1] = [gate_tile((sub + 1) * a_per_sub + i) for i in range(a_per_sub)]
        g = _gelu_tanh_bf16(_pack_rows_bf16(s[sub]))
        for i in range(a_per_sub):
            r0 = sub * EXP_SUB + i * PEER_NKEYS
            a_ref[r0:r0 + PEER_NKEYS, :] = gates[sub][i] * g[i * PEER_NKEYS:(i + 1) * PEER_NKEYS, :]
    acc_ref[...] += _dot(vt_ref[...], a_ref[...])

    @pl.when(j == n_tiles - 1)
    def _():
        g2 = mod_ref[0][:, 5 * D_MODEL:6 * D_MODEL]
        y = x1_ref[...] + g2 * acc_ref[...].T
        if final_norm:
            y = _rms(y, fg_ref[...])
        o_ref[...] = y


def _expert_call(h2t, u_bf, vt_bf, layer, e1, cut, e2, r2, x1, mod, final_g, final_norm, tok_start, tok_count):
    tm, te = TM_EXP, TE_EXP
    b0 = tok_start // tm
    n_tiles = PEER_EXPERTS // te
    stat = pl.BlockSpec((PEER_HEADS, PEER_NKEYS, tm), lambda i, j: (0, 0, b0 + i))
    stat_words = pl.BlockSpec((PEER_HEADS, PEER_NKEYS // 2, tm), lambda i, j: (0, 0, b0 + i))
    return pl.pallas_call(
        functools.partial(_expert_kernel, final_norm=final_norm),
        grid=(tok_count // tm, n_tiles),
        in_specs=[
            pl.BlockSpec((D_MODEL, tm), lambda i, j: (0, b0 + i)),
            pl.BlockSpec((None, te, D_MODEL), lambda i, j: (layer, j, 0)),
            pl.BlockSpec((None, D_MODEL, te), lambda i, j: (layer, 0, j)),
            stat, stat, stat_words, stat_words,
            pl.BlockSpec((tm, D_MODEL), lambda i, j: (b0 + i, 0)),
            pl.BlockSpec((1, 1, 6 * D_MODEL), lambda i, j: (_mod_row_of_block(b0 + i, tm), 0, 0)),
            pl.BlockSpec((1, D_MODEL), lambda i, j: (0, 0)),
        ],
        out_specs=pl.BlockSpec((tm, D_MODEL), lambda i, j: (i, 0)),
        out_shape=jax.ShapeDtypeStruct((tok_count, D_MODEL), F32),
        scratch_shapes=[
            pltpu.VMEM((D_MODEL, tm), F32),
            pltpu.VMEM((te, tm), BF16),
        ],
        compiler_params=pltpu.CompilerParams(dimension_semantics=("parallel", "arbitrary"),
                                             vmem_limit_bytes=VMEM_LIMIT),
        name="peer_experts",
    )(h2t, u_bf, vt_bf, e1, cut, e2, r2, x1, mod, final_g)


def _u_prep_kernel(u_ref, o_ref):
    o_ref[...] = u_ref[...].astype(BF16)


def _v_prep_kernel(v_ref, o_ref):
    o_ref[...] = _pack_rows_bf16(v_ref[...]).T


def _table_prep_call(peer_u, peer_v):
    te = TE_PREP
    grid = (DEPTH, PEER_EXPERTS // te)
    rows = pl.BlockSpec((None, te, D_MODEL), lambda l, j: (l, j, 0))
    params = pltpu.CompilerParams(dimension_semantics=("parallel", "parallel"), vmem_limit_bytes=VMEM_LIMIT)
    u_bf = pl.pallas_call(
        _u_prep_kernel, grid=grid, in_specs=[rows], out_specs=rows,
        out_shape=jax.ShapeDtypeStruct((DEPTH, PEER_EXPERTS, D_MODEL), BF16),
        compiler_params=params, name="peer_u_prep")(peer_u)
    vt_bf = pl.pallas_call(
        _v_prep_kernel, grid=grid, in_specs=[rows],
        out_specs=pl.BlockSpec((None, D_MODEL, te), lambda l, j: (l, 0, j)),
        out_shape=jax.ShapeDtypeStruct((DEPTH, D_MODEL, PEER_EXPERTS), BF16),
        compiler_params=params, name="peer_v_prep")(peer_v)
    return u_bf, vt_bf


def _rope_swap_cols(w):
    return jnp.concatenate([-w[:, 8:16], w[:, 0:8], -w[:, 24:32], w[:, 16:24]], axis=1)


def _pad_heads(parts, n_heads):
    k = next(p[0].shape[0] for p in parts if p[0] is not None)
    cols = []
    for hd in range(n_heads):
        for arr, w in parts:
            cols.append(jnp.zeros((k, w), F32) if arr is None else arr[:, hd * w:(hd + 1) * w])
    return jnp.concatenate(cols, axis=1)


def _rope_tables():
    pos = np.arange(SMP_LEN)
    freqs = ROPE_THETA ** (-np.arange(0, ROPE_AXIS, 2, dtype=np.float32) / ROPE_AXIS)
    ang_r = (pos // GRID_W).astype(np.float32)[:, None] * freqs
    ang_c = (pos % GRID_W).astype(np.float32)[:, None] * freqs
    ang = jnp.asarray(np.concatenate([ang_r, ang_r, ang_c, ang_c], axis=1).astype(np.float32))
    cos32 = jnp.cos(ang)
    sin32 = jnp.sin(ang)
    ones = jnp.ones((SMP_LEN, MLA_NOPE), F32)
    cos_s = jnp.concatenate([ones, cos32, ones[:, :HEAD_PAD - MLA_NOPE - MLA_ROPE]], axis=1)
    sin_s = jnp.concatenate([0 * ones, sin32, 0 * ones[:, :HEAD_PAD - MLA_NOPE - MLA_ROPE]], axis=1)
    cos_t = jnp.concatenate([jnp.ones((T_CTX, HEAD_PAD), F32)] + [cos_s] * N_SMP_SEQ, axis=0)
    sin_t = jnp.concatenate([jnp.zeros((T_CTX, HEAD_PAD), F32)] + [sin_s] * N_SMP_SEQ, axis=0)
    return cos_t, sin_t


def _layer_weights(l, norm1_g, w_in, mlstm_gate_b, cm_norm_g, cm_ws, cm_b, mla_q_norm_g, mla_w_uq, mla_kv_norm_g,
                   mla_w_ukv, w_out, norm2_g, peer_w_q, peer_subkeys):
    w = w_in[l]
    o_g = 4 * ML_WIDTH
    o_cm = o_g + 16
    o_cq = o_cm + 2 * CM_WIDTH
    o_ckv = o_cq + MLA_RANK
    o_kr = o_ckv + MLA_RANK
    w_kr = w[:, o_kr:o_kr + MLA_ROPE]
    zeros_l = jnp.zeros((D_MODEL, MLA_NOPE), F32)
    zeros_r = jnp.zeros((D_MODEL, HEAD_PAD - MLA_NOPE - MLA_ROPE), F32)
    kr128 = jnp.concatenate([zeros_l, w_kr, zeros_r], axis=1)
    krsw128 = jnp.concatenate([zeros_l, _rope_swap_cols(w_kr), zeros_r], axis=1)
    uq = mla_w_uq[l].reshape(MLA_RANK, MLA_HEADS, MLA_NOPE + MLA_ROPE)
    uq_nope = uq[:, :, :MLA_NOPE].reshape(MLA_RANK, -1)
    uq_rope = uq[:, :, MLA_NOPE:].reshape(MLA_RANK, -1)
    uq_rope_sw = jnp.concatenate(
        [_rope_swap_cols(uq_rope[:, hd * MLA_ROPE:(hd + 1) * MLA_ROPE]) for hd in range(MLA_HEADS)], axis=1)
    pad_w = HEAD_PAD - MLA_NOPE - MLA_ROPE
    ukv = mla_w_ukv[l].reshape(MLA_RANK, MLA_HEADS, MLA_NOPE + MLA_V)
    uk = ukv[:, :, :MLA_NOPE].reshape(MLA_RANK, -1)
    uv = ukv[:, :, MLA_NOPE:].reshape(MLA_RANK, -1)
    gb = mlstm_gate_b[l]
    return {
        "g1": norm1_g[l].reshape(1, D_MODEL),
        "wml": w[:, 0:o_g].astype(BF16),
        "wgt": w[:, o_g:o_cm].T.astype(BF16),
        "gb_col": gb.reshape(16, 1),
        "wcm": w[:, o_cm:o_cq].astype(BF16),
        "wmla": jnp.concatenate([w[:, o_cq:o_kr], kr128, krsw128], axis=1).astype(BF16),
        "wkr": w_kr.astype(BF16),
        "cmg": cm_norm_g[l].reshape(1, CM_WIDTH),
        "ws": cm_ws[l].astype(BF16),
        "cmb": jnp.repeat(cm_b[l].T, CM_WIDTH // CM_GROUPS, axis=1),
        "qg": mla_q_norm_g[l].reshape(1, MLA_RANK),
        "kvg": mla_kv_norm_g[l].reshape(1, MLA_RANK),
        "wuqa": _pad_heads([(uq_nope, MLA_NOPE), (uq_rope, MLA_ROPE), (None, pad_w)], MLA_HEADS).astype(BF16),
        "wuqb": _pad_heads([(None, MLA_NOPE), (uq_rope_sw, MLA_ROPE), (None, pad_w)], MLA_HEADS).astype(BF16),
        "wk": _pad_heads([(uk, MLA_NOPE), (None, HEAD_PAD - MLA_NOPE)], MLA_HEADS).astype(BF16),
        "wv": uv.astype(BF16),
        "wo": w_out[l].astype(BF16),
        "g2": norm2_g[l].reshape(1, D_MODEL),
        "wq": peer_w_q[l].astype(BF16),
        "sk": peer_subkeys[l].astype(BF16),
    }


def kernel(x_prompt, x_sample, c, cache_mla_ckv, cache_mla_krope, state_mlstm_C, state_mlstm_n, state_mlstm_m, c_ctx, norm1_g, ada_w, ada_b, w_in, mlstm_gate_b, cm_norm_g, cm_ws, cm_b, mla_q_norm_g, mla_w_uq, mla_kv_norm_g, mla_w_ukv, w_out, norm2_g, peer_w_q, peer_subkeys, peer_u, peer_v, final_g):
    x_ctx = x_prompt.reshape(T_CTX, D_MODEL)
    x_smp = x_sample.reshape(T_SMP, D_MODEL)
    cvecs = jnp.concatenate([c_ctx[None, :], c, jnp.zeros((N_MOD_ROWS - 1 - N_SMP_SEQ, D_MODEL), F32)], axis=0)
    mod_all = _ada_call(cvecs, ada_w, ada_b)
    cos_t, sin_t = _rope_tables()
    place = jnp.concatenate([jnp.zeros((MLA_ROPE, MLA_NOPE), F32), jnp.eye(MLA_ROPE, dtype=F32),
                             jnp.zeros((MLA_ROPE, HEAD_PAD - MLA_NOPE - MLA_ROPE), F32)], axis=1).astype(BF16)
    final_g2 = final_g.reshape(1, D_MODEL)
    u_bf, vt_bf = _table_prep_call(peer_u, peer_v)

    ckvs, krs, Cs, ns, ms = [], [], [], [], []
    for l in range(DEPTH):
        lw = _layer_weights(l, norm1_g, w_in, mlstm_gate_b, cm_norm_g, cm_ws, cm_b, mla_q_norm_g, mla_w_uq,
                            mla_kv_norm_g, mla_w_ukv, w_out, norm2_g, peer_w_q, peer_subkeys)
        mod = mod_all[l].reshape(N_MOD_ROWS, 1, 6 * D_MODEL)
        ml, gates_t, cm, ckvn, kr, q, k, v = _proj_call(x_ctx, x_smp, mod, lw, cos_t, sin_t)

        c_blk = jnp.einsum('bdhij,hg->bdhigj', state_mlstm_C[:, l], jnp.eye(ML_HEADS, dtype=F32))
        c0 = jnp.concatenate([jnp.zeros((N_CTX_SEQ, 2, ML_WIDTH, ML_WIDTH), F32),
                              c_blk.reshape(N_SMP_SEQ, 2, ML_WIDTH, ML_WIDTH)], axis=0)
        n0 = jnp.concatenate([jnp.zeros((N_CTX_SEQ, 2, 1, ML_WIDTH), F32),
                              state_mlstm_n[:, l].reshape(N_SMP_SEQ, 2, 1, ML_WIDTH)], axis=0)
        n0 = jnp.pad(n0, ((0, 0), (0, 0), (0, 7), (0, 0)))
        m0 = jnp.concatenate([jnp.zeros((N_CTX_SEQ, 2, ML_HEADS), F32), state_mlstm_m[:, l]], axis=0)
        m0 = jnp.pad(jnp.broadcast_to(m0[..., None], m0.shape + (128,)), ((0, 0), (0, 0), (0, 8 - ML_HEADS), (0, 0)))
        hf, hb, c_fin, n_fin, m_fin = _mlstm_call(ml, gates_t, c0, n0, m0)

        kc, vc = _cache_kv_call(cache_mla_ckv[:, l].reshape(N_SMP_SEQ * PAST_LEN, MLA_RANK),
                                cache_mla_krope[:, l].reshape(N_SMP_SEQ * PAST_LEN, MLA_ROPE),
                                lw["wk"], lw["wv"], place)
        att_ctx = _attn_ctx_call(q, k, v)
        att_smp = _attn_smp_call(q, k, v, kc, vc)

        x1, h2t, st = _out_call(x_ctx, x_smp, mod, ml, hf, hb, cm, att_ctx, att_smp, lw)
        e1, cut, e2, r2 = _topk_call(st)
        experts = functools.partial(_expert_call, h2t, u_bf, vt_bf, l, e1, cut, e2, r2, x1, mod, final_g2)
        if l < DEPTH - 1:
            x_ctx = x_smp = experts(False, 0, T_ALL)
        else:
            x_ctx = experts(True, 0, T_CTX)
            x_smp = experts(True, T_CTX, T_SMP)

        ckvs.append(ckvn[:T_CTX].reshape(N_CTX_SEQ, CTX_LEN, MLA_RANK))
        krs.append(kr[:T_CTX].reshape(N_CTX_SEQ, CTX_LEN, MLA_ROPE))
        Cs.append(jnp.stack([c_fin[:N_CTX_SEQ, :, hd * ML_DIM:(hd + 1) * ML_DIM, hd * ML_DIM:(hd + 1) * ML_DIM]
                             for hd in range(ML_HEADS)], axis=2))
        ns.append(n_fin[:N_CTX_SEQ, :, 0, :].reshape(N_CTX_SEQ, 2, ML_HEADS, ML_DIM))
        ms.append(m_fin[:N_CTX_SEQ, :, 0:ML_HEADS, 0])

    y_prompt = x_ctx.reshape(N_CTX_SEQ, CTX_LEN, D_MODEL)
    y_sample = x_smp.reshape(N_SMP_SEQ, SMP_LEN, D_MODEL)
    return (y_prompt, y_sample, jnp.stack(ckvs, axis=1), jnp.stack(krs, axis=1), jnp.stack(Cs, axis=1),
            jnp.stack(ns, axis=1), jnp.stack(ms, axis=1))
```

```python
import functools

import numpy as np
import jax
import jax.numpy as jnp
from jax import lax
from jax.experimental import pallas as pl
from jax.experimental.pallas import tpu as pltpu

F32 = jnp.float32
BF16 = jnp.bfloat16

D_MODEL = 1024
N_CTX_SEQ = 16
CTX_LEN = 256
N_SMP_SEQ = 4
SMP_LEN = 2048
PAST_LEN = 256
DEPTH = 2
GRID_W = 64
EPS = 1e-6
T_CTX = N_CTX_SEQ * CTX_LEN
T_SMP = N_SMP_SEQ * SMP_LEN
T_ALL = T_CTX + T_SMP
N_MOD_ROWS = 8

ML_HEADS = 4
ML_DIM = 64
ML_WIDTH = 256
CHUNK = 128
CM_GROUPS = 4
CM_WIDTH = 256
MLA_HEADS = 8
MLA_NOPE = 64
MLA_ROPE = 32
MLA_V = 64
MLA_RANK = 256
HEAD_PAD = 128
ROPE_AXIS = 16
ROPE_THETA = 10000.0
QK_SCALE_LOG2E = (MLA_NOPE + MLA_ROPE) ** -0.5 * 1.4426950408889634
PEER_HEADS = 8
PEER_NKEYS = 128
PEER_EXPERTS = PEER_NKEYS * PEER_NKEYS
PEER_TOPK = 16

TM = 256
TN_TOPK = 256
TM_EXP = 512
TE_EXP = 2048
EXP_SUB = 512
TE_PREP = 2048
ATTN_HEADS_IN_FLIGHT = 2
VMEM_LIMIT = 56 * 1024 * 1024

NEG_INF = float("-inf")


def _dot(a, b):
    return jnp.dot(a, b, preferred_element_type=F32)


def _dot_nt(a, b):
    return lax.dot_general(a, b, (((1,), (1,)), ((), ())), preferred_element_type=F32)


def _dot_tn(a, b):
    return lax.dot_general(a, b, (((0,), (0,)), ((), ())), preferred_element_type=F32)


def _split3(a):
    a1 = a.astype(BF16)
    r1 = a - a1.astype(F32)
    a2 = r1.astype(BF16)
    a3 = (r1 - a2.astype(F32)).astype(BF16)
    return a1, a2, a3


def _rms(x, g):
    return x * lax.rsqrt(jnp.mean(x * x, axis=-1, keepdims=True) + EPS) * g


def _sigmoid(x):
    return 1.0 / (1.0 + jnp.exp(-x))


def _log_sigmoid(x):
    return jnp.minimum(x, 0.0) - jnp.log(1.0 + jnp.exp(-jnp.abs(x)))


def _mod_row_of_block(i, rows_per_block):
    ctx_blocks = T_CTX // rows_per_block
    per_seq = SMP_LEN // rows_per_block
    return jnp.where(i < ctx_blocks, 0, 1 + (i - ctx_blocks) // per_seq)


def _ada_kernel(cv_ref, w_ref, b_ref, o_ref):
    cv = cv_ref[...]
    s = cv * _sigmoid(cv)
    w = w_ref[0]
    w1, w2, w3 = _split3(w)
    s1, s2, s3 = _split3(s)
    acc = _dot(s1, w1) + (_dot(s1, w2) + _dot(s2, w1)) + (_dot(s1, w3) + _dot(s2, w2) + _dot(s3, w1))
    o_ref[0] = acc + b_ref[0]


def _ada_call(cvecs, ada_w, ada_b):
    tn = 1024
    return pl.pallas_call(
        _ada_kernel,
        grid=(DEPTH, 6 * D_MODEL // tn),
        in_specs=[
            pl.BlockSpec((N_MOD_ROWS, D_MODEL), lambda l, j: (0, 0)),
            pl.BlockSpec((1, D_MODEL, tn), lambda l, j: (l, 0, j)),
            pl.BlockSpec((1, 1, tn), lambda l, j: (l, 0, j)),
        ],
        out_specs=pl.BlockSpec((1, N_MOD_ROWS, tn), lambda l, j: (l, 0, j)),
        out_shape=jax.ShapeDtypeStruct((DEPTH, N_MOD_ROWS, 6 * D_MODEL), F32),
        compiler_params=pltpu.CompilerParams(dimension_semantics=("parallel", "parallel")),
        name="ada_mod",
    )(cvecs, ada_w, ada_b.reshape(DEPTH, 1, 6 * D_MODEL))


def _x_specs(x_ctx, x_smp):
    ctx_blk = T_CTX // TM
    smp_off = x_smp.shape[0] // TM - T_SMP // TM
    return [pl.BlockSpec((TM, D_MODEL), lambda i: (jnp.minimum(i, ctx_blk - 1), 0)),
            pl.BlockSpec((TM, D_MODEL), lambda i: (jnp.maximum(i - ctx_blk, 0) + smp_off, 0))]


def _select_x(xc_ref, xs_ref):
    return jnp.where(pl.program_id(0) < T_CTX // TM, xc_ref[...], xs_ref[...])


def _proj_kernel(xc_ref, xs_ref, mod_ref, g1_ref, wml_ref, wgt_ref, gbc_ref, wcm_ref, wmla_ref, wkr_ref,
                 cmg_ref, ws_ref, cmb_ref, qg_ref, kvg_ref, wuqa_ref, wuqb_ref, wk_ref, wv_ref, cos_ref, sin_ref,
                 ml_ref, gatest_ref, cm_ref, ckvn_ref, kr_ref, q_ref, k_ref, v_ref):
    x = _select_x(xc_ref, xs_ref)
    mod = mod_ref[0]
    sh1 = mod[:, 0:D_MODEL]
    sc1 = mod[:, D_MODEL:2 * D_MODEL]
    h = _rms(x, g1_ref[...]) * (1.0 + sc1) + sh1
    hb = h.astype(BF16)

    def mlstm_inputs():
        ml_ref[...] = _dot(hb, wml_ref[...])
        yield None
        gatest_ref[...] = _dot_nt(wgt_ref[...], hb) + gbc_ref[...]
        yield None

    def spatial_gating():
        cm = _dot(hb, wcm_ref[...])
        yield None
        u = cm[:, 0:CM_WIDTH]
        vn = _rms(cm[:, CM_WIDTH:2 * CM_WIDTH], cmg_ref[...]).astype(BF16)
        lane_group = lax.broadcasted_iota(jnp.int32, (CHUNK, CM_WIDTH), 1) >> 6
        yield None
        for c in range(TM // CHUNK):
            rows = slice(c * CHUNK, (c + 1) * CHUNK)
            vc = vn[rows]
            mixed = jnp.zeros((CHUNK, CM_WIDTH), F32)
            for g in range(CM_GROUPS):
                mixed = jnp.where(lane_group == g, _dot(ws_ref[g], vc), mixed)
            cm_ref[rows, :] = u[rows] * (mixed + cmb_ref[...])
            yield None

    def latent_attention():
        mla = _dot(hb, wmla_ref[...])
        yield None
        qn = _rms(mla[:, 0:MLA_RANK], qg_ref[...]).astype(BF16)
        ckvn = _rms(mla[:, MLA_RANK:2 * MLA_RANK], kvg_ref[...])
        ckvn_ref[...] = ckvn
        kr_ref[...] = _dot(hb, wkr_ref[...])
        cos = cos_ref[...]
        sin = sin_ref[...]
        kr_rot = mla[:, 2 * MLA_RANK:2 * MLA_RANK + HEAD_PAD] * cos + mla[:, 2 * MLA_RANK + HEAD_PAD:] * sin
        yield None
        qa = _dot(qn, wuqa_ref[...])
        qb = _dot(qn, wuqb_ref[...])
        ckvb = ckvn.astype(BF16)
        kp = _dot(ckvb, wk_ref[...])
        yield None
        for hd in range(MLA_HEADS):
            cols = slice(hd * HEAD_PAD, (hd + 1) * HEAD_PAD)
            q_ref[hd] = ((qa[:, cols] * cos + qb[:, cols] * sin) * QK_SCALE_LOG2E).astype(BF16)
            k_ref[hd] = (kp[:, cols] + kr_rot).astype(BF16)
            if hd % 4 == 3:
                yield None
        v_ref[...] = _dot(ckvb, wv_ref[...]).astype(BF16)
        yield None

    _interleave([mlstm_inputs(), spatial_gating(), latent_attention()])


def _proj_call(x_ctx, x_smp, mod, lw, cos_t, sin_t):
    nblk = T_ALL // TM
    full = lambda shape: pl.BlockSpec(shape, lambda i: (0,) * len(shape))
    rows = lambda w: pl.BlockSpec((TM, w), lambda i: (i, 0))
    in_specs = _x_specs(x_ctx, x_smp) + [
        pl.BlockSpec((1, 1, 6 * D_MODEL), lambda i: (_mod_row_of_block(i, TM), 0, 0)),
        full((1, D_MODEL)),
        full((D_MODEL, 4 * ML_WIDTH)),
        full((16, D_MODEL)),
        full((16, 1)),
        full((D_MODEL, 2 * CM_WIDTH)),
        full((D_MODEL, 2 * MLA_RANK + 2 * HEAD_PAD)),
        full((D_MODEL, MLA_ROPE)),
        full((1, CM_WIDTH)),
        full((CM_GROUPS, CHUNK, CHUNK)),
        full((CHUNK, CM_WIDTH)),
        full((1, MLA_RANK)),
        full((1, MLA_RANK)),
        full((MLA_RANK, MLA_HEADS * HEAD_PAD)),
        full((MLA_RANK, MLA_HEADS * HEAD_PAD)),
        full((MLA_RANK, MLA_HEADS * HEAD_PAD)),
        full((MLA_RANK, MLA_HEADS * MLA_V)),
        rows(HEAD_PAD),
        rows(HEAD_PAD),
    ]
    out_specs = [
        rows(4 * ML_WIDTH),
        pl.BlockSpec((16, TM), lambda i: (0, i)),
        rows(CM_WIDTH),
        rows(MLA_RANK),
        rows(MLA_ROPE),
        pl.BlockSpec((MLA_HEADS, TM, HEAD_PAD), lambda i: (0, i, 0)),
        pl.BlockSpec((MLA_HEADS, TM, HEAD_PAD), lambda i: (0, i, 0)),
        rows(MLA_HEADS * MLA_V),
    ]
    out_shape = [
        jax.ShapeDtypeStruct((T_ALL, 4 * ML_WIDTH), F32),
        jax.ShapeDtypeStruct((16, T_ALL), F32),
        jax.ShapeDtypeStruct((T_ALL, CM_WIDTH), F32),
        jax.ShapeDtypeStruct((T_ALL, MLA_RANK), F32),
        jax.ShapeDtypeStruct((T_ALL, MLA_ROPE), F32),
        jax.ShapeDtypeStruct((MLA_HEADS, T_ALL, HEAD_PAD), BF16),
        jax.ShapeDtypeStruct((MLA_HEADS, T_ALL, HEAD_PAD), BF16),
        jax.ShapeDtypeStruct((T_ALL, MLA_HEADS * MLA_V), BF16),
    ]
    return pl.pallas_call(
        _proj_kernel,
        grid=(nblk,),
        in_specs=in_specs,
        out_specs=out_specs,
        out_shape=out_shape,
        compiler_params=pltpu.CompilerParams(dimension_semantics=("parallel",), vmem_limit_bytes=VMEM_LIMIT),
        name="proj_in",
    )(x_ctx, x_smp, mod, lw["g1"], lw["wml"], lw["wgt"], lw["gb_col"], lw["wcm"], lw["wmla"],
      lw["wkr"], lw["cmg"], lw["ws"], lw["cmb"], lw["qg"], lw["kvg"], lw["wuqa"], lw["wuqb"], lw["wk"], lw["wv"],
      cos_t, sin_t)


def _mlstm_schedule():
    pair, fwd_a, bwd_a, fwd_b, bwd_b, first = [], [], [], [], [], []
    base = 0
    for p in range((N_CTX_SEQ + N_SMP_SEQ) // 2):
        nc = (CTX_LEN if 2 * p < N_CTX_SEQ else SMP_LEN) // CHUNK
        for j in range(nc):
            pair.append(p)
            fwd_a.append(base + j)
            bwd_a.append(base + nc - 1 - j)
            fwd_b.append(base + nc + j)
            bwd_b.append(base + 2 * nc - 1 - j)
            first.append(1 if j == 0 else 0)
        base += 2 * nc
    as_i32 = lambda a: jnp.asarray(np.asarray(a, np.int32))
    return tuple(as_i32(a) for a in (pair, fwd_a, bwd_a, fwd_b, bwd_b, first))


def _scan_cummax(x, direction):
    L = CHUNK
    rows = x.shape[0]
    x = jnp.concatenate([x, x], axis=0)
    lane = lax.broadcasted_iota(jnp.int32, x.shape, 1)
    k = 1
    while k < L:
        if direction == 0:
            shifted = jnp.where(lane >= k, pltpu.roll(x, k, axis=1), NEG_INF)
        else:
            shifted = jnp.where(lane < L - k, pltpu.roll(x, L - k, axis=1), NEG_INF)
        x = jnp.maximum(x, shifted)
        k *= 2
    return x[0:rows]


def _rows_to_lane_cols(rows, eye, rep, pieces):
    x = jnp.concatenate([jnp.broadcast_to(rows[h:h + 1, :], (rep, CHUNK)) for h in range(ML_HEADS)], axis=0)
    out = None
    for _ in range(pieces):
        xb = x.astype(BF16)
        part = _dot_nt(eye, xb)
        out = part if out is None else out + part
        x = x - xb.astype(F32)
    return out


def _per_head_lanes(x512, lane64):
    lo = jnp.where(lane64, x512[:, 0:128], x512[:, 128:256])
    hi = jnp.where(lane64, x512[:, 256:384], x512[:, 384:512])
    return jnp.concatenate([lo, hi], axis=1)


def _mlstm_direction(ml, g_row, direction, slot, c_ref, n_ref, m_ref):
    L = CHUNK
    t_idx = lax.broadcasted_iota(jnp.int32, (L, L), 0)
    s_idx = lax.broadcasted_iota(jnp.int32, (L, L), 1)
    visible = (s_idx <= t_idx) if direction == 0 else (s_idx >= t_idx)
    tri = jnp.where(visible, 1.0, 0.0).astype(BF16)
    eye = jnp.where(s_idx == t_idx, 1.0, 0.0).astype(BF16)
    lane64 = lax.broadcasted_iota(jnp.int32, (1, 128), 1) < ML_DIM
    head_of_lane = lax.broadcasted_iota(jnp.int32, (1, ML_WIDTH), 1) >> 6
    same_head = ((lax.broadcasted_iota(jnp.int32, (ML_WIDTH, ML_WIDTH), 0) >> 6)
                 == (lax.broadcasted_iota(jnp.int32, (ML_WIDTH, ML_WIDTH), 1) >> 6))

    i0 = 8 * direction
    i_row = g_row[i0:i0 + ML_HEADS, :]
    lf_row = _log_sigmoid(g_row[i0 + ML_HEADS:i0 + 2 * ML_HEADS, :])
    r1, r2, r3 = _split3(lf_row)
    b_row = _dot_nt(r1, tri) + _dot_nt(r2, tri) + _dot_nt(r3, tri)
    yield None
    m_rep = m_ref[slot, direction, 0:ML_HEADS, :]
    a_row = i_row - b_row
    g_row_ = jnp.maximum(m_rep, _scan_cummax(a_row, direction))
    yield None
    b_end = jnp.sum(lf_row, axis=1, keepdims=True)
    log_w = b_end - b_row + i_row
    m_new = jnp.maximum(b_end + m_rep, jnp.max(log_w, axis=1, keepdims=True))
    w_k_row = jnp.exp(log_w - m_new)
    decay_rep = jnp.exp(b_end + m_rep - m_new)

    yield None
    g512 = _rows_to_lane_cols(g_row_, eye, 128, 2)
    g_full = _per_head_lanes(g512, lane64)
    yield None
    b_full = _rows_to_lane_cols(b_row, eye, ML_DIM, 2)
    wk_full = _rows_to_lane_cols(w_k_row, eye, ML_DIM, 1)
    m_full = _per_head_lanes(jnp.concatenate([m_rep[h:h + 1, :] for h in range(ML_HEADS)], axis=1), lane64)
    decay_full = _per_head_lanes(jnp.concatenate([decay_rep[h:h + 1, :] for h in range(ML_HEADS)], axis=1),
                                 lane64)

    yield None
    q = ml[:, 0:ML_WIDTH]
    k = ml[:, ML_WIDTH:2 * ML_WIDTH] * (ML_DIM ** -0.5)
    v = ml[:, 2 * ML_WIDTH:3 * ML_WIDTH]
    qb = q.astype(BF16)
    kb = k.astype(BF16)
    vb = v.astype(BF16)
    num = jnp.zeros((L, ML_WIDTH), F32)
    rowsum = jnp.zeros((L, ML_WIDTH), F32)
    for hd in range(ML_HEADS):
        w_intra = jnp.where(visible, jnp.exp(a_row[hd:hd + 1, :] - g512[:, hd * 128:(hd + 1) * 128]), 0.0)
        q_h = jnp.where(head_of_lane == hd, q, 0.0).astype(BF16)
        sw = _dot_nt(q_h, kb) * w_intra
        num = jnp.where(head_of_lane == hd, _dot(sw.astype(BF16), vb), num)
        rowsum = jnp.where(head_of_lane == hd, jnp.sum(sw, axis=-1, keepdims=True), rowsum)
        yield None

    C = c_ref[slot, direction]
    n_row = n_ref[slot, direction, 0:1, :]
    w_inter = jnp.exp(m_full - g_full)
    block_ones = jnp.where(same_head, 1.0, 0.0).astype(BF16)
    qn = _dot((q * n_row).astype(BF16), block_ones)
    num = num + w_inter * _dot(qb, C.astype(BF16))
    den = rowsum + w_inter * qn
    h_out = num / jnp.maximum(jnp.abs(den), jnp.exp(-(b_full + g_full)))

    kw = wk_full * k
    c_ref[slot, direction] = decay_full * C + jnp.where(same_head, _dot_tn(kw.astype(BF16), vb), 0.0)
    n_ref[slot, direction, 0:1, :] = decay_full * n_row + jnp.sum(kw, axis=0, keepdims=True)
    m_ref[slot, direction, 0:ML_HEADS, :] = m_new
    yield h_out


def _interleave(stage_generators):
    results = [None] * len(stage_generators)
    live = list(range(len(stage_generators)))
    while live:
        for idx in list(live):
            try:
                value = next(stage_generators[idx])
                if value is not None:
                    results[idx] = value
            except StopIteration:
                live.remove(idx)
    return results


def _mlstm_kernel(pair_ref, fa_ref, ba_ref, fb_ref, bb_ref, first_ref,
                  mlfa_ref, mlba_ref, mlfb_ref, mlbb_ref, grfa_ref, grba_ref, grfb_ref, grbb_ref,
                  c0_ref, n0_ref, m0_ref, hf_ref, hb_ref, c_ref, n_ref, m_ref):
    step = pl.program_id(0)

    @pl.when(first_ref[step] == 1)
    def _():
        c_ref[...] = c0_ref[...]
        n_ref[...] = n0_ref[...]
        m_ref[...] = m0_ref[...]

    hfa, hba, hfb, hbb = _interleave([
        _mlstm_direction(mlfa_ref[...], grfa_ref[...], 0, 0, c_ref, n_ref, m_ref),
        _mlstm_direction(mlba_ref[...], grba_ref[...], 1, 0, c_ref, n_ref, m_ref),
        _mlstm_direction(mlfb_ref[...], grfb_ref[...], 0, 1, c_ref, n_ref, m_ref),
        _mlstm_direction(mlbb_ref[...], grbb_ref[...], 1, 1, c_ref, n_ref, m_ref)])
    hf_ref[0, 0] = hfa
    hf_ref[0, 1] = hfb
    hb_ref[0, 0] = hba
    hb_ref[0, 1] = hbb


def _mlstm_call(ml, gates_t, c0, n0, m0):
    sched = _mlstm_schedule()
    nseq = N_CTX_SEQ + N_SMP_SEQ
    nsteps = int(sched[0].shape[0])
    chunk_rows = lambda which: pl.BlockSpec((CHUNK, 4 * ML_WIDTH), lambda i, *s: (s[which][i], 0))
    gate_cols = lambda which: pl.BlockSpec((16, CHUNK), lambda i, *s: (0, s[which][i]))
    c_spec = pl.BlockSpec((2, 2, ML_WIDTH, ML_WIDTH), lambda i, *s: (s[0][i], 0, 0, 0))
    n_spec = pl.BlockSpec((2, 2, 8, ML_WIDTH), lambda i, *s: (s[0][i], 0, 0, 0))
    m_spec = pl.BlockSpec((2, 2, 8, 128), lambda i, *s: (s[0][i], 0, 0, 0))
    h_spec = pl.BlockSpec((1, 2, CHUNK, ML_WIDTH), lambda i, *s: (i, 0, 0, 0))
    grid_spec = pltpu.PrefetchScalarGridSpec(
        num_scalar_prefetch=6,
        grid=(nsteps,),
        in_specs=[chunk_rows(1), chunk_rows(2), chunk_rows(3), chunk_rows(4),
                  gate_cols(1), gate_cols(2), gate_cols(3), gate_cols(4), c_spec, n_spec, m_spec],
        out_specs=[h_spec, h_spec, c_spec, n_spec, m_spec],
    )
    return pl.pallas_call(
        _mlstm_kernel,
        grid_spec=grid_spec,
        out_shape=[
            jax.ShapeDtypeStruct((nsteps, 2, CHUNK, ML_WIDTH), F32),
            jax.ShapeDtypeStruct((nsteps, 2, CHUNK, ML_WIDTH), F32),
            jax.ShapeDtypeStruct((nseq, 2, ML_WIDTH, ML_WIDTH), F32),
            jax.ShapeDtypeStruct((nseq, 2, 8, ML_WIDTH), F32),
            jax.ShapeDtypeStruct((nseq, 2, 8, 128), F32),
        ],
        compiler_params=pltpu.CompilerParams(dimension_semantics=("arbitrary",), vmem_limit_bytes=VMEM_LIMIT),
        name="mlstm",
    )(*sched, ml, ml, ml, ml, gates_t, gates_t, gates_t, gates_t, c0, n0, m0)


def _h_pair_specs():
    ctx_blk = T_CTX // TM
    blk_per_seq = SMP_LEN // TM
    ctx_steps = N_CTX_SEQ // 2 * (CTX_LEN // CHUNK)

    def place(i):
        i_s = jnp.maximum(i - ctx_blk, 0)
        seq_s = i_s // blk_per_seq
        jb = i_s % blk_per_seq
        is_ctx = i < ctx_blk
        slot = jnp.where(is_ctx, i % 2, seq_s % 2)
        base = (ctx_steps + (seq_s // 2) * (SMP_LEN // CHUNK)) // 2
        fwd = jnp.where(is_ctx, i // 2, base + jb)
        bwd = jnp.where(is_ctx, i // 2, base + blk_per_seq - 1 - jb)
        return fwd, bwd, slot

    block = (2, None, CHUNK, ML_WIDTH)
    return (pl.BlockSpec(block, lambda i: (place(i)[0], place(i)[2], 0, 0)),
            pl.BlockSpec(block, lambda i: (place(i)[1], place(i)[2], 0, 0)))


def _cache_kv_kernel(ckv_ref, kr_ref, wk_ref, wv_ref, place_ref, k_ref, v_ref):
    ckvb = ckv_ref[...].astype(BF16)
    kp = _dot(ckvb, wk_ref[...])
    kr128 = _dot(kr_ref[...].astype(BF16), place_ref[...])
    for hd in range(MLA_HEADS):
        k_ref[hd] = (kp[:, hd * HEAD_PAD:(hd + 1) * HEAD_PAD] + kr128).astype(BF16)
    v_ref[...] = _dot(ckvb, wv_ref[...]).astype(BF16)


def _cache_kv_call(ckv, kr, wk, wv, place):
    n = N_SMP_SEQ * PAST_LEN
    tb = PAST_LEN
    return pl.pallas_call(
        _cache_kv_kernel,
        grid=(n // tb,),
        in_specs=[
            pl.BlockSpec((tb, MLA_RANK), lambda i: (i, 0)),
            pl.BlockSpec((tb, MLA_ROPE), lambda i: (i, 0)),
            pl.BlockSpec((MLA_RANK, MLA_HEADS * HEAD_PAD), lambda i: (0, 0)),
            pl.BlockSpec((MLA_RANK, MLA_HEADS * MLA_V), lambda i: (0, 0)),
            pl.BlockSpec((MLA_ROPE, HEAD_PAD), lambda i: (0, 0)),
        ],
        out_specs=[
            pl.BlockSpec((MLA_HEADS, tb, HEAD_PAD), lambda i: (0, i, 0)),
            pl.BlockSpec((tb, MLA_HEADS * MLA_V), lambda i: (i, 0)),
        ],
        out_shape=[
            jax.ShapeDtypeStruct((MLA_HEADS, n, HEAD_PAD), BF16),
            jax.ShapeDtypeStruct((n, MLA_HEADS * MLA_V), BF16),
        ],
        compiler_params=pltpu.CompilerParams(dimension_semantics=("parallel",)),
        name="cache_kv",
    )(ckv, kr, wk, wv, place)


def _attn_kernel(*refs, has_cache):
    if has_cache:
        q_ref, kn_ref, vn_ref, kc_ref, vc_ref, o_ref = refs
    else:
        q_ref, kn_ref, vn_ref, o_ref = refs

    def head(hd):
        lanes = slice(hd * MLA_V, (hd + 1) * MLA_V)
        q = q_ref[hd]
        s_n = _dot_nt(q, kn_ref[hd])
        if has_cache:
            s_c = _dot_nt(q, kc_ref[hd])
        yield None
        m = jnp.max(s_n, axis=-1, keepdims=True)
        if has_cache:
            m = jnp.maximum(m, jnp.max(s_c, axis=-1, keepdims=True))
        yield None
        p_n = jnp.exp2(s_n - m)
        l = jnp.sum(p_n, axis=-1, keepdims=True)
        o = _dot(p_n.astype(BF16), vn_ref[:, lanes])
        if has_cache:
            p_c = jnp.exp2(s_c - m)
            l = l + jnp.sum(p_c, axis=-1, keepdims=True)
            o = o + _dot(p_c.astype(BF16), vc_ref[:, lanes])
        yield o / l

    outs = []
    for hd in range(0, MLA_HEADS, ATTN_HEADS_IN_FLIGHT):
        outs += _interleave([head(hd + i) for i in range(ATTN_HEADS_IN_FLIGHT)])
    o_ref[...] = jnp.concatenate(outs, axis=-1)


def _attn_ctx_call(q, k, v):
    tq = CTX_LEN
    return pl.pallas_call(
        functools.partial(_attn_kernel, has_cache=False),
        grid=(N_CTX_SEQ,),
        in_specs=[
            pl.BlockSpec((MLA_HEADS, tq, HEAD_PAD), lambda s: (0, s, 0)),
            pl.BlockSpec((MLA_HEADS, tq, HEAD_PAD), lambda s: (0, s, 0)),
            pl.BlockSpec((tq, MLA_HEADS * MLA_V), lambda s: (s, 0)),
        ],
        out_specs=pl.BlockSpec((tq, MLA_HEADS * MLA_V), lambda s: (s, 0)),
        out_shape=jax.ShapeDtypeStruct((T_CTX, MLA_HEADS * MLA_V), F32),
        compiler_params=pltpu.CompilerParams(dimension_semantics=("parallel",), vmem_limit_bytes=VMEM_LIMIT),
        name="attn_ctx",
    )(q, k, v)


def _attn_smp_call(q, k, v, kc, vc):
    tq = 256
    qb_per_seq = SMP_LEN // tq
    ctx_qb = T_CTX // tq
    ctx_kb = T_CTX // SMP_LEN
    return pl.pallas_call(
        functools.partial(_attn_kernel, has_cache=True),
        grid=(N_SMP_SEQ, qb_per_seq),
        in_specs=[
            pl.BlockSpec((MLA_HEADS, tq, HEAD_PAD), lambda b, i: (0, ctx_qb + b * qb_per_seq + i, 0)),
            pl.BlockSpec((MLA_HEADS, SMP_LEN, HEAD_PAD), lambda b, i: (0, ctx_kb + b, 0)),
            pl.BlockSpec((SMP_LEN, MLA_HEADS * MLA_V), lambda b, i: (ctx_kb + b, 0)),
            pl.BlockSpec((MLA_HEADS, PAST_LEN, HEAD_PAD), lambda b, i: (0, b, 0)),
            pl.BlockSpec((PAST_LEN, MLA_HEADS * MLA_V), lambda b, i: (b, 0)),
        ],
        out_specs=pl.BlockSpec((tq, MLA_HEADS * MLA_V), lambda b, i: (b * qb_per_seq + i, 0)),
        out_shape=jax.ShapeDtypeStruct((T_SMP, MLA_HEADS * MLA_V), F32),
        compiler_params=pltpu.CompilerParams(dimension_semantics=("parallel", "parallel"),
                                             vmem_limit_bytes=VMEM_LIMIT),
        name="attn_smp",
    )(q, k, v, kc, vc)


def _out_kernel(xc_ref, xs_ref, mod_ref, ml_ref, hf_ref, hb_ref, cm_ref, attc_ref, atts_ref, wo_ref, g2_ref,
                wq_ref, sk_ref, x1_ref, h2t_ref, st_ref):
    is_ctx = pl.program_id(0) < T_CTX // TM
    att = jnp.where(is_ctx, attc_ref[...], atts_ref[...])
    mod = mod_ref[0]
    g1 = mod[:, 2 * D_MODEL:3 * D_MODEL]
    sh2 = mod[:, 3 * D_MODEL:4 * D_MODEL]
    sc2 = mod[:, 4 * D_MODEL:5 * D_MODEL]
    h_sum = jnp.concatenate([hf_ref[0] + hb_ref[1], hf_ref[1] + hb_ref[0]], axis=0)
    mlo = _sigmoid(ml_ref[:, 3 * ML_WIDTH:4 * ML_WIDTH]) * h_sum
    mix = (_dot(mlo.astype(BF16), wo_ref[0:ML_WIDTH, :])
           + _dot(cm_ref[...].astype(BF16), wo_ref[ML_WIDTH:ML_WIDTH + CM_WIDTH, :])
           + _dot(att.astype(BF16), wo_ref[ML_WIDTH + CM_WIDTH:, :]))
    x1 = _select_x(xc_ref, xs_ref) + g1 * mix
    x1_ref[...] = x1
    h2f = _rms(x1, g2_ref[...]) * (1.0 + sc2) + sh2
    h2t_ref[...] = h2f.T.astype(BF16)
    qp = _dot(h2f.astype(BF16), wq_ref[...]).astype(BF16)
    for hh in range(2 * PEER_HEADS):
        st_ref[hh] = _dot_nt(sk_ref[hh % 2], qp[:, hh * 128:(hh + 1) * 128])


def _out_call(x_ctx, x_smp, mod, ml, hf, hb, cm, att_ctx, att_smp, lw):
    nblk = T_ALL // TM
    ctx_blk = T_CTX // TM
    rows = lambda w: pl.BlockSpec((TM, w), lambda i: (i, 0))
    full = lambda shape: pl.BlockSpec(shape, lambda i: (0,) * len(shape))
    att_w = MLA_HEADS * MLA_V
    attc_spec = pl.BlockSpec((TM, att_w), lambda i: (jnp.minimum(i, ctx_blk - 1), 0))
    atts_spec = pl.BlockSpec((TM, att_w), lambda i: (jnp.maximum(i - ctx_blk, 0), 0))
    return pl.pallas_call(
        _out_kernel,
        grid=(nblk,),
        in_specs=_x_specs(x_ctx, x_smp) + [
            pl.BlockSpec((1, 1, 6 * D_MODEL), lambda i: (_mod_row_of_block(i, TM), 0, 0)),
            rows(4 * ML_WIDTH), *_h_pair_specs(), rows(CM_WIDTH), attc_spec, atts_spec,
            full((D_MODEL, D_MODEL)), full((1, D_MODEL)), full((D_MODEL, 2 * PEER_HEADS * 128)),
            full((2, PEER_NKEYS, 128)),
        ],
        out_specs=[rows(D_MODEL), pl.BlockSpec((D_MODEL, TM), lambda i: (0, i)),
                   pl.BlockSpec((2 * PEER_HEADS, PEER_NKEYS, TM), lambda i: (0, 0, i))],
        out_shape=[
            jax.ShapeDtypeStruct((T_ALL, D_MODEL), F32),
            jax.ShapeDtypeStruct((D_MODEL, T_ALL), BF16),
            jax.ShapeDtypeStruct((2 * PEER_HEADS, PEER_NKEYS, T_ALL), F32),
        ],
        compiler_params=pltpu.CompilerParams(dimension_semantics=("parallel",), vmem_limit_bytes=VMEM_LIMIT),
        name="proj_out",
    )(x_ctx, x_smp, mod, ml, hf, hb, cm, att_ctx, att_smp, lw["wo"], lw["g2"], lw["wq"], lw["sk"])


def _sorting_network_pairs(n):
    pairs = []
    p = 1
    while p < n:
        k = p
        while k >= 1:
            for j in range(k % p, n - k, 2 * k):
                for i in range(min(k, n - j - k)):
                    if (i + j) // (2 * p) == (i + j + k) // (2 * p):
                        pairs.append((i + j, i + j + k))
            k //= 2
        p *= 2
    return pairs


_SORT16_PAIRS = _sorting_network_pairs(PEER_TOPK)


def _pop16(lists, singles=None):
    lists = list(lists)
    vals = []
    for k in range(PEER_TOPK):
        heads = lists[0] if singles is None else jnp.maximum(lists[0], singles)
        m = jnp.max(heads, axis=0, keepdims=True)
        vals.append(m)
        if k == PEER_TOPK - 1:
            break
        hit = lists[0] == m
        for i in range(PEER_TOPK - 1 - k):
            lists[i] = jnp.where(hit, lists[i + 1], lists[i])
        if singles is not None:
            singles = jnp.where(singles == m, NEG_INF, singles)
    return vals


def _top16_rows(s):
    tiles = [s[8 * j:8 * j + 8] for j in range(s.shape[0] // 8)]
    assert len(tiles) == PEER_TOPK
    for i, j in _SORT16_PAIRS:
        tiles[i], tiles[j] = jnp.maximum(tiles[i], tiles[j]), jnp.minimum(tiles[i], tiles[j])
    return _pop16(tiles)


def _count_steps(x, thresholds, below):
    r = jnp.zeros(x.shape, F32)
    for q, t in enumerate(thresholds):
        r = jnp.where((x < t) if below else (x >= t), float(q + 1), r)
    return r


def _pack_rows_bf16(x):
    return pltpu.bitcast(_pack_rows_words(x), BF16)


def _pack_rows_words(x):
    r, n = x.shape
    x4 = x.reshape(r // 16, 2, 8, n)
    lo = x4[:, 0].reshape(r // 2, n)
    hi = x4[:, 1].reshape(r // 2, n)
    return pltpu.pack_elementwise([lo, hi], packed_dtype=BF16)


def _dup_bf16_words(x):
    return pltpu.pack_elementwise([x, x], packed_dtype=BF16)


def _rows_to_array(rows, row_iota):
    arr = jnp.zeros(row_iota.shape, F32)
    for i, r in enumerate(rows):
        arr = jnp.where(row_iota == i, r, arr)
    return arr


def _topk_kernel(st_ref, e1_ref, cut_ref, e2_ref, r2_ref):
    n = st_ref.shape[-1]
    row8 = lax.broadcasted_iota(jnp.int32, (8, n), 0)

    def head(hd, carry):
        s1 = st_ref[2 * hd]
        s2 = st_ref[2 * hd + 1]
        v1 = _top16_rows(s1)
        v2 = _top16_rows(s2)
        v1_top = _rows_to_array(v1[0:8], row8)
        v1_bot = _rows_to_array(v1[8:PEER_TOPK], row8)
        c = _pop16([v1_top + v2[q] for q in range(PEER_TOPK)], singles=v1_bot + v2[0])
        tau = c[PEER_TOPK - 1]
        z = jnp.zeros((1, n), F32)
        for ck in c:
            z = z + jnp.exp(ck - c[0])

        def smallest_qualifying(v1_rows, q):
            return jnp.min(jnp.where(v1_rows + v2[q] >= tau, v1_rows, jnp.inf), axis=0, keepdims=True)

        sigma = [jnp.minimum(smallest_qualifying(v1_top, 0), smallest_qualifying(v1_bot, 0))]
        sigma += [smallest_qualifying(v1_top, q) for q in range(1, 8)]
        extra = jnp.zeros((1, n), F32)
        last_selected = v2[7]
        for q in range(8, PEER_TOPK):
            reached = v1[0] + v2[q] >= tau
            extra = extra + jnp.where(reached, 1.0, 0.0)
            last_selected = jnp.where(reached, v2[q], last_selected)
        rank2 = jnp.where(s2 < last_selected, 9.0, _count_steps(s2, v2[0:8], below=True))
        cut = _count_steps(s1, sigma, below=False) + jnp.where(s1 >= v1[0], jnp.minimum(extra, 1.0), 0.0)
        e1_ref[hd] = _dup_bf16_words(jnp.exp(s1 - v1[0]) / z)
        cut_ref[hd] = _dup_bf16_words(cut)
        e2_ref[hd] = _pack_rows_words(jnp.exp(s2 - v2[0]))
        r2_ref[hd] = _pack_rows_words(rank2)
        return carry

    lax.fori_loop(0, PEER_HEADS, head, 0)


def _topk_call(st):
    tn = TN_TOPK
    spec = pl.BlockSpec((PEER_HEADS, PEER_NKEYS, tn), lambda i: (0, 0, i))
    word_spec = pl.BlockSpec((PEER_HEADS, PEER_NKEYS // 2, tn), lambda i: (0, 0, i))
    stat = lambda dt: jax.ShapeDtypeStruct((PEER_HEADS, PEER_NKEYS, T_ALL), dt)
    words = jax.ShapeDtypeStruct((PEER_HEADS, PEER_NKEYS // 2, T_ALL), jnp.uint32)
    return pl.pallas_call(
        _topk_kernel,
        grid=(T_ALL // tn,),
        in_specs=[pl.BlockSpec((2 * PEER_HEADS, PEER_NKEYS, tn), lambda i: (0, 0, i))],
        out_specs=[spec, spec, word_spec, word_spec],
        out_shape=[stat(jnp.uint32), stat(jnp.uint32), words, words],
        compiler_params=pltpu.CompilerParams(dimension_semantics=("parallel",), vmem_limit_bytes=VMEM_LIMIT),
        name="peer_topk",
    )(st)


def _row_tile_bf16(row):
    blk = pltpu.bitcast(jnp.broadcast_to(row, (8, row.shape[1])), BF16)
    return jnp.concatenate([blk] * (PEER_NKEYS // 16), axis=0)


def _gelu_tanh_bf16(x):
    log2e = 1.4426950408889634
    w = x * (-1.5957691216057308 * log2e - (0.07135481627159584 * log2e) * (x * x))
    return x / (1.0 + jnp.exp2(w))


def _expert_kernel(h2t_ref, u_ref, vt_ref, e1_ref, cut_ref, e2_ref, r2_ref, x1_ref, mod_ref, fg_ref,
                   o_ref, acc_ref, a_ref, *, final_norm):
    j = pl.program_id(1)
    n_tiles = pl.num_programs(1)
    tm = h2t_ref.shape[1]
    a_per_tile = TE_EXP // PEER_NKEYS

    @pl.when(j == 0)
    def _():
        acc_ref[...] = jnp.zeros_like(acc_ref)

    def gate_tile(ai):
        a = j * a_per_tile + ai
        gate = jnp.zeros((PEER_NKEYS, tm), BF16)
        for hd in range(PEER_HEADS):
            cut_a = _row_tile_bf16(cut_ref[hd, pl.ds(a, 1), :])
            e1_a = _row_tile_bf16(e1_ref[hd, pl.ds(a, 1), :])
            r2 = pltpu.bitcast(r2_ref[hd], BF16)
            e2 = pltpu.bitcast(e2_ref[hd], BF16)
            gate = gate + jnp.where(r2 < cut_a, e2, jnp.zeros((), BF16)) * e1_a
        return gate

    a_per_sub = EXP_SUB // PEER_NKEYS
    n_sub = TE_EXP // EXP_SUB
    s = [None] * n_sub
    gates = [None] * n_sub
    s[0] = _dot(u_ref[0:EXP_SUB, :], h2t_ref[...])
    gates[0] = [gate_tile(ai) for ai in range(a_per_sub)]
    for sub in range(n_sub):
        if sub + 1 < n_sub:
            s[sub + 1] = _dot(u_ref[(sub + 1) * EXP_SUB:(sub + 2) * EXP_SUB, :], h2t_ref[...])
            gates[sub + 1] = [gate_tile((sub + 1) * a_per_sub + i) for i in range(a_per_sub)]
        g = _gelu_tanh_bf16(_pack_rows_bf16(s[sub]))
        for i in range(a_per_sub):
            r0 = sub * EXP_SUB + i * PEER_NKEYS
            a_ref[r0:r0 + PEER_NKEYS, :] = gates[sub][i] * g[i * PEER_NKEYS:(i + 1) * PEER_NKEYS, :]
    acc_ref[...] += _dot(vt_ref[...], a_ref[...])

    @pl.when(j == n_tiles - 1)
    def _():
        g2 = mod_ref[0][:, 5 * D_MODEL:6 * D_MODEL]
        y = x1_ref[...] + g2 * acc_ref[...].T
        if final_norm:
            y = _rms(y, fg_ref[...])
        o_ref[...] = y


def _expert_call(h2t, u_bf, vt_bf, layer, e1, cut, e2, r2, x1, mod, final_g, final_norm, tok_start, tok_count):
    tm, te = TM_EXP, TE_EXP
    b0 = tok_start // tm
    n_tiles = PEER_EXPERTS // te
    stat = pl.BlockSpec((PEER_HEADS, PEER_NKEYS, tm), lambda i, j: (0, 0, b0 + i))
    stat_words = pl.BlockSpec((PEER_HEADS, PEER_NKEYS // 2, tm), lambda i, j: (0, 0, b0 + i))
    return pl.pallas_call(
        functools.partial(_expert_kernel, final_norm=final_norm),
        grid=(tok_count // tm, n_tiles),
        in_specs=[
            pl.BlockSpec((D_MODEL, tm), lambda i, j: (0, b0 + i)),
            pl.BlockSpec((None, te, D_MODEL), lambda i, j: (layer, j, 0)),
            pl.BlockSpec((None, D_MODEL, te), lambda i, j: (layer, 0, j)),
            stat, stat, stat_words, stat_words,
            pl.BlockSpec((tm, D_MODEL), lambda i, j: (b0 + i, 0)),
            pl.BlockSpec((1, 1, 6 * D_MODEL), lambda i, j: (_mod_row_of_block(b0 + i, tm), 0, 0)),
            pl.BlockSpec((1, D_MODEL), lambda i, j: (0, 0)),
        ],
        out_specs=pl.BlockSpec((tm, D_MODEL), lambda i, j: (i, 0)),
        out_shape=jax.ShapeDtypeStruct((tok_count, D_MODEL), F32),
        scratch_shapes=[
            pltpu.VMEM((D_MODEL, tm), F32),
            pltpu.VMEM((te, tm), BF16),
        ],
        compiler_params=pltpu.CompilerParams(dimension_semantics=("parallel", "arbitrary"),
                                             vmem_limit_bytes=VMEM_LIMIT),
        name="peer_experts",
    )(h2t, u_bf, vt_bf, e1, cut, e2, r2, x1, mod, final_g)


def _u_prep_kernel(u_ref, o_ref):
    o_ref[...] = u_ref[...].astype(BF16)


def _v_prep_kernel(v_ref, o_ref):
    o_ref[...] = _pack_rows_bf16(v_ref[...]).T


def _table_prep_call(peer_u, peer_v):
    te = TE_PREP
    grid = (DEPTH, PEER_EXPERTS // te)
    rows = pl.BlockSpec((None, te, D_MODEL), lambda l, j: (l, j, 0))
    params = pltpu.CompilerParams(dimension_semantics=("parallel", "parallel"), vmem_limit_bytes=VMEM_LIMIT)
    u_bf = pl.pallas_call(
        _u_prep_kernel, grid=grid, in_specs=[rows], out_specs=rows,
        out_shape=jax.ShapeDtypeStruct((DEPTH, PEER_EXPERTS, D_MODEL), BF16),
        compiler_params=params, name="peer_u_prep")(peer_u)
    vt_bf = pl.pallas_call(
        _v_prep_kernel, grid=grid, in_specs=[rows],
        out_specs=pl.BlockSpec((None, D_MODEL, te), lambda l, j: (l, 0, j)),
        out_shape=jax.ShapeDtypeStruct((DEPTH, D_MODEL, PEER_EXPERTS), BF16),
        compiler_params=params, name="peer_v_prep")(peer_v)
    return u_bf, vt_bf


def _rope_swap_cols(w):
    return jnp.concatenate([-w[:, 8:16], w[:, 0:8], -w[:, 24:32], w[:, 16:24]], axis=1)


def _pad_heads(parts, n_heads):
    k = next(p[0].shape[0] for p in parts if p[0] is not None)
    cols = []
    for hd in range(n_heads):
        for arr, w in parts:
            cols.append(jnp.zeros((k, w), F32) if arr is None else arr[:, hd * w:(hd + 1) * w])
    return jnp.concatenate(cols, axis=1)


def _rope_tables():
    pos = np.arange(SMP_LEN)
    freqs = ROPE_THETA ** (-np.arange(0, ROPE_AXIS, 2, dtype=np.float32) / ROPE_AXIS)
    ang_r = (pos // GRID_W).astype(np.float32)[:, None] * freqs
    ang_c = (pos % GRID_W).astype(np.float32)[:, None] * freqs
    ang = jnp.asarray(np.concatenate([ang_r, ang_r, ang_c, ang_c], axis=1).astype(np.float32))
    cos32 = jnp.cos(ang)
    sin32 = jnp.sin(ang)
    ones = jnp.ones((SMP_LEN, MLA_NOPE), F32)
    cos_s = jnp.concatenate([ones, cos32, ones[:, :HEAD_PAD - MLA_NOPE - MLA_ROPE]], axis=1)
    sin_s = jnp.concatenate([0 * ones, sin32, 0 * ones[:, :HEAD_PAD - MLA_NOPE - MLA_ROPE]], axis=1)
    cos_t = jnp.concatenate([jnp.ones((T_CTX, HEAD_PAD), F32)] + [cos_s] * N_SMP_SEQ, axis=0)
    sin_t = jnp.concatenate([jnp.zeros((T_CTX, HEAD_PAD), F32)] + [sin_s] * N_SMP_SEQ, axis=0)
    return cos_t, sin_t


def _layer_weights(l, norm1_g, w_in, mlstm_gate_b, cm_norm_g, cm_ws, cm_b, mla_q_norm_g, mla_w_uq, mla_kv_norm_g,
                   mla_w_ukv, w_out, norm2_g, peer_w_q, peer_subkeys):
    w = w_in[l]
    o_g = 4 * ML_WIDTH
    o_cm = o_g + 16
    o_cq = o_cm + 2 * CM_WIDTH
    o_ckv = o_cq + MLA_RANK
    o_kr = o_ckv + MLA_RANK
    w_kr = w[:, o_kr:o_kr + MLA_ROPE]
    zeros_l = jnp.zeros((D_MODEL, MLA_NOPE), F32)
    zeros_r = jnp.zeros((D_MODEL, HEAD_PAD - MLA_NOPE - MLA_ROPE), F32)
    kr128 = jnp.concatenate([zeros_l, w_kr, zeros_r], axis=1)
    krsw128 = jnp.concatenate([zeros_l, _rope_swap_cols(w_kr), zeros_r], axis=1)
    uq = mla_w_uq[l].reshape(MLA_RANK, MLA_HEADS, MLA_NOPE + MLA_ROPE)
    uq_nope = uq[:, :, :MLA_NOPE].reshape(MLA_RANK, -1)
    uq_rope = uq[:, :, MLA_NOPE:].reshape(MLA_RANK, -1)
    uq_rope_sw = jnp.concatenate(
        [_rope_swap_cols(uq_rope[:, hd * MLA_ROPE:(hd + 1) * MLA_ROPE]) for hd in range(MLA_HEADS)], axis=1)
    pad_w = HEAD_PAD - MLA_NOPE - MLA_ROPE
    ukv = mla_w_ukv[l].reshape(MLA_RANK, MLA_HEADS, MLA_NOPE + MLA_V)
    uk = ukv[:, :, :MLA_NOPE].reshape(MLA_RANK, -1)
    uv = ukv[:, :, MLA_NOPE:].reshape(MLA_RANK, -1)
    gb = mlstm_gate_b[l]
    return {
        "g1": norm1_g[l].reshape(1, D_MODEL),
        "wml": w[:, 0:o_g].astype(BF16),
        "wgt": w[:, o_g:o_cm].T.astype(BF16),
        "gb_col": gb.reshape(16, 1),
        "wcm": w[:, o_cm:o_cq].astype(BF16),
        "wmla": jnp.concatenate([w[:, o_cq:o_kr], kr128, krsw128], axis=1).astype(BF16),
        "wkr": w_kr.astype(BF16),
        "cmg": cm_norm_g[l].reshape(1, CM_WIDTH),
        "ws": cm_ws[l].astype(BF16),
        "cmb": jnp.repeat(cm_b[l].T, CM_WIDTH // CM_GROUPS, axis=1),
        "qg": mla_q_norm_g[l].reshape(1, MLA_RANK),
        "kvg": mla_kv_norm_g[l].reshape(1, MLA_RANK),
        "wuqa": _pad_heads([(uq_nope, MLA_NOPE), (uq_rope, MLA_ROPE), (None, pad_w)], MLA_HEADS).astype(BF16),
        "wuqb": _pad_heads([(None, MLA_NOPE), (uq_rope_sw, MLA_ROPE), (None, pad_w)], MLA_HEADS).astype(BF16),
        "wk": _pad_heads([(uk, MLA_NOPE), (None, HEAD_PAD - MLA_NOPE)], MLA_HEADS).astype(BF16),
        "wv": uv.astype(BF16),
        "wo": w_out[l].astype(BF16),
        "g2": norm2_g[l].reshape(1, D_MODEL),
        "wq": peer_w_q[l].astype(BF16),
        "sk": peer_subkeys[l].astype(BF16),
    }


def kernel(x_prompt, x_sample, c, cache_mla_ckv, cache_mla_krope, state_mlstm_C, state_mlstm_n, state_mlstm_m, c_ctx, norm1_g, ada_w, ada_b, w_in, mlstm_gate_b, cm_norm_g, cm_ws, cm_b, mla_q_norm_g, mla_w_uq, mla_kv_norm_g, mla_w_ukv, w_out, norm2_g, peer_w_q, peer_subkeys, peer_u, peer_v, final_g):
    x_ctx = x_prompt.reshape(T_CTX, D_MODEL)
    x_smp = x_sample.reshape(T_SMP, D_MODEL)
    cvecs = jnp.concatenate([c_ctx[None, :], c, jnp.zeros((N_MOD_ROWS - 1 - N_SMP_SEQ, D_MODEL), F32)], axis=0)
    mod_all = _ada_call(cvecs, ada_w, ada_b)
    cos_t, sin_t = _rope_tables()
    place = jnp.concatenate([jnp.zeros((MLA_ROPE, MLA_NOPE), F32), jnp.eye(MLA_ROPE, dtype=F32),
                             jnp.zeros((MLA_ROPE, HEAD_PAD - MLA_NOPE - MLA_ROPE), F32)], axis=1).astype(BF16)
    final_g2 = final_g.reshape(1, D_MODEL)
    u_bf, vt_bf = _table_prep_call(peer_u, peer_v)

    ckvs, krs, Cs, ns, ms = [], [], [], [], []
    for l in range(DEPTH):
        lw = _layer_weights(l, norm1_g, w_in, mlstm_gate_b, cm_norm_g, cm_ws, cm_b, mla_q_norm_g, mla_w_uq,
                            mla_kv_norm_g, mla_w_ukv, w_out, norm2_g, peer_w_q, peer_subkeys)
        mod = mod_all[l].reshape(N_MOD_ROWS, 1, 6 * D_MODEL)
        ml, gates_t, cm, ckvn, kr, q, k, v = _proj_call(x_ctx, x_smp, mod, lw, cos_t, sin_t)

        c_blk = jnp.einsum('bdhij,hg->bdhigj', state_mlstm_C[:, l], jnp.eye(ML_HEADS, dtype=F32))
        c0 = jnp.concatenate([jnp.zeros((N_CTX_SEQ, 2, ML_WIDTH, ML_WIDTH), F32),
                              c_blk.reshape(N_SMP_SEQ, 2, ML_WIDTH, ML_WIDTH)], axis=0)
        n0 = jnp.concatenate([jnp.zeros((N_CTX_SEQ, 2, 1, ML_WIDTH), F32),
                              state_mlstm_n[:, l].reshape(N_SMP_SEQ, 2, 1, ML_WIDTH)], axis=0)
        n0 = jnp.pad(n0, ((0, 0), (0, 0), (0, 7), (0, 0)))
        m0 = jnp.concatenate([jnp.zeros((N_CTX_SEQ, 2, ML_HEADS), F32), state_mlstm_m[:, l]], axis=0)
        m0 = jnp.pad(jnp.broadcast_to(m0[..., None], m0.shape + (128,)), ((0, 0), (0, 0), (0, 8 - ML_HEADS), (0, 0)))
        hf, hb, c_fin, n_fin, m_fin = _mlstm_call(ml, gates_t, c0, n0, m0)

        kc, vc = _cache_kv_call(cache_mla_ckv[:, l].reshape(N_SMP_SEQ * PAST_LEN, MLA_RANK),
                                cache_mla_krope[:, l].reshape(N_SMP_SEQ * PAST_LEN, MLA_ROPE),
                                lw["wk"], lw["wv"], place)
        att_ctx = _attn_ctx_call(q, k, v)
        att_smp = _attn_smp_call(q, k, v, kc, vc)

        x1, h2t, st = _out_call(x_ctx, x_smp, mod, ml, hf, hb, cm, att_ctx, att_smp, lw)
        e1, cut, e2, r2 = _topk_call(st)
        experts = functools.partial(_expert_call, h2t, u_bf, vt_bf, l, e1, cut, e2, r2, x1, mod, final_g2)
        if l < DEPTH - 1:
            x_ctx = x_smp = experts(False, 0, T_ALL)
        else:
            x_ctx = experts(True, 0, T_CTX)
            x_smp = experts(True, T_CTX, T_SMP)

        ckvs.append(ckvn[:T_CTX].reshape(N_CTX_SEQ, CTX_LEN, MLA_RANK))
        krs.append(kr[:T_CTX].reshape(N_CTX_SEQ, CTX_LEN, MLA_ROPE))
        Cs.append(jnp.stack([c_fin[:N_CTX_SEQ, :, hd * ML_DIM:(hd + 1) * ML_DIM, hd * ML_DIM:(hd + 1) * ML_DIM]
                             for hd in range(ML_HEADS)], axis=2))
        ns.append(n_fin[:N_CTX_SEQ, :, 0, :].reshape(N_CTX_SEQ, 2, ML_HEADS, ML_DIM))
        ms.append(m_fin[:N_CTX_SEQ, :, 0:ML_HEADS, 0])

    y_prompt = x_ctx.reshape(N_CTX_SEQ, CTX_LEN, D_MODEL)
    y_sample = x_smp.reshape(N_SMP_SEQ, SMP_LEN, D_MODEL)
    return (y_prompt, y_sample, jnp.stack(ckvs, axis=1), jnp.stack(krs, axis=1), jnp.stack(Cs, axis=1),
            jnp.stack(ns, axis=1), jnp.stack(ms, axis=1))
```

```python
import functools

import numpy as np
import jax
import jax.numpy as jnp
from jax import lax
from jax.experimental import pallas as pl
from jax.experimental.pallas import tpu as pltpu

F32 = jnp.float32
BF16 = jnp.bfloat16

D_MODEL = 1024
N_CTX_SEQ = 16
CTX_LEN = 256
N_SMP_SEQ = 4
SMP_LEN = 2048
PAST_LEN = 256
DEPTH = 2
GRID_W = 64
EPS = 1e-6
T_CTX = N_CTX_SEQ * CTX_LEN
T_SMP = N_SMP_SEQ * SMP_LEN
T_ALL = T_CTX + T_SMP
N_MOD_ROWS = 8

ML_HEADS = 4
ML_DIM = 64
ML_WIDTH = 256
CHUNK = 128
CM_GROUPS = 4
CM_WIDTH = 256
MLA_HEADS = 8
MLA_NOPE = 64
MLA_ROPE = 32
MLA_V = 64
MLA_RANK = 256
HEAD_PAD = 128
ROPE_AXIS = 16
ROPE_THETA = 10000.0
QK_SCALE_LOG2E = (MLA_NOPE + MLA_ROPE) ** -0.5 * 1.4426950408889634
PEER_HEADS = 8
PEER_NKEYS = 128
PEER_EXPERTS = PEER_NKEYS * PEER_NKEYS
PEER_TOPK = 16

TM = 256
TN_TOPK = 256
TM_EXP = 512
TE_EXP = 2048
EXP_SUB = 512
TE_PREP = 2048
ATTN_HEADS_IN_FLIGHT = 2
VMEM_LIMIT = 56 * 1024 * 1024

NEG_INF = float("-inf")


def _dot(a, b):
    return jnp.dot(a, b, preferred_element_type=F32)


def _dot_nt(a, b):
    return lax.dot_general(a, b, (((1,), (1,)), ((), ())), preferred_element_type=F32)


def _dot_tn(a, b):
    return lax.dot_general(a, b, (((0,), (0,)), ((), ())), preferred_element_type=F32)


def _split3(a):
    a1 = a.astype(BF16)
    r1 = a - a1.astype(F32)
    a2 = r1.astype(BF16)
    a3 = (r1 - a2.astype(F32)).astype(BF16)
    return a1, a2, a3


def _rms(x, g):
    return x * lax.rsqrt(jnp.mean(x * x, axis=-1, keepdims=True) + EPS) * g


def _sigmoid(x):
    return 1.0 / (1.0 + jnp.exp(-x))


def _log_sigmoid(x):
    return jnp.minimum(x, 0.0) - jnp.log(1.0 + jnp.exp(-jnp.abs(x)))


def _mod_row_of_block(i, rows_per_block):
    ctx_blocks = T_CTX // rows_per_block
    per_seq = SMP_LEN // rows_per_block
    return jnp.where(i < ctx_blocks, 0, 1 + (i - ctx_blocks) // per_seq)


def _ada_kernel(cv_ref, w_ref, b_ref, o_ref):
    cv = cv_ref[...]
    s = cv * _sigmoid(cv)
    w = w_ref[0]
    w1, w2, w3 = _split3(w)
    s1, s2, s3 = _split3(s)
    acc = _dot(s1, w1) + (_dot(s1, w2) + _dot(s2, w1)) + (_dot(s1, w3) + _dot(s2, w2) + _dot(s3, w1))
    o_ref[0] = acc + b_ref[0]


def _ada_call(cvecs, ada_w, ada_b):
    tn = 1024
    return pl.pallas_call(
        _ada_kernel,
        grid=(DEPTH, 6 * D_MODEL // tn),
        in_specs=[
            pl.BlockSpec((N_MOD_ROWS, D_MODEL), lambda l, j: (0, 0)),
            pl.BlockSpec((1, D_MODEL, tn), lambda l, j: (l, 0, j)),
            pl.BlockSpec((1, 1, tn), lambda l, j: (l, 0, j)),
        ],
        out_specs=pl.BlockSpec((1, N_MOD_ROWS, tn), lambda l, j: (l, 0, j)),
        out_shape=jax.ShapeDtypeStruct((DEPTH, N_MOD_ROWS, 6 * D_MODEL), F32),
        compiler_params=pltpu.CompilerParams(dimension_semantics=("parallel", "parallel")),
        name="ada_mod",
    )(cvecs, ada_w, ada_b.reshape(DEPTH, 1, 6 * D_MODEL))


def _x_specs(x_ctx, x_smp):
    ctx_blk = T_CTX // TM
    smp_off = x_smp.shape[0] // TM - T_SMP // TM
    return [pl.BlockSpec((TM, D_MODEL), lambda i: (jnp.minimum(i, ctx_blk - 1), 0)),
            pl.BlockSpec((TM, D_MODEL), lambda i: (jnp.maximum(i - ctx_blk, 0) + smp_off, 0))]


def _select_x(xc_ref, xs_ref):
    return jnp.where(pl.program_id(0) < T_CTX // TM, xc_ref[...], xs_ref[...])


def _proj_kernel(xc_ref, xs_ref, mod_ref, g1_ref, wml_ref, wgt_ref, gbc_ref, wcm_ref, wmla_ref, wkr_ref,
                 cmg_ref, ws_ref, cmb_ref, qg_ref, kvg_ref, wuqa_ref, wuqb_ref, wk_ref, wv_ref, cos_ref, sin_ref,
                 ml_ref, gatest_ref, cm_ref, ckvn_ref, kr_ref, q_ref, k_ref, v_ref):
    x = _select_x(xc_ref, xs_ref)
    mod = mod_ref[0]
    sh1 = mod[:, 0:D_MODEL]
    sc1 = mod[:, D_MODEL:2 * D_MODEL]
    h = _rms(x, g1_ref[...]) * (1.0 + sc1) + sh1
    hb = h.astype(BF16)

    def mlstm_inputs():
        ml_ref[...] = _dot(hb, wml_ref[...])
        yield None
        gatest_ref[...] = _dot_nt(wgt_ref[...], hb) + gbc_ref[...]
        yield None

    def spatial_gating():
        cm = _dot(hb, wcm_ref[...])
        yield None
        u = cm[:, 0:CM_WIDTH]
        vn = _rms(cm[:, CM_WIDTH:2 * CM_WIDTH], cmg_ref[...]).astype(BF16)
        lane_group = lax.broadcasted_iota(jnp.int32, (CHUNK, CM_WIDTH), 1) >> 6
        yield None
        for c in range(TM // CHUNK):
            rows = slice(c * CHUNK, (c + 1) * CHUNK)
            vc = vn[rows]
            mixed = jnp.zeros((CHUNK, CM_WIDTH), F32)
            for g in range(CM_GROUPS):
                mixed = jnp.where(lane_group == g, _dot(ws_ref[g], vc), mixed)
            cm_ref[rows, :] = u[rows] * (mixed + cmb_ref[...])
            yield None

    def latent_attention():
        mla = _dot(hb, wmla_ref[...])
        yield None
        qn = _rms(mla[:, 0:MLA_RANK], qg_ref[...]).astype(BF16)
        ckvn = _rms(mla[:, MLA_RANK:2 * MLA_RANK], kvg_ref[...])
        ckvn_ref[...] = ckvn
        kr_ref[...] = _dot(hb, wkr_ref[...])
        cos = cos_ref[...]
        sin = sin_ref[...]
        kr_rot = mla[:, 2 * MLA_RANK:2 * MLA_RANK + HEAD_PAD] * cos + mla[:, 2 * MLA_RANK + HEAD_PAD:] * sin
        yield None
        qa = _dot(qn, wuqa_ref[...])
        qb = _dot(qn, wuqb_ref[...])
        ckvb = ckvn.astype(BF16)
        kp = _dot(ckvb, wk_ref[...])
        yield None
        for hd in range(MLA_HEADS):
            cols = slice(hd * HEAD_PAD, (hd + 1) * HEAD_PAD)
            q_ref[hd] = ((qa[:, cols] * cos + qb[:, cols] * sin) * QK_SCALE_LOG2E).astype(BF16)
            k_ref[hd] = (kp[:, cols] + kr_rot).astype(BF16)
            if hd % 4 == 3:
                yield None
        v_ref[...] = _dot(ckvb, wv_ref[...]).astype(BF16)
        yield None

    _interleave([mlstm_inputs(), spatial_gating(), latent_attention()])


def _proj_call(x_ctx, x_smp, mod, lw, cos_t, sin_t):
    nblk = T_ALL // TM
    full = lambda shape: pl.BlockSpec(shape, lambda i: (0,) * len(shape))
    rows = lambda w: pl.BlockSpec((TM, w), lambda i: (i, 0))
    in_specs = _x_specs(x_ctx, x_smp) + [
        pl.BlockSpec((1, 1, 6 * D_MODEL), lambda i: (_mod_row_of_block(i, TM), 0, 0)),
        full((1, D_MODEL)),
        full((D_MODEL, 4 * ML_WIDTH)),
        full((16, D_MODEL)),
        full((16, 1)),
        full((D_MODEL, 2 * CM_WIDTH)),
        full((D_MODEL, 2 * MLA_RANK + 2 * HEAD_PAD)),
        full((D_MODEL, MLA_ROPE)),
        full((1, CM_WIDTH)),
        full((CM_GROUPS, CHUNK, CHUNK)),
        full((CHUNK, CM_WIDTH)),
        full((1, MLA_RANK)),
        full((1, MLA_RANK)),
        full((MLA_RANK, MLA_HEADS * HEAD_PAD)),
        full((MLA_RANK, MLA_HEADS * HEAD_PAD)),
        full((MLA_RANK, MLA_HEADS * HEAD_PAD)),
        full((MLA_RANK, MLA_HEADS * MLA_V)),
        rows(HEAD_PAD),
        rows(HEAD_PAD),
    ]
    out_specs = [
        rows(4 * ML_WIDTH),
        pl.BlockSpec((16, TM), lambda i: (0, i)),
        rows(CM_WIDTH),
        rows(MLA_RANK),
        rows(MLA_ROPE),
        pl.BlockSpec((MLA_HEADS, TM, HEAD_PAD), lambda i: (0, i, 0)),
        pl.BlockSpec((MLA_HEADS, TM, HEAD_PAD), lambda i: (0, i, 0)),
        rows(MLA_HEADS * MLA_V),
    ]
    out_shape = [
        jax.ShapeDtypeStruct((T_ALL, 4 * ML_WIDTH), F32),
        jax.ShapeDtypeStruct((16, T_ALL), F32),
        jax.ShapeDtypeStruct((T_ALL, CM_WIDTH), F32),
        jax.ShapeDtypeStruct((T_ALL, MLA_RANK), F32),
        jax.ShapeDtypeStruct((T_ALL, MLA_ROPE), F32),
        jax.ShapeDtypeStruct((MLA_HEADS, T_ALL, HEAD_PAD), BF16),
        jax.ShapeDtypeStruct((MLA_HEADS, T_ALL, HEAD_PAD), BF16),
        jax.ShapeDtypeStruct((T_ALL, MLA_HEADS * MLA_V), BF16),
    ]
    return pl.pallas_call(
        _proj_kernel,
        grid=(nblk,),
        in_specs=in_specs,
        out_specs=out_specs,
        out_shape=out_shape,
        compiler_params=pltpu.CompilerParams(dimension_semantics=("parallel",), vmem_limit_bytes=VMEM_LIMIT),
        name="proj_in",
    )(x_ctx, x_smp, mod, lw["g1"], lw["wml"], lw["wgt"], lw["gb_col"], lw["wcm"], lw["wmla"],
      lw["wkr"], lw["cmg"], lw["ws"], lw["cmb"], lw["qg"], lw["kvg"], lw["wuqa"], lw["wuqb"], lw["wk"], lw["wv"],
      cos_t, sin_t)


def _mlstm_schedule():
    pair, fwd_a, bwd_a, fwd_b, bwd_b, first = [], [], [], [], [], []
    base = 0
    for p in range((N_CTX_SEQ + N_SMP_SEQ) // 2):
        nc = (CTX_LEN if 2 * p < N_CTX_SEQ else SMP_LEN) // CHUNK
        for j in range(nc):
            pair.append(p)
            fwd_a.append(base + j)
            bwd_a.append(base + nc - 1 - j)
            fwd_b.append(base + nc + j)
            bwd_b.append(base + 2 * nc - 1 - j)
            first.append(1 if j == 0 else 0)
        base += 2 * nc
    as_i32 = lambda a: jnp.asarray(np.asarray(a, np.int32))
    return tuple(as_i32(a) for a in (pair, fwd_a, bwd_a, fwd_b, bwd_b, first))


def _scan_cummax(x, direction):
    L = CHUNK
    rows = x.shape[0]
    x = jnp.concatenate([x, x], axis=0)
    lane = lax.broadcasted_iota(jnp.int32, x.shape, 1)
    k = 1
    while k < L:
        if direction == 0:
            shifted = jnp.where(lane >= k, pltpu.roll(x, k, axis=1), NEG_INF)
        else:
            shifted = jnp.where(lane < L - k, pltpu.roll(x, L - k, axis=1), NEG_INF)
        x = jnp.maximum(x, shifted)
        k *= 2
    return x[0:rows]


def _rows_to_lane_cols(rows, eye, rep, pieces):
    x = jnp.concatenate([jnp.broadcast_to(rows[h:h + 1, :], (rep, CHUNK)) for h in range(ML_HEADS)], axis=0)
    out = None
    for _ in range(pieces):
        xb = x.astype(BF16)
        part = _dot_nt(eye, xb)
        out = part if out is None else out + part
        x = x - xb.astype(F32)
    return out


def _per_head_lanes(x512, lane64):
    lo = jnp.where(lane64, x512[:, 0:128], x512[:, 128:256])
    hi = jnp.where(lane64, x512[:, 256:384], x512[:, 384:512])
    return jnp.concatenate([lo, hi], axis=1)


def _mlstm_direction(ml, g_row, direction, slot, c_ref, n_ref, m_ref):
    L = CHUNK
    t_idx = lax.broadcasted_iota(jnp.int32, (L, L), 0)
    s_idx = lax.broadcasted_iota(jnp.int32, (L, L), 1)
    visible = (s_idx <= t_idx) if direction == 0 else (s_idx >= t_idx)
    tri = jnp.where(visible, 1.0, 0.0).astype(BF16)
    eye = jnp.where(s_idx == t_idx, 1.0, 0.0).astype(BF16)
    lane64 = lax.broadcasted_iota(jnp.int32, (1, 128), 1) < ML_DIM
    head_of_lane = lax.broadcasted_iota(jnp.int32, (1, ML_WIDTH), 1) >> 6
    same_head = ((lax.broadcasted_iota(jnp.int32, (ML_WIDTH, ML_WIDTH), 0) >> 6)
                 == (lax.broadcasted_iota(jnp.int32, (ML_WIDTH, ML_WIDTH), 1) >> 6))

    i0 = 8 * direction
    i_row = g_row[i0:i0 + ML_HEADS, :]
    lf_row = _log_sigmoid(g_row[i0 + ML_HEADS:i0 + 2 * ML_HEADS, :])
    r1, r2, r3 = _split3(lf_row)
    b_row = _dot_nt(r1, tri) + _dot_nt(r2, tri) + _dot_nt(r3, tri)
    yield None
    m_rep = m_ref[slot, direction, 0:ML_HEADS, :]
    a_row = i_row - b_row
    g_row_ = jnp.maximum(m_rep, _scan_cummax(a_row, direction))
    yield None
    b_end = jnp.sum(lf_row, axis=1, keepdims=True)
    log_w = b_end - b_row + i_row
    m_new = jnp.maximum(b_end + m_rep, jnp.max(log_w, axis=1, keepdims=True))
    w_k_row = jnp.exp(log_w - m_new)
    decay_rep = jnp.exp(b_end + m_rep - m_new)

    yield None
    g512 = _rows_to_lane_cols(g_row_, eye, 128, 2)
    g_full = _per_head_lanes(g512, lane64)
    yield None
    b_full = _rows_to_lane_cols(b_row, eye, ML_DIM, 2)
    wk_full = _rows_to_lane_cols(w_k_row, eye, ML_DIM, 1)
    m_full = _per_head_lanes(jnp.concatenate([m_rep[h:h + 1, :] for h in range(ML_HEADS)], axis=1), lane64)
    decay_full = _per_head_lanes(jnp.concatenate([decay_rep[h:h + 1, :] for h in range(ML_HEADS)], axis=1),
                                 lane64)

    yield None
    q = ml[:, 0:ML_WIDTH]
    k = ml[:, ML_WIDTH:2 * ML_WIDTH] * (ML_DIM ** -0.5)
    v = ml[:, 2 * ML_WIDTH:3 * ML_WIDTH]
    qb = q.astype(BF16)
    kb = k.astype(BF16)
    vb = v.astype(BF16)
    num = jnp.zeros((L, ML_WIDTH), F32)
    rowsum = jnp.zeros((L, ML_WIDTH), F32)
    for hd in range(ML_HEADS):
        w_intra = jnp.where(visible, jnp.exp(a_row[hd:hd + 1, :] - g512[:, hd * 128:(hd + 1) * 128]), 0.0)
        q_h = jnp.where(head_of_lane == hd, q, 0.0).astype(BF16)
        sw = _dot_nt(q_h, kb) * w_intra
        num = jnp.where(head_of_lane == hd, _dot(sw.astype(BF16), vb), num)
        rowsum = jnp.where(head_of_lane == hd, jnp.sum(sw, axis=-1, keepdims=True), rowsum)
        yield None

    C = c_ref[slot, direction]
    n_row = n_ref[slot, direction, 0:1, :]
    w_inter = jnp.exp(m_full - g_full)
    block_ones = jnp.where(same_head, 1.0, 0.0).astype(BF16)
    qn = _dot((q * n_row).astype(BF16), block_ones)
    num = num + w_inter * _dot(qb, C.astype(BF16))
    den = rowsum + w_inter * qn
    h_out = num / jnp.maximum(jnp.abs(den), jnp.exp(-(b_full + g_full)))

    kw = wk_full * k
    c_ref[slot, direction] = decay_full * C + jnp.where(same_head, _dot_tn(kw.astype(BF16), vb), 0.0)
    n_ref[slot, direction, 0:1, :] = decay_full * n_row + jnp.sum(kw, axis=0, keepdims=True)
    m_ref[slot, direction, 0:ML_HEADS, :] = m_new
    yield h_out


def _interleave(stage_generators):
    results = [None] * len(stage_generators)
    live = list(range(len(stage_generators)))
    while live:
        for idx in list(live):
            try:
                value = next(stage_generators[idx])
                if value is not None:
                    results[idx] = value
            except StopIteration:
                live.remove(idx)
    return results


def _mlstm_kernel(pair_ref, fa_ref, ba_ref, fb_ref, bb_ref, first_ref,
                  mlfa_ref, mlba_ref, mlfb_ref, mlbb_ref, grfa_ref, grba_ref, grfb_ref, grbb_ref,
                  c0_ref, n0_ref, m0_ref, hf_ref, hb_ref, c_ref, n_ref, m_ref):
    step = pl.program_id(0)

    @pl.when(first_ref[step] == 1)
    def _():
        c_ref[...] = c0_ref[...]
        n_ref[...] = n0_ref[...]
        m_ref[...] = m0_ref[...]

    hfa, hba, hfb, hbb = _interleave([
        _mlstm_direction(mlfa_ref[...], grfa_ref[...], 0, 0, c_ref, n_ref, m_ref),
        _mlstm_direction(mlba_ref[...], grba_ref[...], 1, 0, c_ref, n_ref, m_ref),
        _mlstm_direction(mlfb_ref[...], grfb_ref[...], 0, 1, c_ref, n_ref, m_ref),
        _mlstm_direction(mlbb_ref[...], grbb_ref[...], 1, 1, c_ref, n_ref, m_ref)])
    hf_ref[0, 0] = hfa
    hf_ref[0, 1] = hfb
    hb_ref[0, 0] = hba
    hb_ref[0, 1] = hbb


def _mlstm_call(ml, gates_t, c0, n0, m0):
    sched = _mlstm_schedule()
    nseq = N_CTX_SEQ + N_SMP_SEQ
    nsteps = int(sched[0].shape[0])
    chunk_rows = lambda which: pl.BlockSpec((CHUNK, 4 * ML_WIDTH), lambda i, *s: (s[which][i], 0))
    gate_cols = lambda which: pl.BlockSpec((16, CHUNK), lambda i, *s: (0, s[which][i]))
    c_spec = pl.BlockSpec((2, 2, ML_WIDTH, ML_WIDTH), lambda i, *s: (s[0][i], 0, 0, 0))
    n_spec = pl.BlockSpec((2, 2, 8, ML_WIDTH), lambda i, *s: (s[0][i], 0, 0, 0))
    m_spec = pl.BlockSpec((2, 2, 8, 128), lambda i, *s: (s[0][i], 0, 0, 0))
    h_spec = pl.BlockSpec((1, 2, CHUNK, ML_WIDTH), lambda i, *s: (i, 0, 0, 0))
    grid_spec = pltpu.PrefetchScalarGridSpec(
        num_scalar_prefetch=6,
        grid=(nsteps,),
        in_specs=[chunk_rows(1), chunk_rows(2), chunk_rows(3), chunk_rows(4),
                  gate_cols(1), gate_cols(2), gate_cols(3), gate_cols(4), c_spec, n_spec, m_spec],
        out_specs=[h_spec, h_spec, c_spec, n_spec, m_spec],
    )
    return pl.pallas_call(
        _mlstm_kernel,
        grid_spec=grid_spec,
        out_shape=[
            jax.ShapeDtypeStruct((nsteps, 2, CHUNK, ML_WIDTH), F32),
            jax.ShapeDtypeStruct((nsteps, 2, CHUNK, ML_WIDTH), F32),
            jax.ShapeDtypeStruct((nseq, 2, ML_WIDTH, ML_WIDTH), F32),
            jax.ShapeDtypeStruct((nseq, 2, 8, ML_WIDTH), F32),
            jax.ShapeDtypeStruct((nseq, 2, 8, 128), F32),
        ],
        compiler_params=pltpu.CompilerParams(dimension_semantics=("arbitrary",), vmem_limit_bytes=VMEM_LIMIT),
        name="mlstm",
    )(*sched, ml, ml, ml, ml, gates_t, gates_t, gates_t, gates_t, c0, n0, m0)


def _h_pair_specs():
    ctx_blk = T_CTX // TM
    blk_per_seq = SMP_LEN // TM
    ctx_steps = N_CTX_SEQ // 2 * (CTX_LEN // CHUNK)

    def place(i):
        i_s = jnp.maximum(i - ctx_blk, 0)
        seq_s = i_s // blk_per_seq
        jb = i_s % blk_per_seq
        is_ctx = i < ctx_blk
        slot = jnp.where(is_ctx, i % 2, seq_s % 2)
        base = (ctx_steps + (seq_s // 2) * (SMP_LEN // CHUNK)) // 2
        fwd = jnp.where(is_ctx, i // 2, base + jb)
        bwd = jnp.where(is_ctx, i // 2, base + blk_per_seq - 1 - jb)
        return fwd, bwd, slot

    block = (2, None, CHUNK, ML_WIDTH)
    return (pl.BlockSpec(block, lambda i: (place(i)[0], place(i)[2], 0, 0)),
            pl.BlockSpec(block, lambda i: (place(i)[1], place(i)[2], 0, 0)))


def _cache_kv_kernel(ckv_ref, kr_ref, wk_ref, wv_ref, place_ref, k_ref, v_ref):
    ckvb = ckv_ref[...].astype(BF16)
    kp = _dot(ckvb, wk_ref[...])
    kr128 = _dot(kr_ref[...].astype(BF16), place_ref[...])
    for hd in range(MLA_HEADS):
        k_ref[hd] = (kp[:, hd * HEAD_PAD:(hd + 1) * HEAD_PAD] + kr128).astype(BF16)
    v_ref[...] = _dot(ckvb, wv_ref[...]).astype(BF16)


def _cache_kv_call(ckv, kr, wk, wv, place):
    n = N_SMP_SEQ * PAST_LEN
    tb = PAST_LEN
    return pl.pallas_call(
        _cache_kv_kernel,
        grid=(n // tb,),
        in_specs=[
            pl.BlockSpec((tb, MLA_RANK), lambda i: (i, 0)),
            pl.BlockSpec((tb, MLA_ROPE), lambda i: (i, 0)),
            pl.BlockSpec((MLA_RANK, MLA_HEADS * HEAD_PAD), lambda i: (0, 0)),
            pl.BlockSpec((MLA_RANK, MLA_HEADS * MLA_V), lambda i: (0, 0)),
            pl.BlockSpec((MLA_ROPE, HEAD_PAD), lambda i: (0, 0)),
        ],
        out_specs=[
            pl.BlockSpec((MLA_HEADS, tb, HEAD_PAD), lambda i: (0, i, 0)),
            pl.BlockSpec((tb, MLA_HEADS * MLA_V), lambda i: (i, 0)),
        ],
        out_shape=[
            jax.ShapeDtypeStruct((MLA_HEADS, n, HEAD_PAD), BF16),
            jax.ShapeDtypeStruct((n, MLA_HEADS * MLA_V), BF16),
        ],
        compiler_params=pltpu.CompilerParams(dimension_semantics=("parallel",)),
        name="cache_kv",
    )(ckv, kr, wk, wv, place)


def _attn_kernel(*refs, has_cache):
    if has_cache:
        q_ref, kn_ref, vn_ref, kc_ref, vc_ref, o_ref = refs
    else:
        q_ref, kn_ref, vn_ref, o_ref = refs

    def head(hd):
        lanes = slice(hd * MLA_V, (hd + 1) * MLA_V)
        q = q_ref[hd]
        s_n = _dot_nt(q, kn_ref[hd])
        if has_cache:
            s_c = _dot_nt(q, kc_ref[hd])
        yield None
        m = jnp.max(s_n, axis=-1, keepdims=True)
        if has_cache:
            m = jnp.maximum(m, jnp.max(s_c, axis=-1, keepdims=True))
        yield None
        p_n = jnp.exp2(s_n - m)
        l = jnp.sum(p_n, axis=-1, keepdims=True)
        o = _dot(p_n.astype(BF16), vn_ref[:, lanes])
        if has_cache:
            p_c = jnp.exp2(s_c - m)
            l = l + jnp.sum(p_c, axis=-1, keepdims=True)
            o = o + _dot(p_c.astype(BF16), vc_ref[:, lanes])
        yield o / l

    outs = []
    for hd in range(0, MLA_HEADS, ATTN_HEADS_IN_FLIGHT):
        outs += _interleave([head(hd + i) for i in range(ATTN_HEADS_IN_FLIGHT)])
    o_ref[...] = jnp.concatenate(outs, axis=-1)


def _attn_ctx_call(q, k, v):
    tq = CTX_LEN
    return pl.pallas_call(
        functools.partial(_attn_kernel, has_cache=False),
        grid=(N_CTX_SEQ,),
        in_specs=[
            pl.BlockSpec((MLA_HEADS, tq, HEAD_PAD), lambda s: (0, s, 0)),
            pl.BlockSpec((MLA_HEADS, tq, HEAD_PAD), lambda s: (0, s, 0)),
            pl.BlockSpec((tq, MLA_HEADS * MLA_V), lambda s: (s, 0)),
        ],
        out_specs=pl.BlockSpec((tq, MLA_HEADS * MLA_V), lambda s: (s, 0)),
        out_shape=jax.ShapeDtypeStruct((T_CTX, MLA_HEADS * MLA_V), F32),
        compiler_params=pltpu.CompilerParams(dimension_semantics=("parallel",), vmem_limit_bytes=VMEM_LIMIT),
        name="attn_ctx",
    )(q, k, v)


def _attn_smp_call(q, k, v, kc, vc):
    tq = 256
    qb_per_seq = SMP_LEN // tq
    ctx_qb = T_CTX // tq
    ctx_kb = T_CTX // SMP_LEN
    return pl.pallas_call(
        functools.partial(_attn_kernel, has_cache=True),
        grid=(N_SMP_SEQ, qb_per_seq),
        in_specs=[
            pl.BlockSpec((MLA_HEADS, tq, HEAD_PAD), lambda b, i: (0, ctx_qb + b * qb_per_seq + i, 0)),
            pl.BlockSpec((MLA_HEADS, SMP_LEN, HEAD_PAD), lambda b, i: (0, ctx_kb + b, 0)),
            pl.BlockSpec((SMP_LEN, MLA_HEADS * MLA_V), lambda b, i: (ctx_kb + b, 0)),
            pl.BlockSpec((MLA_HEADS, PAST_LEN, HEAD_PAD), lambda b, i: (0, b, 0)),
            pl.BlockSpec((PAST_LEN, MLA_HEADS * MLA_V), lambda b, i: (b, 0)),
        ],
        out_specs=pl.BlockSpec((tq, MLA_HEADS * MLA_V), lambda b, i: (b * qb_per_seq + i, 0)),
        out_shape=jax.ShapeDtypeStruct((T_SMP, MLA_HEADS * MLA_V), F32),
        compiler_params=pltpu.CompilerParams(dimension_semantics=("parallel", "parallel"),
                                             vmem_limit_bytes=VMEM_LIMIT),
        name="attn_smp",
    )(q, k, v, kc, vc)


def _out_kernel(xc_ref, xs_ref, mod_ref, ml_ref, hf_ref, hb_ref, cm_ref, attc_ref, atts_ref, wo_ref, g2_ref,
                wq_ref, sk_ref, x1_ref, h2t_ref, st_ref):
    is_ctx = pl.program_id(0) < T_CTX // TM
    att = jnp.where(is_ctx, attc_ref[...], atts_ref[...])
    mod = mod_ref[0]
    g1 = mod[:, 2 * D_MODEL:3 * D_MODEL]
    sh2 = mod[:, 3 * D_MODEL:4 * D_MODEL]
    sc2 = mod[:, 4 * D_MODEL:5 * D_MODEL]
    h_sum = jnp.concatenate([hf_ref[0] + hb_ref[1], hf_ref[1] + hb_ref[0]], axis=0)
    mlo = _sigmoid(ml_ref[:, 3 * ML_WIDTH:4 * ML_WIDTH]) * h_sum
    mix = (_dot(mlo.astype(BF16), wo_ref[0:ML_WIDTH, :])
           + _dot(cm_ref[...].astype(BF16), wo_ref[ML_WIDTH:ML_WIDTH + CM_WIDTH, :])
           + _dot(att.astype(BF16), wo_ref[ML_WIDTH + CM_WIDTH:, :]))
    x1 = _select_x(xc_ref, xs_ref) + g1 * mix
    x1_ref[...] = x1
    h2f = _rms(x1, g2_ref[...]) * (1.0 + sc2) + sh2
    h2t_ref[...] = h2f.T.astype(BF16)
    qp = _dot(h2f.astype(BF16), wq_ref[...]).astype(BF16)
    for hh in range(2 * PEER_HEADS):
        st_ref[hh] = _dot_nt(sk_ref[hh % 2], qp[:, hh * 128:(hh + 1) * 128])


def _out_call(x_ctx, x_smp, mod, ml, hf, hb, cm, att_ctx, att_smp, lw):
    nblk = T_ALL // TM
    ctx_blk = T_CTX // TM
    rows = lambda w: pl.BlockSpec((TM, w), lambda i: (i, 0))
    full = lambda shape: pl.BlockSpec(shape, lambda i: (0,) * len(shape))
    att_w = MLA_HEADS * MLA_V
    attc_spec = pl.BlockSpec((TM, att_w), lambda i: (jnp.minimum(i, ctx_blk - 1), 0))
    atts_spec = pl.BlockSpec((TM, att_w), lambda i: (jnp.maximum(i - ctx_blk, 0), 0))
    return pl.pallas_call(
        _out_kernel,
        grid=(nblk,),
        in_specs=_x_specs(x_ctx, x_smp) + [
            pl.BlockSpec((1, 1, 6 * D_MODEL), lambda i: (_mod_row_of_block(i, TM), 0, 0)),
            rows(4 * ML_WIDTH), *_h_pair_specs(), rows(CM_WIDTH), attc_spec, atts_spec,
            full((D_MODEL, D_MODEL)), full((1, D_MODEL)), full((D_MODEL, 2 * PEER_HEADS * 128)),
            full((2, PEER_NKEYS, 128)),
        ],
        out_specs=[rows(D_MODEL), pl.BlockSpec((D_MODEL, TM), lambda i: (0, i)),
                   pl.BlockSpec((2 * PEER_HEADS, PEER_NKEYS, TM), lambda i: (0, 0, i))],
        out_shape=[
            jax.ShapeDtypeStruct((T_ALL, D_MODEL), F32),
            jax.ShapeDtypeStruct((D_MODEL, T_ALL), BF16),
            jax.ShapeDtypeStruct((2 * PEER_HEADS, PEER_NKEYS, T_ALL), F32),
        ],
        compiler_params=pltpu.CompilerParams(dimension_semantics=("parallel",), vmem_limit_bytes=VMEM_LIMIT),
        name="proj_out",
    )(x_ctx, x_smp, mod, ml, hf, hb, cm, att_ctx, att_smp, lw["wo"], lw["g2"], lw["wq"], lw["sk"])


def _sorting_network_pairs(n):
    pairs = []
    p = 1
    while p < n:
        k = p
        while k >= 1:
            for j in range(k % p, n - k, 2 * k):
                for i in range(min(k, n - j - k)):
                    if (i + j) // (2 * p) == (i + j + k) // (2 * p):
                        pairs.append((i + j, i + j + k))
            k //= 2
        p *= 2
    return pairs


_SORT16_PAIRS = _sorting_network_pairs(PEER_TOPK)


def _pop16(lists, singles=None):
    lists = list(lists)
    vals = []
    for k in range(PEER_TOPK):
        heads = lists[0] if singles is None else jnp.maximum(lists[0], singles)
        m = jnp.max(heads, axis=0, keepdims=True)
        vals.append(m)
        if k == PEER_TOPK - 1:
            break
        hit = lists[0] == m
        for i in range(PEER_TOPK - 1 - k):
            lists[i] = jnp.where(hit, lists[i + 1], lists[i])
        if singles is not None:
            singles = jnp.where(singles == m, NEG_INF, singles)
    return vals


def _top16_rows(s):
    tiles = [s[8 * j:8 * j + 8] for j in range(s.shape[0] // 8)]
    assert len(tiles) == PEER_TOPK
    for i, j in _SORT16_PAIRS:
        tiles[i], tiles[j] = jnp.maximum(tiles[i], tiles[j]), jnp.minimum(tiles[i], tiles[j])
    return _pop16(tiles)


def _count_steps(x, thresholds, below):
    r = jnp.zeros(x.shape, F32)
    for q, t in enumerate(thresholds):
        r = jnp.where((x < t) if below else (x >= t), float(q + 1), r)
    return r


def _pack_rows_bf16(x):
    return pltpu.bitcast(_pack_rows_words(x), BF16)


def _pack_rows_words(x):
    r, n = x.shape
    x4 = x.reshape(r // 16, 2, 8, n)
    lo = x4[:, 0].reshape(r // 2, n)
    hi = x4[:, 1].reshape(r // 2, n)
    return pltpu.pack_elementwise([lo, hi], packed_dtype=BF16)


def _dup_bf16_words(x):
    return pltpu.pack_elementwise([x, x], packed_dtype=BF16)


def _rows_to_array(rows, row_iota):
    arr = jnp.zeros(row_iota.shape, F32)
    for i, r in enumerate(rows):
        arr = jnp.where(row_iota == i, r, arr)
    return arr


def _topk_kernel(st_ref, e1_ref, cut_ref, e2_ref, r2_ref):
    n = st_ref.shape[-1]
    row8 = lax.broadcasted_iota(jnp.int32, (8, n), 0)

    def head(hd, carry):
        s1 = st_ref[2 * hd]
        s2 = st_ref[2 * hd + 1]
        v1 = _top16_rows(s1)
        v2 = _top16_rows(s2)
        v1_top = _rows_to_array(v1[0:8], row8)
        v1_bot = _rows_to_array(v1[8:PEER_TOPK], row8)
        c = _pop16([v1_top + v2[q] for q in range(PEER_TOPK)], singles=v1_bot + v2[0])
        tau = c[PEER_TOPK - 1]
        z = jnp.zeros((1, n), F32)
        for ck in c:
            z = z + jnp.exp(ck - c[0])

        def smallest_qualifying(v1_rows, q):
            return jnp.min(jnp.where(v1_rows + v2[q] >= tau, v1_rows, jnp.inf), axis=0, keepdims=True)

        sigma = [jnp.minimum(smallest_qualifying(v1_top, 0), smallest_qualifying(v1_bot, 0))]
        sigma += [smallest_qualifying(v1_top, q) for q in range(1, 8)]
        extra = jnp.zeros((1, n), F32)
        last_selected = v2[7]
        for q in range(8, PEER_TOPK):
            reached = v1[0] + v2[q] >= tau
            extra = extra + jnp.where(reached, 1.0, 0.0)
            last_selected = jnp.where(reached, v2[q], last_selected)
        rank2 = jnp.where(s2 < last_selected, 9.0, _count_steps(s2, v2[0:8], below=True))
        cut = _count_steps(s1, sigma, below=False) + jnp.where(s1 >= v1[0], jnp.minimum(extra, 1.0), 0.0)
        e1_ref[hd] = _dup_bf16_words(jnp.exp(s1 - v1[0]) / z)
        cut_ref[hd] = _dup_bf16_words(cut)
        e2_ref[hd] = _pack_rows_words(jnp.exp(s2 - v2[0]))
        r2_ref[hd] = _pack_rows_words(rank2)
        return carry

    lax.fori_loop(0, PEER_HEADS, head, 0)


def _topk_call(st):
    tn = TN_TOPK
    spec = pl.BlockSpec((PEER_HEADS, PEER_NKEYS, tn), lambda i: (0, 0, i))
    word_spec = pl.BlockSpec((PEER_HEADS, PEER_NKEYS // 2, tn), lambda i: (0, 0, i))
    stat = lambda dt: jax.ShapeDtypeStruct((PEER_HEADS, PEER_NKEYS, T_ALL), dt)
    words = jax.ShapeDtypeStruct((PEER_HEADS, PEER_NKEYS // 2, T_ALL), jnp.uint32)
    return pl.pallas_call(
        _topk_kernel,
        grid=(T_ALL // tn,),
        in_specs=[pl.BlockSpec((2 * PEER_HEADS, PEER_NKEYS, tn), lambda i: (0, 0, i))],
        out_specs=[spec, spec, word_spec, word_spec],
        out_shape=[stat(jnp.uint32), stat(jnp.uint32), words, words],
        compiler_params=pltpu.CompilerParams(dimension_semantics=("parallel",), vmem_limit_bytes=VMEM_LIMIT),
        name="peer_topk",
    )(st)


def _row_tile_bf16(row):
    blk = pltpu.bitcast(jnp.broadcast_to(row, (8, row.shape[1])), BF16)
    return jnp.concatenate([blk] * (PEER_NKEYS // 16), axis=0)


def _gelu_tanh_bf16(x):
    log2e = 1.4426950408889634
    w = x * (-1.5957691216057308 * log2e - (0.07135481627159584 * log2e) * (x * x))
    return x / (1.0 + jnp.exp2(w))


def _expert_kernel(h2t_ref, u_ref, vt_ref, e1_ref, cut_ref, e2_ref, r2_ref, x1_ref, mod_ref, fg_ref,
                   o_ref, acc_ref, a_ref, *, final_norm):
    j = pl.program_id(1)
    n_tiles = pl.num_programs(1)
    tm = h2t_ref.shape[1]
    a_per_tile = TE_EXP // PEER_NKEYS

    @pl.when(j == 0)
    def _():
        acc_ref[...] = jnp.zeros_like(acc_ref)

    def gate_tile(ai):
        a = j * a_per_tile + ai
        gate = None
        for hd in range(PEER_HEADS):
            cut_a = _row_tile_bf16(cut_ref[hd, pl.ds(a, 1), :])
            e1_a = _row_tile_bf16(e1_ref[hd, pl.ds(a, 1), :])
            r2 = pltpu.bitcast(r2_ref[hd], BF16)
            e2 = pltpu.bitcast(e2_ref[hd], BF16)
            term = jnp.where(r2 < cut_a, e2, jnp.zeros((), BF16)) * e1_a
            gate = term if gate is None else gate + term
        return gate

    a_per_sub = EXP_SUB // PEER_NKEYS
    n_sub = TE_EXP // EXP_SUB
    s = [None] * n_sub
    gates = [None] * n_sub
    s[0] = _dot(u_ref[0:EXP_SUB, :], h2t_ref[...])
    gates[0] = [gate_tile(ai) for ai in range(a_per_sub)]
    for sub in range(n_sub):
        if sub + 1 < n_sub:
            s[sub + 1] = _dot(u_ref[(sub + 1) * EXP_SUB:(sub + 2) * EXP_SUB, :], h2t_ref[...])
            gates[sub + 1] = [gate_tile((sub + 1) * a_per_sub + i) for i in range(a_per_sub)]
        g = _gelu_tanh_bf16(_pack_rows_bf16(s[sub]))
        for i in range(a_per_sub):
            r0 = sub * EXP_SUB + i * PEER_NKEYS
            a_ref[r0:r0 + PEER_NKEYS, :] = gates[sub][i] * g[i * PEER_NKEYS:(i + 1) * PEER_NKEYS, :]
    acc_ref[...] += _dot(vt_ref[...], a_ref[...])

    @pl.when(j == n_tiles - 1)
    def _():
        g2 = mod_ref[0][:, 5 * D_MODEL:6 * D_MODEL]
        y = x1_ref[...] + g2 * acc_ref[...].T
        if final_norm:
            y = _rms(y, fg_ref[...])
        o_ref[...] = y


def _expert_call(h2t, u_bf, vt_bf, layer, e1, cut, e2, r2, x1, mod, final_g, final_norm, tok_start, tok_count):
    tm, te = TM_EXP, TE_EXP
    b0 = tok_start // tm
    n_tiles = PEER_EXPERTS // te
    stat = pl.BlockSpec((PEER_HEADS, PEER_NKEYS, tm), lambda i, j: (0, 0, b0 + i))
    stat_words = pl.BlockSpec((PEER_HEADS, PEER_NKEYS // 2, tm), lambda i, j: (0, 0, b0 + i))
    return pl.pallas_call(
        functools.partial(_expert_kernel, final_norm=final_norm),
        grid=(tok_count // tm, n_tiles),
        in_specs=[
            pl.BlockSpec((D_MODEL, tm), lambda i, j: (0, b0 + i)),
            pl.BlockSpec((None, te, D_MODEL), lambda i, j: (layer, j, 0)),
            pl.BlockSpec((None, D_MODEL, te), lambda i, j: (layer, 0, j)),
            stat, stat, stat_words, stat_words,
            pl.BlockSpec((tm, D_MODEL), lambda i, j: (b0 + i, 0)),
            pl.BlockSpec((1, 1, 6 * D_MODEL), lambda i, j: (_mod_row_of_block(b0 + i, tm), 0, 0)),
            pl.BlockSpec((1, D_MODEL), lambda i, j: (0, 0)),
        ],
        out_specs=pl.BlockSpec((tm, D_MODEL), lambda i, j: (i, 0)),
        out_shape=jax.ShapeDtypeStruct((tok_count, D_MODEL), F32),
        scratch_shapes=[
            pltpu.VMEM((D_MODEL, tm), F32),
            pltpu.VMEM((te, tm), BF16),
        ],
        compiler_params=pltpu.CompilerParams(dimension_semantics=("parallel", "arbitrary"),
                                             vmem_limit_bytes=VMEM_LIMIT),
        name="peer_experts",
    )(h2t, u_bf, vt_bf, e1, cut, e2, r2, x1, mod, final_g)


def _u_prep_kernel(u_ref, o_ref):
    o_ref[...] = u_ref[...].astype(BF16)


def _v_prep_kernel(v_ref, o_ref):
    o_ref[...] = _pack_rows_bf16(v_ref[...]).T


def _table_prep_call(peer_u, peer_v):
    te = TE_PREP
    grid = (DEPTH, PEER_EXPERTS // te)
    rows = pl.BlockSpec((None, te, D_MODEL), lambda l, j: (l, j, 0))
    params = pltpu.CompilerParams(dimension_semantics=("parallel", "parallel"), vmem_limit_bytes=VMEM_LIMIT)
    u_bf = pl.pallas_call(
        _u_prep_kernel, grid=grid, in_specs=[rows], out_specs=rows,
        out_shape=jax.ShapeDtypeStruct((DEPTH, PEER_EXPERTS, D_MODEL), BF16),
        compiler_params=params, name="peer_u_prep")(peer_u)
    vt_bf = pl.pallas_call(
        _v_prep_kernel, grid=grid, in_specs=[rows],
        out_specs=pl.BlockSpec((None, D_MODEL, te), lambda l, j: (l, 0, j)),
        out_shape=jax.ShapeDtypeStruct((DEPTH, D_MODEL, PEER_EXPERTS), BF16),
        compiler_params=params, name="peer_v_prep")(peer_v)
    return u_bf, vt_bf


def _rope_swap_cols(w):
    return jnp.concatenate([-w[:, 8:16], w[:, 0:8], -w[:, 24:32], w[:, 16:24]], axis=1)


def _pad_heads(parts, n_heads):
    k = next(p[0].shape[0] for p in parts if p[0] is not None)
    cols = []
    for hd in range(n_heads):
        for arr, w in parts:
            cols.append(jnp.zeros((k, w), F32) if arr is None else arr[:, hd * w:(hd + 1) * w])
    return jnp.concatenate(cols, axis=1)


def _rope_tables():
    pos = np.arange(SMP_LEN)
    freqs = ROPE_THETA ** (-np.arange(0, ROPE_AXIS, 2, dtype=np.float32) / ROPE_AXIS)
    ang_r = (pos // GRID_W).astype(np.float32)[:, None] * freqs
    ang_c = (pos % GRID_W).astype(np.float32)[:, None] * freqs
    ang = jnp.asarray(np.concatenate([ang_r, ang_r, ang_c, ang_c], axis=1).astype(np.float32))
    cos32 = jnp.cos(ang)
    sin32 = jnp.sin(ang)
    ones = jnp.ones((SMP_LEN, MLA_NOPE), F32)
    cos_s = jnp.concatenate([ones, cos32, ones[:, :HEAD_PAD - MLA_NOPE - MLA_ROPE]], axis=1)
    sin_s = jnp.concatenate([0 * ones, sin32, 0 * ones[:, :HEAD_PAD - MLA_NOPE - MLA_ROPE]], axis=1)
    cos_t = jnp.concatenate([jnp.ones((T_CTX, HEAD_PAD), F32)] + [cos_s] * N_SMP_SEQ, axis=0)
    sin_t = jnp.concatenate([jnp.zeros((T_CTX, HEAD_PAD), F32)] + [sin_s] * N_SMP_SEQ, axis=0)
    return cos_t, sin_t


def _layer_weights(l, norm1_g, w_in, mlstm_gate_b, cm_norm_g, cm_ws, cm_b, mla_q_norm_g, mla_w_uq, mla_kv_norm_g,
                   mla_w_ukv, w_out, norm2_g, peer_w_q, peer_subkeys):
    w = w_in[l]
    o_g = 4 * ML_WIDTH
    o_cm = o_g + 16
    o_cq = o_cm + 2 * CM_WIDTH
    o_ckv = o_cq + MLA_RANK
    o_kr = o_ckv + MLA_RANK
    w_kr = w[:, o_kr:o_kr + MLA_ROPE]
    zeros_l = jnp.zeros((D_MODEL, MLA_NOPE), F32)
    zeros_r = jnp.zeros((D_MODEL, HEAD_PAD - MLA_NOPE - MLA_ROPE), F32)
    kr128 = jnp.concatenate([zeros_l, w_kr, zeros_r], axis=1)
    krsw128 = jnp.concatenate([zeros_l, _rope_swap_cols(w_kr), zeros_r], axis=1)
    uq = mla_w_uq[l].reshape(MLA_RANK, MLA_HEADS, MLA_NOPE + MLA_ROPE)
    uq_nope = uq[:, :, :MLA_NOPE].reshape(MLA_RANK, -1)
    uq_rope = uq[:, :, MLA_NOPE:].reshape(MLA_RANK, -1)
    uq_rope_sw = jnp.concatenate(
        [_rope_swap_cols(uq_rope[:, hd * MLA_ROPE:(hd + 1) * MLA_ROPE]) for hd in range(MLA_HEADS)], axis=1)
    pad_w = HEAD_PAD - MLA_NOPE - MLA_ROPE
    ukv = mla_w_ukv[l].reshape(MLA_RANK, MLA_HEADS, MLA_NOPE + MLA_V)
    uk = ukv[:, :, :MLA_NOPE].reshape(MLA_RANK, -1)
    uv = ukv[:, :, MLA_NOPE:].reshape(MLA_RANK, -1)
    gb = mlstm_gate_b[l]
    return {
        "g1": norm1_g[l].reshape(1, D_MODEL),
        "wml": w[:, 0:o_g].astype(BF16),
        "wgt": w[:, o_g:o_cm].T.astype(BF16),
        "gb_col": gb.reshape(16, 1),
        "wcm": w[:, o_cm:o_cq].astype(BF16),
        "wmla": jnp.concatenate([w[:, o_cq:o_kr], kr128, krsw128], axis=1).astype(BF16),
        "wkr": w_kr.astype(BF16),
        "cmg": cm_norm_g[l].reshape(1, CM_WIDTH),
        "ws": cm_ws[l].astype(BF16),
        "cmb": jnp.repeat(cm_b[l].T, CM_WIDTH // CM_GROUPS, axis=1),
        "qg": mla_q_norm_g[l].reshape(1, MLA_RANK),
        "kvg": mla_kv_norm_g[l].reshape(1, MLA_RANK),
        "wuqa": _pad_heads([(uq_nope, MLA_NOPE), (uq_rope, MLA_ROPE), (None, pad_w)], MLA_HEADS).astype(BF16),
        "wuqb": _pad_heads([(None, MLA_NOPE), (uq_rope_sw, MLA_ROPE), (None, pad_w)], MLA_HEADS).astype(BF16),
        "wk": _pad_heads([(uk, MLA_NOPE), (None, HEAD_PAD - MLA_NOPE)], MLA_HEADS).astype(BF16),
        "wv": uv.astype(BF16),
        "wo": w_out[l].astype(BF16),
        "g2": norm2_g[l].reshape(1, D_MODEL),
        "wq": peer_w_q[l].astype(BF16),
        "sk": peer_subkeys[l].astype(BF16),
    }


def kernel(x_prompt, x_sample, c, cache_mla_ckv, cache_mla_krope, state_mlstm_C, state_mlstm_n, state_mlstm_m, c_ctx, norm1_g, ada_w, ada_b, w_in, mlstm_gate_b, cm_norm_g, cm_ws, cm_b, mla_q_norm_g, mla_w_uq, mla_kv_norm_g, mla_w_ukv, w_out, norm2_g, peer_w_q, peer_subkeys, peer_u, peer_v, final_g):
    x_ctx = x_prompt.reshape(T_CTX, D_MODEL)
    x_smp = x_sample.reshape(T_SMP, D_MODEL)
    cvecs = jnp.concatenate([c_ctx[None, :], c, jnp.zeros((N_MOD_ROWS - 1 - N_SMP_SEQ, D_MODEL), F32)], axis=0)
    mod_all = _ada_call(cvecs, ada_w, ada_b)
    cos_t, sin_t = _rope_tables()
    place = jnp.concatenate([jnp.zeros((MLA_ROPE, MLA_NOPE), F32), jnp.eye(MLA_ROPE, dtype=F32),
                             jnp.zeros((MLA_ROPE, HEAD_PAD - MLA_NOPE - MLA_ROPE), F32)], axis=1).astype(BF16)
    final_g2 = final_g.reshape(1, D_MODEL)
    u_bf, vt_bf = _table_prep_call(peer_u, peer_v)

    ckvs, krs, Cs, ns, ms = [], [], [], [], []
    for l in range(DEPTH):
        lw = _layer_weights(l, norm1_g, w_in, mlstm_gate_b, cm_norm_g, cm_ws, cm_b, mla_q_norm_g, mla_w_uq,
                            mla_kv_norm_g, mla_w_ukv, w_out, norm2_g, peer_w_q, peer_subkeys)
        mod = mod_all[l].reshape(N_MOD_ROWS, 1, 6 * D_MODEL)
        ml, gates_t, cm, ckvn, kr, q, k, v = _proj_call(x_ctx, x_smp, mod, lw, cos_t, sin_t)

        c_blk = jnp.einsum('bdhij,hg->bdhigj', state_mlstm_C[:, l], jnp.eye(ML_HEADS, dtype=F32))
        c0 = jnp.concatenate([jnp.zeros((N_CTX_SEQ, 2, ML_WIDTH, ML_WIDTH), F32),
                              c_blk.reshape(N_SMP_SEQ, 2, ML_WIDTH, ML_WIDTH)], axis=0)
        n0 = jnp.concatenate([jnp.zeros((N_CTX_SEQ, 2, 1, ML_WIDTH), F32),
                              state_mlstm_n[:, l].reshape(N_SMP_SEQ, 2, 1, ML_WIDTH)], axis=0)
        n0 = jnp.pad(n0, ((0, 0), (0, 0), (0, 7), (0, 0)))
        m0 = jnp.concatenate([jnp.zeros((N_CTX_SEQ, 2, ML_HEADS), F32), state_mlstm_m[:, l]], axis=0)
        m0 = jnp.pad(jnp.broadcast_to(m0[..., None], m0.shape + (128,)), ((0, 0), (0, 0), (0, 8 - ML_HEADS), (0, 0)))
        hf, hb, c_fin, n_fin, m_fin = _mlstm_call(ml, gates_t, c0, n0, m0)

        kc, vc = _cache_kv_call(cache_mla_ckv[:, l].reshape(N_SMP_SEQ * PAST_LEN, MLA_RANK),
                                cache_mla_krope[:, l].reshape(N_SMP_SEQ * PAST_LEN, MLA_ROPE),
                                lw["wk"], lw["wv"], place)
        att_ctx = _attn_ctx_call(q, k, v)
        att_smp = _attn_smp_call(q, k, v, kc, vc)

        x1, h2t, st = _out_call(x_ctx, x_smp, mod, ml, hf, hb, cm, att_ctx, att_smp, lw)
        e1, cut, e2, r2 = _topk_call(st)
        experts = functools.partial(_expert_call, h2t, u_bf, vt_bf, l, e1, cut, e2, r2, x1, mod, final_g2)
        if l < DEPTH - 1:
            x_ctx = x_smp = experts(False, 0, T_ALL)
        else:
            x_ctx = experts(True, 0, T_CTX)
            x_smp = experts(True, T_CTX, T_SMP)

        ckvs.append(ckvn[:T_CTX].reshape(N_CTX_SEQ, CTX_LEN, MLA_RANK))
        krs.append(kr[:T_CTX].reshape(N_CTX_SEQ, CTX_LEN, MLA_ROPE))
        Cs.append(jnp.stack([c_fin[:N_CTX_SEQ, :, hd * ML_DIM:(hd + 1) * ML_DIM, hd * ML_DIM:(hd + 1) * ML_DIM]
                             for hd in range(ML_HEADS)], axis=2))
        ns.append(n_fin[:N_CTX_SEQ, :, 0, :].reshape(N_CTX_SEQ, 2, ML_HEADS, ML_DIM))
        ms.append(m_fin[:N_CTX_SEQ, :, 0:ML_HEADS, 0])

    y_prompt = x_ctx.reshape(N_CTX_SEQ, CTX_LEN, D_MODEL)
    y_sample = x_smp.reshape(N_SMP_SEQ, SMP_LEN, D_MODEL)
    return (y_prompt, y_sample, jnp.stack(ckvs, axis=1), jnp.stack(krs, axis=1), jnp.stack(Cs, axis=1),
            jnp.stack(ns, axis=1), jnp.stack(ms, axis=1))
```

```python
import functools

import numpy as np
import jax
import jax.numpy as jnp
from jax import lax
from jax.experimental import pallas as pl
from jax.experimental.pallas import tpu as pltpu

F32 = jnp.float32
BF16 = jnp.bfloat16

D_MODEL = 1024
N_CTX_SEQ = 16
CTX_LEN = 256
N_SMP_SEQ = 4
SMP_LEN = 2048
PAST_LEN = 256
DEPTH = 2
GRID_W = 64
EPS = 1e-6
T_CTX = N_CTX_SEQ * CTX_LEN
T_SMP = N_SMP_SEQ * SMP_LEN
T_ALL = T_CTX + T_SMP
N_MOD_ROWS = 8

ML_HEADS = 4
ML_DIM = 64
ML_WIDTH = 256
CHUNK = 128
CM_GROUPS = 4
CM_WIDTH = 256
MLA_HEADS = 8
MLA_NOPE = 64
MLA_ROPE = 32
MLA_V = 64
MLA_RANK = 256
HEAD_PAD = 128
ROPE_AXIS = 16
ROPE_THETA = 10000.0
QK_SCALE_LOG2E = (MLA_NOPE + MLA_ROPE) ** -0.5 * 1.4426950408889634
PEER_HEADS = 8
PEER_NKEYS = 128
PEER_EXPERTS = PEER_NKEYS * PEER_NKEYS
PEER_TOPK = 16

TM = 256
TN_TOPK = 256
TM_EXP = 512
TE_EXP = 2048
EXP_SUB = 512
TE_PREP = 2048
ATTN_HEADS_IN_FLIGHT = 4
VMEM_LIMIT = 56 * 1024 * 1024

NEG_INF = float("-inf")


def _dot(a, b):
    return jnp.dot(a, b, preferred_element_type=F32)


def _dot_nt(a, b):
    return lax.dot_general(a, b, (((1,), (1,)), ((), ())), preferred_element_type=F32)


def _dot_tn(a, b):
    return lax.dot_general(a, b, (((0,), (0,)), ((), ())), preferred_element_type=F32)


def _split3(a):
    a1 = a.astype(BF16)
    r1 = a - a1.astype(F32)
    a2 = r1.astype(BF16)
    a3 = (r1 - a2.astype(F32)).astype(BF16)
    return a1, a2, a3


def _rms(x, g):
    return x * lax.rsqrt(jnp.mean(x * x, axis=-1, keepdims=True) + EPS) * g


def _sigmoid(x):
    return 1.0 / (1.0 + jnp.exp(-x))


def _log_sigmoid(x):
    return jnp.minimum(x, 0.0) - jnp.log(1.0 + jnp.exp(-jnp.abs(x)))


def _mod_row_of_block(i, rows_per_block):
    ctx_blocks = T_CTX // rows_per_block
    per_seq = SMP_LEN // rows_per_block
    return jnp.where(i < ctx_blocks, 0, 1 + (i - ctx_blocks) // per_seq)


def _ada_kernel(cv_ref, w_ref, b_ref, o_ref):
    cv = cv_ref[...]
    s = cv * _sigmoid(cv)
    w = w_ref[0]
    w1, w2, w3 = _split3(w)
    s1, s2, s3 = _split3(s)
    acc = _dot(s1, w1) + (_dot(s1, w2) + _dot(s2, w1)) + (_dot(s1, w3) + _dot(s2, w2) + _dot(s3, w1))
    o_ref[0] = acc + b_ref[0]


def _ada_call(cvecs, ada_w, ada_b):
    tn = 2048
    return pl.pallas_call(
        _ada_kernel,
        grid=(DEPTH, 6 * D_MODEL // tn),
        in_specs=[
            pl.BlockSpec((N_MOD_ROWS, D_MODEL), lambda l, j: (0, 0)),
            pl.BlockSpec((1, D_MODEL, tn), lambda l, j: (l, 0, j)),
            pl.BlockSpec((1, 1, tn), lambda l, j: (l, 0, j)),
        ],
        out_specs=pl.BlockSpec((1, N_MOD_ROWS, tn), lambda l, j: (l, 0, j)),
        out_shape=jax.ShapeDtypeStruct((DEPTH, N_MOD_ROWS, 6 * D_MODEL), F32),
        compiler_params=pltpu.CompilerParams(dimension_semantics=("parallel", "parallel")),
        name="ada_mod",
    )(cvecs, ada_w, ada_b.reshape(DEPTH, 1, 6 * D_MODEL))


def _x_specs(x_ctx, x_smp):
    ctx_blk = T_CTX // TM
    smp_off = x_smp.shape[0] // TM - T_SMP // TM
    return [pl.BlockSpec((TM, D_MODEL), lambda i: (jnp.minimum(i, ctx_blk - 1), 0)),
            pl.BlockSpec((TM, D_MODEL), lambda i: (jnp.maximum(i - ctx_blk, 0) + smp_off, 0))]


def _select_x(xc_ref, xs_ref):
    return jnp.where(pl.program_id(0) < T_CTX // TM, xc_ref[...], xs_ref[...])


def _proj_kernel(xc_ref, xs_ref, mod_ref, g1_ref, wml_ref, wgt_ref, gbc_ref, wcm_ref, wmla_ref, wkr_ref,
                 cmg_ref, ws_ref, cmb_ref, qg_ref, kvg_ref, wuqa_ref, wuqb_ref, wk_ref, wv_ref, cos_ref, sin_ref,
                 ml_ref, gatest_ref, cm_ref, ckvn_ref, kr_ref, q_ref, k_ref, v_ref):
    x = _select_x(xc_ref, xs_ref)
    mod = mod_ref[0]
    sh1 = mod[:, 0:D_MODEL]
    sc1 = mod[:, D_MODEL:2 * D_MODEL]
    h = _rms(x, g1_ref[...]) * (1.0 + sc1) + sh1
    hb = h.astype(BF16)

    def mlstm_inputs():
        ml_ref[...] = _dot(hb, wml_ref[...])
        yield None
        gatest_ref[...] = _dot_nt(wgt_ref[...], hb) + gbc_ref[...]
        yield None

    def spatial_gating():
        cm = _dot(hb, wcm_ref[...])
        yield None
        u = cm[:, 0:CM_WIDTH]
        vn = _rms(cm[:, CM_WIDTH:2 * CM_WIDTH], cmg_ref[...]).astype(BF16)
        lane_group = lax.broadcasted_iota(jnp.int32, (CHUNK, CM_WIDTH), 1) >> 6
        yield None
        for c in range(TM // CHUNK):
            rows = slice(c * CHUNK, (c + 1) * CHUNK)
            vc = vn[rows]
            mixed = jnp.zeros((CHUNK, CM_WIDTH), F32)
            for g in range(CM_GROUPS):
                mixed = jnp.where(lane_group == g, _dot(ws_ref[g], vc), mixed)
            cm_ref[rows, :] = u[rows] * (mixed + cmb_ref[...])
            yield None

    def latent_attention():
        mla = _dot(hb, wmla_ref[...])
        yield None
        qn = _rms(mla[:, 0:MLA_RANK], qg_ref[...]).astype(BF16)
        ckvn = _rms(mla[:, MLA_RANK:2 * MLA_RANK], kvg_ref[...])
        ckvn_ref[...] = ckvn
        kr_ref[...] = _dot(hb, wkr_ref[...])
        cos = cos_ref[...]
        sin = sin_ref[...]
        kr_rot = mla[:, 2 * MLA_RANK:2 * MLA_RANK + HEAD_PAD] * cos + mla[:, 2 * MLA_RANK + HEAD_PAD:] * sin
        yield None
        qa = _dot(qn, wuqa_ref[...])
        qb = _dot(qn, wuqb_ref[...])
        ckvb = ckvn.astype(BF16)
        kp = _dot(ckvb, wk_ref[...])
        yield None
        for hd in range(MLA_HEADS):
            cols = slice(hd * HEAD_PAD, (hd + 1) * HEAD_PAD)
            q_ref[hd] = ((qa[:, cols] * cos + qb[:, cols] * sin) * QK_SCALE_LOG2E).astype(BF16)
            k_ref[hd] = (kp[:, cols] + kr_rot).astype(BF16)
            if hd % 4 == 3:
                yield None
        v_ref[...] = _dot(ckvb, wv_ref[...]).astype(BF16)
        yield None

    _interleave([mlstm_inputs(), spatial_gating(), latent_attention()])


def _proj_call(x_ctx, x_smp, mod, lw, cos_t, sin_t):
    nblk = T_ALL // TM
    full = lambda shape: pl.BlockSpec(shape, lambda i: (0,) * len(shape))
    rows = lambda w: pl.BlockSpec((TM, w), lambda i: (i, 0))
    in_specs = _x_specs(x_ctx, x_smp) + [
        pl.BlockSpec((1, 1, 6 * D_MODEL), lambda i: (_mod_row_of_block(i, TM), 0, 0)),
        full((1, D_MODEL)),
        full((D_MODEL, 4 * ML_WIDTH)),
        full((16, D_MODEL)),
        full((16, 1)),
        full((D_MODEL, 2 * CM_WIDTH)),
        full((D_MODEL, 2 * MLA_RANK + 2 * HEAD_PAD)),
        full((D_MODEL, MLA_ROPE)),
        full((1, CM_WIDTH)),
        full((CM_GROUPS, CHUNK, CHUNK)),
        full((CHUNK, CM_WIDTH)),
        full((1, MLA_RANK)),
        full((1, MLA_RANK)),
        full((MLA_RANK, MLA_HEADS * HEAD_PAD)),
        full((MLA_RANK, MLA_HEADS * HEAD_PAD)),
        full((MLA_RANK, MLA_HEADS * HEAD_PAD)),
        full((MLA_RANK, MLA_HEADS * MLA_V)),
        rows(HEAD_PAD),
        rows(HEAD_PAD),
    ]
    out_specs = [
        rows(4 * ML_WIDTH),
        pl.BlockSpec((16, TM), lambda i: (0, i)),
        rows(CM_WIDTH),
        rows(MLA_RANK),
        rows(MLA_ROPE),
        pl.BlockSpec((MLA_HEADS, TM, HEAD_PAD), lambda i: (0, i, 0)),
        pl.BlockSpec((MLA_HEADS, TM, HEAD_PAD), lambda i: (0, i, 0)),
        rows(MLA_HEADS * MLA_V),
    ]
    out_shape = [
        jax.ShapeDtypeStruct((T_ALL, 4 * ML_WIDTH), F32),
        jax.ShapeDtypeStruct((16, T_ALL), F32),
        jax.ShapeDtypeStruct((T_ALL, CM_WIDTH), F32),
        jax.ShapeDtypeStruct((T_ALL, MLA_RANK), F32),
        jax.ShapeDtypeStruct((T_ALL, MLA_ROPE), F32),
        jax.ShapeDtypeStruct((MLA_HEADS, T_ALL, HEAD_PAD), BF16),
        jax.ShapeDtypeStruct((MLA_HEADS, T_ALL, HEAD_PAD), BF16),
        jax.ShapeDtypeStruct((T_ALL, MLA_HEADS * MLA_V), BF16),
    ]
    return pl.pallas_call(
        _proj_kernel,
        grid=(nblk,),
        in_specs=in_specs,
        out_specs=out_specs,
        out_shape=out_shape,
        compiler_params=pltpu.CompilerParams(dimension_semantics=("parallel",), vmem_limit_bytes=VMEM_LIMIT),
        name="proj_in",
    )(x_ctx, x_smp, mod, lw["g1"], lw["wml"], lw["wgt"], lw["gb_col"], lw["wcm"], lw["wmla"],
      lw["wkr"], lw["cmg"], lw["ws"], lw["cmb"], lw["qg"], lw["kvg"], lw["wuqa"], lw["wuqb"], lw["wk"], lw["wv"],
      cos_t, sin_t)


def _mlstm_schedule():
    pair, fwd_a, bwd_a, fwd_b, bwd_b, first = [], [], [], [], [], []
    base = 0
    for p in range((N_CTX_SEQ + N_SMP_SEQ) // 2):
        nc = (CTX_LEN if 2 * p < N_CTX_SEQ else SMP_LEN) // CHUNK
        for j in range(nc):
            pair.append(p)
            fwd_a.append(base + j)
            bwd_a.append(base + nc - 1 - j)
            fwd_b.append(base + nc + j)
            bwd_b.append(base + 2 * nc - 1 - j)
            first.append(1 if j == 0 else 0)
        base += 2 * nc
    as_i32 = lambda a: jnp.asarray(np.asarray(a, np.int32))
    return tuple(as_i32(a) for a in (pair, fwd_a, bwd_a, fwd_b, bwd_b, first))


def _scan_cummax(x, direction):
    L = CHUNK
    rows = x.shape[0]
    x = jnp.concatenate([x, x], axis=0)
    lane = lax.broadcasted_iota(jnp.int32, x.shape, 1)
    k = 1
    while k < L:
        if direction == 0:
            shifted = jnp.where(lane >= k, pltpu.roll(x, k, axis=1), NEG_INF)
        else:
            shifted = jnp.where(lane < L - k, pltpu.roll(x, L - k, axis=1), NEG_INF)
        x = jnp.maximum(x, shifted)
        k *= 2
    return x[0:rows]


def _rows_to_lane_cols(rows, eye, rep, pieces):
    x = jnp.concatenate([jnp.broadcast_to(rows[h:h + 1, :], (rep, CHUNK)) for h in range(ML_HEADS)], axis=0)
    out = None
    for _ in range(pieces):
        xb = x.astype(BF16)
        part = _dot_nt(eye, xb)
        out = part if out is None else out + part
        x = x - xb.astype(F32)
    return out


def _per_head_lanes(x512, lane64):
    lo = jnp.where(lane64, x512[:, 0:128], x512[:, 128:256])
    hi = jnp.where(lane64, x512[:, 256:384], x512[:, 384:512])
    return jnp.concatenate([lo, hi], axis=1)


def _mlstm_direction(ml, g_row, direction, slot, c_ref, n_ref, m_ref):
    L = CHUNK
    t_idx = lax.broadcasted_iota(jnp.int32, (L, L), 0)
    s_idx = lax.broadcasted_iota(jnp.int32, (L, L), 1)
    visible = (s_idx <= t_idx) if direction == 0 else (s_idx >= t_idx)
    tri = jnp.where(visible, 1.0, 0.0).astype(BF16)
    eye = jnp.where(s_idx == t_idx, 1.0, 0.0).astype(BF16)
    lane64 = lax.broadcasted_iota(jnp.int32, (1, 128), 1) < ML_DIM
    head_of_lane = lax.broadcasted_iota(jnp.int32, (1, ML_WIDTH), 1) >> 6
    same_head = ((lax.broadcasted_iota(jnp.int32, (ML_WIDTH, ML_WIDTH), 0) >> 6)
                 == (lax.broadcasted_iota(jnp.int32, (ML_WIDTH, ML_WIDTH), 1) >> 6))

    i0 = 8 * direction
    i_row = g_row[i0:i0 + ML_HEADS, :]
    lf_row = _log_sigmoid(g_row[i0 + ML_HEADS:i0 + 2 * ML_HEADS, :])
    r1, r2, r3 = _split3(lf_row)
    b_row = _dot_nt(r1, tri) + _dot_nt(r2, tri) + _dot_nt(r3, tri)
    yield None
    m_rep = m_ref[slot, direction, 0:ML_HEADS, :]
    a_row = i_row - b_row
    g_row_ = jnp.maximum(m_rep, _scan_cummax(a_row, direction))
    yield None
    b_end = jnp.sum(lf_row, axis=1, keepdims=True)
    log_w = b_end - b_row + i_row
    m_new = jnp.maximum(b_end + m_rep, jnp.max(log_w, axis=1, keepdims=True))
    w_k_row = jnp.exp(log_w - m_new)
    decay_rep = jnp.exp(b_end + m_rep - m_new)

    yield None
    g512 = _rows_to_lane_cols(g_row_, eye, 128, 2)
    g_full = _per_head_lanes(g512, lane64)
    yield None
    b_full = _rows_to_lane_cols(b_row, eye, ML_DIM, 2)
    wk_full = _rows_to_lane_cols(w_k_row, eye, ML_DIM, 1)
    m_full = _per_head_lanes(jnp.concatenate([m_rep[h:h + 1, :] for h in range(ML_HEADS)], axis=1), lane64)
    decay_full = _per_head_lanes(jnp.concatenate([decay_rep[h:h + 1, :] for h in range(ML_HEADS)], axis=1),
                                 lane64)

    yield None
    q = ml[:, 0:ML_WIDTH]
    k = ml[:, ML_WIDTH:2 * ML_WIDTH] * (ML_DIM ** -0.5)
    v = ml[:, 2 * ML_WIDTH:3 * ML_WIDTH]
    qb = q.astype(BF16)
    kb = k.astype(BF16)
    vb = v.astype(BF16)
    num = jnp.zeros((L, ML_WIDTH), F32)
    rowsum = jnp.zeros((L, ML_WIDTH), F32)
    for hd in range(ML_HEADS):
        w_intra = jnp.where(visible, jnp.exp(a_row[hd:hd + 1, :] - g512[:, hd * 128:(hd + 1) * 128]), 0.0)
        q_h = jnp.where(head_of_lane == hd, q, 0.0).astype(BF16)
        sw = _dot_nt(q_h, kb) * w_intra
        num = jnp.where(head_of_lane == hd, _dot(sw.astype(BF16), vb), num)
        rowsum = jnp.where(head_of_lane == hd, jnp.sum(sw, axis=-1, keepdims=True), rowsum)
        yield None

    C = c_ref[slot, direction]
    n_row = n_ref[slot, direction, 0:1, :]
    w_inter = jnp.exp(m_full - g_full)
    block_ones = jnp.where(same_head, 1.0, 0.0).astype(BF16)
    qn = _dot((q * n_row).astype(BF16), block_ones)
    num = num + w_inter * _dot(qb, C.astype(BF16))
    den = rowsum + w_inter * qn
    h_out = num / jnp.maximum(jnp.abs(den), jnp.exp(-(b_full + g_full)))

    kw = wk_full * k
    c_ref[slot, direction] = decay_full * C + jnp.where(same_head, _dot_tn(kw.astype(BF16), vb), 0.0)
    n_ref[slot, direction, 0:1, :] = decay_full * n_row + jnp.sum(kw, axis=0, keepdims=True)
    m_ref[slot, direction, 0:ML_HEADS, :] = m_new
    yield h_out


def _interleave(stage_generators):
    results = [None] * len(stage_generators)
    live = list(range(len(stage_generators)))
    while live:
        for idx in list(live):
            try:
                value = next(stage_generators[idx])
                if value is not None:
                    results[idx] = value
            except StopIteration:
                live.remove(idx)
    return results


def _mlstm_kernel(pair_ref, fa_ref, ba_ref, fb_ref, bb_ref, first_ref,
                  mlfa_ref, mlba_ref, mlfb_ref, mlbb_ref, grfa_ref, grba_ref, grfb_ref, grbb_ref,
                  c0_ref, n0_ref, m0_ref, hf_ref, hb_ref, c_ref, n_ref, m_ref):
    step = pl.program_id(0)

    @pl.when(first_ref[step] == 1)
    def _():
        c_ref[...] = c0_ref[...]
        n_ref[...] = n0_ref[...]
        m_ref[...] = m0_ref[...]

    hfa, hba, hfb, hbb = _interleave([
        _mlstm_direction(mlfa_ref[...], grfa_ref[...], 0, 0, c_ref, n_ref, m_ref),
        _mlstm_direction(mlba_ref[...], grba_ref[...], 1, 0, c_ref, n_ref, m_ref),
        _mlstm_direction(mlfb_ref[...], grfb_ref[...], 0, 1, c_ref, n_ref, m_ref),
        _mlstm_direction(mlbb_ref[...], grbb_ref[...], 1, 1, c_ref, n_ref, m_ref)])
    hf_ref[0, 0] = hfa
    hf_ref[0, 1] = hfb
    hb_ref[0, 0] = hba
    hb_ref[0, 1] = hbb


def _mlstm_call(ml, gates_t, c0, n0, m0):
    sched = _mlstm_schedule()
    nseq = N_CTX_SEQ + N_SMP_SEQ
    nsteps = int(sched[0].shape[0])
    chunk_rows = lambda which: pl.BlockSpec((CHUNK, 4 * ML_WIDTH), lambda i, *s: (s[which][i], 0))
    gate_cols = lambda which: pl.BlockSpec((16, CHUNK), lambda i, *s: (0, s[which][i]))
    c_spec = pl.BlockSpec((2, 2, ML_WIDTH, ML_WIDTH), lambda i, *s: (s[0][i], 0, 0, 0))
    n_spec = pl.BlockSpec((2, 2, 8, ML_WIDTH), lambda i, *s: (s[0][i], 0, 0, 0))
    m_spec = pl.BlockSpec((2, 2, 8, 128), lambda i, *s: (s[0][i], 0, 0, 0))
    h_spec = pl.BlockSpec((1, 2, CHUNK, ML_WIDTH), lambda i, *s: (i, 0, 0, 0))
    grid_spec = pltpu.PrefetchScalarGridSpec(
        num_scalar_prefetch=6,
        grid=(nsteps,),
        in_specs=[chunk_rows(1), chunk_rows(2), chunk_rows(3), chunk_rows(4),
                  gate_cols(1), gate_cols(2), gate_cols(3), gate_cols(4), c_spec, n_spec, m_spec],
        out_specs=[h_spec, h_spec, c_spec, n_spec, m_spec],
    )
    return pl.pallas_call(
        _mlstm_kernel,
        grid_spec=grid_spec,
        out_shape=[
            jax.ShapeDtypeStruct((nsteps, 2, CHUNK, ML_WIDTH), F32),
            jax.ShapeDtypeStruct((nsteps, 2, CHUNK, ML_WIDTH), F32),
            jax.ShapeDtypeStruct((nseq, 2, ML_WIDTH, ML_WIDTH), F32),
            jax.ShapeDtypeStruct((nseq, 2, 8, ML_WIDTH), F32),
            jax.ShapeDtypeStruct((nseq, 2, 8, 128), F32),
        ],
        compiler_params=pltpu.CompilerParams(dimension_semantics=("arbitrary",), vmem_limit_bytes=VMEM_LIMIT),
        name="mlstm",
    )(*sched, ml, ml, ml, ml, gates_t, gates_t, gates_t, gates_t, c0, n0, m0)


def _h_pair_specs():
    ctx_blk = T_CTX // TM
    blk_per_seq = SMP_LEN // TM
    ctx_steps = N_CTX_SEQ // 2 * (CTX_LEN // CHUNK)

    def place(i):
        i_s = jnp.maximum(i - ctx_blk, 0)
        seq_s = i_s // blk_per_seq
        jb = i_s % blk_per_seq
        is_ctx = i < ctx_blk
        slot = jnp.where(is_ctx, i % 2, seq_s % 2)
        base = (ctx_steps + (seq_s // 2) * (SMP_LEN // CHUNK)) // 2
        fwd = jnp.where(is_ctx, i // 2, base + jb)
        bwd = jnp.where(is_ctx, i // 2, base + blk_per_seq - 1 - jb)
        return fwd, bwd, slot

    block = (2, None, CHUNK, ML_WIDTH)
    return (pl.BlockSpec(block, lambda i: (place(i)[0], place(i)[2], 0, 0)),
            pl.BlockSpec(block, lambda i: (place(i)[1], place(i)[2], 0, 0)))


def _cache_kv_kernel(ckv_ref, kr_ref, wk_ref, wv_ref, place_ref, k_ref, v_ref):
    ckvb = ckv_ref[...].astype(BF16)
    kp = _dot(ckvb, wk_ref[...])
    kr128 = _dot(kr_ref[...].astype(BF16), place_ref[...])
    for hd in range(MLA_HEADS):
        k_ref[hd] = (kp[:, hd * HEAD_PAD:(hd + 1) * HEAD_PAD] + kr128).astype(BF16)
    v_ref[...] = _dot(ckvb, wv_ref[...]).astype(BF16)


def _cache_kv_call(ckv, kr, wk, wv, place):
    n = N_SMP_SEQ * PAST_LEN
    tb = PAST_LEN
    return pl.pallas_call(
        _cache_kv_kernel,
        grid=(n // tb,),
        in_specs=[
            pl.BlockSpec((tb, MLA_RANK), lambda i: (i, 0)),
            pl.BlockSpec((tb, MLA_ROPE), lambda i: (i, 0)),
            pl.BlockSpec((MLA_RANK, MLA_HEADS * HEAD_PAD), lambda i: (0, 0)),
            pl.BlockSpec((MLA_RANK, MLA_HEADS * MLA_V), lambda i: (0, 0)),
            pl.BlockSpec((MLA_ROPE, HEAD_PAD), lambda i: (0, 0)),
        ],
        out_specs=[
            pl.BlockSpec((MLA_HEADS, tb, HEAD_PAD), lambda i: (0, i, 0)),
            pl.BlockSpec((tb, MLA_HEADS * MLA_V), lambda i: (i, 0)),
        ],
        out_shape=[
            jax.ShapeDtypeStruct((MLA_HEADS, n, HEAD_PAD), BF16),
            jax.ShapeDtypeStruct((n, MLA_HEADS * MLA_V), BF16),
        ],
        compiler_params=pltpu.CompilerParams(dimension_semantics=("parallel",)),
        name="cache_kv",
    )(ckv, kr, wk, wv, place)


def _attn_kernel(*refs, has_cache):
    if has_cache:
        q_ref, kn_ref, vn_ref, kc_ref, vc_ref, o_ref = refs
    else:
        q_ref, kn_ref, vn_ref, o_ref = refs

    def head(hd):
        lanes = slice(hd * MLA_V, (hd + 1) * MLA_V)
        q = q_ref[hd]
        s_n = _dot_nt(q, kn_ref[hd])
        if has_cache:
            s_c = _dot_nt(q, kc_ref[hd])
        yield None
        m = jnp.max(s_n, axis=-1, keepdims=True)
        if has_cache:
            m = jnp.maximum(m, jnp.max(s_c, axis=-1, keepdims=True))
        yield None
        p_n = jnp.exp2(s_n - m)
        l = jnp.sum(p_n, axis=-1, keepdims=True)
        o = _dot(p_n.astype(BF16), vn_ref[:, lanes])
        if has_cache:
            p_c = jnp.exp2(s_c - m)
            l = l + jnp.sum(p_c, axis=-1, keepdims=True)
            o = o + _dot(p_c.astype(BF16), vc_ref[:, lanes])
        yield o / l

    outs = []
    for hd in range(0, MLA_HEADS, ATTN_HEADS_IN_FLIGHT):
        outs += _interleave([head(hd + i) for i in range(ATTN_HEADS_IN_FLIGHT)])
    o_ref[...] = jnp.concatenate(outs, axis=-1)


def _attn_ctx_call(q, k, v):
    tq = CTX_LEN
    return pl.pallas_call(
        functools.partial(_attn_kernel, has_cache=False),
        grid=(N_CTX_SEQ,),
        in_specs=[
            pl.BlockSpec((MLA_HEADS, tq, HEAD_PAD), lambda s: (0, s, 0)),
            pl.BlockSpec((MLA_HEADS, tq, HEAD_PAD), lambda s: (0, s, 0)),
            pl.BlockSpec((tq, MLA_HEADS * MLA_V), lambda s: (s, 0)),
        ],
        out_specs=pl.BlockSpec((tq, MLA_HEADS * MLA_V), lambda s: (s, 0)),
        out_shape=jax.ShapeDtypeStruct((T_CTX, MLA_HEADS * MLA_V), F32),
        compiler_params=pltpu.CompilerParams(dimension_semantics=("parallel",), vmem_limit_bytes=VMEM_LIMIT),
        name="attn_ctx",
    )(q, k, v)


def _attn_smp_call(q, k, v, kc, vc):
    tq = 256
    qb_per_seq = SMP_LEN // tq
    ctx_qb = T_CTX // tq
    ctx_kb = T_CTX // SMP_LEN
    return pl.pallas_call(
        functools.partial(_attn_kernel, has_cache=True),
        grid=(N_SMP_SEQ, qb_per_seq),
        in_specs=[
            pl.BlockSpec((MLA_HEADS, tq, HEAD_PAD), lambda b, i: (0, ctx_qb + b * qb_per_seq + i, 0)),
            pl.BlockSpec((MLA_HEADS, SMP_LEN, HEAD_PAD), lambda b, i: (0, ctx_kb + b, 0)),
            pl.BlockSpec((SMP_LEN, MLA_HEADS * MLA_V), lambda b, i: (ctx_kb + b, 0)),
            pl.BlockSpec((MLA_HEADS, PAST_LEN, HEAD_PAD), lambda b, i: (0, b, 0)),
            pl.BlockSpec((PAST_LEN, MLA_HEADS * MLA_V), lambda b, i: (b, 0)),
        ],
        out_specs=pl.BlockSpec((tq, MLA_HEADS * MLA_V), lambda b, i: (b * qb_per_seq + i, 0)),
        out_shape=jax.ShapeDtypeStruct((T_SMP, MLA_HEADS * MLA_V), F32),
        compiler_params=pltpu.CompilerParams(dimension_semantics=("parallel", "parallel"),
                                             vmem_limit_bytes=VMEM_LIMIT),
        name="attn_smp",
    )(q, k, v, kc, vc)


def _out_kernel(xc_ref, xs_ref, mod_ref, ml_ref, hf_ref, hb_ref, cm_ref, attc_ref, atts_ref, wo_ref, g2_ref,
                wq_ref, sk_ref, x1_ref, h2t_ref, st_ref):
    is_ctx = pl.program_id(0) < T_CTX // TM
    att = jnp.where(is_ctx, attc_ref[...], atts_ref[...])
    mod = mod_ref[0]
    g1 = mod[:, 2 * D_MODEL:3 * D_MODEL]
    sh2 = mod[:, 3 * D_MODEL:4 * D_MODEL]
    sc2 = mod[:, 4 * D_MODEL:5 * D_MODEL]
    h_sum = jnp.concatenate([hf_ref[0] + hb_ref[1], hf_ref[1] + hb_ref[0]], axis=0)
    mlo = _sigmoid(ml_ref[:, 3 * ML_WIDTH:4 * ML_WIDTH]) * h_sum
    mix = (_dot(mlo.astype(BF16), wo_ref[0:ML_WIDTH, :])
           + _dot(cm_ref[...].astype(BF16), wo_ref[ML_WIDTH:ML_WIDTH + CM_WIDTH, :])
           + _dot(att.astype(BF16), wo_ref[ML_WIDTH + CM_WIDTH:, :]))
    x1 = _select_x(xc_ref, xs_ref) + g1 * mix
    x1_ref[...] = x1
    h2f = _rms(x1, g2_ref[...]) * (1.0 + sc2) + sh2
    h2t_ref[...] = h2f.T.astype(BF16)
    qp = _dot(h2f.astype(BF16), wq_ref[...]).astype(BF16)
    for hh in range(2 * PEER_HEADS):
        st_ref[hh] = _dot_nt(sk_ref[hh % 2], qp[:, hh * 128:(hh + 1) * 128])


def _out_call(x_ctx, x_smp, mod, ml, hf, hb, cm, att_ctx, att_smp, lw):
    nblk = T_ALL // TM
    ctx_blk = T_CTX // TM
    rows = lambda w: pl.BlockSpec((TM, w), lambda i: (i, 0))
    full = lambda shape: pl.BlockSpec(shape, lambda i: (0,) * len(shape))
    att_w = MLA_HEADS * MLA_V
    attc_spec = pl.BlockSpec((TM, att_w), lambda i: (jnp.minimum(i, ctx_blk - 1), 0))
    atts_spec = pl.BlockSpec((TM, att_w), lambda i: (jnp.maximum(i - ctx_blk, 0), 0))
    return pl.pallas_call(
        _out_kernel,
        grid=(nblk,),
        in_specs=_x_specs(x_ctx, x_smp) + [
            pl.BlockSpec((1, 1, 6 * D_MODEL), lambda i: (_mod_row_of_block(i, TM), 0, 0)),
            rows(4 * ML_WIDTH), *_h_pair_specs(), rows(CM_WIDTH), attc_spec, atts_spec,
            full((D_MODEL, D_MODEL)), full((1, D_MODEL)), full((D_MODEL, 2 * PEER_HEADS * 128)),
            full((2, PEER_NKEYS, 128)),
        ],
        out_specs=[rows(D_MODEL), pl.BlockSpec((D_MODEL, TM), lambda i: (0, i)),
                   pl.BlockSpec((2 * PEER_HEADS, PEER_NKEYS, TM), lambda i: (0, 0, i))],
        out_shape=[
            jax.ShapeDtypeStruct((T_ALL, D_MODEL), F32),
            jax.ShapeDtypeStruct((D_MODEL, T_ALL), BF16),
            jax.ShapeDtypeStruct((2 * PEER_HEADS, PEER_NKEYS, T_ALL), F32),
        ],
        compiler_params=pltpu.CompilerParams(dimension_semantics=("parallel",), vmem_limit_bytes=VMEM_LIMIT),
        name="proj_out",
    )(x_ctx, x_smp, mod, ml, hf, hb, cm, att_ctx, att_smp, lw["wo"], lw["g2"], lw["wq"], lw["sk"])


def _sorting_network_pairs(n):
    pairs = []
    p = 1
    while p < n:
        k = p
        while k >= 1:
            for j in range(k % p, n - k, 2 * k):
                for i in range(min(k, n - j - k)):
                    if (i + j) // (2 * p) == (i + j + k) // (2 * p):
                        pairs.append((i + j, i + j + k))
            k //= 2
        p *= 2
    return pairs


_SORT16_PAIRS = _sorting_network_pairs(PEER_TOPK)


def _pop16(lists, singles=None):
    lists = list(lists)
    vals = []
    for k in range(PEER_TOPK):
        heads = lists[0] if singles is None else jnp.maximum(lists[0], singles)
        m = jnp.max(heads, axis=0, keepdims=True)
        vals.append(m)
        if k == PEER_TOPK - 1:
            break
        hit = lists[0] == m
        for i in range(PEER_TOPK - 1 - k):
            lists[i] = jnp.where(hit, lists[i + 1], lists[i])
        if singles is not None:
            singles = jnp.where(singles == m, NEG_INF, singles)
    return vals


def _top16_rows(s):
    tiles = [s[8 * j:8 * j + 8] for j in range(s.shape[0] // 8)]
    assert len(tiles) == PEER_TOPK
    for i, j in _SORT16_PAIRS:
        tiles[i], tiles[j] = jnp.maximum(tiles[i], tiles[j]), jnp.minimum(tiles[i], tiles[j])
    return _pop16(tiles)


def _count_steps(x, thresholds, below):
    r = jnp.zeros(x.shape, F32)
    for q, t in enumerate(thresholds):
        r = jnp.where((x < t) if below else (x >= t), float(q + 1), r)
    return r


def _pack_rows_bf16(x):
    return pltpu.bitcast(_pack_rows_words(x), BF16)


def _pack_rows_words(x):
    r, n = x.shape
    x4 = x.reshape(r // 16, 2, 8, n)
    lo = x4[:, 0].reshape(r // 2, n)
    hi = x4[:, 1].reshape(r // 2, n)
    return pltpu.pack_elementwise([lo, hi], packed_dtype=BF16)


def _dup_bf16_words(x):
    return pltpu.pack_elementwise([x, x], packed_dtype=BF16)


def _rows_to_array(rows, row_iota):
    arr = jnp.zeros(row_iota.shape, F32)
    for i, r in enumerate(rows):
        arr = jnp.where(row_iota == i, r, arr)
    return arr


def _topk_kernel(st_ref, e1_ref, cut_ref, e2_ref, r2_ref):
    n = st_ref.shape[-1]
    row8 = lax.broadcasted_iota(jnp.int32, (8, n), 0)

    def head(hd, carry):
        s1 = st_ref[2 * hd]
        s2 = st_ref[2 * hd + 1]
        v1 = _top16_rows(s1)
        v2 = _top16_rows(s2)
        v1_top = _rows_to_array(v1[0:8], row8)
        v1_bot = _rows_to_array(v1[8:PEER_TOPK], row8)
        c = _pop16([v1_top + v2[q] for q in range(PEER_TOPK)], singles=v1_bot + v2[0])
        tau = c[PEER_TOPK - 1]
        z = jnp.zeros((1, n), F32)
        for ck in c:
            z = z + jnp.exp(ck - c[0])

        def smallest_qualifying(v1_rows, q):
            return jnp.min(jnp.where(v1_rows + v2[q] >= tau, v1_rows, jnp.inf), axis=0, keepdims=True)

        sigma = [jnp.minimum(smallest_qualifying(v1_top, 0), smallest_qualifying(v1_bot, 0))]
        sigma += [smallest_qualifying(v1_top, q) for q in range(1, 8)]
        extra = jnp.zeros((1, n), F32)
        last_selected = v2[7]
        for q in range(8, PEER_TOPK):
            reached = v1[0] + v2[q] >= tau
            extra = extra + jnp.where(reached, 1.0, 0.0)
            last_selected = jnp.where(reached, v2[q], last_selected)
        rank2 = jnp.where(s2 < last_selected, 9.0, _count_steps(s2, v2[0:8], below=True))
        cut = _count_steps(s1, sigma, below=False) + jnp.where(s1 >= v1[0], jnp.minimum(extra, 1.0), 0.0)
        e1_ref[hd] = _dup_bf16_words(jnp.exp(s1 - v1[0]) / z)
        cut_ref[hd] = _dup_bf16_words(cut)
        e2_ref[hd] = _pack_rows_words(jnp.exp(s2 - v2[0]))
        r2_ref[hd] = _pack_rows_words(rank2)
        return carry

    lax.fori_loop(0, PEER_HEADS, head, 0)


def _topk_call(st):
    tn = TN_TOPK
    spec = pl.BlockSpec((PEER_HEADS, PEER_NKEYS, tn), lambda i: (0, 0, i))
    word_spec = pl.BlockSpec((PEER_HEADS, PEER_NKEYS // 2, tn), lambda i: (0, 0, i))
    stat = lambda dt: jax.ShapeDtypeStruct((PEER_HEADS, PEER_NKEYS, T_ALL), dt)
    words = jax.ShapeDtypeStruct((PEER_HEADS, PEER_NKEYS // 2, T_ALL), jnp.uint32)
    return pl.pallas_call(
        _topk_kernel,
        grid=(T_ALL // tn,),
        in_specs=[pl.BlockSpec((2 * PEER_HEADS, PEER_NKEYS, tn), lambda i: (0, 0, i))],
        out_specs=[spec, spec, word_spec, word_spec],
        out_shape=[stat(jnp.uint32), stat(jnp.uint32), words, words],
        compiler_params=pltpu.CompilerParams(dimension_semantics=("parallel",), vmem_limit_bytes=VMEM_LIMIT),
        name="peer_topk",
    )(st)


def _row_tile_bf16(row):
    blk = pltpu.bitcast(jnp.broadcast_to(row, (8, row.shape[1])), BF16)
    return jnp.concatenate([blk] * (PEER_NKEYS // 16), axis=0)


def _gelu_tanh_bf16(x):
    log2e = 1.4426950408889634
    w = x * (-1.5957691216057308 * log2e - (0.07135481627159584 * log2e) * (x * x))
    return x / (1.0 + jnp.exp2(w))


def _expert_kernel(h2t_ref, u_ref, vt_ref, e1_ref, cut_ref, e2_ref, r2_ref, x1_ref, mod_ref, fg_ref,
                   o_ref, acc_ref, a_ref, *, final_norm):
    j = pl.program_id(1)
    n_tiles = pl.num_programs(1)
    tm = h2t_ref.shape[1]
    a_per_tile = TE_EXP // PEER_NKEYS

    @pl.when(j == 0)
    def _():
        acc_ref[...] = jnp.zeros_like(acc_ref)

    def gate_tile(ai):
        a = j * a_per_tile + ai
        gate = jnp.zeros((PEER_NKEYS, tm), BF16)
        for hd in range(PEER_HEADS):
            cut_a = _row_tile_bf16(cut_ref[hd, pl.ds(a, 1), :])
            e1_a = _row_tile_bf16(e1_ref[hd, pl.ds(a, 1), :])
            r2 = pltpu.bitcast(r2_ref[hd], BF16)
            e2 = pltpu.bitcast(e2_ref[hd], BF16)
            gate = gate + jnp.where(r2 < cut_a, e2, jnp.zeros((), BF16)) * e1_a
        return gate

    a_per_sub = EXP_SUB // PEER_NKEYS
    n_sub = TE_EXP // EXP_SUB
    s = [None] * n_sub
    gates = [None] * n_sub
    s[0] = _dot(u_ref[0:EXP_SUB, :], h2t_ref[...])
    gates[0] = [gate_tile(ai) for ai in range(a_per_sub)]
    for sub in range(n_sub):
        if sub + 1 < n_sub:
            s[sub + 1] = _dot(u_ref[(sub + 1) * EXP_SUB:(sub + 2) * EXP_SUB, :], h2t_ref[...])
            gates[sub + 1] = [gate_tile((sub + 1) * a_per_sub + i) for i in range(a_per_sub)]
        g = _gelu_tanh_bf16(_pack_rows_bf16(s[sub]))
        for i in range(a_per_sub):
            r0 = sub * EXP_SUB + i * PEER_NKEYS
            a_ref[r0:r0 + PEER_NKEYS, :] = gates[sub][i] * g[i * PEER_NKEYS:(i + 1) * PEER_NKEYS, :]
    acc_ref[...] += _dot(vt_ref[...], a_ref[...])

    @pl.when(j == n_tiles - 1)
    def _():
        g2 = mod_ref[0][:, 5 * D_MODEL:6 * D_MODEL]
        y = x1_ref[...] + g2 * acc_ref[...].T
        if final_norm:
            y = _rms(y, fg_ref[...])
        o_ref[...] = y


def _expert_call(h2t, u_bf, vt_bf, layer, e1, cut, e2, r2, x1, mod, final_g, final_norm, tok_start, tok_count):
    tm, te = TM_EXP, TE_EXP
    b0 = tok_start // tm
    n_tiles = PEER_EXPERTS // te
    stat = pl.BlockSpec((PEER_HEADS, PEER_NKEYS, tm), lambda i, j: (0, 0, b0 + i))
    stat_words = pl.BlockSpec((PEER_HEADS, PEER_NKEYS // 2, tm), lambda i, j: (0, 0, b0 + i))
    return pl.pallas_call(
        functools.partial(_expert_kernel, final_norm=final_norm),
        grid=(tok_count // tm, n_tiles),
        in_specs=[
            pl.BlockSpec((D_MODEL, tm), lambda i, j: (0, b0 + i)),
            pl.BlockSpec((None, te, D_MODEL), lambda i, j: (layer, j, 0)),
            pl.BlockSpec((None, D_MODEL, te), lambda i, j: (layer, 0, j)),
            stat, stat, stat_words, stat_words,
            pl.BlockSpec((tm, D_MODEL), lambda i, j: (b0 + i, 0)),
            pl.BlockSpec((1, 1, 6 * D_MODEL), lambda i, j: (_mod_row_of_block(b0 + i, tm), 0, 0)),
            pl.BlockSpec((1, D_MODEL), lambda i, j: (0, 0)),
        ],
        out_specs=pl.BlockSpec((tm, D_MODEL), lambda i, j: (i, 0)),
        out_shape=jax.ShapeDtypeStruct((tok_count, D_MODEL), F32),
        scratch_shapes=[
            pltpu.VMEM((D_MODEL, tm), F32),
            pltpu.VMEM((te, tm), BF16),
        ],
        compiler_params=pltpu.CompilerParams(dimension_semantics=("parallel", "arbitrary"),
                                             vmem_limit_bytes=VMEM_LIMIT),
        name="peer_experts",
    )(h2t, u_bf, vt_bf, e1, cut, e2, r2, x1, mod, final_g)


def _u_prep_kernel(u_ref, o_ref):
    o_ref[...] = u_ref[...].astype(BF16)


def _v_prep_kernel(v_ref, o_ref):
    o_ref[...] = _pack_rows_bf16(v_ref[...]).T


def _table_prep_call(peer_u, peer_v):
    te = TE_PREP
    grid = (DEPTH, PEER_EXPERTS // te)
    rows = pl.BlockSpec((None, te, D_MODEL), lambda l, j: (l, j, 0))
    params = pltpu.CompilerParams(dimension_semantics=("parallel", "parallel"), vmem_limit_bytes=VMEM_LIMIT)
    u_bf = pl.pallas_call(
        _u_prep_kernel, grid=grid, in_specs=[rows], out_specs=rows,
        out_shape=jax.ShapeDtypeStruct((DEPTH, PEER_EXPERTS, D_MODEL), BF16),
        compiler_params=params, name="peer_u_prep")(peer_u)
    vt_bf = pl.pallas_call(
        _v_prep_kernel, grid=grid, in_specs=[rows],
        out_specs=pl.BlockSpec((None, D_MODEL, te), lambda l, j: (l, 0, j)),
        out_shape=jax.ShapeDtypeStruct((DEPTH, D_MODEL, PEER_EXPERTS), BF16),
        compiler_params=params, name="peer_v_prep")(peer_v)
    return u_bf, vt_bf


def _rope_swap_cols(w):
    return jnp.concatenate([-w[:, 8:16], w[:, 0:8], -w[:, 24:32], w[:, 16:24]], axis=1)


def _pad_heads(parts, n_heads):
    k = next(p[0].shape[0] for p in parts if p[0] is not None)
    cols = []
    for hd in range(n_heads):
        for arr, w in parts:
            cols.append(jnp.zeros((k, w), F32) if arr is None else arr[:, hd * w:(hd + 1) * w])
    return jnp.concatenate(cols, axis=1)


def _rope_tables():
    pos = np.arange(SMP_LEN)
    freqs = ROPE_THETA ** (-np.arange(0, ROPE_AXIS, 2, dtype=np.float32) / ROPE_AXIS)
    ang_r = (pos // GRID_W).astype(np.float32)[:, None] * freqs
    ang_c = (pos % GRID_W).astype(np.float32)[:, None] * freqs
    ang = jnp.asarray(np.concatenate([ang_r, ang_r, ang_c, ang_c], axis=1).astype(np.float32))
    cos32 = jnp.cos(ang)
    sin32 = jnp.sin(ang)
    ones = jnp.ones((SMP_LEN, MLA_NOPE), F32)
    cos_s = jnp.concatenate([ones, cos32, ones[:, :HEAD_PAD - MLA_NOPE - MLA_ROPE]], axis=1)
    sin_s = jnp.concatenate([0 * ones, sin32, 0 * ones[:, :HEAD_PAD - MLA_NOPE - MLA_ROPE]], axis=1)
    cos_t = jnp.concatenate([jnp.ones((T_CTX, HEAD_PAD), F32)] + [cos_s] * N_SMP_SEQ, axis=0)
    sin_t = jnp.concatenate([jnp.zeros((T_CTX, HEAD_PAD), F32)] + [sin_s] * N_SMP_SEQ, axis=0)
    return cos_t, sin_t


def _layer_weights(l, norm1_g, w_in, mlstm_gate_b, cm_norm_g, cm_ws, cm_b, mla_q_norm_g, mla_w_uq, mla_kv_norm_g,
                   mla_w_ukv, w_out, norm2_g, peer_w_q, peer_subkeys):
    w = w_in[l]
    o_g = 4 * ML_WIDTH
    o_cm = o_g + 16
    o_cq = o_cm + 2 * CM_WIDTH
    o_ckv = o_cq + MLA_RANK
    o_kr = o_ckv + MLA_RANK
    w_kr = w[:, o_kr:o_kr + MLA_ROPE]
    zeros_l = jnp.zeros((D_MODEL, MLA_NOPE), F32)
    zeros_r = jnp.zeros((D_MODEL, HEAD_PAD - MLA_NOPE - MLA_ROPE), F32)
    kr128 = jnp.concatenate([zeros_l, w_kr, zeros_r], axis=1)
    krsw128 = jnp.concatenate([zeros_l, _rope_swap_cols(w_kr), zeros_r], axis=1)
    uq = mla_w_uq[l].reshape(MLA_RANK, MLA_HEADS, MLA_NOPE + MLA_ROPE)
    uq_nope = uq[:, :, :MLA_NOPE].reshape(MLA_RANK, -1)
    uq_rope = uq[:, :, MLA_NOPE:].reshape(MLA_RANK, -1)
    uq_rope_sw = jnp.concatenate(
        [_rope_swap_cols(uq_rope[:, hd * MLA_ROPE:(hd + 1) * MLA_ROPE]) for hd in range(MLA_HEADS)], axis=1)
    pad_w = HEAD_PAD - MLA_NOPE - MLA_ROPE
    ukv = mla_w_ukv[l].reshape(MLA_RANK, MLA_HEADS, MLA_NOPE + MLA_V)
    uk = ukv[:, :, :MLA_NOPE].reshape(MLA_RANK, -1)
    uv = ukv[:, :, MLA_NOPE:].reshape(MLA_RANK, -1)
    gb = mlstm_gate_b[l]
    return {
        "g1": norm1_g[l].reshape(1, D_MODEL),
        "wml": w[:, 0:o_g].astype(BF16),
        "wgt": w[:, o_g:o_cm].T.astype(BF16),
        "gb_col": gb.reshape(16, 1),
        "wcm": w[:, o_cm:o_cq].astype(BF16),
        "wmla": jnp.concatenate([w[:, o_cq:o_kr], kr128, krsw128], axis=1).astype(BF16),
        "wkr": w_kr.astype(BF16),
        "cmg": cm_norm_g[l].reshape(1, CM_WIDTH),
        "ws": cm_ws[l].astype(BF16),
        "cmb": jnp.repeat(cm_b[l].T, CM_WIDTH // CM_GROUPS, axis=1),
        "qg": mla_q_norm_g[l].reshape(1, MLA_RANK),
        "kvg": mla_kv_norm_g[l].reshape(1, MLA_RANK),
        "wuqa": _pad_heads([(uq_nope, MLA_NOPE), (uq_rope, MLA_ROPE), (None, pad_w)], MLA_HEADS).astype(BF16),
        "wuqb": _pad_heads([(None, MLA_NOPE), (uq_rope_sw, MLA_ROPE), (None, pad_w)], MLA_HEADS).astype(BF16),
        "wk": _pad_heads([(uk, MLA_NOPE), (None, HEAD_PAD - MLA_NOPE)], MLA_HEADS).astype(BF16),
        "wv": uv.astype(BF16),
        "wo": w_out[l].astype(BF16),
        "g2": norm2_g[l].reshape(1, D_MODEL),
        "wq": peer_w_q[l].astype(BF16),
        "sk": peer_subkeys[l].astype(BF16),
    }


def kernel(x_prompt, x_sample, c, cache_mla_ckv, cache_mla_krope, state_mlstm_C, state_mlstm_n, state_mlstm_m, c_ctx, norm1_g, ada_w, ada_b, w_in, mlstm_gate_b, cm_norm_g, cm_ws, cm_b, mla_q_norm_g, mla_w_uq, mla_kv_norm_g, mla_w_ukv, w_out, norm2_g, peer_w_q, peer_subkeys, peer_u, peer_v, final_g):
    x_ctx = x_prompt.reshape(T_CTX, D_MODEL)
    x_smp = x_sample.reshape(T_SMP, D_MODEL)
    cvecs = jnp.concatenate([c_ctx[None, :], c, jnp.zeros((N_MOD_ROWS - 1 - N_SMP_SEQ, D_MODEL), F32)], axis=0)
    mod_all = _ada_call(cvecs, ada_w, ada_b)
    cos_t, sin_t = _rope_tables()
    place = jnp.concatenate([jnp.zeros((MLA_ROPE, MLA_NOPE), F32), jnp.eye(MLA_ROPE, dtype=F32),
                             jnp.zeros((MLA_ROPE, HEAD_PAD - MLA_NOPE - MLA_ROPE), F32)], axis=1).astype(BF16)
    final_g2 = final_g.reshape(1, D_MODEL)
    u_bf, vt_bf = _table_prep_call(peer_u, peer_v)

    ckvs, krs, Cs, ns, ms = [], [], [], [], []
    for l in range(DEPTH):
        lw = _layer_weights(l, norm1_g, w_in, mlstm_gate_b, cm_norm_g, cm_ws, cm_b, mla_q_norm_g, mla_w_uq,
                            mla_kv_norm_g, mla_w_ukv, w_out, norm2_g, peer_w_q, peer_subkeys)
        mod = mod_all[l].reshape(N_MOD_ROWS, 1, 6 * D_MODEL)
        ml, gates_t, cm, ckvn, kr, q, k, v = _proj_call(x_ctx, x_smp, mod, lw, cos_t, sin_t)

        c_blk = jnp.einsum('bdhij,hg->bdhigj', state_mlstm_C[:, l], jnp.eye(ML_HEADS, dtype=F32))
        c0 = jnp.concatenate([jnp.zeros((N_CTX_SEQ, 2, ML_WIDTH, ML_WIDTH), F32),
                              c_blk.reshape(N_SMP_SEQ, 2, ML_WIDTH, ML_WIDTH)], axis=0)
        n0 = jnp.concatenate([jnp.zeros((N_CTX_SEQ, 2, 1, ML_WIDTH), F32),
                              state_mlstm_n[:, l].reshape(N_SMP_SEQ, 2, 1, ML_WIDTH)], axis=0)
        n0 = jnp.pad(n0, ((0, 0), (0, 0), (0, 7), (0, 0)))
        m0 = jnp.concatenate([jnp.zeros((N_CTX_SEQ, 2, ML_HEADS), F32), state_mlstm_m[:, l]], axis=0)
        m0 = jnp.pad(jnp.broadcast_to(m0[..., None], m0.shape + (128,)), ((0, 0), (0, 0), (0, 8 - ML_HEADS), (0, 0)))
        hf, hb, c_fin, n_fin, m_fin = _mlstm_call(ml, gates_t, c0, n0, m0)

        kc, vc = _cache_kv_call(cache_mla_ckv[:, l].reshape(N_SMP_SEQ * PAST_LEN, MLA_RANK),
                                cache_mla_krope[:, l].reshape(N_SMP_SEQ * PAST_LEN, MLA_ROPE),
                                lw["wk"], lw["wv"], place)
        att_ctx = _attn_ctx_call(q, k, v)
        att_smp = _attn_smp_call(q, k, v, kc, vc)

        x1, h2t, st = _out_call(x_ctx, x_smp, mod, ml, hf, hb, cm, att_ctx, att_smp, lw)
        e1, cut, e2, r2 = _topk_call(st)
        experts = functools.partial(_expert_call, h2t, u_bf, vt_bf, l, e1, cut, e2, r2, x1, mod, final_g2)
        if l < DEPTH - 1:
            x_ctx = x_smp = experts(False, 0, T_ALL)
        else:
            x_ctx = experts(True, 0, T_CTX)
            x_smp = experts(True, T_CTX, T_SMP)

        ckvs.append(ckvn[:T_CTX].reshape(N_CTX_SEQ, CTX_LEN, MLA_RANK))
        krs.append(kr[:T_CTX].reshape(N_CTX_SEQ, CTX_LEN, MLA_ROPE))
        Cs.append(jnp.stack([c_fin[:N_CTX_SEQ, :, hd * ML_DIM:(hd + 1) * ML_DIM, hd * ML_DIM:(hd + 1) * ML_DIM]
                             for hd in range(ML_HEADS)], axis=2))
        ns.append(n_fin[:N_CTX_SEQ, :, 0, :].reshape(N_CTX_SEQ, 2, ML_HEADS, ML_DIM))
        ms.append(m_fin[:N_CTX_SEQ, :, 0:ML_HEADS, 0])

    y_prompt = x_ctx.reshape(N_CTX_SEQ, CTX_LEN, D_MODEL)
    y_sample = x_smp.reshape(N_SMP_SEQ, SMP_LEN, D_MODEL)
    return (y_prompt, y_sample, jnp.stack(ckvs, axis=1), jnp.stack(krs, axis=1), jnp.stack(Cs, axis=1),
            jnp.stack(ns, axis=1), jnp.stack(ms, axis=1))
```

```python
import functools

import numpy as np
import jax
import jax.numpy as jnp
from jax import lax
from jax.experimental import pallas as pl
from jax.experimental.pallas import tpu as pltpu

F32 = jnp.float32
BF16 = jnp.bfloat16

D_MODEL = 1024
N_CTX_SEQ = 16
CTX_LEN = 256
N_SMP_SEQ = 4
SMP_LEN = 2048
PAST_LEN = 256
DEPTH = 2
GRID_W = 64
EPS = 1e-6
T_CTX = N_CTX_SEQ * CTX_LEN
T_SMP = N_SMP_SEQ * SMP_LEN
T_ALL = T_CTX + T_SMP
N_MOD_ROWS = 8

ML_HEADS = 4
ML_DIM = 64
ML_WIDTH = 256
CHUNK = 128
CM_GROUPS = 4
CM_WIDTH = 256
MLA_HEADS = 8
MLA_NOPE = 64
MLA_ROPE = 32
MLA_V = 64
MLA_RANK = 256
HEAD_PAD = 128
ROPE_AXIS = 16
ROPE_THETA = 10000.0
QK_SCALE_LOG2E = (MLA_NOPE + MLA_ROPE) ** -0.5 * 1.4426950408889634
PEER_HEADS = 8
PEER_NKEYS = 128
PEER_EXPERTS = PEER_NKEYS * PEER_NKEYS
PEER_TOPK = 16

TM = 256
TN_TOPK = 256
TM_EXP = 512
TE_EXP = 2048
EXP_SUB = 512
TE_PREP = 2048
ATTN_HEADS_IN_FLIGHT = 4
VMEM_LIMIT = 56 * 1024 * 1024

NEG_INF = float("-inf")


def _dot(a, b):
    return jnp.dot(a, b, preferred_element_type=F32)


def _dot_nt(a, b):
    return lax.dot_general(a, b, (((1,), (1,)), ((), ())), preferred_element_type=F32)


def _dot_tn(a, b):
    return lax.dot_general(a, b, (((0,), (0,)), ((), ())), preferred_element_type=F32)


def _split3(a):
    a1 = a.astype(BF16)
    r1 = a - a1.astype(F32)
    a2 = r1.astype(BF16)
    a3 = (r1 - a2.astype(F32)).astype(BF16)
    return a1, a2, a3


def _rms(x, g):
    return x * lax.rsqrt(jnp.mean(x * x, axis=-1, keepdims=True) + EPS) * g


def _sigmoid(x):
    return 1.0 / (1.0 + jnp.exp(-x))


def _log_sigmoid(x):
    return jnp.minimum(x, 0.0) - jnp.log(1.0 + jnp.exp(-jnp.abs(x)))


def _mod_row_of_block(i, rows_per_block):
    ctx_blocks = T_CTX // rows_per_block
    per_seq = SMP_LEN // rows_per_block
    return jnp.where(i < ctx_blocks, 0, 1 + (i - ctx_blocks) // per_seq)


def _ada_kernel(cv_ref, w_ref, b_ref, o_ref):
    cv = cv_ref[...]
    s = cv * _sigmoid(cv)
    w = w_ref[0]
    w1, w2, w3 = _split3(w)
    s1, s2, s3 = _split3(s)
    acc = _dot(s1, w1) + (_dot(s1, w2) + _dot(s2, w1)) + (_dot(s1, w3) + _dot(s2, w2) + _dot(s3, w1))
    o_ref[0] = acc + b_ref[0]


def _ada_call(cvecs, ada_w, ada_b):
    tn = 2048
    return pl.pallas_call(
        _ada_kernel,
        grid=(DEPTH, 6 * D_MODEL // tn),
        in_specs=[
            pl.BlockSpec((N_MOD_ROWS, D_MODEL), lambda l, j: (0, 0)),
            pl.BlockSpec((1, D_MODEL, tn), lambda l, j: (l, 0, j)),
            pl.BlockSpec((1, 1, tn), lambda l, j: (l, 0, j)),
        ],
        out_specs=pl.BlockSpec((1, N_MOD_ROWS, tn), lambda l, j: (l, 0, j)),
        out_shape=jax.ShapeDtypeStruct((DEPTH, N_MOD_ROWS, 6 * D_MODEL), F32),
        compiler_params=pltpu.CompilerParams(dimension_semantics=("parallel", "parallel")),
        name="ada_mod",
    )(cvecs, ada_w, ada_b.reshape(DEPTH, 1, 6 * D_MODEL))


def _x_specs(x_ctx, x_smp):
    ctx_blk = T_CTX // TM
    smp_off = x_smp.shape[0] // TM - T_SMP // TM
    return [pl.BlockSpec((TM, D_MODEL), lambda i: (jnp.minimum(i, ctx_blk - 1), 0)),
            pl.BlockSpec((TM, D_MODEL), lambda i: (jnp.maximum(i - ctx_blk, 0) + smp_off, 0))]


def _select_x(xc_ref, xs_ref):
    return jnp.where(pl.program_id(0) < T_CTX // TM, xc_ref[...], xs_ref[...])


def _proj_kernel(xc_ref, xs_ref, mod_ref, g1_ref, wml_ref, wgt_ref, gbc_ref, wcm_ref, wmla_ref, wkr_ref,
                 cmg_ref, ws_ref, cmb_ref, qg_ref, kvg_ref, wuqa_ref, wuqb_ref, wk_ref, wv_ref, cos_ref, sin_ref,
                 ml_ref, gatest_ref, cm_ref, ckvn_ref, kr_ref, q_ref, k_ref, v_ref):
    x = _select_x(xc_ref, xs_ref)
    mod = mod_ref[0]
    sh1 = mod[:, 0:D_MODEL]
    sc1 = mod[:, D_MODEL:2 * D_MODEL]
    h = _rms(x, g1_ref[...]) * (1.0 + sc1) + sh1
    hb = h.astype(BF16)

    def mlstm_inputs():
        ml_ref[...] = _dot(hb, wml_ref[...])
        yield None
        gatest_ref[...] = _dot_nt(wgt_ref[...], hb) + gbc_ref[...]
        yield None

    def spatial_gating():
        cm = _dot(hb, wcm_ref[...])
        yield None
        u = cm[:, 0:CM_WIDTH]
        vn = _rms(cm[:, CM_WIDTH:2 * CM_WIDTH], cmg_ref[...]).astype(BF16)
        lane_group = lax.broadcasted_iota(jnp.int32, (CHUNK, CM_WIDTH), 1) >> 6
        yield None
        for c in range(TM // CHUNK):
            rows = slice(c * CHUNK, (c + 1) * CHUNK)
            vc = vn[rows]
            mixed = jnp.zeros((CHUNK, CM_WIDTH), F32)
            for g in range(CM_GROUPS):
                mixed = jnp.where(lane_group == g, _dot(ws_ref[g], vc), mixed)
            cm_ref[rows, :] = u[rows] * (mixed + cmb_ref[...])
            yield None

    def latent_attention():
        mla = _dot(hb, wmla_ref[...])
        yield None
        qn = _rms(mla[:, 0:MLA_RANK], qg_ref[...]).astype(BF16)
        ckvn = _rms(mla[:, MLA_RANK:2 * MLA_RANK], kvg_ref[...])
        ckvn_ref[...] = ckvn
        kr_ref[...] = _dot(hb, wkr_ref[...])
        cos = cos_ref[...]
        sin = sin_ref[...]
        kr_rot = mla[:, 2 * MLA_RANK:2 * MLA_RANK + HEAD_PAD] * cos + mla[:, 2 * MLA_RANK + HEAD_PAD:] * sin
        yield None
        qa = _dot(qn, wuqa_ref[...])
        qb = _dot(qn, wuqb_ref[...])
        ckvb = ckvn.astype(BF16)
        kp = _dot(ckvb, wk_ref[...])
        yield None
        for hd in range(MLA_HEADS):
            cols = slice(hd * HEAD_PAD, (hd + 1) * HEAD_PAD)
            q_ref[hd] = ((qa[:, cols] * cos + qb[:, cols] * sin) * QK_SCALE_LOG2E).astype(BF16)
            k_ref[hd] = (kp[:, cols] + kr_rot).astype(BF16)
            if hd % 4 == 3:
                yield None
        v_ref[...] = _dot(ckvb, wv_ref[...]).astype(BF16)
        yield None

    _interleave([mlstm_inputs(), spatial_gating(), latent_attention()])


def _proj_call(x_ctx, x_smp, mod, lw, cos_t, sin_t):
    nblk = T_ALL // TM
    full = lambda shape: pl.BlockSpec(shape, lambda i: (0,) * len(shape))
    rows = lambda w: pl.BlockSpec((TM, w), lambda i: (i, 0))
    in_specs = _x_specs(x_ctx, x_smp) + [
        pl.BlockSpec((1, 1, 6 * D_MODEL), lambda i: (_mod_row_of_block(i, TM), 0, 0)),
        full((1, D_MODEL)),
        full((D_MODEL, 4 * ML_WIDTH)),
        full((16, D_MODEL)),
        full((16, 1)),
        full((D_MODEL, 2 * CM_WIDTH)),
        full((D_MODEL, 2 * MLA_RANK + 2 * HEAD_PAD)),
        full((D_MODEL, MLA_ROPE)),
        full((1, CM_WIDTH)),
        full((CM_GROUPS, CHUNK, CHUNK)),
        full((CHUNK, CM_WIDTH)),
        full((1, MLA_RANK)),
        full((1, MLA_RANK)),
        full((MLA_RANK, MLA_HEADS * HEAD_PAD)),
        full((MLA_RANK, MLA_HEADS * HEAD_PAD)),
        full((MLA_RANK, MLA_HEADS * HEAD_PAD)),
        full((MLA_RANK, MLA_HEADS * MLA_V)),
        rows(HEAD_PAD),
        rows(HEAD_PAD),
    ]
    out_specs = [
        rows(4 * ML_WIDTH),
        pl.BlockSpec((16, TM), lambda i: (0, i)),
        rows(CM_WIDTH),
        rows(MLA_RANK),
        rows(MLA_ROPE),
        pl.BlockSpec((MLA_HEADS, TM, HEAD_PAD), lambda i: (0, i, 0)),
        pl.BlockSpec((MLA_HEADS, TM, HEAD_PAD), lambda i: (0, i, 0)),
        rows(MLA_HEADS * MLA_V),
    ]
    out_shape = [
        jax.ShapeDtypeStruct((T_ALL, 4 * ML_WIDTH), F32),
        jax.ShapeDtypeStruct((16, T_ALL), F32),
        jax.ShapeDtypeStruct((T_ALL, CM_WIDTH), F32),
        jax.ShapeDtypeStruct((T_ALL, MLA_RANK), F32),
        jax.ShapeDtypeStruct((T_ALL, MLA_ROPE), F32),
        jax.ShapeDtypeStruct((MLA_HEADS, T_ALL, HEAD_PAD), BF16),
        jax.ShapeDtypeStruct((MLA_HEADS, T_ALL, HEAD_PAD), BF16),
        jax.ShapeDtypeStruct((T_ALL, MLA_HEADS * MLA_V), BF16),
    ]
    return pl.pallas_call(
        _proj_kernel,
        grid=(nblk,),
        in_specs=in_specs,
        out_specs=out_specs,
        out_shape=out_shape,
        compiler_params=pltpu.CompilerParams(dimension_semantics=("parallel",), vmem_limit_bytes=VMEM_LIMIT),
        name="proj_in",
    )(x_ctx, x_smp, mod, lw["g1"], lw["wml"], lw["wgt"], lw["gb_col"], lw["wcm"], lw["wmla"],
      lw["wkr"], lw["cmg"], lw["ws"], lw["cmb"], lw["qg"], lw["kvg"], lw["wuqa"], lw["wuqb"], lw["wk"], lw["wv"],
      cos_t, sin_t)


def _mlstm_schedule():
    pair, fwd_a, bwd_a, fwd_b, bwd_b, first = [], [], [], [], [], []
    base = 0
    for p in range((N_CTX_SEQ + N_SMP_SEQ) // 2):
        nc = (CTX_LEN if 2 * p < N_CTX_SEQ else SMP_LEN) // CHUNK
        for j in range(nc):
            pair.append(p)
            fwd_a.append(base + j)
            bwd_a.append(base + nc - 1 - j)
            fwd_b.append(base + nc + j)
            bwd_b.append(base + 2 * nc - 1 - j)
            first.append(1 if j == 0 else 0)
        base += 2 * nc
    as_i32 = lambda a: jnp.asarray(np.asarray(a, np.int32))
    return tuple(as_i32(a) for a in (pair, fwd_a, bwd_a, fwd_b, bwd_b, first))


def _scan_cummax(x, direction):
    L = CHUNK
    rows = x.shape[0]
    x = jnp.concatenate([x, x], axis=0)
    lane = lax.broadcasted_iota(jnp.int32, x.shape, 1)
    k = 1
    while k < L:
        if direction == 0:
            shifted = jnp.where(lane >= k, pltpu.roll(x, k, axis=1), NEG_INF)
        else:
            shifted = jnp.where(lane < L - k, pltpu.roll(x, L - k, axis=1), NEG_INF)
        x = jnp.maximum(x, shifted)
        k *= 2
    return x[0:rows]


def _rows_to_lane_cols(rows, eye, rep, pieces):
    x = jnp.concatenate([jnp.broadcast_to(rows[h:h + 1, :], (rep, CHUNK)) for h in range(ML_HEADS)], axis=0)
    out = None
    for _ in range(pieces):
        xb = x.astype(BF16)
        part = _dot_nt(eye, xb)
        out = part if out is None else out + part
        x = x - xb.astype(F32)
    return out


def _per_head_lanes(x512, lane64):
    lo = jnp.where(lane64, x512[:, 0:128], x512[:, 128:256])
    hi = jnp.where(lane64, x512[:, 256:384], x512[:, 384:512])
    return jnp.concatenate([lo, hi], axis=1)


def _mlstm_direction(ml, g_row, direction, slot, c_ref, n_ref, m_ref):
    L = CHUNK
    t_idx = lax.broadcasted_iota(jnp.int32, (L, L), 0)
    s_idx = lax.broadcasted_iota(jnp.int32, (L, L), 1)
    visible = (s_idx <= t_idx) if direction == 0 else (s_idx >= t_idx)
    tri = jnp.where(visible, 1.0, 0.0).astype(BF16)
    eye = jnp.where(s_idx == t_idx, 1.0, 0.0).astype(BF16)
    lane64 = lax.broadcasted_iota(jnp.int32, (1, 128), 1) < ML_DIM
    head_of_lane = lax.broadcasted_iota(jnp.int32, (1, ML_WIDTH), 1) >> 6
    same_head = ((lax.broadcasted_iota(jnp.int32, (ML_WIDTH, ML_WIDTH), 0) >> 6)
                 == (lax.broadcasted_iota(jnp.int32, (ML_WIDTH, ML_WIDTH), 1) >> 6))

    i0 = 8 * direction
    i_row = g_row[i0:i0 + ML_HEADS, :]
    lf_row = _log_sigmoid(g_row[i0 + ML_HEADS:i0 + 2 * ML_HEADS, :])
    r1, r2, r3 = _split3(lf_row)
    b_row = _dot_nt(r1, tri) + _dot_nt(r2, tri) + _dot_nt(r3, tri)
    yield None
    m_rep = m_ref[slot, direction, 0:ML_HEADS, :]
    a_row = i_row - b_row
    g_row_ = jnp.maximum(m_rep, _scan_cummax(a_row, direction))
    yield None
    b_end = jnp.sum(lf_row, axis=1, keepdims=True)
    log_w = b_end - b_row + i_row
    m_new = jnp.maximum(b_end + m_rep, jnp.max(log_w, axis=1, keepdims=True))
    w_k_row = jnp.exp(log_w - m_new)
    decay_rep = jnp.exp(b_end + m_rep - m_new)

    yield None
    g512 = _rows_to_lane_cols(g_row_, eye, 128, 2)
    g_full = _per_head_lanes(g512, lane64)
    yield None
    b_full = _rows_to_lane_cols(b_row, eye, ML_DIM, 2)
    wk_full = _rows_to_lane_cols(w_k_row, eye, ML_DIM, 1)
    m_full = _per_head_lanes(jnp.concatenate([m_rep[h:h + 1, :] for h in range(ML_HEADS)], axis=1), lane64)
    decay_full = _per_head_lanes(jnp.concatenate([decay_rep[h:h + 1, :] for h in range(ML_HEADS)], axis=1),
                                 lane64)

    yield None
    q = ml[:, 0:ML_WIDTH]
    k = ml[:, ML_WIDTH:2 * ML_WIDTH] * (ML_DIM ** -0.5)
    v = ml[:, 2 * ML_WIDTH:3 * ML_WIDTH]
    qb = q.astype(BF16)
    kb = k.astype(BF16)
    vb = v.astype(BF16)
    num = jnp.zeros((L, ML_WIDTH), F32)
    rowsum = jnp.zeros((L, ML_WIDTH), F32)
    for hd in range(ML_HEADS):
        w_intra = jnp.where(visible, jnp.exp(a_row[hd:hd + 1, :] - g512[:, hd * 128:(hd + 1) * 128]), 0.0)
        q_h = jnp.where(head_of_lane == hd, q, 0.0).astype(BF16)
        sw = _dot_nt(q_h, kb) * w_intra
        num = jnp.where(head_of_lane == hd, _dot(sw.astype(BF16), vb), num)
        rowsum = jnp.where(head_of_lane == hd, jnp.sum(sw, axis=-1, keepdims=True), rowsum)
        yield None

    C = c_ref[slot, direction]
    n_row = n_ref[slot, direction, 0:1, :]
    w_inter = jnp.exp(m_full - g_full)
    block_ones = jnp.where(same_head, 1.0, 0.0).astype(BF16)
    qn = _dot((q * n_row).astype(BF16), block_ones)
    num = num + w_inter * _dot(qb, C.astype(BF16))
    den = rowsum + w_inter * qn
    h_out = num / jnp.maximum(jnp.abs(den), jnp.exp(-(b_full + g_full)))

    kw = wk_full * k
    c_ref[slot, direction] = decay_full * C + jnp.where(same_head, _dot_tn(kw.astype(BF16), vb), 0.0)
    n_ref[slot, direction, 0:1, :] = decay_full * n_row + jnp.sum(kw, axis=0, keepdims=True)
    m_ref[slot, direction, 0:ML_HEADS, :] = m_new
    yield h_out


def _interleave(stage_generators):
    results = [None] * len(stage_generators)
    live = list(range(len(stage_generators)))
    while live:
        for idx in list(live):
            try:
                value = next(stage_generators[idx])
                if value is not None:
                    results[idx] = value
            except StopIteration:
                live.remove(idx)
    return results


def _mlstm_kernel(pair_ref, fa_ref, ba_ref, fb_ref, bb_ref, first_ref,
                  mlfa_ref, mlba_ref, mlfb_ref, mlbb_ref, grfa_ref, grba_ref, grfb_ref, grbb_ref,
                  c0_ref, n0_ref, m0_ref, hf_ref, hb_ref, c_ref, n_ref, m_ref):
    step = pl.program_id(0)

    @pl.when(first_ref[step] == 1)
    def _():
        c_ref[...] = c0_ref[...]
        n_ref[...] = n0_ref[...]
        m_ref[...] = m0_ref[...]

    hfa, hba, hfb, hbb = _interleave([
        _mlstm_direction(mlfa_ref[...], grfa_ref[...], 0, 0, c_ref, n_ref, m_ref),
        _mlstm_direction(mlba_ref[...], grba_ref[...], 1, 0, c_ref, n_ref, m_ref),
        _mlstm_direction(mlfb_ref[...], grfb_ref[...], 0, 1, c_ref, n_ref, m_ref),
        _mlstm_direction(mlbb_ref[...], grbb_ref[...], 1, 1, c_ref, n_ref, m_ref)])
    hf_ref[0, 0] = hfa
    hf_ref[0, 1] = hfb
    hb_ref[0, 0] = hba
    hb_ref[0, 1] = hbb


def _mlstm_call(ml, gates_t, c0, n0, m0):
    sched = _mlstm_schedule()
    nseq = N_CTX_SEQ + N_SMP_SEQ
    nsteps = int(sched[0].shape[0])
    chunk_rows = lambda which: pl.BlockSpec((CHUNK, 4 * ML_WIDTH), lambda i, *s: (s[which][i], 0))
    gate_cols = lambda which: pl.BlockSpec((16, CHUNK), lambda i, *s: (0, s[which][i]))
    c_spec = pl.BlockSpec((2, 2, ML_WIDTH, ML_WIDTH), lambda i, *s: (s[0][i], 0, 0, 0))
    n_spec = pl.BlockSpec((2, 2, 8, ML_WIDTH), lambda i, *s: (s[0][i], 0, 0, 0))
    m_spec = pl.BlockSpec((2, 2, 8, 128), lambda i, *s: (s[0][i], 0, 0, 0))
    h_spec = pl.BlockSpec((1, 2, CHUNK, ML_WIDTH), lambda i, *s: (i, 0, 0, 0))
    grid_spec = pltpu.PrefetchScalarGridSpec(
        num_scalar_prefetch=6,
        grid=(nsteps,),
        in_specs=[chunk_rows(1), chunk_rows(2), chunk_rows(3), chunk_rows(4),
                  gate_cols(1), gate_cols(2), gate_cols(3), gate_cols(4), c_spec, n_spec, m_spec],
        out_specs=[h_spec, h_spec, c_spec, n_spec, m_spec],
    )
    return pl.pallas_call(
        _mlstm_kernel,
        grid_spec=grid_spec,
        out_shape=[
            jax.ShapeDtypeStruct((nsteps, 2, CHUNK, ML_WIDTH), F32),
            jax.ShapeDtypeStruct((nsteps, 2, CHUNK, ML_WIDTH), F32),
            jax.ShapeDtypeStruct((nseq, 2, ML_WIDTH, ML_WIDTH), F32),
            jax.ShapeDtypeStruct((nseq, 2, 8, ML_WIDTH), F32),
            jax.ShapeDtypeStruct((nseq, 2, 8, 128), F32),
        ],
        compiler_params=pltpu.CompilerParams(dimension_semantics=("arbitrary",), vmem_limit_bytes=VMEM_LIMIT),
        name="mlstm",
    )(*sched, ml, ml, ml, ml, gates_t, gates_t, gates_t, gates_t, c0, n0, m0)


def _h_pair_specs():
    ctx_blk = T_CTX // TM
    blk_per_seq = SMP_LEN // TM
    ctx_steps = N_CTX_SEQ // 2 * (CTX_LEN // CHUNK)

    def place(i):
        i_s = jnp.maximum(i - ctx_blk, 0)
        seq_s = i_s // blk_per_seq
        jb = i_s % blk_per_seq
        is_ctx = i < ctx_blk
        slot = jnp.where(is_ctx, i % 2, seq_s % 2)
        base = (ctx_steps + (seq_s // 2) * (SMP_LEN // CHUNK)) // 2
        fwd = jnp.where(is_ctx, i // 2, base + jb)
        bwd = jnp.where(is_ctx, i // 2, base + blk_per_seq - 1 - jb)
        return fwd, bwd, slot

    block = (2, None, CHUNK, ML_WIDTH)
    return (pl.BlockSpec(block, lambda i: (place(i)[0], place(i)[2], 0, 0)),
            pl.BlockSpec(block, lambda i: (place(i)[1], place(i)[2], 0, 0)))


def _cache_kv_kernel(ckv_ref, kr_ref, wk_ref, wv_ref, place_ref, k_ref, v_ref):
    ckvb = ckv_ref[...].astype(BF16)
    kp = _dot(ckvb, wk_ref[...])
    kr128 = _dot(kr_ref[...].astype(BF16), place_ref[...])
    for hd in range(MLA_HEADS):
        k_ref[hd] = (kp[:, hd * HEAD_PAD:(hd + 1) * HEAD_PAD] + kr128).astype(BF16)
    v_ref[...] = _dot(ckvb, wv_ref[...]).astype(BF16)


def _cache_kv_call(ckv, kr, wk, wv, place):
    n = N_SMP_SEQ * PAST_LEN
    tb = PAST_LEN
    return pl.pallas_call(
        _cache_kv_kernel,
        grid=(n // tb,),
        in_specs=[
            pl.BlockSpec((tb, MLA_RANK), lambda i: (i, 0)),
            pl.BlockSpec((tb, MLA_ROPE), lambda i: (i, 0)),
            pl.BlockSpec((MLA_RANK, MLA_HEADS * HEAD_PAD), lambda i: (0, 0)),
            pl.BlockSpec((MLA_RANK, MLA_HEADS * MLA_V), lambda i: (0, 0)),
            pl.BlockSpec((MLA_ROPE, HEAD_PAD), lambda i: (0, 0)),
        ],
        out_specs=[
            pl.BlockSpec((MLA_HEADS, tb, HEAD_PAD), lambda i: (0, i, 0)),
            pl.BlockSpec((tb, MLA_HEADS * MLA_V), lambda i: (i, 0)),
        ],
        out_shape=[
            jax.ShapeDtypeStruct((MLA_HEADS, n, HEAD_PAD), BF16),
            jax.ShapeDtypeStruct((n, MLA_HEADS * MLA_V), BF16),
        ],
        compiler_params=pltpu.CompilerParams(dimension_semantics=("parallel",)),
        name="cache_kv",
    )(ckv, kr, wk, wv, place)


def _attn_kernel(*refs, has_cache):
    if has_cache:
        q_ref, kn_ref, vn_ref, kc_ref, vc_ref, o_ref = refs
    else:
        q_ref, kn_ref, vn_ref, o_ref = refs

    def head(hd):
        lanes = slice(hd * MLA_V, (hd + 1) * MLA_V)
        q = q_ref[hd]
        s_n = _dot_nt(q, kn_ref[hd])
        if has_cache:
            s_c = _dot_nt(q, kc_ref[hd])
        yield None
        m = jnp.max(s_n, axis=-1, keepdims=True)
        if has_cache:
            m = jnp.maximum(m, jnp.max(s_c, axis=-1, keepdims=True))
        yield None
        p_n = jnp.exp2(s_n - m)
        l = jnp.sum(p_n, axis=-1, keepdims=True)
        o = _dot(p_n.astype(BF16), vn_ref[:, lanes])
        if has_cache:
            p_c = jnp.exp2(s_c - m)
            l = l + jnp.sum(p_c, axis=-1, keepdims=True)
            o = o + _dot(p_c.astype(BF16), vc_ref[:, lanes])
        yield o / l

    outs = []
    for hd in range(0, MLA_HEADS, ATTN_HEADS_IN_FLIGHT):
        outs += _interleave([head(hd + i) for i in range(ATTN_HEADS_IN_FLIGHT)])
    o_ref[...] = jnp.concatenate(outs, axis=-1)


def _attn_ctx_call(q, k, v):
    tq = CTX_LEN
    return pl.pallas_call(
        functools.partial(_attn_kernel, has_cache=False),
        grid=(N_CTX_SEQ,),
        in_specs=[
            pl.BlockSpec((MLA_HEADS, tq, HEAD_PAD), lambda s: (0, s, 0)),
            pl.BlockSpec((MLA_HEADS, tq, HEAD_PAD), lambda s: (0, s, 0)),
            pl.BlockSpec((tq, MLA_HEADS * MLA_V), lambda s: (s, 0)),
        ],
        out_specs=pl.BlockSpec((tq, MLA_HEADS * MLA_V), lambda s: (s, 0)),
        out_shape=jax.ShapeDtypeStruct((T_CTX, MLA_HEADS * MLA_V), F32),
        compiler_params=pltpu.CompilerParams(dimension_semantics=("parallel",), vmem_limit_bytes=VMEM_LIMIT),
        name="attn_ctx",
    )(q, k, v)


def _attn_smp_call(q, k, v, kc, vc):
    tq = 256
    qb_per_seq = SMP_LEN // tq
    ctx_qb = T_CTX // tq
    ctx_kb = T_CTX // SMP_LEN
    return pl.pallas_call(
        functools.partial(_attn_kernel, has_cache=True),
        grid=(N_SMP_SEQ, qb_per_seq),
        in_specs=[
            pl.BlockSpec((MLA_HEADS, tq, HEAD_PAD), lambda b, i: (0, ctx_qb + b * qb_per_seq + i, 0)),
            pl.BlockSpec((MLA_HEADS, SMP_LEN, HEAD_PAD), lambda b, i: (0, ctx_kb + b, 0)),
            pl.BlockSpec((SMP_LEN, MLA_HEADS * MLA_V), lambda b, i: (ctx_kb + b, 0)),
            pl.BlockSpec((MLA_HEADS, PAST_LEN, HEAD_PAD), lambda b, i: (0, b, 0)),
            pl.BlockSpec((PAST_LEN, MLA_HEADS * MLA_V), lambda b, i: (b, 0)),
        ],
        out_specs=pl.BlockSpec((tq, MLA_HEADS * MLA_V), lambda b, i: (b * qb_per_seq + i, 0)),
        out_shape=jax.ShapeDtypeStruct((T_SMP, MLA_HEADS * MLA_V), F32),
        compiler_params=pltpu.CompilerParams(dimension_semantics=("parallel", "parallel"),
                                             vmem_limit_bytes=VMEM_LIMIT),
        name="attn_smp",
    )(q, k, v, kc, vc)


def _out_kernel(xc_ref, xs_ref, mod_ref, ml_ref, hf_ref, hb_ref, cm_ref, attc_ref, atts_ref, wo_ref, g2_ref,
                wq_ref, sk_ref, x1_ref, h2t_ref, st_ref):
    is_ctx = pl.program_id(0) < T_CTX // TM
    att = jnp.where(is_ctx, attc_ref[...], atts_ref[...])
    mod = mod_ref[0]
    g1 = mod[:, 2 * D_MODEL:3 * D_MODEL]
    sh2 = mod[:, 3 * D_MODEL:4 * D_MODEL]
    sc2 = mod[:, 4 * D_MODEL:5 * D_MODEL]
    h_sum = jnp.concatenate([hf_ref[0] + hb_ref[1], hf_ref[1] + hb_ref[0]], axis=0)
    mlo = _sigmoid(ml_ref[:, 3 * ML_WIDTH:4 * ML_WIDTH]) * h_sum
    mix = (_dot(mlo.astype(BF16), wo_ref[0:ML_WIDTH, :])
           + _dot(cm_ref[...].astype(BF16), wo_ref[ML_WIDTH:ML_WIDTH + CM_WIDTH, :])
           + _dot(att.astype(BF16), wo_ref[ML_WIDTH + CM_WIDTH:, :]))
    x1 = _select_x(xc_ref, xs_ref) + g1 * mix
    x1_ref[...] = x1
    h2f = _rms(x1, g2_ref[...]) * (1.0 + sc2) + sh2
    h2t_ref[...] = h2f.T.astype(BF16)
    qp = _dot(h2f.astype(BF16), wq_ref[...]).astype(BF16)
    for hh in range(2 * PEER_HEADS):
        st_ref[hh] = _dot_nt(sk_ref[hh % 2], qp[:, hh * 128:(hh + 1) * 128])


def _out_call(x_ctx, x_smp, mod, ml, hf, hb, cm, att_ctx, att_smp, lw):
    nblk = T_ALL // TM
    ctx_blk = T_CTX // TM
    rows = lambda w: pl.BlockSpec((TM, w), lambda i: (i, 0))
    full = lambda shape: pl.BlockSpec(shape, lambda i: (0,) * len(shape))
    att_w = MLA_HEADS * MLA_V
    attc_spec = pl.BlockSpec((TM, att_w), lambda i: (jnp.minimum(i, ctx_blk - 1), 0))
    atts_spec = pl.BlockSpec((TM, att_w), lambda i: (jnp.maximum(i - ctx_blk, 0), 0))
    return pl.pallas_call(
        _out_kernel,
        grid=(nblk,),
        in_specs=_x_specs(x_ctx, x_smp) + [
            pl.BlockSpec((1, 1, 6 * D_MODEL), lambda i: (_mod_row_of_block(i, TM), 0, 0)),
            rows(4 * ML_WIDTH), *_h_pair_specs(), rows(CM_WIDTH), attc_spec, atts_spec,
            full((D_MODEL, D_MODEL)), full((1, D_MODEL)), full((D_MODEL, 2 * PEER_HEADS * 128)),
            full((2, PEER_NKEYS, 128)),
        ],
        out_specs=[rows(D_MODEL), pl.BlockSpec((D_MODEL, TM), lambda i: (0, i)),
                   pl.BlockSpec((2 * PEER_HEADS, PEER_NKEYS, TM), lambda i: (0, 0, i))],
        out_shape=[
            jax.ShapeDtypeStruct((T_ALL, D_MODEL), F32),
            jax.ShapeDtypeStruct((D_MODEL, T_ALL), BF16),
            jax.ShapeDtypeStruct((2 * PEER_HEADS, PEER_NKEYS, T_ALL), F32),
        ],
        compiler_params=pltpu.CompilerParams(dimension_semantics=("parallel",), vmem_limit_bytes=VMEM_LIMIT),
        name="proj_out",
    )(x_ctx, x_smp, mod, ml, hf, hb, cm, att_ctx, att_smp, lw["wo"], lw["g2"], lw["wq"], lw["sk"])


def _sorting_network_pairs(n):
    pairs = []
    p = 1
    while p < n:
        k = p
        while k >= 1:
            for j in range(k % p, n - k, 2 * k):
                for i in range(min(k, n - j - k)):
                    if (i + j) // (2 * p) == (i + j + k) // (2 * p):
                        pairs.append((i + j, i + j + k))
            k //= 2
        p *= 2
    return pairs


_SORT16_PAIRS = _sorting_network_pairs(PEER_TOPK)


def _pop16(lists, singles=None):
    lists = list(lists)
    vals = []
    for k in range(PEER_TOPK):
        heads = lists[0] if singles is None else jnp.maximum(lists[0], singles)
        m = jnp.max(heads, axis=0, keepdims=True)
        vals.append(m)
        if k == PEER_TOPK - 1:
            break
        hit = lists[0] == m
        for i in range(PEER_TOPK - 1 - k):
            lists[i] = jnp.where(hit, lists[i + 1], lists[i])
        if singles is not None:
            singles = jnp.where(singles == m, NEG_INF, singles)
    return vals


def _top16_rows(s):
    tiles = [s[8 * j:8 * j + 8] for j in range(s.shape[0] // 8)]
    assert len(tiles) == PEER_TOPK
    for i, j in _SORT16_PAIRS:
        tiles[i], tiles[j] = jnp.maximum(tiles[i], tiles[j]), jnp.minimum(tiles[i], tiles[j])
    return _pop16(tiles)


def _count_steps(x, thresholds, below):
    r = jnp.zeros(x.shape, F32)
    for q, t in enumerate(thresholds):
        r = jnp.where((x < t) if below else (x >= t), float(q + 1), r)
    return r


def _pack_rows_bf16(x):
    return pltpu.bitcast(_pack_rows_words(x), BF16)


def _pack_rows_words(x):
    r, n = x.shape
    x4 = x.reshape(r // 16, 2, 8, n)
    lo = x4[:, 0].reshape(r // 2, n)
    hi = x4[:, 1].reshape(r // 2, n)
    return pltpu.pack_elementwise([lo, hi], packed_dtype=BF16)


def _dup_bf16_words(x):
    return pltpu.pack_elementwise([x, x], packed_dtype=BF16)


def _rows_to_array(rows, row_iota):
    arr = jnp.zeros(row_iota.shape, F32)
    for i, r in enumerate(rows):
        arr = jnp.where(row_iota == i, r, arr)
    return arr


def _topk_kernel(st_ref, e1_ref, cut_ref, e2_ref, r2_ref):
    n = st_ref.shape[-1]
    row8 = lax.broadcasted_iota(jnp.int32, (8, n), 0)

    def head(hd, carry):
        s1 = st_ref[2 * hd]
        s2 = st_ref[2 * hd + 1]
        v1 = _top16_rows(s1)
        v2 = _top16_rows(s2)
        v1_top = _rows_to_array(v1[0:8], row8)
        v1_bot = _rows_to_array(v1[8:PEER_TOPK], row8)
        c = _pop16([v1_top + v2[q] for q in range(PEER_TOPK)], singles=v1_bot + v2[0])
        tau = c[PEER_TOPK - 1]
        z = jnp.zeros((1, n), F32)
        for ck in c:
            z = z + jnp.exp(ck - c[0])

        def smallest_qualifying(v1_rows, q):
            return jnp.min(jnp.where(v1_rows + v2[q] >= tau, v1_rows, jnp.inf), axis=0, keepdims=True)

        sigma = [jnp.minimum(smallest_qualifying(v1_top, 0), smallest_qualifying(v1_bot, 0))]
        sigma += [smallest_qualifying(v1_top, q) for q in range(1, 8)]
        extra = jnp.zeros((1, n), F32)
        last_selected = v2[7]
        for q in range(8, PEER_TOPK):
            reached = v1[0] + v2[q] >= tau
            extra = extra + jnp.where(reached, 1.0, 0.0)
            last_selected = jnp.where(reached, v2[q], last_selected)
        rank2 = jnp.where(s2 < last_selected, 9.0, _count_steps(s2, v2[0:8], below=True))
        cut = _count_steps(s1, sigma, below=False) + jnp.where(s1 >= v1[0], jnp.minimum(extra, 1.0), 0.0)
        e1_ref[hd] = _dup_bf16_words(jnp.exp(s1 - v1[0]) / z)
        cut_ref[hd] = _dup_bf16_words(cut)
        e2_ref[hd] = _pack_rows_words(jnp.exp(s2 - v2[0]))
        r2_ref[hd] = _pack_rows_words(rank2)
        return carry

    lax.fori_loop(0, PEER_HEADS, head, 0)


def _topk_call(st):
    tn = TN_TOPK
    spec = pl.BlockSpec((PEER_HEADS, PEER_NKEYS, tn), lambda i: (0, 0, i))
    word_spec = pl.BlockSpec((PEER_HEADS, PEER_NKEYS // 2, tn), lambda i: (0, 0, i))
    stat = lambda dt: jax.ShapeDtypeStruct((PEER_HEADS, PEER_NKEYS, T_ALL), dt)
    words = jax.ShapeDtypeStruct((PEER_HEADS, PEER_NKEYS // 2, T_ALL), jnp.uint32)
    return pl.pallas_call(
        _topk_kernel,
        grid=(T_ALL // tn,),
        in_specs=[pl.BlockSpec((2 * PEER_HEADS, PEER_NKEYS, tn), lambda i: (0, 0, i))],
        out_specs=[spec, spec, word_spec, word_spec],
        out_shape=[stat(jnp.uint32), stat(jnp.uint32), words, words],
        compiler_params=pltpu.CompilerParams(dimension_semantics=("parallel",), vmem_limit_bytes=VMEM_LIMIT),
        name="peer_topk",
    )(st)


def _row_tile_bf16(row):
    blk = pltpu.bitcast(jnp.broadcast_to(row, (8, row.shape[1])), BF16)
    return jnp.concatenate([blk] * (PEER_NKEYS // 16), axis=0)


def _gelu_tanh_bf16(x):
    log2e = 1.4426950408889634
    w = x * (-1.5957691216057308 * log2e - (0.07135481627159584 * log2e) * (x * x))
    return x / (1.0 + jnp.exp2(w))


def _expert_kernel(h2t_ref, u_ref, vt_ref, e1_ref, cut_ref, e2_ref, r2_ref, x1_ref, mod_ref, fg_ref,
                   o_ref, acc_ref, a_ref, *, final_norm):
    j = pl.program_id(1)
    n_tiles = pl.num_programs(1)
    tm = h2t_ref.shape[1]
    a_per_tile = TE_EXP // PEER_NKEYS

    @pl.when(j == 0)
    def _():
        acc_ref[...] = jnp.zeros_like(acc_ref)

    def gate_tile(ai):
        a = j * a_per_tile + ai
        halves = []
        for lanes in (slice(0, tm // 2), slice(tm // 2, tm)):
            gate = jnp.zeros((PEER_NKEYS, tm // 2), BF16)
            for hd in range(PEER_HEADS):
                cut_a = _row_tile_bf16(cut_ref[hd, pl.ds(a, 1), lanes])
                e1_a = _row_tile_bf16(e1_ref[hd, pl.ds(a, 1), lanes])
                r2 = pltpu.bitcast(r2_ref[hd, :, lanes], BF16)
                e2 = pltpu.bitcast(e2_ref[hd, :, lanes], BF16)
                gate = gate + jnp.where(r2 < cut_a, e2, jnp.zeros((), BF16)) * e1_a
            halves.append(gate)
        return jnp.concatenate(halves, axis=1)

    a_per_sub = EXP_SUB // PEER_NKEYS
    n_sub = TE_EXP // EXP_SUB
    s = [None] * n_sub
    gates = [None] * n_sub
    s[0] = _dot(u_ref[0:EXP_SUB, :], h2t_ref[...])
    gates[0] = [gate_tile(ai) for ai in range(a_per_sub)]
    for sub in range(n_sub):
        if sub + 1 < n_sub:
            s[sub + 1] = _dot(u_ref[(sub + 1) * EXP_SUB:(sub + 2) * EXP_SUB, :], h2t_ref[...])
            gates[sub + 1] = [gate_tile((sub + 1) * a_per_sub + i) for i in range(a_per_sub)]
        g = _gelu_tanh_bf16(_pack_rows_bf16(s[sub]))
        for i in range(a_per_sub):
            r0 = sub * EXP_SUB + i * PEER_NKEYS
            a_ref[r0:r0 + PEER_NKEYS, :] = gates[sub][i] * g[i * PEER_NKEYS:(i + 1) * PEER_NKEYS, :]
    acc_ref[...] += _dot(vt_ref[...], a_ref[...])

    @pl.when(j == n_tiles - 1)
    def _():
        g2 = mod_ref[0][:, 5 * D_MODEL:6 * D_MODEL]
        y = x1_ref[...] + g2 * acc_ref[...].T
        if final_norm:
            y = _rms(y, fg_ref[...])
        o_ref[...] = y


def _expert_call(h2t, u_bf, vt_bf, layer, e1, cut, e2, r2, x1, mod, final_g, final_norm, tok_start, tok_count):
    tm, te = TM_EXP, TE_EXP
    b0 = tok_start // tm
    n_tiles = PEER_EXPERTS // te
    stat = pl.BlockSpec((PEER_HEADS, PEER_NKEYS, tm), lambda i, j: (0, 0, b0 + i))
    stat_words = pl.BlockSpec((PEER_HEADS, PEER_NKEYS // 2, tm), lambda i, j: (0, 0, b0 + i))
    return pl.pallas_call(
        functools.partial(_expert_kernel, final_norm=final_norm),
        grid=(tok_count // tm, n_tiles),
        in_specs=[
            pl.BlockSpec((D_MODEL, tm), lambda i, j: (0, b0 + i)),
            pl.BlockSpec((None, te, D_MODEL), lambda i, j: (layer, j, 0)),
            pl.BlockSpec((None, D_MODEL, te), lambda i, j: (layer, 0, j)),
            stat, stat, stat_words, stat_words,
            pl.BlockSpec((tm, D_MODEL), lambda i, j: (b0 + i, 0)),
            pl.BlockSpec((1, 1, 6 * D_MODEL), lambda i, j: (_mod_row_of_block(b0 + i, tm), 0, 0)),
            pl.BlockSpec((1, D_MODEL), lambda i, j: (0, 0)),
        ],
        out_specs=pl.BlockSpec((tm, D_MODEL), lambda i, j: (i, 0)),
        out_shape=jax.ShapeDtypeStruct((tok_count, D_MODEL), F32),
        scratch_shapes=[
            pltpu.VMEM((D_MODEL, tm), F32),
            pltpu.VMEM((te, tm), BF16),
        ],
        compiler_params=pltpu.CompilerParams(dimension_semantics=("parallel", "arbitrary"),
                                             vmem_limit_bytes=VMEM_LIMIT),
        name="peer_experts",
    )(h2t, u_bf, vt_bf, e1, cut, e2, r2, x1, mod, final_g)


def _u_prep_kernel(u_ref, o_ref):
    o_ref[...] = u_ref[...].astype(BF16)


def _v_prep_kernel(v_ref, o_ref):
    o_ref[...] = _pack_rows_bf16(v_ref[...]).T


def _table_prep_call(peer_u, peer_v):
    te = TE_PREP
    grid = (DEPTH, PEER_EXPERTS // te)
    rows = pl.BlockSpec((None, te, D_MODEL), lambda l, j: (l, j, 0))
    params = pltpu.CompilerParams(dimension_semantics=("parallel", "parallel"), vmem_limit_bytes=VMEM_LIMIT)
    u_bf = pl.pallas_call(
        _u_prep_kernel, grid=grid, in_specs=[rows], out_specs=rows,
        out_shape=jax.ShapeDtypeStruct((DEPTH, PEER_EXPERTS, D_MODEL), BF16),
        compiler_params=params, name="peer_u_prep")(peer_u)
    vt_bf = pl.pallas_call(
        _v_prep_kernel, grid=grid, in_specs=[rows],
        out_specs=pl.BlockSpec((None, D_MODEL, te), lambda l, j: (l, 0, j)),
        out_shape=jax.ShapeDtypeStruct((DEPTH, D_MODEL, PEER_EXPERTS), BF16),
        compiler_params=params, name="peer_v_prep")(peer_v)
    return u_bf, vt_bf


def _rope_swap_cols(w):
    return jnp.concatenate([-w[:, 8:16], w[:, 0:8], -w[:, 24:32], w[:, 16:24]], axis=1)


def _pad_heads(parts, n_heads):
    k = next(p[0].shape[0] for p in parts if p[0] is not None)
    cols = []
    for hd in range(n_heads):
        for arr, w in parts:
            cols.append(jnp.zeros((k, w), F32) if arr is None else arr[:, hd * w:(hd + 1) * w])
    return jnp.concatenate(cols, axis=1)


def _rope_tables():
    pos = np.arange(SMP_LEN)
    freqs = ROPE_THETA ** (-np.arange(0, ROPE_AXIS, 2, dtype=np.float32) / ROPE_AXIS)
    ang_r = (pos // GRID_W).astype(np.float32)[:, None] * freqs
    ang_c = (pos % GRID_W).astype(np.float32)[:, None] * freqs
    ang = jnp.asarray(np.concatenate([ang_r, ang_r, ang_c, ang_c], axis=1).astype(np.float32))
    cos32 = jnp.cos(ang)
    sin32 = jnp.sin(ang)
    ones = jnp.ones((SMP_LEN, MLA_NOPE), F32)
    cos_s = jnp.concatenate([ones, cos32, ones[:, :HEAD_PAD - MLA_NOPE - MLA_ROPE]], axis=1)
    sin_s = jnp.concatenate([0 * ones, sin32, 0 * ones[:, :HEAD_PAD - MLA_NOPE - MLA_ROPE]], axis=1)
    cos_t = jnp.concatenate([jnp.ones((T_CTX, HEAD_PAD), F32)] + [cos_s] * N_SMP_SEQ, axis=0)
    sin_t = jnp.concatenate([jnp.zeros((T_CTX, HEAD_PAD), F32)] + [sin_s] * N_SMP_SEQ, axis=0)
    return cos_t, sin_t


def _layer_weights(l, norm1_g, w_in, mlstm_gate_b, cm_norm_g, cm_ws, cm_b, mla_q_norm_g, mla_w_uq, mla_kv_norm_g,
                   mla_w_ukv, w_out, norm2_g, peer_w_q, peer_subkeys):
    w = w_in[l]
    o_g = 4 * ML_WIDTH
    o_cm = o_g + 16
    o_cq = o_cm + 2 * CM_WIDTH
    o_ckv = o_cq + MLA_RANK
    o_kr = o_ckv + MLA_RANK
    w_kr = w[:, o_kr:o_kr + MLA_ROPE]
    zeros_l = jnp.zeros((D_MODEL, MLA_NOPE), F32)
    zeros_r = jnp.zeros((D_MODEL, HEAD_PAD - MLA_NOPE - MLA_ROPE), F32)
    kr128 = jnp.concatenate([zeros_l, w_kr, zeros_r], axis=1)
    krsw128 = jnp.concatenate([zeros_l, _rope_swap_cols(w_kr), zeros_r], axis=1)
    uq = mla_w_uq[l].reshape(MLA_RANK, MLA_HEADS, MLA_NOPE + MLA_ROPE)
    uq_nope = uq[:, :, :MLA_NOPE].reshape(MLA_RANK, -1)
    uq_rope = uq[:, :, MLA_NOPE:].reshape(MLA_RANK, -1)
    uq_rope_sw = jnp.concatenate(
        [_rope_swap_cols(uq_rope[:, hd * MLA_ROPE:(hd + 1) * MLA_ROPE]) for hd in range(MLA_HEADS)], axis=1)
    pad_w = HEAD_PAD - MLA_NOPE - MLA_ROPE
    ukv = mla_w_ukv[l].reshape(MLA_RANK, MLA_HEADS, MLA_NOPE + MLA_V)
    uk = ukv[:, :, :MLA_NOPE].reshape(MLA_RANK, -1)
    uv = ukv[:, :, MLA_NOPE:].reshape(MLA_RANK, -1)
    gb = mlstm_gate_b[l]
    return {
        "g1": norm1_g[l].reshape(1, D_MODEL),
        "wml": w[:, 0:o_g].astype(BF16),
        "wgt": w[:, o_g:o_cm].T.astype(BF16),
        "gb_col": gb.reshape(16, 1),
        "wcm": w[:, o_cm:o_cq].astype(BF16),
        "wmla": jnp.concatenate([w[:, o_cq:o_kr], kr128, krsw128], axis=1).astype(BF16),
        "wkr": w_kr.astype(BF16),
        "cmg": cm_norm_g[l].reshape(1, CM_WIDTH),
        "ws": cm_ws[l].astype(BF16),
        "cmb": jnp.repeat(cm_b[l].T, CM_WIDTH // CM_GROUPS, axis=1),
        "qg": mla_q_norm_g[l].reshape(1, MLA_RANK),
        "kvg": mla_kv_norm_g[l].reshape(1, MLA_RANK),
        "wuqa": _pad_heads([(uq_nope, MLA_NOPE), (uq_rope, MLA_ROPE), (None, pad_w)], MLA_HEADS).astype(BF16),
        "wuqb": _pad_heads([(None, MLA_NOPE), (uq_rope_sw, MLA_ROPE), (None, pad_w)], MLA_HEADS).astype(BF16),
        "wk": _pad_heads([(uk, MLA_NOPE), (None, HEAD_PAD - MLA_NOPE)], MLA_HEADS).astype(BF16),
        "wv": uv.astype(BF16),
        "wo": w_out[l].astype(BF16),
        "g2": norm2_g[l].reshape(1, D_MODEL),
        "wq": peer_w_q[l].astype(BF16),
        "sk": peer_subkeys[l].astype(BF16),
    }


def kernel(x_prompt, x_sample, c, cache_mla_ckv, cache_mla_krope, state_mlstm_C, state_mlstm_n, state_mlstm_m, c_ctx, norm1_g, ada_w, ada_b, w_in, mlstm_gate_b, cm_norm_g, cm_ws, cm_b, mla_q_norm_g, mla_w_uq, mla_kv_norm_g, mla_w_ukv, w_out, norm2_g, peer_w_q, peer_subkeys, peer_u, peer_v, final_g):
    x_ctx = x_prompt.reshape(T_CTX, D_MODEL)
    x_smp = x_sample.reshape(T_SMP, D_MODEL)
    cvecs = jnp.concatenate([c_ctx[None, :], c, jnp.zeros((N_MOD_ROWS - 1 - N_SMP_SEQ, D_MODEL), F32)], axis=0)
    mod_all = _ada_call(cvecs, ada_w, ada_b)
    cos_t, sin_t = _rope_tables()
    place = jnp.concatenate([jnp.zeros((MLA_ROPE, MLA_NOPE), F32), jnp.eye(MLA_ROPE, dtype=F32),
                             jnp.zeros((MLA_ROPE, HEAD_PAD - MLA_NOPE - MLA_ROPE), F32)], axis=1).astype(BF16)
    final_g2 = final_g.reshape(1, D_MODEL)
    u_bf, vt_bf = _table_prep_call(peer_u, peer_v)

    ckvs, krs, Cs, ns, ms = [], [], [], [], []
    for l in range(DEPTH):
        lw = _layer_weights(l, norm1_g, w_in, mlstm_gate_b, cm_norm_g, cm_ws, cm_b, mla_q_norm_g, mla_w_uq,
                            mla_kv_norm_g, mla_w_ukv, w_out, norm2_g, peer_w_q, peer_subkeys)
        mod = mod_all[l].reshape(N_MOD_ROWS, 1, 6 * D_MODEL)
        ml, gates_t, cm, ckvn, kr, q, k, v = _proj_call(x_ctx, x_smp, mod, lw, cos_t, sin_t)

        c_blk = jnp.einsum('bdhij,hg->bdhigj', state_mlstm_C[:, l], jnp.eye(ML_HEADS, dtype=F32))
        c0 = jnp.concatenate([jnp.zeros((N_CTX_SEQ, 2, ML_WIDTH, ML_WIDTH), F32),
                              c_blk.reshape(N_SMP_SEQ, 2, ML_WIDTH, ML_WIDTH)], axis=0)
        n0 = jnp.concatenate([jnp.zeros((N_CTX_SEQ, 2, 1, ML_WIDTH), F32),
                              state_mlstm_n[:, l].reshape(N_SMP_SEQ, 2, 1, ML_WIDTH)], axis=0)
        n0 = jnp.pad(n0, ((0, 0), (0, 0), (0, 7), (0, 0)))
        m0 = jnp.concatenate([jnp.zeros((N_CTX_SEQ, 2, ML_HEADS), F32), state_mlstm_m[:, l]], axis=0)
        m0 = jnp.pad(jnp.broadcast_to(m0[..., None], m0.shape + (128,)), ((0, 0), (0, 0), (0, 8 - ML_HEADS), (0, 0)))
        hf, hb, c_fin, n_fin, m_fin = _mlstm_call(ml, gates_t, c0, n0, m0)

        kc, vc = _cache_kv_call(cache_mla_ckv[:, l].reshape(N_SMP_SEQ * PAST_LEN, MLA_RANK),
                                cache_mla_krope[:, l].reshape(N_SMP_SEQ * PAST_LEN, MLA_ROPE),
                                lw["wk"], lw["wv"], place)
        att_ctx = _attn_ctx_call(q, k, v)
        att_smp = _attn_smp_call(q, k, v, kc, vc)

        x1, h2t, st = _out_call(x_ctx, x_smp, mod, ml, hf, hb, cm, att_ctx, att_smp, lw)
        e1, cut, e2, r2 = _topk_call(st)
        experts = functools.partial(_expert_call, h2t, u_bf, vt_bf, l, e1, cut, e2, r2, x1, mod, final_g2)
        if l < DEPTH - 1:
            x_ctx = x_smp = experts(False, 0, T_ALL)
        else:
            x_ctx = experts(True, 0, T_CTX)
            x_smp = experts(True, T_CTX, T_SMP)

        ckvs.append(ckvn[:T_CTX].reshape(N_CTX_SEQ, CTX_LEN, MLA_RANK))
        krs.append(kr[:T_CTX].reshape(N_CTX_SEQ, CTX_LEN, MLA_ROPE))
        Cs.append(jnp.stack([c_fin[:N_CTX_SEQ, :, hd * ML_DIM:(hd + 1) * ML_DIM, hd * ML_DIM:(hd + 1) * ML_DIM]
                             for hd in range(ML_HEADS)], axis=2))
        ns.append(n_fin[:N_CTX_SEQ, :, 0, :].reshape(N_CTX_SEQ, 2, ML_HEADS, ML_DIM))
        ms.append(m_fin[:N_CTX_SEQ, :, 0:ML_HEADS, 0])

    y_prompt = x_ctx.reshape(N_CTX_SEQ, CTX_LEN, D_MODEL)
    y_sample = x_smp.reshape(N_SMP_SEQ, SMP_LEN, D_MODEL)
    return (y_prompt, y_sample, jnp.stack(ckvs, axis=1), jnp.stack(krs, axis=1), jnp.stack(Cs, axis=1),
            jnp.stack(ns, axis=1), jnp.stack(ms, axis=1))
```
